```python
import math
import jax, jax.numpy as jnp
from jax import lax
import numpy as np

D_MODEL = 1024
BATCH = 1
SEQ = 16384
DEPTH = 1

HEAD_DIM = D_MODEL // 8
DN_HEADS = 4
DN_DK = HEAD_DIM
DN_DV = HEAD_DIM
DN_CONV = 4
DN_CHUNK = 64
FOX_HEADS = 4
FOX_HD = HEAD_DIM
FOX_BLOCK = 128
MOE_EXPERTS = 32
MOE_TOPK = 4
MOE_FF = D_MODEL
MOE_BLOCK = 128
SWIGLU_LIMIT = 7.0
SWIGLU_ALPHA = 1.702
N_MOD = 6
RMS_EPS = 1e-6
L2_EPS = 1e-6

DN_QKV = DN_HEADS * (2 * DN_DK + DN_DV)
DN_W = DN_HEADS * DN_DV
FOX_W = FOX_HEADS * FOX_HD
IN_SIZES = (DN_QKV, DN_W, DN_HEADS, DN_HEADS, 3 * FOX_W, FOX_HEADS, D_MODEL, D_MODEL)
N_IN = sum(IN_SIZES)

kernel_name = "hybrid_deltanet_fox_moe_adaln"


def rms_norm(x, g):
    xf = x.astype(jnp.float32)
    y = xf * lax.rsqrt(jnp.mean(xf * xf, axis=-1, keepdims=True) + RMS_EPS)
    return (y * g.astype(jnp.float32)).astype(x.dtype)


def l2_normalize(x):
    return x * lax.rsqrt(jnp.sum(x * x, axis=-1, keepdims=True) + L2_EPS)


def causal_depthwise_conv(x, w):
    K, C = w.shape
    return lax.conv_general_dilated(x, w[:, None, :].astype(x.dtype), window_strides=(1,),
                                    padding=[(K - 1, 0)], dimension_numbers=("NWC", "WIO", "NWC"),
                                    feature_group_count=C)


def gated_delta_rule(q, k, v, g, beta):
    B, S, H, DK = q.shape
    DV = v.shape[-1]
    C = DN_CHUNK
    N = S // C
    f32 = jnp.float32
    q = l2_normalize(q.astype(f32)) * (DK ** -0.5)
    k = l2_normalize(k.astype(f32))
    v = v.astype(f32)

    def chunks(t):
        return jnp.transpose(t.reshape(B, N, C, H, t.shape[-1]), (0, 3, 1, 2, 4))

    q, k, v = chunks(q), chunks(k), chunks(v)
    g = jnp.transpose(g.astype(f32).reshape(B, N, C, H), (0, 3, 1, 2))
    beta = jnp.transpose(beta.astype(f32).reshape(B, N, C, H), (0, 3, 1, 2))
    gc = jnp.cumsum(g, axis=-1)
    incl = jnp.tril(jnp.ones((C, C), dtype=bool))
    strict = jnp.tril(jnp.ones((C, C), dtype=bool), -1)
    decay = jnp.exp(jnp.where(incl, gc[..., :, None] - gc[..., None, :], -jnp.inf))
    kb = k * beta[..., None]
    vb = v * beta[..., None]
    L = jnp.where(strict, jnp.einsum("bhncd,bhnmd->bhncm", kb, k) * decay, 0.0)
    eye = jnp.eye(C, dtype=f32)
    T = lax.linalg.triangular_solve(L + eye, jnp.broadcast_to(eye, L.shape), left_side=True, lower=True)
    u = jnp.einsum("bhncm,bhnmv->bhncv", T, vb)
    w = jnp.einsum("bhncm,bhnmk->bhnck", T, kb * jnp.exp(gc)[..., None])
    attn = jnp.where(incl, jnp.einsum("bhncd,bhnmd->bhncm", q, k) * decay, 0.0)
    q_dec = q * jnp.exp(gc)[..., None]
    g_last = gc[..., -1]
    k_dec = k * jnp.exp(g_last[..., None] - gc)[..., None]
    xs = tuple(jnp.moveaxis(t, 2, 0) for t in (u, w, q_dec, k_dec, attn, g_last))

    def step(state, inp):
        u_n, w_n, qd_n, kd_n, a_n, gl_n = inp
        v_new = u_n - jnp.einsum("bhck,bhkv->bhcv", w_n, state)
        o = jnp.einsum("bhck,bhkv->bhcv", qd_n, state) + jnp.einsum("bhcm,bhmv->bhcv", a_n, v_new)
        state = state * jnp.exp(gl_n)[..., None, None] + jnp.einsum("bhck,bhcv->bhkv", kd_n, v_new)
        return state, o

    state0 = jnp.zeros((B, H, DK, DV), f32)
    _, o = lax.scan(step, state0, xs)
    return jnp.transpose(o, (1, 0, 3, 2, 4)).reshape(B, S, H, DV)


def forgetting_attention(q, k, v, log_f):
    B, S, H, Dh = q.shape
    nb = S // FOX_BLOCK
    scale = Dh ** -0.5
    F = jnp.cumsum(log_f.astype(jnp.float32), axis=1)
    F_k = jnp.transpose(F, (0, 2, 1))
    q_blocks = jnp.moveaxis(q.reshape(B, nb, FOX_BLOCK, H, Dh), 1, 0)
    F_blocks = jnp.moveaxis(F.reshape(B, nb, FOX_BLOCK, H), 1, 0)
    k_pos = jnp.arange(S)

    def block(args):
        q_blk, F_blk, b_idx = args
        s = jnp.einsum("bqhd,bkhd->bhqk", q_blk, k).astype(jnp.float32) * scale
        s = s + jnp.transpose(F_blk, (0, 2, 1))[..., None] - F_k[:, :, None, :]
        q_pos = b_idx * FOX_BLOCK + jnp.arange(FOX_BLOCK)
        s = jnp.where(k_pos[None, :] <= q_pos[:, None], s, -jnp.inf)
        p = jax.nn.softmax(s, axis=-1).astype(v.dtype)
        return jnp.einsum("bhqk,bkhd->bqhd", p, v)

    o = lax.map(block, (q_blocks, F_blocks, jnp.arange(nb)))
    return jnp.moveaxis(o, 0, 1).reshape(B, S, H, Dh)


def moe_ffn(h, w_router, b_router, w_gate_up, b_gate_up, w_down, b_down):
    B, S, D = h.shape
    T = B * S
    A = T * MOE_TOPK
    xt = h.reshape(T, D)
    logits = (xt @ w_router + b_router).astype(jnp.float32)
    top_val, top_idx = lax.top_k(logits, MOE_TOPK)
    gates = jax.nn.softmax(top_val, axis=-1)
    e_flat = top_idx.reshape(A).astype(jnp.int32)
    tok_flat = jnp.arange(A, dtype=jnp.int32) // MOE_TOPK
    g_flat = gates.reshape(A)
    order = jnp.argsort(e_flat)
    e_sorted = e_flat[order]
    counts = jnp.bincount(e_flat, length=MOE_EXPERTS).astype(jnp.int32)
    padded = (counts + MOE_BLOCK - 1) // MOE_BLOCK * MOE_BLOCK
    pad_end = jnp.cumsum(padded)
    pad_start = pad_end - padded
    grp_start = jnp.cumsum(counts) - counts
    dest = pad_start[e_sorted] + (jnp.arange(A, dtype=jnp.int32) - grp_start[e_sorted])
    n_blocks = -(-A // MOE_BLOCK) + MOE_EXPERTS
    P = n_blocks * MOE_BLOCK
    tok_buf = jnp.full((P,), T, jnp.int32).at[dest].set(tok_flat[order])
    gate_buf = jnp.zeros((P,), jnp.float32).at[dest].set(g_flat[order])
    blk_start = jnp.arange(n_blocks, dtype=jnp.int32) * MOE_BLOCK
    blk_e = jnp.minimum(jnp.searchsorted(pad_end, blk_start, side="right"), MOE_EXPERTS - 1)
    x_pad = jnp.concatenate([xt, jnp.zeros((1, D), xt.dtype)], axis=0)
    x_buf = x_pad[tok_buf].reshape(n_blocks, MOE_BLOCK, D)

    def expert_block(args):
        xb, e = args
        gu = xb @ w_gate_up[e] + b_gate_up[e]
        gate, up = gu[:, :MOE_FF], gu[:, MOE_FF:]
        gate = jnp.minimum(gate, SWIGLU_LIMIT)
        up = jnp.clip(up, -SWIGLU_LIMIT, SWIGLU_LIMIT)
        act = (up + 1.0) * (gate * jax.nn.sigmoid(SWIGLU_ALPHA * gate))
        return act @ w_down[e] + b_down[e]

    y_buf = lax.map(expert_block, (x_buf, blk_e)).reshape(P, D)
    y = jnp.zeros((T + 1, D), jnp.float32).at[tok_buf].add(y_buf.astype(jnp.float32) * gate_buf[:, None])
    return y[:T].reshape(B, S, D).astype(h.dtype)


def setup_inputs(seed: int = 0) -> dict:
    key = jax.random.key(seed)
    ks = jax.random.split(key, 24)
    f32 = jnp.float32
    nrm = jax.random.normal
    x = nrm(ks[0], (BATCH, SEQ, D_MODEL), f32)
    c = nrm(ks[1], (BATCH, D_MODEL), f32)
    w_ada = nrm(ks[2], (DEPTH, D_MODEL, N_MOD * D_MODEL), f32) * (0.5 * D_MODEL ** -0.5)
    b_ada = 0.02 * nrm(ks[3], (DEPTH, N_MOD * D_MODEL), f32)
    g_norm1 = 1.0 + 0.02 * nrm(ks[4], (DEPTH, D_MODEL), f32)
    w_in = nrm(ks[5], (DEPTH, D_MODEL, N_IN), f32) * D_MODEL ** -0.5
    conv_w = nrm(ks[6], (DEPTH, DN_CONV, DN_QKV), f32) * DN_CONV ** -0.5
    a_log = jnp.log(jax.random.uniform(ks[7], (DEPTH, DN_HEADS), f32, 1.0, 16.0))
    dt = jnp.exp(jax.random.uniform(ks[8], (DEPTH, DN_HEADS), f32, math.log(1e-3), math.log(1e-1)))
    dt_bias = jnp.log(jnp.expm1(dt))
    g_onorm = 1.0 + 0.02 * nrm(ks[9], (DEPTH, DN_DV), f32)
    b_fgate = jax.random.uniform(ks[10], (DEPTH, FOX_HEADS), f32, 1.0, 4.0)
    w_o_delta = nrm(ks[11], (DEPTH, DN_W, D_MODEL), f32) * DN_W ** -0.5
    w_o_fox = nrm(ks[12], (DEPTH, FOX_W, D_MODEL), f32) * FOX_W ** -0.5
    w_out = nrm(ks[13], (DEPTH, D_MODEL, D_MODEL), f32) * D_MODEL ** -0.5
    g_norm2 = 1.0 + 0.02 * nrm(ks[14], (DEPTH, D_MODEL), f32)
    w_router = nrm(ks[15], (DEPTH, D_MODEL, MOE_EXPERTS), f32) * D_MODEL ** -0.5
    b_router = 0.01 * nrm(ks[16], (DEPTH, MOE_EXPERTS), f32)
    w_gate_up = nrm(ks[17], (DEPTH, MOE_EXPERTS, D_MODEL, 2 * MOE_FF), f32) * D_MODEL ** -0.5
    b_gate_up = 0.01 * nrm(ks[18], (DEPTH, MOE_EXPERTS, 2 * MOE_FF), f32)
    w_down = nrm(ks[19], (DEPTH, MOE_EXPERTS, MOE_FF, D_MODEL), f32) * MOE_FF ** -0.5
    b_down = 0.01 * nrm(ks[20], (DEPTH, MOE_EXPERTS, D_MODEL), f32)
    g_final = 1.0 + 0.02 * nrm(ks[21], (D_MODEL,), f32)
    return {"x": x, "c": c, "w_ada": w_ada, "b_ada": b_ada, "g_norm1": g_norm1, "w_in": w_in,
            "conv_w": conv_w, "a_log": a_log, "dt_bias": dt_bias, "g_onorm": g_onorm, "b_fgate": b_fgate,
            "w_o_delta": w_o_delta, "w_o_fox": w_o_fox, "w_out": w_out, "g_norm2": g_norm2,
            "w_router": w_router, "b_router": b_router, "w_gate_up": w_gate_up, "b_gate_up": b_gate_up,
            "w_down": w_down, "b_down": b_down, "g_final": g_final}


def reference(x, c, w_ada, b_ada, g_norm1, w_in, conv_w, a_log, dt_bias, g_onorm, b_fgate,
              w_o_delta, w_o_fox, w_out, g_norm2, w_router, b_router, w_gate_up, b_gate_up,
              w_down, b_down, g_final):
    B, S, D = x.shape
    split_points = np.cumsum(IN_SIZES)[:-1].tolist()
    for l in range(DEPTH):
        mod = (c @ w_ada[l] + b_ada[l]).reshape(B, N_MOD, D)
        sh1, sc1, gt1, sh2, sc2, gt2 = [mod[:, i, None, :] for i in range(N_MOD)]

        h = rms_norm(x, g_norm1[l]) * (1.0 + sc1) + sh1
        proj = h @ w_in[l]
        dn_qkv, dn_z, dn_b, dn_a, fx_qkv, fx_f, gate_a, gate_b = jnp.split(proj, split_points, axis=-1)

        qkv = jax.nn.silu(causal_depthwise_conv(dn_qkv, conv_w[l]))
        q_a, k_a, v_a = jnp.split(qkv, [DN_HEADS * DN_DK, 2 * DN_HEADS * DN_DK], axis=-1)
        q_a = q_a.reshape(B, S, DN_HEADS, DN_DK)
        k_a = k_a.reshape(B, S, DN_HEADS, DN_DK)
        v_a = v_a.reshape(B, S, DN_HEADS, DN_DV)
        beta = jax.nn.sigmoid(dn_b.astype(jnp.float32))
        g = -jnp.exp(a_log[l]) * jax.nn.softplus(dn_a.astype(jnp.float32) + dt_bias[l])
        o_a = gated_delta_rule(q_a, k_a, v_a, g, beta).astype(x.dtype)
        o_a = rms_norm(o_a, g_onorm[l]) * jax.nn.silu(dn_z.reshape(B, S, DN_HEADS, DN_DV))
        y_a = o_a.reshape(B, S, DN_W) @ w_o_delta[l]

        q_b, k_b, v_b = jnp.split(fx_qkv, [FOX_W, 2 * FOX_W], axis=-1)
        log_f = jax.nn.log_sigmoid(fx_f.astype(jnp.float32) + b_fgate[l])
        o_b = forgetting_attention(q_b.reshape(B, S, FOX_HEADS, FOX_HD), k_b.reshape(B, S, FOX_HEADS, FOX_HD),
                                   v_b.reshape(B, S, FOX_HEADS, FOX_HD), log_f)
        y_b = o_b.reshape(B, S, FOX_W) @ w_o_fox[l]

        merged = jax.nn.sigmoid(gate_a) * y_a + jax.nn.sigmoid(gate_b) * y_b
        x = x + gt1 * (merged @ w_out[l])

        h2 = rms_norm(x, g_norm2[l]) * (1.0 + sc2) + sh2
        x = x + gt2 * moe_ffn(h2, w_router[l], b_router[l], w_gate_up[l], b_gate_up[l], w_down[l], b_down[l])
    return rms_norm(x, g_final)
```

```python
import functools

import jax
import jax.numpy as jnp
from jax import lax
from jax.experimental import pallas as pl
from jax.experimental.pallas import tpu as pltpu

F32 = jnp.float32
BF16 = jnp.bfloat16
I32 = jnp.int32
HIGHEST = lax.Precision.HIGHEST

D_MODEL = 1024
HEAD_DIM = 128
N_HEADS = 4
MIX_W = N_HEADS * HEAD_DIM
QKV_W = 3 * MIX_W
CONV_K = 4
CHUNK = 64
STACK = N_HEADS * CHUNK
N_EXPERTS = 32
TOP_K = 4
FF = D_MODEL
SWIGLU_LIMIT = 7.0
SWIGLU_ALPHA = 1.702
RMS_EPS = 1e-6
L2_EPS = 1e-6
N_MOD = 6
LANES = 128
NEG_BIG = -1e30

EXPERT_BLOCK = 256
ROUTE_TILE = 256
VMEM_LIMIT = 56 * 1024 * 1024


def _params(sem, vmem=VMEM_LIMIT):
    return pltpu.CompilerParams(dimension_semantics=sem, vmem_limit_bytes=vmem)


def _softplus(x):
    return jnp.maximum(x, 0.0) + jnp.log(1.0 + jnp.exp(-jnp.abs(x)))


def _sigmoid(x):
    return 1.0 / (1.0 + jnp.exp(-x))


def _silu(x):
    return x * _sigmoid(x)


def _dot(a, b, precision=None):
    return jnp.dot(a, b, preferred_element_type=F32, precision=precision)


def _dot_nt(a, b, precision=None):
    return lax.dot_general(a, b, (((1,), (1,)), ((), ())), preferred_element_type=F32, precision=precision)


def _mod_kernel(c_ref, w_ref, b_ref, o_ref):
    o_ref[...] = jnp.sum(c_ref[...] * w_ref[...], axis=0, keepdims=True) + b_ref[...]


def _modulation(c, w_ada, b_ada):
    d, n = w_ada.shape
    tn = 1024
    return pl.pallas_call(
        _mod_kernel,
        grid=(n // tn,),
        in_specs=[pl.BlockSpec((d, 1), lambda j: (0, 0)),
                  pl.BlockSpec((d, tn), lambda j: (0, j)),
                  pl.BlockSpec((1, tn), lambda j: (0, j))],
        out_specs=pl.BlockSpec((1, tn), lambda j: (0, j)),
        out_shape=jax.ShapeDtypeStruct((1, n), F32),
        compiler_params=_params(("arbitrary",)),
        name="mod",
    )(c.reshape(d, 1), w_ada, b_ada.reshape(1, n))


def _inproj_kernel(x_ref, g_ref, sc_ref, sh_ref, wq_ref, wz_ref, wf_ref, wga_ref, wgb_ref, ws_ref,
                   oq_ref, oz_ref, of_ref, oga_ref, ogb_ref, os_ref):
    x = x_ref[...]
    y = x * lax.rsqrt(jnp.mean(x * x, axis=-1, keepdims=True) + RMS_EPS) * g_ref[...]
    h = (y * (1.0 + sc_ref[...]) + sh_ref[...]).astype(BF16)
    oq_ref[...] = _dot(h, wq_ref[...])
    oz_ref[...] = _dot(h, wz_ref[...]).astype(BF16)
    of_ref[...] = _dot(h, wf_ref[...]).astype(BF16)
    oga_ref[...] = _dot(h, wga_ref[...]).astype(BF16)
    ogb_ref[...] = _dot(h, wgb_ref[...]).astype(BF16)
    os_ref[...] = _dot(h, ws_ref[...])


def _inproj(x, g1, sc1, sh1, wq, wz, wf, wga, wgb, ws, tm):
    t, d = x.shape
    row = lambda i: (i, 0)
    fixed = lambda i: (0, 0)
    ws_list = [wq, wz, wf, wga, wgb, ws]
    out_dt = [F32, BF16, BF16, BF16, BF16, F32]
    return pl.pallas_call(
        _inproj_kernel,
        grid=(t // tm,),
        in_specs=[pl.BlockSpec((tm, d), row)] + [pl.BlockSpec((1, d), fixed)] * 3
                 + [pl.BlockSpec(w.shape, fixed) for w in ws_list],
        out_specs=[pl.BlockSpec((tm, w.shape[1]), row) for w in ws_list],
        out_shape=[jax.ShapeDtypeStruct((t, w.shape[1]), dt) for w, dt in zip(ws_list, out_dt)],
        compiler_params=_params(("arbitrary",)),
        name="inproj",
    )(x, g1, sc1, sh1, *ws_list)


L_BETA, L_G, L_F, L_GC, L_GL = 0, 4, 8, 12, 16


def _gates_kernel(s_ref, alog_ref, dtb_ref, bf_ref, col_ref, row_ref, carry_ref):
    i = pl.program_id(0)
    tm = s_ref.shape[0]

    @pl.when(i == 0)
    def _():
        carry_ref[...] = jnp.zeros_like(carry_ref)

    s = s_ref[...]
    lane = lax.broadcasted_iota(I32, s.shape, 1)
    beta = _sigmoid(s)
    g = -jnp.exp(alog_ref[...]) * _softplus(s + dtb_ref[...])
    logf = -_softplus(-(s + bf_ref[...]))
    is_g = (lane >= L_G) & (lane < L_G + N_HEADS)
    is_f = (lane >= L_F) & (lane < L_F + N_HEADS)
    g = jnp.where(is_g, g, 0.0)
    logf = jnp.where(is_f, logf, 0.0)

    r = lax.broadcasted_iota(I32, (tm, tm), 0)
    c = lax.broadcasted_iota(I32, (tm, tm), 1)
    same_chunk = (r // CHUNK) == (c // CHUNK)
    tri = (r >= c).astype(F32)
    tri_chunk = jnp.where(same_chunk & (r >= c), 1.0, 0.0)
    ones_chunk = jnp.where(same_chunk, 1.0, 0.0)
    f_cum = _dot(tri, logf, HIGHEST) + carry_ref[...]
    carry_ref[...] = f_cum[tm - 1:tm, :]
    gc = _dot(tri_chunk, g, HIGHEST)
    gl = _dot(ones_chunk, g, HIGHEST)

    out = jnp.where(lane < N_HEADS, beta, 0.0) + g + f_cum
    out = out + pltpu.roll(gc, L_GC - L_G, 1) + pltpu.roll(gl, L_GL - L_G, 1)
    col_ref[...] = out
    row_ref[...] = out.T[:row_ref.shape[0], :]


def _gates(small, alog_row, dtb_row, bf_row, tm):
    t = small.shape[0]
    n_rows = 24
    return pl.pallas_call(
        _gates_kernel,
        grid=(t // tm,),
        in_specs=[pl.BlockSpec((tm, LANES), lambda i: (i, 0))] + [pl.BlockSpec((1, LANES), lambda i: (0, 0))] * 3,
        out_specs=[pl.BlockSpec((tm, LANES), lambda i: (i, 0)), pl.BlockSpec((n_rows, tm), lambda i: (0, i))],
        out_shape=[jax.ShapeDtypeStruct((t, LANES), F32), jax.ShapeDtypeStruct((n_rows, t), F32)],
        scratch_shapes=[pltpu.VMEM((1, LANES), F32)],
        compiler_params=_params(("arbitrary",)),
        name="gates",
    )(small, alog_row, dtb_row, bf_row)


def _stack_heads(a, col0):
    return jnp.concatenate([a[:, col0 + h * HEAD_DIM: col0 + (h + 1) * HEAD_DIM] for h in range(N_HEADS)], axis=0)


def _stack_cols(a, lane0):
    return jnp.concatenate([a[:, lane0 + h: lane0 + h + 1] for h in range(N_HEADS)], axis=0)


def _delta_kernel(qkv_ref, z_ref, gcol_ref, grow_ref, cw_ref, gon_ref, o_ref, ext_ref, tail_ref, s_ref):
    i = pl.program_id(0)
    tb = qkv_ref.shape[0]
    pad = tail_ref.shape[0]

    @pl.when(i == 0)
    def _():
        tail_ref[...] = jnp.zeros_like(tail_ref)
        s_ref[...] = jnp.zeros_like(s_ref)

    ext_ref[0:pad, :] = tail_ref[...]
    ext_ref[pad:pad + tb, :] = qkv_ref[...]
    tail_ref[...] = qkv_ref[tb - pad:tb, :]
    conv = cw_ref[0:1, :] * ext_ref[pad - 3:pad - 3 + tb, :]
    for j in range(1, CONV_K):
        conv = conv + cw_ref[j:j + 1, :] * ext_ref[pad - 3 + j:pad - 3 + j + tb, :]
    act = _silu(conv)

    r = lax.broadcasted_iota(I32, (STACK, STACK), 0)
    c = lax.broadcasted_iota(I32, (STACK, STACK), 1)
    same = (r // CHUNK) == (c // CHUNK)
    m_incl = same & (r >= c)
    m_strict = same & (r > c)
    eye = jnp.where(r == c, 1.0, 0.0)
    rb = lax.broadcasted_iota(I32, (STACK, N_HEADS * HEAD_DIM), 0) // CHUNK
    cb = lax.broadcasted_iota(I32, (STACK, N_HEADS * HEAD_DIM), 1) // HEAD_DIM
    head_match = rb == cb

    for ch in range(tb // CHUNK):
        r0 = ch * CHUNK
        a = act[r0:r0 + CHUNK, :]
        q = _stack_heads(a, 0)
        k = _stack_heads(a, MIX_W)
        v = _stack_heads(a, 2 * MIX_W)
        q = q * lax.rsqrt(jnp.sum(q * q, axis=-1, keepdims=True) + L2_EPS) * (HEAD_DIM ** -0.5)
        k = k * lax.rsqrt(jnp.sum(k * k, axis=-1, keepdims=True) + L2_EPS)

        gcols = gcol_ref[r0:r0 + CHUNK, :]
        beta = _stack_cols(gcols, L_BETA)
        gc = _stack_cols(gcols, L_GC)
        gc_row = jnp.concatenate(
            [grow_ref[L_GC + h:L_GC + h + 1, r0:r0 + CHUNK] for h in range(N_HEADS)], axis=1)
        gl_row = jnp.concatenate(
            [grow_ref[L_GL + h:L_GL + h + 1, r0:r0 + CHUNK] for h in range(N_HEADS)], axis=1)

        decay = jnp.exp(jnp.where(m_incl, gc - gc_row, NEG_BIG))
        kb16 = k.astype(BF16)
        kk = _dot_nt(kb16, kb16)
        lmat = jnp.where(m_strict, kk * beta * decay, 0.0)
        attn = _dot_nt(q.astype(BF16), kb16) * decay

        inv = eye - lmat
        pw = _dot(lmat, lmat, HIGHEST)
        n_sq = CHUNK.bit_length() - 3
        for s in range(n_sq + 1):
            inv = inv + _dot(inv, pw, HIGHEST)
            if s < n_sq:
                pw = _dot(pw, pw, HIGHEST)

        egc = jnp.exp(gc)
        rhs = jnp.concatenate([v * beta, k * (beta * egc)], axis=1).astype(BF16)
        uw = _dot(inv.astype(BF16), rhs)
        u, w = uw[:, :HEAD_DIM], uw[:, HEAD_DIM:]

        state16 = s_ref[...].astype(BF16)
        wq_s = _dot(jnp.concatenate([w, q * egc], axis=0).astype(BF16), state16)
        ws_d = jnp.concatenate([wq_s[h * CHUNK:(h + 1) * CHUNK, h * HEAD_DIM:(h + 1) * HEAD_DIM]
                                for h in range(N_HEADS)], axis=0)
        qs_d = jnp.concatenate([wq_s[STACK + h * CHUNK:STACK + (h + 1) * CHUNK, h * HEAD_DIM:(h + 1) * HEAD_DIM]
                                for h in range(N_HEADS)], axis=0)
        v_new = u - ws_d
        v16 = v_new.astype(BF16)
        o = qs_d + _dot(attn.astype(BF16), v16)

        kt_dec = (k.T * jnp.exp(gl_row - gc_row)).astype(BF16)
        v_bd = jnp.where(head_match, jnp.concatenate([v_new] * N_HEADS, axis=1), 0.0).astype(BF16)
        upd = _dot(kt_dec, v_bd)
        for h in range(N_HEADS):
            e_h = jnp.exp(grow_ref[L_GL + h:L_GL + h + 1, r0:r0 + 1])
            sl = slice(h * HEAD_DIM, (h + 1) * HEAD_DIM)
            s_ref[:, sl] = s_ref[:, sl] * e_h + upd[:, sl]

        for h in range(N_HEADS):
            oh = o[h * CHUNK:(h + 1) * CHUNK, :]
            oh = oh * lax.rsqrt(jnp.mean(oh * oh, axis=-1, keepdims=True) + RMS_EPS) * gon_ref[...]
            zh = z_ref[r0:r0 + CHUNK, h * HEAD_DIM:(h + 1) * HEAD_DIM].astype(F32)
            o_ref[r0:r0 + CHUNK, h * HEAD_DIM:(h + 1) * HEAD_DIM] = (oh * _silu(zh)).astype(o_ref.dtype)


def _delta(qkv, z, gcol, grow, conv_w, g_onorm, tb):
    t = qkv.shape[0]
    pad = 8
    return pl.pallas_call(
        _delta_kernel,
        grid=(t // tb,),
        in_specs=[pl.BlockSpec((tb, QKV_W), lambda i: (i, 0)),
                  pl.BlockSpec((tb, MIX_W), lambda i: (i, 0)),
                  pl.BlockSpec((tb, LANES), lambda i: (i, 0)),
                  pl.BlockSpec((grow.shape[0], tb), lambda i: (0, i)),
                  pl.BlockSpec((CONV_K, QKV_W), lambda i: (0, 0)),
                  pl.BlockSpec((1, HEAD_DIM), lambda i: (0, 0))],
        out_specs=pl.BlockSpec((tb, MIX_W), lambda i: (i, 0)),
        out_shape=jax.ShapeDtypeStruct((t, MIX_W), BF16),
        scratch_shapes=[pltpu.VMEM((tb + pad, QKV_W), F32),
                        pltpu.VMEM((pad, QKV_W), F32),
                        pltpu.VMEM((HEAD_DIM, N_HEADS * HEAD_DIM), F32)],
        compiler_params=_params(("arbitrary",)),
        name="delta",
    )(qkv, z, gcol, grow, conv_w, g_onorm)


def _fox_kernel(q_ref, k_ref, v_ref, f_ref, o_ref, *, tk):
    qi = pl.program_id(1)
    tq = q_ref.shape[0]
    q = q_ref[...]
    scale = HEAD_DIM ** -0.5

    def step(k_blk, v_blk, f_blk, carry, mask):
        m, l, acc = carry
        s = _dot_nt(q, k_blk) * scale - f_blk
        if mask is not None:
            s = jnp.where(mask, s, NEG_BIG)
        m_new = jnp.maximum(m, jnp.max(s, axis=-1, keepdims=True))
        alpha = jnp.exp(m - m_new)
        p = jnp.exp(s - m_new)
        l = alpha * l + jnp.sum(p, axis=-1, keepdims=True)
        acc = alpha * acc + _dot(p.astype(BF16), v_blk)
        return m_new, l, acc

    def body(j, carry):
        off = pl.multiple_of(j * tk, tk)
        return step(k_ref[pl.ds(off, tk), :], v_ref[pl.ds(off, tk), :], f_ref[j], carry, None)

    init = (jnp.full((tq, 1), NEG_BIG, F32), jnp.zeros((tq, 1), F32), jnp.zeros((tq, HEAD_DIM), F32))
    n_full = qi * (tq // tk)
    carry = lax.fori_loop(0, n_full, body, init)
    row = lax.broadcasted_iota(I32, (tq, tk), 0)
    col = lax.broadcasted_iota(I32, (tq, tk), 1)
    for d in range(tq // tk):
        off = pl.multiple_of(qi * tq + d * tk, tk)
        carry = step(k_ref[pl.ds(off, tk), :], v_ref[pl.ds(off, tk), :], f_ref[n_full + d], carry,
                     col + d * tk <= row)
    _, l, acc = carry
    o_ref[...] = (acc / l).astype(o_ref.dtype)


def _fox(fx, f_rows, tq, tk):
    t = fx.shape[0]
    return pl.pallas_call(
        functools.partial(_fox_kernel, tk=tk),
        grid=(N_HEADS, t // tq),
        in_specs=[pl.BlockSpec((tq, HEAD_DIM), lambda h, i: (i, h)),
                  pl.BlockSpec((t, HEAD_DIM), lambda h, i: (0, N_HEADS + h)),
                  pl.BlockSpec((t, HEAD_DIM), lambda h, i: (0, 2 * N_HEADS + h)),
                  pl.BlockSpec((None, t // tk, 1, tk), lambda h, i: (h, 0, 0, 0))],
        out_specs=pl.BlockSpec((tq, HEAD_DIM), lambda h, i: (i, h)),
        out_shape=jax.ShapeDtypeStruct((t, MIX_W), BF16),
        compiler_params=_params(("arbitrary", "arbitrary")),
        name="fox",
    )(fx, fx, fx, f_rows)


def _merge_kernel(oa_ref, ob_ref, ga_ref, gb_ref, x_ref, woa_ref, wob_ref, wout_ref, gt1_ref, g2_ref, sc2_ref,
                  sh2_ref, wr_ref, br_ref, x1_ref, h2_ref, ri_ref, gate_ref, cnt_ref, carry_ref):
    i = pl.program_id(0)
    tm = x_ref.shape[0]

    @pl.when(i == 0)
    def _():
        carry_ref[...] = jnp.zeros_like(carry_ref)

    ya = _dot(oa_ref[...], woa_ref[...])
    yb = _dot(ob_ref[...], wob_ref[...])
    merged = _sigmoid(ga_ref[...].astype(F32)) * ya + _sigmoid(gb_ref[...].astype(F32)) * yb
    x1 = x_ref[...] + gt1_ref[...] * _dot(merged.astype(BF16), wout_ref[...])
    x1_ref[...] = x1
    y = x1 * lax.rsqrt(jnp.mean(x1 * x1, axis=-1, keepdims=True) + RMS_EPS) * g2_ref[...]
    h2 = y * (1.0 + sc2_ref[...]) + sh2_ref[...]
    h2_ref[...] = h2

    logits = _dot(h2, wr_ref[...], HIGHEST) + br_ref[...]
    lane = lax.broadcasted_iota(I32, logits.shape, 1)
    lanef = lane.astype(F32)
    cur = jnp.where(lane < N_EXPERTS, logits, -jnp.inf)
    vals, idxs = [], []
    for _ in range(TOP_K):
        m = jnp.max(cur, axis=-1, keepdims=True)
        ix = jnp.min(jnp.where(cur == m, lanef, float(LANES)), axis=-1, keepdims=True)
        vals.append(m)
        idxs.append(ix)
        cur = jnp.where(lanef == ix, -jnp.inf, cur)
    exps = [jnp.exp(v - vals[0]) for v in vals]
    denom = exps[0] + exps[1] + exps[2] + exps[3]

    onehot = jnp.zeros(logits.shape, F32)
    for ix in idxs:
        onehot = onehot + jnp.where(lanef == ix, 1.0, 0.0)
    r = lax.broadcasted_iota(I32, (tm, tm), 0)
    c = lax.broadcasted_iota(I32, (tm, tm), 1)
    strict = jnp.where(r > c, 1.0, 0.0).astype(BF16)
    before = _dot(strict, onehot.astype(BF16)) + carry_ref[...]
    carry_ref[...] = carry_ref[...] + jnp.sum(onehot, axis=0, keepdims=True)
    cnt_ref[...] = carry_ref[...].astype(I32)

    ri = jnp.zeros(logits.shape, F32)
    gates = jnp.zeros(logits.shape, F32)
    for kk in range(TOP_K):
        rank = jnp.sum(jnp.where(lanef == idxs[kk], before, 0.0), axis=-1, keepdims=True)
        ri = ri + jnp.where(lane == kk, idxs[kk], 0.0) + jnp.where(lane == TOP_K + kk, rank, 0.0)
        gates = gates + jnp.where(lane == kk, exps[kk] / denom, 0.0)
    ri_ref[...] = ri.astype(I32)
    gate_ref[...] = gates


def _merge(o_a, o_b, gate_a, gate_b, x, woa, wob, wout, gt1, g2, sc2, sh2, wr, br, tm):
    t, d = x.shape
    row = lambda i: (i, 0)
    fixed = lambda i: (0, 0)
    return pl.pallas_call(
        _merge_kernel,
        grid=(t // tm,),
        in_specs=[pl.BlockSpec((tm, MIX_W), row), pl.BlockSpec((tm, MIX_W), row),
                  pl.BlockSpec((tm, d), row), pl.BlockSpec((tm, d), row), pl.BlockSpec((tm, d), row),
                  pl.BlockSpec((MIX_W, d), fixed), pl.BlockSpec((MIX_W, d), fixed), pl.BlockSpec((d, d), fixed),
                  pl.BlockSpec((1, d), fixed), pl.BlockSpec((1, d), fixed), pl.BlockSpec((1, d), fixed),
                  pl.BlockSpec((1, d), fixed), pl.BlockSpec((d, LANES), fixed), pl.BlockSpec((1, LANES), fixed)],
        out_specs=[pl.BlockSpec((tm, d), row), pl.BlockSpec((tm, d), row), pl.BlockSpec((tm, LANES), row),
                   pl.BlockSpec((tm, LANES), row), pl.BlockSpec((1, LANES), fixed)],
        out_shape=[jax.ShapeDtypeStruct((t, d), F32), jax.ShapeDtypeStruct((t, d), F32),
                   jax.ShapeDtypeStruct((t, LANES), I32), jax.ShapeDtypeStruct((t, LANES), F32),
                   jax.ShapeDtypeStruct((1, LANES), I32)],
        scratch_shapes=[pltpu.VMEM((1, LANES), F32)],
        compiler_params=_params(("arbitrary",)),
        name="merge",
    )(o_a, o_b, gate_a, gate_b, x, woa, wob, wout, gt1, g2, sc2, sh2, wr, br)


def _dest_kernel(ri_ref, cnt_ref, dest_ref, blk_ref, pend_ref):
    shift = EXPERT_BLOCK.bit_length() - 1
    cnt = jnp.broadcast_to(cnt_ref[...], (8, LANES))
    lane_row = lax.broadcasted_iota(I32, cnt.shape, 1)
    padded = jnp.where(lane_row < N_EXPERTS, ((cnt + (EXPERT_BLOCK - 1)) >> shift) << shift, 0)
    pend = padded
    s = 1
    while s < N_EXPERTS:
        pend = pend + jnp.where(lane_row >= s, pltpu.roll(pend, s, 1), 0)
        s *= 2
    pstart = (pend - padded)[0:1, :].astype(F32)
    pend = pend[0:1, :]
    pend_ref[...] = pend

    ri = ri_ref[...]
    lane = lax.broadcasted_iota(I32, ri.shape, 1)
    dest = jnp.zeros(ri.shape, F32)
    for kk in range(TOP_K):
        ix = ri[:, kk:kk + 1]
        rank = ri[:, TOP_K + kk:TOP_K + kk + 1].astype(F32)
        start = jnp.sum(jnp.where(lane == ix, pstart, 0.0), axis=-1, keepdims=True)
        dest = dest + jnp.where(lane == kk, start + rank, 0.0)
    dest_ref[...] = dest.astype(I32)

    bstart = lax.broadcasted_iota(I32, blk_ref.shape, 0) * EXPERT_BLOCK
    lane_b = lax.broadcasted_iota(I32, blk_ref.shape, 1)
    ended = jnp.where((lane_b < N_EXPERTS) & (pend <= bstart), 1.0, 0.0)
    e = jnp.minimum(jnp.sum(ended, axis=-1, keepdims=True), float(N_EXPERTS - 1))
    blk_ref[...] = jnp.broadcast_to(e, blk_ref.shape).astype(I32)


def _dest(ri, cnt, n_blocks, tm):
    t = ri.shape[0]
    nb_pad = -(-n_blocks // 8) * 8
    return pl.pallas_call(
        _dest_kernel,
        grid=(t // tm,),
        in_specs=[pl.BlockSpec((tm, LANES), lambda i: (i, 0)), pl.BlockSpec((1, LANES), lambda i: (0, 0))],
        out_specs=[pl.BlockSpec((tm, LANES), lambda i: (i, 0)), pl.BlockSpec((nb_pad, LANES), lambda i: (0, 0)),
                   pl.BlockSpec((1, LANES), lambda i: (0, 0))],
        out_shape=[jax.ShapeDtypeStruct((t, LANES), I32), jax.ShapeDtypeStruct((nb_pad, LANES), I32),
                   jax.ShapeDtypeStruct((1, LANES), I32)],
        compiler_params=_params(("arbitrary",)),
        name="dest",
    )(ri, cnt)


def _row_copy(src_ref, src_row, dst_ref, dst_row, sem):
    return pltpu.make_async_copy(src_ref.at[pl.ds(src_row, 1), :], dst_ref.at[pl.ds(dst_row, 1), :], sem)


def _dispatch_kernel(dest_ref, h_ref, init_ref, xbuf_ref, sem):
    del init_ref
    tm = h_ref.shape[0]

    def issue(r, _):
        for kk in range(TOP_K):
            _row_copy(h_ref, r, xbuf_ref, dest_ref[r * TOP_K + kk], sem).start()
        return 0

    lax.fori_loop(0, tm, issue, 0)

    def drain(r, _):
        for kk in range(TOP_K):
            _row_copy(h_ref, r, xbuf_ref, dest_ref[r * TOP_K + kk], sem).wait()
        return 0

    lax.fori_loop(0, tm, drain, 0)


def _dispatch(dest_flat, h2, n_rows):
    t, d = h2.shape
    tm = ROUTE_TILE
    init = jnp.zeros((n_rows, d), h2.dtype)
    return pl.pallas_call(
        _dispatch_kernel,
        grid=(t // tm,),
        in_specs=[pl.BlockSpec((tm * TOP_K,), lambda i: (i,), memory_space=pltpu.SMEM),
                  pl.BlockSpec((tm, d), lambda i: (i, 0)),
                  pl.BlockSpec(memory_space=pl.ANY)],
        out_specs=pl.BlockSpec(memory_space=pl.ANY),
        out_shape=jax.ShapeDtypeStruct((n_rows, d), h2.dtype),
        scratch_shapes=[pltpu.SemaphoreType.DMA(())],
        input_output_aliases={2: 0},
        compiler_params=_params(("arbitrary",)),
        name="dispatch",
    )(dest_flat, h2, init)


def _experts_kernel(blk_e_ref, nact_ref, x_ref, wgu_ref, bgu_ref, wd_ref, bd_ref, y_ref):
    del blk_e_ref
    b = pl.program_id(0)

    @pl.when(b < nact_ref[0])
    def _():
        gu = _dot(x_ref[...].astype(BF16), wgu_ref[...]) + bgu_ref[...]
        gate = jnp.minimum(gu[:, :FF], SWIGLU_LIMIT)
        up = jnp.clip(gu[:, FF:], -SWIGLU_LIMIT, SWIGLU_LIMIT)
        act = (up + 1.0) * (gate * _sigmoid(SWIGLU_ALPHA * gate))
        y_ref[...] = _dot(act.astype(BF16), wd_ref[...]) + bd_ref[...]

    @pl.when(b >= nact_ref[0])
    def _():
        y_ref[...] = jnp.zeros_like(y_ref)


def _experts(blk_e, nact, xbuf, wgu, bgu, wd, bd):
    n_rows, d = xbuf.shape
    nb = n_rows // EXPERT_BLOCK
    grid_spec = pltpu.PrefetchScalarGridSpec(
        num_scalar_prefetch=2,
        grid=(nb,),
        in_specs=[pl.BlockSpec((EXPERT_BLOCK, d), lambda b, e, n: (b, 0)),
                  pl.BlockSpec((None, d, 2 * FF), lambda b, e, n: (e[b], 0, 0)),
                  pl.BlockSpec((None, 1, 2 * FF), lambda b, e, n: (e[b], 0, 0)),
                  pl.BlockSpec((None, FF, d), lambda b, e, n: (e[b], 0, 0)),
                  pl.BlockSpec((None, 1, d), lambda b, e, n: (e[b], 0, 0))],
        out_specs=pl.BlockSpec((EXPERT_BLOCK, d), lambda b, e, n: (b, 0)),
    )
    return pl.pallas_call(
        _experts_kernel,
        grid_spec=grid_spec,
        out_shape=jax.ShapeDtypeStruct((n_rows, d), F32),
        compiler_params=_params(("arbitrary",)),
        name="experts",
    )(blk_e, nact, xbuf, wgu, bgu, wd, bd)


def _combine_kernel(dest_ref, ybuf_ref, x1_ref, gate_ref, gt2_ref, gf_ref, o_ref, rows_ref, sem):
    tm = x1_ref.shape[0]

    def issue(r, _):
        for kk in range(TOP_K):
            _row_copy(ybuf_ref, dest_ref[r * TOP_K + kk], rows_ref.at[kk], r, sem).start()
        return 0

    lax.fori_loop(0, tm, issue, 0)

    def drain(r, _):
        for kk in range(TOP_K):
            _row_copy(ybuf_ref, dest_ref[r * TOP_K + kk], rows_ref.at[kk], r, sem).wait()
        return 0

    lax.fori_loop(0, tm, drain, 0)

    gates = gate_ref[...]
    moe = gates[:, 0:1] * rows_ref[0]
    for kk in range(1, TOP_K):
        moe = moe + gates[:, kk:kk + 1] * rows_ref[kk]
    xo = x1_ref[...] + gt2_ref[...] * moe
    o_ref[...] = xo * lax.rsqrt(jnp.mean(xo * xo, axis=-1, keepdims=True) + RMS_EPS) * gf_ref[...]


def _combine(dest_flat, ybuf, x1, gates, gt2, g_final):
    t, d = x1.shape
    tm = ROUTE_TILE
    return pl.pallas_call(
        _combine_kernel,
        grid=(t // tm,),
        in_specs=[pl.BlockSpec((tm * TOP_K,), lambda i: (i,), memory_space=pltpu.SMEM),
                  pl.BlockSpec(memory_space=pl.ANY),
                  pl.BlockSpec((tm, d), lambda i: (i, 0)),
                  pl.BlockSpec((tm, LANES), lambda i: (i, 0)),
                  pl.BlockSpec((1, d), lambda i: (0, 0)),
                  pl.BlockSpec((1, d), lambda i: (0, 0))],
        out_specs=pl.BlockSpec((tm, d), lambda i: (i, 0)),
        out_shape=jax.ShapeDtypeStruct((t, d), F32),
        scratch_shapes=[pltpu.VMEM((TOP_K, tm, d), F32), pltpu.SemaphoreType.DMA(())],
        compiler_params=_params(("arbitrary",)),
        name="combine",
    )(dest_flat, ybuf, x1, gates, gt2, g_final)


def _pad_lanes(v, lane0):
    return jnp.zeros((1, LANES), F32).at[0, lane0:lane0 + v.shape[0]].set(v.astype(F32))


def _layer(x, mod, g_norm1, w_in, conv_w, a_log, dt_bias, g_onorm, b_fgate, w_o_delta, w_o_fox, w_out,
           g_norm2, w_router, b_router, w_gate_up, b_gate_up, w_down, b_down, g_final):
    t, d = x.shape
    sh1, sc1, gt1, sh2, sc2, gt2 = [mod[:, i * d:(i + 1) * d] for i in range(N_MOD)]

    o = 0
    wq = w_in[:, o:o + QKV_W]; o += QKV_W
    wz = w_in[:, o:o + MIX_W]; o += MIX_W
    w_beta = w_in[:, o:o + N_HEADS]; o += N_HEADS
    w_dec = w_in[:, o:o + N_HEADS]; o += N_HEADS
    wf = w_in[:, o:o + QKV_W]; o += QKV_W
    w_fg = w_in[:, o:o + N_HEADS]; o += N_HEADS
    wga = w_in[:, o:o + d]; o += d
    wgb = w_in[:, o:o + d]
    ws = jnp.zeros((d, LANES), F32)
    ws = ws.at[:, L_BETA:L_BETA + N_HEADS].set(w_beta).at[:, L_G:L_G + N_HEADS].set(w_dec)
    ws = ws.at[:, L_F:L_F + N_HEADS].set(w_fg)
    bf = lambda w: w.astype(BF16)

    tm = min(256, t)
    qkv, z, fx, gate_a, gate_b, small = _inproj(
        x, g_norm1.reshape(1, d), sc1, sh1, bf(wq), bf(wz), bf(wf), bf(wga), bf(wgb), bf(ws), tm)

    gcol, grow = _gates(small, _pad_lanes(a_log, L_G), _pad_lanes(dt_bias, L_G), _pad_lanes(b_fgate, L_F), tm)

    o_a = _delta(qkv, z, gcol, grow, conv_w, g_onorm.reshape(1, HEAD_DIM), min(256, t))
    tq = min(512, t)
    f_rows = grow[L_F:L_F + N_HEADS].reshape(N_HEADS, t // tq, 1, tq)
    o_b = _fox(fx, f_rows, tq, tq)

    wr = jnp.zeros((d, LANES), F32).at[:, :N_EXPERTS].set(w_router)
    br = _pad_lanes(b_router, 0)
    x1, h2, ri, gates, cnt = _merge(o_a, o_b, gate_a, gate_b, x, bf(w_o_delta), bf(w_o_fox), bf(w_out), gt1,
                                    g_norm2.reshape(1, d), sc2, sh2, wr, br, tm)

    n_blocks = (t * TOP_K) // EXPERT_BLOCK + N_EXPERTS
    dest, blk, pend = _dest(ri, cnt, n_blocks, tm)
    dest_flat = dest[:, :TOP_K].reshape(t * TOP_K)
    blk_e = blk[:n_blocks, 0]
    nact = pend[0, N_EXPERTS - 1:N_EXPERTS] // EXPERT_BLOCK

    xbuf = _dispatch(dest_flat, h2, n_blocks * EXPERT_BLOCK)
    ybuf = _experts(blk_e, nact, xbuf, bf(w_gate_up), b_gate_up.reshape(N_EXPERTS, 1, 2 * FF),
                    bf(w_down), b_down.reshape(N_EXPERTS, 1, d))
    return _combine(dest_flat, ybuf, x1, gates, gt2, g_final.reshape(1, d))


def kernel(x, c, w_ada, b_ada, g_norm1, w_in, conv_w, a_log, dt_bias, g_onorm, b_fgate, w_o_delta, w_o_fox, w_out,
           g_norm2, w_router, b_router, w_gate_up, b_gate_up, w_down, b_down, g_final):
    b, s, d = x.shape
    assert b == 1 and d == D_MODEL and w_ada.shape[0] == 1
    h = x[0]
    for l in range(w_ada.shape[0]):
        mod = _modulation(c[0], w_ada[l], b_ada[l])
        h = _layer(h, mod, g_norm1[l], w_in[l], conv_w[l], a_log[l], dt_bias[l], g_onorm[l], b_fgate[l],
                   w_o_delta[l], w_o_fox[l], w_out[l], g_norm2[l], w_router[l], b_router[l], w_gate_up[l],
                   b_gate_up[l], w_down[l], b_down[l], g_final)
    return h[None]
```

```python
import functools

import jax
import jax.numpy as jnp
from jax import lax
from jax.experimental import pallas as pl
from jax.experimental.pallas import tpu as pltpu

F32 = jnp.float32
BF16 = jnp.bfloat16
I32 = jnp.int32
HIGHEST = lax.Precision.HIGHEST

D_MODEL = 1024
HEAD_DIM = 128
N_HEADS = 4
MIX_W = N_HEADS * HEAD_DIM
QKV_W = 3 * MIX_W
CONV_K = 4
CHUNK = 64
STACK = N_HEADS * CHUNK
SUB = 16
N_EXPERTS = 32
TOP_K = 4
FF = D_MODEL
SWIGLU_LIMIT = 7.0
SWIGLU_ALPHA = 1.702
RMS_EPS = 1e-6
L2_EPS = 1e-6
N_MOD = 6
LANES = 128
NEG_BIG = -1e30
LOG2E = 1.4426950408889634

EXPERT_BLOCK = 256
ROUTE_TILE = 256
VMEM_LIMIT = 56 * 1024 * 1024


def _params(sem, vmem=VMEM_LIMIT):
    return pltpu.CompilerParams(dimension_semantics=sem, vmem_limit_bytes=vmem)


def _softplus(x):
    return jnp.maximum(x, 0.0) + jnp.log(1.0 + jnp.exp(-jnp.abs(x)))


def _sigmoid(x):
    return 1.0 / (1.0 + jnp.exp(-x))


def _silu(x):
    return x * _sigmoid(x)


def _dot(a, b, precision=None):
    return jnp.dot(a, b, preferred_element_type=F32, precision=precision)


def _dot_nt(a, b, precision=None):
    return lax.dot_general(a, b, (((1,), (1,)), ((), ())), preferred_element_type=F32, precision=precision)


def _mod_kernel(c_ref, w_ref, b_ref, o_ref):
    o_ref[...] = jnp.sum(c_ref[...] * w_ref[...], axis=0, keepdims=True) + b_ref[...]


def _modulation(c, w_ada, b_ada):
    d, n = w_ada.shape
    tn = 1024
    return pl.pallas_call(
        _mod_kernel,
        grid=(n // tn,),
        in_specs=[pl.BlockSpec((d, 1), lambda j: (0, 0)),
                  pl.BlockSpec((d, tn), lambda j: (0, j)),
                  pl.BlockSpec((1, tn), lambda j: (0, j))],
        out_specs=pl.BlockSpec((1, tn), lambda j: (0, j)),
        out_shape=jax.ShapeDtypeStruct((1, n), F32),
        compiler_params=_params(("arbitrary",)),
        name="mod",
    )(c.reshape(d, 1), w_ada, b_ada.reshape(1, n))


def _inproj_kernel(x_ref, g_ref, sc_ref, sh_ref, wq_ref, wz_ref, wf_ref, wga_ref, wgb_ref, ws_ref,
                   oq_ref, oz_ref, of_ref, oga_ref, ogb_ref, os_ref):
    x = x_ref[...]
    y = x * lax.rsqrt(jnp.mean(x * x, axis=-1, keepdims=True) + RMS_EPS) * g_ref[...]
    h = (y * (1.0 + sc_ref[...]) + sh_ref[...]).astype(BF16)
    oq_ref[...] = _dot(h, wq_ref[...])
    oz_ref[...] = _dot(h, wz_ref[...]).astype(BF16)
    of_ref[...] = _dot(h, wf_ref[...]).astype(BF16)
    oga_ref[...] = _dot(h, wga_ref[...]).astype(BF16)
    ogb_ref[...] = _dot(h, wgb_ref[...]).astype(BF16)
    os_ref[...] = _dot(h, ws_ref[...])


def _inproj(x, g1, sc1, sh1, wq, wz, wf, wga, wgb, ws, tm):
    t, d = x.shape
    row = lambda i: (i, 0)
    fixed = lambda i: (0, 0)
    ws_list = [wq, wz, wf, wga, wgb, ws]
    out_dt = [F32, BF16, BF16, BF16, BF16, F32]
    return pl.pallas_call(
        _inproj_kernel,
        grid=(t // tm,),
        in_specs=[pl.BlockSpec((tm, d), row)] + [pl.BlockSpec((1, d), fixed)] * 3
                 + [pl.BlockSpec(w.shape, fixed) for w in ws_list],
        out_specs=[pl.BlockSpec((tm, w.shape[1]), row) for w in ws_list],
        out_shape=[jax.ShapeDtypeStruct((t, w.shape[1]), dt) for w, dt in zip(ws_list, out_dt)],
        compiler_params=_params(("arbitrary",)),
        name="inproj",
    )(x, g1, sc1, sh1, *ws_list)


L_BETA, L_G, L_F, L_GC, L_GL = 0, 4, 8, 12, 16


def _gates_kernel(s_ref, alog_ref, dtb_ref, bf_ref, col_ref, row_ref, carry_ref):
    i = pl.program_id(0)
    tm = s_ref.shape[0]

    @pl.when(i == 0)
    def _():
        carry_ref[...] = jnp.zeros_like(carry_ref)

    s = s_ref[...]
    lane = lax.broadcasted_iota(I32, s.shape, 1)
    beta = _sigmoid(s)
    g = -jnp.exp(alog_ref[...]) * _softplus(s + dtb_ref[...])
    logf = -_softplus(-(s + bf_ref[...]))
    is_g = (lane >= L_G) & (lane < L_G + N_HEADS)
    is_f = (lane >= L_F) & (lane < L_F + N_HEADS)
    g = jnp.where(is_g, g, 0.0)
    logf = jnp.where(is_f, logf, 0.0)

    r = lax.broadcasted_iota(I32, (tm, tm), 0)
    c = lax.broadcasted_iota(I32, (tm, tm), 1)
    same_chunk = (r // CHUNK) == (c // CHUNK)
    tri = (r >= c).astype(F32)
    tri_chunk = jnp.where(same_chunk & (r >= c), 1.0, 0.0)
    ones_chunk = jnp.where(same_chunk, 1.0, 0.0)
    f_cum = _dot(tri, logf, HIGHEST) + carry_ref[...]
    carry_ref[...] = f_cum[tm - 1:tm, :]
    gc = _dot(tri_chunk, g, HIGHEST)
    gl = _dot(ones_chunk, g, HIGHEST)

    out = jnp.where(lane < N_HEADS, beta, 0.0) + g + f_cum
    out = out + pltpu.roll(gc, L_GC - L_G, 1) + pltpu.roll(gl, L_GL - L_G, 1)
    col_ref[...] = out
    row_ref[...] = out.T[:row_ref.shape[0], :]


def _gates(small, alog_row, dtb_row, bf_row, tm):
    t = small.shape[0]
    n_rows = 24
    return pl.pallas_call(
        _gates_kernel,
        grid=(t // tm,),
        in_specs=[pl.BlockSpec((tm, LANES), lambda i: (i, 0))] + [pl.BlockSpec((1, LANES), lambda i: (0, 0))] * 3,
        out_specs=[pl.BlockSpec((tm, LANES), lambda i: (i, 0)), pl.BlockSpec((n_rows, tm), lambda i: (0, i))],
        out_shape=[jax.ShapeDtypeStruct((t, LANES), F32), jax.ShapeDtypeStruct((n_rows, t), F32)],
        scratch_shapes=[pltpu.VMEM((1, LANES), F32)],
        compiler_params=_params(("arbitrary",)),
        name="gates",
    )(small, alog_row, dtb_row, bf_row)


def _stack_heads(a, col0):
    return jnp.concatenate([a[:, col0 + h * HEAD_DIM: col0 + (h + 1) * HEAD_DIM] for h in range(N_HEADS)], axis=0)


def _stack_cols(a, lane0):
    return jnp.concatenate([a[:, lane0 + h: lane0 + h + 1] for h in range(N_HEADS)], axis=0)


def _delta_kernel(qkv_ref, z_ref, gcol_ref, grow_ref, cw_ref, gon_ref, o_ref, ext_ref, tail_ref, s_ref):
    i = pl.program_id(0)
    tb = qkv_ref.shape[0]
    pad = tail_ref.shape[0]

    @pl.when(i == 0)
    def _():
        tail_ref[...] = jnp.zeros_like(tail_ref)
        s_ref[...] = jnp.zeros_like(s_ref)

    ext_ref[0:pad, :] = tail_ref[...]
    ext_ref[pad:pad + tb, :] = qkv_ref[...]
    tail_ref[...] = qkv_ref[tb - pad:tb, :]
    conv = cw_ref[0:1, :] * ext_ref[pad - 3:pad - 3 + tb, :]
    for j in range(1, CONV_K):
        conv = conv + cw_ref[j:j + 1, :] * ext_ref[pad - 3 + j:pad - 3 + j + tb, :]
    act = _silu(conv)

    r = lax.broadcasted_iota(I32, (STACK, STACK), 0)
    c = lax.broadcasted_iota(I32, (STACK, STACK), 1)
    same = (r // CHUNK) == (c // CHUNK)
    m_incl = same & (r >= c)
    m_strict = same & (r > c)
    eye = jnp.where(r == c, 1.0, 0.0)
    m_sub = []
    size = SUB
    while size <= CHUNK:
        m_sub.append((r // size) == (c // size))
        size *= 2
    rb = lax.broadcasted_iota(I32, (STACK, N_HEADS * HEAD_DIM), 0) // CHUNK
    cb = lax.broadcasted_iota(I32, (STACK, N_HEADS * HEAD_DIM), 1) // HEAD_DIM
    head_match = rb == cb

    for ch in range(tb // CHUNK):
        r0 = ch * CHUNK
        a = act[r0:r0 + CHUNK, :]
        q = _stack_heads(a, 0)
        k = _stack_heads(a, MIX_W)
        v = _stack_heads(a, 2 * MIX_W)
        q = q * lax.rsqrt(jnp.sum(q * q, axis=-1, keepdims=True) + L2_EPS) * (HEAD_DIM ** -0.5)
        k = k * lax.rsqrt(jnp.sum(k * k, axis=-1, keepdims=True) + L2_EPS)

        gcols = gcol_ref[r0:r0 + CHUNK, :]
        beta = _stack_cols(gcols, L_BETA)
        gc = _stack_cols(gcols, L_GC)
        gc_row = jnp.concatenate(
            [grow_ref[L_GC + h:L_GC + h + 1, r0:r0 + CHUNK] for h in range(N_HEADS)], axis=1)
        gl_row = jnp.concatenate(
            [grow_ref[L_GL + h:L_GL + h + 1, r0:r0 + CHUNK] for h in range(N_HEADS)], axis=1)

        decay = jnp.exp(jnp.where(m_incl, gc - gc_row, NEG_BIG))
        kb16 = k.astype(BF16)
        kk = _dot_nt(kb16, kb16)
        lmat = jnp.where(m_strict, kk * beta * decay, 0.0)
        attn = _dot_nt(q.astype(BF16), kb16) * decay

        l_d = jnp.where(m_sub[0], lmat, 0.0)
        inv = eye - l_d
        l16 = l_d.astype(BF16)
        pw = _dot(l16, l16).astype(BF16)
        n_sq = SUB.bit_length() - 3
        for s in range(n_sq + 1):
            inv = inv + _dot(inv.astype(BF16), pw)
            if s < n_sq:
                pw = _dot(pw, pw).astype(BF16)
        for lvl in range(1, len(m_sub)):
            off = jnp.where(m_sub[lvl] & ~m_sub[lvl - 1], lmat, 0.0).astype(BF16)
            inv16 = inv.astype(BF16)
            inv = inv - _dot(_dot(inv16, off).astype(BF16), inv16)

        egc = jnp.exp(gc)
        rhs = jnp.concatenate([v * beta, k * (beta * egc)], axis=1).astype(BF16)
        uw = _dot(inv.astype(BF16), rhs)
        u, w = uw[:, :HEAD_DIM], uw[:, HEAD_DIM:]

        state16 = s_ref[...].astype(BF16)
        wq_s = _dot(jnp.concatenate([w, q * egc], axis=0).astype(BF16), state16)
        ws_d = jnp.concatenate([wq_s[h * CHUNK:(h + 1) * CHUNK, h * HEAD_DIM:(h + 1) * HEAD_DIM]
                                for h in range(N_HEADS)], axis=0)
        qs_d = jnp.concatenate([wq_s[STACK + h * CHUNK:STACK + (h + 1) * CHUNK, h * HEAD_DIM:(h + 1) * HEAD_DIM]
                                for h in range(N_HEADS)], axis=0)
        v_new = u - ws_d
        v16 = v_new.astype(BF16)
        o = qs_d + _dot(attn.astype(BF16), v16)

        kt_dec = (k.T * jnp.exp(gl_row - gc_row)).astype(BF16)
        v_bd = jnp.where(head_match, jnp.concatenate([v_new] * N_HEADS, axis=1), 0.0).astype(BF16)
        upd = _dot(kt_dec, v_bd)
        for h in range(N_HEADS):
            e_h = jnp.exp(grow_ref[L_GL + h:L_GL + h + 1, r0:r0 + 1])
            sl = slice(h * HEAD_DIM, (h + 1) * HEAD_DIM)
            s_ref[:, sl] = s_ref[:, sl] * e_h + upd[:, sl]

        for h in range(N_HEADS):
            oh = o[h * CHUNK:(h + 1) * CHUNK, :]
            oh = oh * lax.rsqrt(jnp.mean(oh * oh, axis=-1, keepdims=True) + RMS_EPS) * gon_ref[...]
            zh = z_ref[r0:r0 + CHUNK, h * HEAD_DIM:(h + 1) * HEAD_DIM].astype(F32)
            o_ref[r0:r0 + CHUNK, h * HEAD_DIM:(h + 1) * HEAD_DIM] = (oh * _silu(zh)).astype(o_ref.dtype)


def _delta(qkv, z, gcol, grow, conv_w, g_onorm, tb):
    t = qkv.shape[0]
    pad = 8
    return pl.pallas_call(
        _delta_kernel,
        grid=(t // tb,),
        in_specs=[pl.BlockSpec((tb, QKV_W), lambda i: (i, 0)),
                  pl.BlockSpec((tb, MIX_W), lambda i: (i, 0)),
                  pl.BlockSpec((tb, LANES), lambda i: (i, 0)),
                  pl.BlockSpec((grow.shape[0], tb), lambda i: (0, i)),
                  pl.BlockSpec((CONV_K, QKV_W), lambda i: (0, 0)),
                  pl.BlockSpec((1, HEAD_DIM), lambda i: (0, 0))],
        out_specs=pl.BlockSpec((tb, MIX_W), lambda i: (i, 0)),
        out_shape=jax.ShapeDtypeStruct((t, MIX_W), BF16),
        scratch_shapes=[pltpu.VMEM((tb + pad, QKV_W), F32),
                        pltpu.VMEM((pad, QKV_W), F32),
                        pltpu.VMEM((HEAD_DIM, N_HEADS * HEAD_DIM), F32)],
        compiler_params=_params(("arbitrary",)),
        name="delta",
    )(qkv, z, gcol, grow, conv_w, g_onorm)


FOX_GROUP = 4
FOX_TQ, FOX_TK = 256, 512


def _fox_kernel(*refs, tk):
    g = FOX_GROUP
    q_ref, o_ref = refs[0], refs[-1]
    k_refs, v_refs, f_refs = (refs[1 + i * g:1 + (i + 1) * g] for i in range(3))
    qi = pl.program_id(1)
    tq = q_ref.shape[0]
    ones = jnp.ones((tk, HEAD_DIM), BF16)

    def step(j, carry, mask):
        off = pl.multiple_of(j * tk, tk)
        out = []
        for h in range(g):
            m, acc = carry[h]
            q = q_ref[:, h * HEAD_DIM:(h + 1) * HEAD_DIM]
            s = _dot_nt(q, k_refs[h][pl.ds(off, tk), :]) - f_refs[h][j] * LOG2E
            if mask is not None:
                s = jnp.where(mask, s, NEG_BIG)
            m_new = jnp.maximum(m, jnp.max(s, axis=-1, keepdims=True))
            alpha = jnp.exp2(m - m_new)
            p = jnp.exp2(s - m_new).astype(BF16)
            v_aug = jnp.concatenate([v_refs[h][pl.ds(off, tk), :], ones], axis=1)
            out.append((m_new, alpha * acc + _dot(p, v_aug)))
        return tuple(out)

    init = tuple((jnp.full((tq, 1), NEG_BIG, F32), jnp.zeros((tq, 2 * HEAD_DIM), F32)) for _ in range(g))
    n_full = (qi * tq) // tk
    carry = lax.fori_loop(0, n_full, lambda j, c: step(j, c, None), init)
    row = lax.broadcasted_iota(I32, (tq, tk), 0)
    col = lax.broadcasted_iota(I32, (tq, tk), 1)
    for d in range(max(tq // tk, 1)):
        j = n_full + d
        carry = step(j, carry, col + (j * tk - qi * tq) <= row)
    for h in range(g):
        acc = carry[h][1]
        o_ref[:, h * HEAD_DIM:(h + 1) * HEAD_DIM] = (acc[:, :HEAD_DIM] / acc[:, HEAD_DIM:]).astype(o_ref.dtype)


def _fox(fx, f_rows, tq, tk):
    t = fx.shape[0]
    g = FOX_GROUP
    heads = lambda hg, u: hg * g + u
    q_specs = [pl.BlockSpec((tq, g * HEAD_DIM), lambda hg, i: (i, hg))]
    k_specs = [pl.BlockSpec((t, HEAD_DIM), lambda hg, i, u=u: (0, N_HEADS + heads(hg, u)),
                            pipeline_mode=pl.Buffered(1)) for u in range(g)]
    v_specs = [pl.BlockSpec((t, HEAD_DIM), lambda hg, i, u=u: (0, 2 * N_HEADS + heads(hg, u)),
                            pipeline_mode=pl.Buffered(1)) for u in range(g)]
    f_specs = [pl.BlockSpec((None, t // tk, 1, tk), lambda hg, i, u=u: (heads(hg, u), 0, 0, 0)) for u in range(g)]
    return pl.pallas_call(
        functools.partial(_fox_kernel, tk=tk),
        grid=(N_HEADS // g, t // tq),
        in_specs=q_specs + k_specs + v_specs + f_specs,
        out_specs=pl.BlockSpec((tq, g * HEAD_DIM), lambda hg, i: (i, hg)),
        out_shape=jax.ShapeDtypeStruct((t, MIX_W), BF16),
        compiler_params=_params(("arbitrary", "arbitrary")),
        name="fox",
    )(*([fx] * (1 + 2 * g) + [f_rows] * g))


def _merge_kernel(oa_ref, ob_ref, ga_ref, gb_ref, x_ref, woa_ref, wob_ref, wout_ref, gt1_ref, g2_ref, sc2_ref,
                  sh2_ref, wr_ref, br_ref, x1_ref, h2_ref, ri_ref, gate_ref, cnt_ref, carry_ref):
    i = pl.program_id(0)
    tm = x_ref.shape[0]

    @pl.when(i == 0)
    def _():
        carry_ref[...] = jnp.zeros_like(carry_ref)

    ya = _dot(oa_ref[...], woa_ref[...])
    yb = _dot(ob_ref[...], wob_ref[...])
    merged = _sigmoid(ga_ref[...].astype(F32)) * ya + _sigmoid(gb_ref[...].astype(F32)) * yb
    x1 = x_ref[...] + gt1_ref[...] * _dot(merged.astype(BF16), wout_ref[...])
    x1_ref[...] = x1
    y = x1 * lax.rsqrt(jnp.mean(x1 * x1, axis=-1, keepdims=True) + RMS_EPS) * g2_ref[...]
    h2 = y * (1.0 + sc2_ref[...]) + sh2_ref[...]
    h2_ref[...] = h2

    logits = _dot(h2, wr_ref[...], HIGHEST) + br_ref[...]
    lane = lax.broadcasted_iota(I32, logits.shape, 1)
    lanef = lane.astype(F32)
    cur = jnp.where(lane < N_EXPERTS, logits, -jnp.inf)
    vals, idxs = [], []
    for _ in range(TOP_K):
        m = jnp.max(cur, axis=-1, keepdims=True)
        ix = jnp.min(jnp.where(cur == m, lanef, float(LANES)), axis=-1, keepdims=True)
        vals.append(m)
        idxs.append(ix)
        cur = jnp.where(lanef == ix, -jnp.inf, cur)
    exps = [jnp.exp(v - vals[0]) for v in vals]
    denom = exps[0] + exps[1] + exps[2] + exps[3]

    onehot = jnp.zeros(logits.shape, F32)
    for ix in idxs:
        onehot = onehot + jnp.where(lanef == ix, 1.0, 0.0)
    r = lax.broadcasted_iota(I32, (tm, tm), 0)
    c = lax.broadcasted_iota(I32, (tm, tm), 1)
    strict = jnp.where(r > c, 1.0, 0.0).astype(BF16)
    before = _dot(strict, onehot.astype(BF16)) + carry_ref[...]
    carry_ref[...] = carry_ref[...] + jnp.sum(onehot, axis=0, keepdims=True)
    cnt_ref[...] = carry_ref[...].astype(I32)

    ri = jnp.zeros(logits.shape, F32)
    gates = jnp.zeros(logits.shape, F32)
    for kk in range(TOP_K):
        rank = jnp.sum(jnp.where(lanef == idxs[kk], before, 0.0), axis=-1, keepdims=True)
        ri = ri + jnp.where(lane == kk, idxs[kk], 0.0) + jnp.where(lane == TOP_K + kk, rank, 0.0)
        gates = gates + jnp.where(lane == kk, exps[kk] / denom, 0.0)
    ri_ref[...] = ri.astype(I32)
    gate_ref[...] = gates


def _merge(o_a, o_b, gate_a, gate_b, x, woa, wob, wout, gt1, g2, sc2, sh2, wr, br, tm):
    t, d = x.shape
    row = lambda i: (i, 0)
    fixed = lambda i: (0, 0)
    return pl.pallas_call(
        _merge_kernel,
        grid=(t // tm,),
        in_specs=[pl.BlockSpec((tm, MIX_W), row), pl.BlockSpec((tm, MIX_W), row),
                  pl.BlockSpec((tm, d), row), pl.BlockSpec((tm, d), row), pl.BlockSpec((tm, d), row),
                  pl.BlockSpec((MIX_W, d), fixed), pl.BlockSpec((MIX_W, d), fixed), pl.BlockSpec((d, d), fixed),
                  pl.BlockSpec((1, d), fixed), pl.BlockSpec((1, d), fixed), pl.BlockSpec((1, d), fixed),
                  pl.BlockSpec((1, d), fixed), pl.BlockSpec((d, LANES), fixed), pl.BlockSpec((1, LANES), fixed)],
        out_specs=[pl.BlockSpec((tm, d), row), pl.BlockSpec((tm, d), row), pl.BlockSpec((tm, LANES), row),
                   pl.BlockSpec((tm, LANES), row), pl.BlockSpec((1, LANES), fixed)],
        out_shape=[jax.ShapeDtypeStruct((t, d), F32), jax.ShapeDtypeStruct((t, d), F32),
                   jax.ShapeDtypeStruct((t, LANES), I32), jax.ShapeDtypeStruct((t, LANES), F32),
                   jax.ShapeDtypeStruct((1, LANES), I32)],
        scratch_shapes=[pltpu.VMEM((1, LANES), F32)],
        compiler_params=_params(("arbitrary",)),
        name="merge",
    )(o_a, o_b, gate_a, gate_b, x, woa, wob, wout, gt1, g2, sc2, sh2, wr, br)


def _dest_kernel(ri_ref, cnt_ref, dest_ref, blk_ref, pend_ref):
    shift = EXPERT_BLOCK.bit_length() - 1
    cnt = jnp.broadcast_to(cnt_ref[...], (8, LANES))
    lane_row = lax.broadcasted_iota(I32, cnt.shape, 1)
    padded = jnp.where(lane_row < N_EXPERTS, ((cnt + (EXPERT_BLOCK - 1)) >> shift) << shift, 0)
    pend = padded
    s = 1
    while s < N_EXPERTS:
        pend = pend + jnp.where(lane_row >= s, pltpu.roll(pend, s, 1), 0)
        s *= 2
    pstart = (pend - padded)[0:1, :].astype(F32)
    pend = pend[0:1, :]
    pend_ref[...] = pend

    ri = ri_ref[...]
    lane = lax.broadcasted_iota(I32, ri.shape, 1)
    dest = jnp.zeros(ri.shape, F32)
    for kk in range(TOP_K):
        ix = ri[:, kk:kk + 1]
        rank = ri[:, TOP_K + kk:TOP_K + kk + 1].astype(F32)
        start = jnp.sum(jnp.where(lane == ix, pstart, 0.0), axis=-1, keepdims=True)
        dest = dest + jnp.where(lane == kk, start + rank, 0.0)
    dest_ref[...] = dest.astype(I32)

    bstart = lax.broadcasted_iota(I32, blk_ref.shape, 0) * EXPERT_BLOCK
    lane_b = lax.broadcasted_iota(I32, blk_ref.shape, 1)
    ended = jnp.where((lane_b < N_EXPERTS) & (pend <= bstart), 1.0, 0.0)
    e = jnp.minimum(jnp.sum(ended, axis=-1, keepdims=True), float(N_EXPERTS - 1))
    vend = pstart + cnt[0:1, :].astype(F32)
    vend_e = jnp.sum(jnp.where(lane_b.astype(F32) == e, vend, 0.0), axis=-1, keepdims=True)
    nvalid = jnp.clip(vend_e - bstart[:, 0:1].astype(F32), 0.0, float(EXPERT_BLOCK))
    blk_ref[...] = jnp.where(lane_b == 0, e, jnp.where(lane_b == 1, nvalid, 0.0)).astype(I32)


def _dest(ri, cnt, n_blocks, tm):
    t = ri.shape[0]
    nb_pad = -(-n_blocks // 8) * 8
    return pl.pallas_call(
        _dest_kernel,
        grid=(t // tm,),
        in_specs=[pl.BlockSpec((tm, LANES), lambda i: (i, 0)), pl.BlockSpec((1, LANES), lambda i: (0, 0))],
        out_specs=[pl.BlockSpec((tm, LANES), lambda i: (i, 0)), pl.BlockSpec((nb_pad, LANES), lambda i: (0, 0)),
                   pl.BlockSpec((1, LANES), lambda i: (0, 0))],
        out_shape=[jax.ShapeDtypeStruct((t, LANES), I32), jax.ShapeDtypeStruct((nb_pad, LANES), I32),
                   jax.ShapeDtypeStruct((1, LANES), I32)],
        compiler_params=_params(("arbitrary",)),
        name="dest",
    )(ri, cnt)


def _row_copy(src_ref, src_row, dst_ref, dst_row, sem):
    return pltpu.make_async_copy(src_ref.at[pl.ds(src_row, 1), :], dst_ref.at[pl.ds(dst_row, 1), :], sem)


ROW_UNROLL = 8


def _drain_rows(src_ref, dst_ref, sem, n_rows):
    def drain(_, carry):
        for _ in range(ROW_UNROLL * TOP_K):
            _row_copy(src_ref, 0, dst_ref, 0, sem).wait()
        return carry

    lax.fori_loop(0, n_rows // ROW_UNROLL, drain, 0)


def _dispatch_kernel(dest_ref, h_ref, xbuf_ref, sem):
    tm = h_ref.shape[0]

    def issue(g, carry):
        for u in range(ROW_UNROLL):
            r = g * ROW_UNROLL + u
            for kk in range(TOP_K):
                _row_copy(h_ref, r, xbuf_ref, dest_ref[r * TOP_K + kk], sem).start(priority=kk % 2)
        return carry

    lax.fori_loop(0, tm // ROW_UNROLL, issue, 0)
    _drain_rows(h_ref, xbuf_ref, sem, tm)


def _dispatch(dest_flat, h2, n_rows):
    t, d = h2.shape
    tm = ROUTE_TILE
    return pl.pallas_call(
        _dispatch_kernel,
        grid=(t // tm,),
        in_specs=[pl.BlockSpec((tm * TOP_K,), lambda i: (i,), memory_space=pltpu.SMEM),
                  pl.BlockSpec((tm, d), lambda i: (i, 0))],
        out_specs=pl.BlockSpec(memory_space=pl.ANY),
        out_shape=jax.ShapeDtypeStruct((n_rows, d), h2.dtype),
        scratch_shapes=[pltpu.SemaphoreType.DMA(())],
        compiler_params=_params(("arbitrary",)),
        name="dispatch",
    )(dest_flat, h2)


def _experts_kernel(blk_e_ref, nvalid_ref, x_ref, wgu_ref, bgu_ref, wd_ref, bd_ref, y_ref, wgu16_ref, wd16_ref):
    b = pl.program_id(0)
    nvalid = nvalid_ref[b]

    @pl.when((b == 0) | (blk_e_ref[b] != blk_e_ref[jnp.maximum(b - 1, 0)]))
    def _():
        wgu16_ref[...] = wgu_ref[...].astype(BF16)
        wd16_ref[...] = wd_ref[...].astype(BF16)

    @pl.when(nvalid > 0)
    def _():
        rows = lax.broadcasted_iota(I32, x_ref.shape, 0)
        x = jnp.where(rows < nvalid, x_ref[...], 0.0).astype(BF16)
        gu = _dot(x, wgu16_ref[...]) + bgu_ref[...]
        gate = jnp.minimum(gu[:, :FF], SWIGLU_LIMIT)
        up = jnp.clip(gu[:, FF:], -SWIGLU_LIMIT, SWIGLU_LIMIT)
        act = (up + 1.0) * (gate * _sigmoid(SWIGLU_ALPHA * gate))
        y_ref[...] = _dot(act.astype(BF16), wd16_ref[...]) + bd_ref[...]

    @pl.when(nvalid <= 0)
    def _():
        y_ref[...] = jnp.zeros_like(y_ref)


def _experts(blk_e, nvalid, xbuf, wgu, bgu, wd, bd):
    n_rows, d = xbuf.shape
    nb = n_rows // EXPERT_BLOCK
    grid_spec = pltpu.PrefetchScalarGridSpec(
        num_scalar_prefetch=2,
        grid=(nb,),
        in_specs=[pl.BlockSpec((EXPERT_BLOCK, d), lambda b, e, n: (b, 0)),
                  pl.BlockSpec((None, d, 2 * FF), lambda b, e, n: (e[b], 0, 0)),
                  pl.BlockSpec((None, 1, 2 * FF), lambda b, e, n: (e[b], 0, 0)),
                  pl.BlockSpec((None, FF, d), lambda b, e, n: (e[b], 0, 0)),
                  pl.BlockSpec((None, 1, d), lambda b, e, n: (e[b], 0, 0))],
        out_specs=pl.BlockSpec((EXPERT_BLOCK, d), lambda b, e, n: (b, 0)),
        scratch_shapes=[pltpu.VMEM((d, 2 * FF), BF16), pltpu.VMEM((FF, d), BF16)],
    )
    return pl.pallas_call(
        _experts_kernel,
        grid_spec=grid_spec,
        out_shape=jax.ShapeDtypeStruct((n_rows, d), F32),
        compiler_params=_params(("arbitrary",)),
        name="experts",
    )(blk_e, nvalid, xbuf, wgu, bgu, wd, bd)


def _combine_kernel(dest_ref, ybuf_ref, x1_ref, gate_ref, gt2_ref, gf_ref, o_ref, rows_ref, sem):
    tm = x1_ref.shape[0]

    def issue(g, carry):
        for u in range(ROW_UNROLL):
            r = g * ROW_UNROLL + u
            for kk in range(TOP_K):
                _row_copy(ybuf_ref, dest_ref[r * TOP_K + kk], rows_ref.at[kk], r, sem).start(priority=kk % 2)
        return carry

    lax.fori_loop(0, tm // ROW_UNROLL, issue, 0)
    _drain_rows(ybuf_ref, rows_ref.at[0], sem, tm)

    gates = gate_ref[...]
    moe = gates[:, 0:1] * rows_ref[0]
    for kk in range(1, TOP_K):
        moe = moe + gates[:, kk:kk + 1] * rows_ref[kk]
    xo = x1_ref[...] + gt2_ref[...] * moe
    o_ref[...] = xo * lax.rsqrt(jnp.mean(xo * xo, axis=-1, keepdims=True) + RMS_EPS) * gf_ref[...]


def _combine(dest_flat, ybuf, x1, gates, gt2, g_final):
    t, d = x1.shape
    tm = ROUTE_TILE
    return pl.pallas_call(
        _combine_kernel,
        grid=(t // tm,),
        in_specs=[pl.BlockSpec((tm * TOP_K,), lambda i: (i,), memory_space=pltpu.SMEM),
                  pl.BlockSpec(memory_space=pl.ANY),
                  pl.BlockSpec((tm, d), lambda i: (i, 0)),
                  pl.BlockSpec((tm, LANES), lambda i: (i, 0)),
                  pl.BlockSpec((1, d), lambda i: (0, 0)),
                  pl.BlockSpec((1, d), lambda i: (0, 0))],
        out_specs=pl.BlockSpec((tm, d), lambda i: (i, 0)),
        out_shape=jax.ShapeDtypeStruct((t, d), F32),
        scratch_shapes=[pltpu.VMEM((TOP_K, tm, d), F32), pltpu.SemaphoreType.DMA(())],
        compiler_params=_params(("arbitrary",)),
        name="combine",
    )(dest_flat, ybuf, x1, gates, gt2, g_final)


def _pad_lanes(v, lane0):
    return jnp.zeros((1, LANES), F32).at[0, lane0:lane0 + v.shape[0]].set(v.astype(F32))


def _layer(x, mod, g_norm1, w_in, conv_w, a_log, dt_bias, g_onorm, b_fgate, w_o_delta, w_o_fox, w_out,
           g_norm2, w_router, b_router, w_gate_up, b_gate_up, w_down, b_down, g_final):
    t, d = x.shape
    sh1, sc1, gt1, sh2, sc2, gt2 = [mod[:, i * d:(i + 1) * d] for i in range(N_MOD)]

    o = 0
    wq = w_in[:, o:o + QKV_W]; o += QKV_W
    wz = w_in[:, o:o + MIX_W]; o += MIX_W
    w_beta = w_in[:, o:o + N_HEADS]; o += N_HEADS
    w_dec = w_in[:, o:o + N_HEADS]; o += N_HEADS
    wf = w_in[:, o:o + QKV_W]; o += QKV_W
    wf = jnp.concatenate([wf[:, :MIX_W] * (LOG2E * HEAD_DIM ** -0.5), wf[:, MIX_W:]], axis=1)
    w_fg = w_in[:, o:o + N_HEADS]; o += N_HEADS
    wga = w_in[:, o:o + d]; o += d
    wgb = w_in[:, o:o + d]
    ws = jnp.zeros((d, LANES), F32)
    ws = ws.at[:, L_BETA:L_BETA + N_HEADS].set(w_beta).at[:, L_G:L_G + N_HEADS].set(w_dec)
    ws = ws.at[:, L_F:L_F + N_HEADS].set(w_fg)
    bf = lambda w: w.astype(BF16)

    tm = min(256, t)
    qkv, z, fx, gate_a, gate_b, small = _inproj(
        x, g_norm1.reshape(1, d), sc1, sh1, bf(wq), bf(wz), bf(wf), bf(wga), bf(wgb), bf(ws), tm)

    gcol, grow = _gates(small, _pad_lanes(a_log, L_G), _pad_lanes(dt_bias, L_G), _pad_lanes(b_fgate, L_F), tm)

    o_a = _delta(qkv, z, gcol, grow, conv_w, g_onorm.reshape(1, HEAD_DIM), min(256, t))
    tq, tk = min(FOX_TQ, t), min(FOX_TK, t)
    f_rows = grow[L_F:L_F + N_HEADS].reshape(N_HEADS, t // tk, 1, tk)
    o_b = _fox(fx, f_rows, tq, tk)

    wr = jnp.zeros((d, LANES), F32).at[:, :N_EXPERTS].set(w_router)
    br = _pad_lanes(b_router, 0)
    x1, h2, ri, gates, cnt = _merge(o_a, o_b, gate_a, gate_b, x, bf(w_o_delta), bf(w_o_fox), bf(w_out), gt1,
                                    g_norm2.reshape(1, d), sc2, sh2, wr, br, tm)

    n_blocks = (t * TOP_K) // EXPERT_BLOCK + N_EXPERTS
    dest, blk, _ = _dest(ri, cnt, n_blocks, min(2048, t))
    dest_flat = dest[:, :TOP_K].reshape(t * TOP_K)
    blk_e, nvalid = blk[:n_blocks, 0], blk[:n_blocks, 1]

    xbuf = _dispatch(dest_flat, h2, n_blocks * EXPERT_BLOCK)
    ybuf = _experts(blk_e, nvalid, xbuf, w_gate_up, b_gate_up.reshape(N_EXPERTS, 1, 2 * FF),
                    w_down, b_down.reshape(N_EXPERTS, 1, d))
    return _combine(dest_flat, ybuf, x1, gates, gt2, g_final.reshape(1, d))


def kernel(x, c, w_ada, b_ada, g_norm1, w_in, conv_w, a_log, dt_bias, g_onorm, b_fgate, w_o_delta, w_o_fox, w_out,
           g_norm2, w_router, b_router, w_gate_up, b_gate_up, w_down, b_down, g_final):
    b, s, d = x.shape
    assert b == 1 and d == D_MODEL and w_ada.shape[0] == 1
    h = x[0]
    for l in range(w_ada.shape[0]):
        mod = _modulation(c[0], w_ada[l], b_ada[l])
        h = _layer(h, mod, g_norm1[l], w_in[l], conv_w[l], a_log[l], dt_bias[l], g_onorm[l], b_fgate[l],
                   w_o_delta[l], w_o_fox[l], w_out[l], g_norm2[l], w_router[l], b_router[l], w_gate_up[l],
                   b_gate_up[l], w_down[l], b_down[l], g_final)
    return h[None]
```

```python
import functools

import jax
import jax.numpy as jnp
from jax import lax
from jax.experimental import pallas as pl
from jax.experimental.pallas import tpu as pltpu

F32 = jnp.float32
BF16 = jnp.bfloat16
I32 = jnp.int32
HIGHEST = lax.Precision.HIGHEST

D_MODEL = 1024
HEAD_DIM = 128
N_HEADS = 4
MIX_W = N_HEADS * HEAD_DIM
QKV_W = 3 * MIX_W
CONV_K = 4
CHUNK = 64
STACK = N_HEADS * CHUNK
SUB = 16
N_EXPERTS = 32
TOP_K = 4
FF = D_MODEL
SWIGLU_LIMIT = 7.0
SWIGLU_ALPHA = 1.702
RMS_EPS = 1e-6
L2_EPS = 1e-6
N_MOD = 6
LANES = 128
NEG_BIG = -1e30
LOG2E = 1.4426950408889634

ROW_TILE = 512
GATES_TILE = 256
DELTA_TILE = 256
EXPERT_BLOCK = 512
ROUTE_TILE = 256
VMEM_LIMIT = 56 * 1024 * 1024


def _params(sem, vmem=VMEM_LIMIT, flags=None):
    return pltpu.CompilerParams(dimension_semantics=sem, vmem_limit_bytes=vmem, flags=flags)


def _softplus(x):
    return jnp.maximum(x, 0.0) + jnp.log(1.0 + jnp.exp(-jnp.abs(x)))


def _sigmoid(x):
    return 1.0 / (1.0 + jnp.exp(-x))


def _silu(x):
    return x * _sigmoid(x)


def _dot(a, b, precision=None):
    return jnp.dot(a, b, preferred_element_type=F32, precision=precision)


def _dot_nt(a, b, precision=None):
    return lax.dot_general(a, b, (((1,), (1,)), ((), ())), preferred_element_type=F32, precision=precision)


def _mod_kernel(c_ref, w_ref, b_ref, o_ref):
    o_ref[...] = jnp.sum(c_ref[...] * w_ref[...], axis=0, keepdims=True) + b_ref[...]


def _modulation(c, w_ada, b_ada):
    d, n = w_ada.shape
    tn = 1024
    return pl.pallas_call(
        _mod_kernel,
        grid=(n // tn,),
        in_specs=[pl.BlockSpec((d, 1), lambda j: (0, 0)),
                  pl.BlockSpec((d, tn), lambda j: (0, j)),
                  pl.BlockSpec((1, tn), lambda j: (0, j))],
        out_specs=pl.BlockSpec((1, tn), lambda j: (0, j)),
        out_shape=jax.ShapeDtypeStruct((1, n), F32),
        compiler_params=_params(("arbitrary",)),
        name="mod",
    )(c.reshape(d, 1), w_ada, b_ada.reshape(1, n))


def _inproj_kernel(x_ref, g_ref, sc_ref, sh_ref, wq_ref, wz_ref, wf_ref, wga_ref, wgb_ref, ws_ref,
                   oq_ref, oz_ref, of_ref, oga_ref, ogb_ref, os_ref):
    x = x_ref[...]
    y = x * lax.rsqrt(jnp.mean(x * x, axis=-1, keepdims=True) + RMS_EPS) * g_ref[...]
    h = (y * (1.0 + sc_ref[...]) + sh_ref[...]).astype(BF16)
    oq_ref[...] = _dot(h, wq_ref[...])
    oz_ref[...] = _dot(h, wz_ref[...]).astype(BF16)
    of_ref[...] = _dot(h, wf_ref[...]).astype(BF16)
    oga_ref[...] = _dot(h, wga_ref[...]).astype(BF16)
    ogb_ref[...] = _dot(h, wgb_ref[...]).astype(BF16)
    os_ref[...] = _dot(h, ws_ref[...])


def _inproj(x, g1, sc1, sh1, wq, wz, wf, wga, wgb, ws, tm):
    t, d = x.shape
    row = lambda i: (i, 0)
    fixed = lambda i: (0, 0)
    ws_list = [wq, wz, wf, wga, wgb, ws]
    out_dt = [F32, BF16, BF16, BF16, BF16, F32]
    return pl.pallas_call(
        _inproj_kernel,
        grid=(t // tm,),
        in_specs=[pl.BlockSpec((tm, d), row)] + [pl.BlockSpec((1, d), fixed)] * 3
                 + [pl.BlockSpec(w.shape, fixed) for w in ws_list],
        out_specs=[pl.BlockSpec((tm, w.shape[1]), row) for w in ws_list],
        out_shape=[jax.ShapeDtypeStruct((t, w.shape[1]), dt) for w, dt in zip(ws_list, out_dt)],
        compiler_params=_params(("arbitrary",)),
        name="inproj",
    )(x, g1, sc1, sh1, *ws_list)


L_BETA, L_G, L_F, L_GC, L_GL = 0, 4, 8, 12, 16


def _gates_kernel(s_ref, alog_ref, dtb_ref, bf_ref, col_ref, row_ref, carry_ref):
    i = pl.program_id(0)
    tm = s_ref.shape[0]

    @pl.when(i == 0)
    def _():
        carry_ref[...] = jnp.zeros_like(carry_ref)

    s = s_ref[...]
    lane = lax.broadcasted_iota(I32, s.shape, 1)
    beta = _sigmoid(s)
    g = -jnp.exp(alog_ref[...]) * _softplus(s + dtb_ref[...])
    logf = -_softplus(-(s + bf_ref[...]))
    is_g = (lane >= L_G) & (lane < L_G + N_HEADS)
    is_f = (lane >= L_F) & (lane < L_F + N_HEADS)
    g = jnp.where(is_g, g, 0.0)
    logf = jnp.where(is_f, logf, 0.0)

    r = lax.broadcasted_iota(I32, (tm, tm), 0)
    c = lax.broadcasted_iota(I32, (tm, tm), 1)
    same_chunk = (r // CHUNK) == (c // CHUNK)
    tri = (r >= c).astype(F32)
    tri_chunk = jnp.where(same_chunk & (r >= c), 1.0, 0.0)
    ones_chunk = jnp.where(same_chunk, 1.0, 0.0)
    f_cum = _dot(tri, logf, HIGHEST) + carry_ref[...]
    carry_ref[...] = f_cum[tm - 1:tm, :]
    gc = _dot(tri_chunk, g, HIGHEST)
    gl = _dot(ones_chunk, g, HIGHEST)

    out = jnp.where(lane < N_HEADS, beta, 0.0) + g + f_cum
    out = out + pltpu.roll(gc, L_GC - L_G, 1) + pltpu.roll(gl, L_GL - L_G, 1)
    col_ref[...] = out
    row_ref[...] = out.T[:row_ref.shape[0], :]


def _gates(small, alog_row, dtb_row, bf_row, tm):
    t = small.shape[0]
    n_rows = 24
    return pl.pallas_call(
        _gates_kernel,
        grid=(t // tm,),
        in_specs=[pl.BlockSpec((tm, LANES), lambda i: (i, 0))] + [pl.BlockSpec((1, LANES), lambda i: (0, 0))] * 3,
        out_specs=[pl.BlockSpec((tm, LANES), lambda i: (i, 0)), pl.BlockSpec((n_rows, tm), lambda i: (0, i))],
        out_shape=[jax.ShapeDtypeStruct((t, LANES), F32), jax.ShapeDtypeStruct((n_rows, t), F32)],
        scratch_shapes=[pltpu.VMEM((1, LANES), F32)],
        compiler_params=_params(("arbitrary",)),
        name="gates",
    )(small, alog_row, dtb_row, bf_row)


def _stack_heads(a, col0):
    return jnp.concatenate([a[:, col0 + h * HEAD_DIM: col0 + (h + 1) * HEAD_DIM] for h in range(N_HEADS)], axis=0)


def _stack_cols(a, lane0):
    return jnp.concatenate([a[:, lane0 + h: lane0 + h + 1] for h in range(N_HEADS)], axis=0)


def _delta_kernel(qkv_ref, z_ref, gcol_ref, grow_ref, cw_ref, gon_ref, o_ref, ext_ref, tail_ref, s_ref):
    i = pl.program_id(0)
    tb = qkv_ref.shape[0]
    pad = tail_ref.shape[0]

    @pl.when(i == 0)
    def _():
        tail_ref[...] = jnp.zeros_like(tail_ref)
        s_ref[...] = jnp.zeros_like(s_ref)

    ext_ref[0:pad, :] = tail_ref[...]
    ext_ref[pad:pad + tb, :] = qkv_ref[...]
    tail_ref[...] = qkv_ref[tb - pad:tb, :]
    conv = cw_ref[0:1, :] * ext_ref[pad - 3:pad - 3 + tb, :]
    for j in range(1, CONV_K):
        conv = conv + cw_ref[j:j + 1, :] * ext_ref[pad - 3 + j:pad - 3 + j + tb, :]
    act = _silu(conv)

    r = lax.broadcasted_iota(I32, (STACK, STACK), 0)
    c = lax.broadcasted_iota(I32, (STACK, STACK), 1)
    same = (r // CHUNK) == (c // CHUNK)
    m_incl = same & (r >= c)
    m_strict = same & (r > c)
    eye = jnp.where(r == c, 1.0, 0.0)
    m_sub = []
    size = SUB
    while size <= CHUNK:
        m_sub.append((r // size) == (c // size))
        size *= 2
    rb = lax.broadcasted_iota(I32, (STACK, N_HEADS * HEAD_DIM), 0) // CHUNK
    cb = lax.broadcasted_iota(I32, (STACK, N_HEADS * HEAD_DIM), 1) // HEAD_DIM
    head_match = rb == cb

    chunks = range(tb // CHUNK)
    pre = []
    for ch in chunks:
        r0 = ch * CHUNK
        a = act[r0:r0 + CHUNK, :]
        q = _stack_heads(a, 0)
        k = _stack_heads(a, MIX_W)
        v = _stack_heads(a, 2 * MIX_W)
        q = q * lax.rsqrt(jnp.sum(q * q, axis=-1, keepdims=True) + L2_EPS) * (HEAD_DIM ** -0.5)
        k = k * lax.rsqrt(jnp.sum(k * k, axis=-1, keepdims=True) + L2_EPS)

        gcols = gcol_ref[r0:r0 + CHUNK, :]
        beta = _stack_cols(gcols, L_BETA)
        gc = _stack_cols(gcols, L_GC)
        gc_row = jnp.concatenate(
            [grow_ref[L_GC + h:L_GC + h + 1, r0:r0 + CHUNK] for h in range(N_HEADS)], axis=1)
        gl_row = jnp.concatenate(
            [grow_ref[L_GL + h:L_GL + h + 1, r0:r0 + CHUNK] for h in range(N_HEADS)], axis=1)

        decay = jnp.exp(jnp.where(m_incl, gc - gc_row, NEG_BIG))
        kb16 = k.astype(BF16)
        kk = _dot_nt(kb16, kb16)
        lmat = jnp.where(m_strict, kk * beta * decay, 0.0)
        attn = (_dot_nt(q.astype(BF16), kb16) * decay).astype(BF16)
        egc = jnp.exp(gc)
        rhs = jnp.concatenate([v * beta, k * (beta * egc)], axis=1).astype(BF16)
        kt_dec = (k.T * jnp.exp(gl_row - gc_row)).astype(BF16)
        pre.append(dict(lmat=lmat, attn=attn, rhs=rhs, kt_dec=kt_dec, q_dec=q * egc))

    l_d = [jnp.where(m_sub[0], p["lmat"], 0.0) for p in pre]
    inv = [eye - l for l in l_d]
    l16 = [l.astype(BF16) for l in l_d]
    pw = [_dot(l, l).astype(BF16) for l in l16]
    n_sq = SUB.bit_length() - 3
    for s in range(n_sq + 1):
        inv = [x + _dot(x.astype(BF16), p) for x, p in zip(inv, pw)]
        if s < n_sq:
            pw = [_dot(p, p).astype(BF16) for p in pw]
    for lvl in range(1, len(m_sub)):
        m_off = m_sub[lvl] & ~m_sub[lvl - 1]
        off = [jnp.where(m_off, p["lmat"], 0.0).astype(BF16) for p in pre]
        inv16 = [x.astype(BF16) for x in inv]
        half = [_dot(x, o).astype(BF16) for x, o in zip(inv16, off)]
        inv = [x - _dot(h, x16) for x, h, x16 in zip(inv, half, inv16)]
    uws = [_dot(x.astype(BF16), p["rhs"]) for x, p in zip(inv, pre)]

    for ch in chunks:
        r0 = ch * CHUNK
        attn, kt_dec = pre[ch]["attn"], pre[ch]["kt_dec"]
        u, w = uws[ch][:, :HEAD_DIM], uws[ch][:, HEAD_DIM:]

        state16 = s_ref[...].astype(BF16)
        wq_s = _dot(jnp.concatenate([w, pre[ch]["q_dec"]], axis=0).astype(BF16), state16)
        ws_d = jnp.concatenate([wq_s[h * CHUNK:(h + 1) * CHUNK, h * HEAD_DIM:(h + 1) * HEAD_DIM]
                                for h in range(N_HEADS)], axis=0)
        qs_d = jnp.concatenate([wq_s[STACK + h * CHUNK:STACK + (h + 1) * CHUNK, h * HEAD_DIM:(h + 1) * HEAD_DIM]
                                for h in range(N_HEADS)], axis=0)
        v_new = u - ws_d
        v16 = v_new.astype(BF16)
        o = qs_d + _dot(attn, v16)

        v_bd =jnp.where(head_match, jnp.concatenate([v_new] * N_HEADS, axis=1), 0.0).astype(BF16)
        upd = _dot(kt_dec, v_bd)
        for h in range(N_HEADS):
            e_h = jnp.exp(grow_ref[L_GL + h:L_GL + h + 1, r0:r0 + 1])
            sl = slice(h * HEAD_DIM, (h + 1) * HEAD_DIM)
            s_ref[:, sl] = s_ref[:, sl] * e_h + upd[:, sl]

        for h in range(N_HEADS):
            oh = o[h * CHUNK:(h + 1) * CHUNK, :]
            oh = oh * lax.rsqrt(jnp.mean(oh * oh, axis=-1, keepdims=True) + RMS_EPS) * gon_ref[...]
            zh = z_ref[r0:r0 + CHUNK, h * HEAD_DIM:(h + 1) * HEAD_DIM].astype(F32)
            o_ref[r0:r0 + CHUNK, h * HEAD_DIM:(h + 1) * HEAD_DIM] = (oh * _silu(zh)).astype(o_ref.dtype)


def _delta(qkv, z, gcol, grow, conv_w, g_onorm, tb):
    t = qkv.shape[0]
    pad = 8
    return pl.pallas_call(
        _delta_kernel,
        grid=(t // tb,),
        in_specs=[pl.BlockSpec((tb, QKV_W), lambda i: (i, 0)),
                  pl.BlockSpec((tb, MIX_W), lambda i: (i, 0)),
                  pl.BlockSpec((tb, LANES), lambda i: (i, 0)),
                  pl.BlockSpec((grow.shape[0], tb), lambda i: (0, i)),
                  pl.BlockSpec((CONV_K, QKV_W), lambda i: (0, 0)),
                  pl.BlockSpec((1, HEAD_DIM), lambda i: (0, 0))],
        out_specs=pl.BlockSpec((tb, MIX_W), lambda i: (i, 0)),
        out_shape=jax.ShapeDtypeStruct((t, MIX_W), BF16),
        scratch_shapes=[pltpu.VMEM((tb + pad, QKV_W), F32),
                        pltpu.VMEM((pad, QKV_W), F32),
                        pltpu.VMEM((HEAD_DIM, N_HEADS * HEAD_DIM), F32)],
        compiler_params=_params(("arbitrary",)),
        name="delta",
    )(qkv, z, gcol, grow, conv_w, g_onorm)


FOX_GROUP = 4
FOX_TQ, FOX_TK = 256, 512


def _fox_kernel(*refs, tk):
    g = FOX_GROUP
    q_ref, o_ref = refs[0], refs[-1]
    k_refs, v_refs, f_refs = (refs[1 + i * g:1 + (i + 1) * g] for i in range(3))
    qi = pl.program_id(1)
    tq = q_ref.shape[0]
    ones = jnp.ones((tk, HEAD_DIM), BF16)

    def step(j, carry, mask):
        off = pl.multiple_of(j * tk, tk)
        ss = []
        for h in range(g):
            q = q_ref[:, h * HEAD_DIM:(h + 1) * HEAD_DIM]
            s = _dot_nt(q, k_refs[h][pl.ds(off, tk), :]) - f_refs[h][j] * LOG2E
            if mask is not None:
                s = jnp.where(mask, s, NEG_BIG)
            ss.append(s)
        ps, ms, alphas = [], [], []
        for h in range(g):
            m = carry[h][0]
            m_new = jnp.maximum(m, jnp.max(ss[h], axis=-1, keepdims=True))
            alphas.append(jnp.exp2(m - m_new))
            ps.append(jnp.exp2(ss[h] - m_new).astype(BF16))
            ms.append(m_new)
        out = []
        for h in range(g):
            v_aug = jnp.concatenate([v_refs[h][pl.ds(off, tk), :], ones], axis=1)
            out.append((ms[h], alphas[h] * carry[h][1] + _dot(ps[h], v_aug)))
        return tuple(out)

    init = tuple((jnp.full((tq, 1), NEG_BIG, F32), jnp.zeros((tq, 2 * HEAD_DIM), F32)) for _ in range(g))
    n_full = (qi * tq) // tk
    carry = lax.fori_loop(0, n_full, lambda j, c: step(j, c, None), init)
    row = lax.broadcasted_iota(I32, (tq, tk), 0)
    col = lax.broadcasted_iota(I32, (tq, tk), 1)
    for d in range(max(tq // tk, 1)):
        j = n_full + d
        carry = step(j, carry, col + (j * tk - qi * tq) <= row)
    for h in range(g):
        acc = carry[h][1]
        o_ref[:, h * HEAD_DIM:(h + 1) * HEAD_DIM] = (acc[:, :HEAD_DIM] / acc[:, HEAD_DIM:]).astype(o_ref.dtype)


def _fox(fx, f_rows, tq, tk):
    t = fx.shape[0]
    g = FOX_GROUP
    heads = lambda hg, u: hg * g + u
    q_specs = [pl.BlockSpec((tq, g * HEAD_DIM), lambda hg, i: (i, hg))]
    k_specs = [pl.BlockSpec((t, HEAD_DIM), lambda hg, i, u=u: (0, N_HEADS + heads(hg, u)),
                            pipeline_mode=pl.Buffered(1)) for u in range(g)]
    v_specs = [pl.BlockSpec((t, HEAD_DIM), lambda hg, i, u=u: (0, 2 * N_HEADS + heads(hg, u)),
                            pipeline_mode=pl.Buffered(1)) for u in range(g)]
    f_specs = [pl.BlockSpec((None, t // tk, 1, tk), lambda hg, i, u=u: (heads(hg, u), 0, 0, 0)) for u in range(g)]
    return pl.pallas_call(
        functools.partial(_fox_kernel, tk=tk),
        grid=(N_HEADS // g, t // tq),
        in_specs=q_specs + k_specs + v_specs + f_specs,
        out_specs=pl.BlockSpec((tq, g * HEAD_DIM), lambda hg, i: (i, hg)),
        out_shape=jax.ShapeDtypeStruct((t, MIX_W), BF16),
        compiler_params=_params(("arbitrary", "arbitrary")),
        name="fox",
    )(*([fx] * (1 + 2 * g) + [f_rows] * g))


def _merge_kernel(oa_ref, ob_ref, ga_ref, gb_ref, x_ref, woa_ref, wob_ref, wout_ref, gt1_ref, g2_ref, sc2_ref,
                  sh2_ref, wr_ref, br_ref, x1_ref, h2_ref, ri_ref, gate_ref, cnt_ref, carry_ref):
    i = pl.program_id(0)
    tm = x_ref.shape[0]

    @pl.when(i == 0)
    def _():
        carry_ref[...] = jnp.zeros_like(carry_ref)

    ya = _dot(oa_ref[...], woa_ref[...])
    yb = _dot(ob_ref[...], wob_ref[...])
    merged = _sigmoid(ga_ref[...].astype(F32)) * ya + _sigmoid(gb_ref[...].astype(F32)) * yb
    x1 = x_ref[...] + gt1_ref[...] * _dot(merged.astype(BF16), wout_ref[...])
    x1_ref[...] = x1
    y = x1 * lax.rsqrt(jnp.mean(x1 * x1, axis=-1, keepdims=True) + RMS_EPS) * g2_ref[...]
    h2 = y * (1.0 + sc2_ref[...]) + sh2_ref[...]
    h2_ref[...] = h2

    logits = _dot(h2, wr_ref[...], HIGHEST) + br_ref[...]
    lane = lax.broadcasted_iota(I32, logits.shape, 1)
    lanef = lane.astype(F32)
    cur = jnp.where(lane < N_EXPERTS, logits, -jnp.inf)
    vals, idxs = [], []
    for _ in range(TOP_K):
        m = jnp.max(cur, axis=-1, keepdims=True)
        ix = jnp.min(jnp.where(cur == m, lanef, float(LANES)), axis=-1, keepdims=True)
        vals.append(m)
        idxs.append(ix)
        cur = jnp.where(lanef == ix, -jnp.inf, cur)
    exps = [jnp.exp(v - vals[0]) for v in vals]
    denom = exps[0] + exps[1] + exps[2] + exps[3]

    onehot = jnp.zeros(logits.shape, F32)
    for ix in idxs:
        onehot = onehot + jnp.where(lanef == ix, 1.0, 0.0)
    r = lax.broadcasted_iota(I32, (tm, tm), 0)
    c = lax.broadcasted_iota(I32, (tm, tm), 1)
    strict = jnp.where(r > c, 1.0, 0.0).astype(BF16)
    before = _dot(strict, onehot.astype(BF16)) + carry_ref[...]
    carry_ref[...] = carry_ref[...] + jnp.sum(onehot, axis=0, keepdims=True)
    cnt_ref[...] = carry_ref[...].astype(I32)

    ri = jnp.zeros(logits.shape, F32)
    gates = jnp.zeros(logits.shape, F32)
    for kk in range(TOP_K):
        rank = jnp.sum(jnp.where(lanef == idxs[kk], before, 0.0), axis=-1, keepdims=True)
        ri = ri + jnp.where(lane == kk, idxs[kk], 0.0) + jnp.where(lane == TOP_K + kk, rank, 0.0)
        gates = gates + jnp.where(lane == kk, exps[kk] / denom, 0.0)
    ri_ref[...] = ri.astype(I32)
    gate_ref[...] = gates


def _merge(o_a, o_b, gate_a, gate_b, x, woa, wob, wout, gt1, g2, sc2, sh2, wr, br, tm):
    t, d = x.shape
    row = lambda i: (i, 0)
    fixed = lambda i: (0, 0)
    return pl.pallas_call(
        _merge_kernel,
        grid=(t // tm,),
        in_specs=[pl.BlockSpec((tm, MIX_W), row), pl.BlockSpec((tm, MIX_W), row),
                  pl.BlockSpec((tm, d), row), pl.BlockSpec((tm, d), row), pl.BlockSpec((tm, d), row),
                  pl.BlockSpec((MIX_W, d), fixed), pl.BlockSpec((MIX_W, d), fixed), pl.BlockSpec((d, d), fixed),
                  pl.BlockSpec((1, d), fixed), pl.BlockSpec((1, d), fixed), pl.BlockSpec((1, d), fixed),
                  pl.BlockSpec((1, d), fixed), pl.BlockSpec((d, LANES), fixed), pl.BlockSpec((1, LANES), fixed)],
        out_specs=[pl.BlockSpec((tm, d), row), pl.BlockSpec((tm, d), row), pl.BlockSpec((tm, LANES), row),
                   pl.BlockSpec((tm, LANES), row), pl.BlockSpec((1, LANES), fixed)],
        out_shape=[jax.ShapeDtypeStruct((t, d), F32), jax.ShapeDtypeStruct((t, d), F32),
                   jax.ShapeDtypeStruct((t, LANES), I32), jax.ShapeDtypeStruct((t, LANES), F32),
                   jax.ShapeDtypeStruct((1, LANES), I32)],
        scratch_shapes=[pltpu.VMEM((1, LANES), F32)],
        compiler_params=_params(("arbitrary",)),
        name="merge",
    )(o_a, o_b, gate_a, gate_b, x, woa, wob, wout, gt1, g2, sc2, sh2, wr, br)


def _dest_kernel(ri_ref, cnt_ref, dest_ref, blk_ref, pend_ref):
    shift = EXPERT_BLOCK.bit_length() - 1
    cnt = jnp.broadcast_to(cnt_ref[...], (8, LANES))
    lane_row = lax.broadcasted_iota(I32, cnt.shape, 1)
    padded = jnp.where(lane_row < N_EXPERTS, ((cnt + (EXPERT_BLOCK - 1)) >> shift) << shift, 0)
    pend = padded
    s = 1
    while s < N_EXPERTS:
        pend = pend + jnp.where(lane_row >= s, pltpu.roll(pend, s, 1), 0)
        s *= 2
    pstart = (pend - padded)[0:1, :].astype(F32)
    pend = pend[0:1, :]
    pend_ref[...] = pend

    ri = ri_ref[...]
    lane = lax.broadcasted_iota(I32, ri.shape, 1)
    dest = jnp.zeros(ri.shape, F32)
    for kk in range(TOP_K):
        ix = ri[:, kk:kk + 1]
        rank = ri[:, TOP_K + kk:TOP_K + kk + 1].astype(F32)
        start = jnp.sum(jnp.where(lane == ix, pstart, 0.0), axis=-1, keepdims=True)
        dest = dest + jnp.where(lane == kk, start + rank, 0.0)
    dest_ref[...] = dest.astype(I32)

    bstart = lax.broadcasted_iota(I32, blk_ref.shape, 0) * EXPERT_BLOCK
    lane_b = lax.broadcasted_iota(I32, blk_ref.shape, 1)
    ended = jnp.where((lane_b < N_EXPERTS) & (pend <= bstart), 1.0, 0.0)
    e = jnp.minimum(jnp.sum(ended, axis=-1, keepdims=True), float(N_EXPERTS - 1))
    vend = pstart + cnt[0:1, :].astype(F32)
    vend_e = jnp.sum(jnp.where(lane_b.astype(F32) == e, vend, 0.0), axis=-1, keepdims=True)
    nvalid = jnp.clip(vend_e - bstart[:, 0:1].astype(F32), 0.0, float(EXPERT_BLOCK))
    blk_ref[...] = jnp.where(lane_b == 0, e, jnp.where(lane_b == 1, nvalid, 0.0)).astype(I32)


def _dest(ri, cnt, n_blocks, tm):
    t = ri.shape[0]
    nb_pad = -(-n_blocks // 8) * 8
    return pl.pallas_call(
        _dest_kernel,
        grid=(t // tm,),
        in_specs=[pl.BlockSpec((tm, LANES), lambda i: (i, 0)), pl.BlockSpec((1, LANES), lambda i: (0, 0))],
        out_specs=[pl.BlockSpec((tm, LANES), lambda i: (i, 0)), pl.BlockSpec((nb_pad, LANES), lambda i: (0, 0)),
                   pl.BlockSpec((1, LANES), lambda i: (0, 0))],
        out_shape=[jax.ShapeDtypeStruct((t, LANES), I32), jax.ShapeDtypeStruct((nb_pad, LANES), I32),
                   jax.ShapeDtypeStruct((1, LANES), I32)],
        compiler_params=_params(("arbitrary",)),
        name="dest",
    )(ri, cnt)


def _row_copy(src_ref, src_row, dst_ref, dst_row, sem):
    return pltpu.make_async_copy(src_ref.at[pl.ds(src_row, 1), :], dst_ref.at[pl.ds(dst_row, 1), :], sem)


ROW_UNROLL = 8


def _drain_rows(src_ref, dst_ref, sem, n_rows):
    def drain(_, carry):
        for _ in range(ROW_UNROLL * TOP_K):
            _row_copy(src_ref, 0, dst_ref, 0, sem).wait()
        return carry

    lax.fori_loop(0, n_rows // ROW_UNROLL, drain, 0)


def _dispatch_kernel(dest_ref, h_ref, xbuf_ref, sem):
    tm = h_ref.shape[0]

    def issue(g, carry):
        for u in range(ROW_UNROLL):
            r = g * ROW_UNROLL + u
            for kk in range(TOP_K):
                _row_copy(h_ref, r, xbuf_ref, dest_ref[r * TOP_K + kk], sem).start(priority=kk % 2)
        return carry

    lax.fori_loop(0, tm // ROW_UNROLL, issue, 0)
    _drain_rows(h_ref, xbuf_ref, sem, tm)


def _dispatch(dest_flat, h2, n_rows):
    t, d = h2.shape
    tm = ROUTE_TILE
    return pl.pallas_call(
        _dispatch_kernel,
        grid=(t // tm,),
        in_specs=[pl.BlockSpec((tm * TOP_K,), lambda i: (i,), memory_space=pltpu.SMEM),
                  pl.BlockSpec((tm, d), lambda i: (i, 0))],
        out_specs=pl.BlockSpec(memory_space=pl.ANY),
        out_shape=jax.ShapeDtypeStruct((n_rows, d), h2.dtype),
        scratch_shapes=[pltpu.SemaphoreType.DMA(())],
        compiler_params=_params(("arbitrary",)),
        name="dispatch",
    )(dest_flat, h2)


def _experts_kernel(blk_e_ref, nvalid_ref, x_ref, wgu_ref, bgu_ref, wd_ref, bd_ref, y_ref, wgu16_ref, wd16_ref):
    b = pl.program_id(0)
    nvalid = nvalid_ref[b]

    @pl.when((b == 0) | (blk_e_ref[b] != blk_e_ref[jnp.maximum(b - 1, 0)]))
    def _():
        wgu16_ref[...] = wgu_ref[...].astype(BF16)
        wd16_ref[...] = wd_ref[...].astype(BF16)

    @pl.when(nvalid > 0)
    def _():
        rows = lax.broadcasted_iota(I32, x_ref.shape, 0)
        x = jnp.where(rows < nvalid, x_ref[...], 0.0).astype(BF16)
        gu = _dot(x, wgu16_ref[...]) + bgu_ref[...]
        gate = jnp.minimum(gu[:, :FF], SWIGLU_LIMIT)
        up = jnp.clip(gu[:, FF:], -SWIGLU_LIMIT, SWIGLU_LIMIT)
        act = (up + 1.0) * (gate * _sigmoid(SWIGLU_ALPHA * gate))
        y_ref[...] = _dot(act.astype(BF16), wd16_ref[...]) + bd_ref[...]

    @pl.when(nvalid <= 0)
    def _():
        y_ref[...] = jnp.zeros_like(y_ref)


def _experts(blk_e, nvalid, xbuf, wgu, bgu, wd, bd):
    n_rows, d = xbuf.shape
    nb = n_rows // EXPERT_BLOCK
    grid_spec = pltpu.PrefetchScalarGridSpec(
        num_scalar_prefetch=2,
        grid=(nb,),
        in_specs=[pl.BlockSpec((EXPERT_BLOCK, d), lambda b, e, n: (b, 0)),
                  pl.BlockSpec((None, d, 2 * FF), lambda b, e, n: (e[b], 0, 0)),
                  pl.BlockSpec((None, 1, 2 * FF), lambda b, e, n: (e[b], 0, 0)),
                  pl.BlockSpec((None, FF, d), lambda b, e, n: (e[b], 0, 0)),
                  pl.BlockSpec((None, 1, d), lambda b, e, n: (e[b], 0, 0))],
        out_specs=pl.BlockSpec((EXPERT_BLOCK, d), lambda b, e, n: (b, 0)),
        scratch_shapes=[pltpu.VMEM((d, 2 * FF), BF16), pltpu.VMEM((FF, d), BF16)],
    )
    return pl.pallas_call(
        _experts_kernel,
        grid_spec=grid_spec,
        out_shape=jax.ShapeDtypeStruct((n_rows, d), F32),
        compiler_params=_params(("arbitrary",)),
        name="experts",
    )(blk_e, nvalid, xbuf, wgu, bgu, wd, bd)


def _combine_kernel(dest_ref, ybuf_ref, x1_ref, gate_ref, gt2_ref, gf_ref, o_ref, rows_ref, sem):
    tm = x1_ref.shape[0]

    def issue(g, carry):
        for u in range(ROW_UNROLL):
            r = g * ROW_UNROLL + u
            for kk in range(TOP_K):
                _row_copy(ybuf_ref, dest_ref[r * TOP_K + kk], rows_ref.at[kk], r, sem).start(priority=kk % 2)
        return carry

    lax.fori_loop(0, tm // ROW_UNROLL, issue, 0)
    _drain_rows(ybuf_ref, rows_ref.at[0], sem, tm)

    gates = gate_ref[...]
    moe = gates[:, 0:1] * rows_ref[0]
    for kk in range(1, TOP_K):
        moe = moe + gates[:, kk:kk + 1] * rows_ref[kk]
    xo = x1_ref[...] + gt2_ref[...] * moe
    o_ref[...] = xo * lax.rsqrt(jnp.mean(xo * xo, axis=-1, keepdims=True) + RMS_EPS) * gf_ref[...]


def _combine(dest_flat, ybuf, x1, gates, gt2, g_final):
    t, d = x1.shape
    tm = ROUTE_TILE
    return pl.pallas_call(
        _combine_kernel,
        grid=(t // tm,),
        in_specs=[pl.BlockSpec((tm * TOP_K,), lambda i: (i,), memory_space=pltpu.SMEM),
                  pl.BlockSpec(memory_space=pl.ANY),
                  pl.BlockSpec((tm, d), lambda i: (i, 0)),
                  pl.BlockSpec((tm, LANES), lambda i: (i, 0)),
                  pl.BlockSpec((1, d), lambda i: (0, 0)),
                  pl.BlockSpec((1, d), lambda i: (0, 0))],
        out_specs=pl.BlockSpec((tm, d), lambda i: (i, 0)),
        out_shape=jax.ShapeDtypeStruct((t, d), F32),
        scratch_shapes=[pltpu.VMEM((TOP_K, tm, d), F32), pltpu.SemaphoreType.DMA(())],
        compiler_params=_params(("arbitrary",)),
        name="combine",
    )(dest_flat, ybuf, x1, gates, gt2, g_final)


def _pad_lanes(v, lane0):
    return jnp.zeros((1, LANES), F32).at[0, lane0:lane0 + v.shape[0]].set(v.astype(F32))


def _layer(x, mod, g_norm1, w_in, conv_w, a_log, dt_bias, g_onorm, b_fgate, w_o_delta, w_o_fox, w_out,
           g_norm2, w_router, b_router, w_gate_up, b_gate_up, w_down, b_down, g_final):
    t, d = x.shape
    sh1, sc1, gt1, sh2, sc2, gt2 = [mod[:, i * d:(i + 1) * d] for i in range(N_MOD)]

    o = 0
    wq = w_in[:, o:o + QKV_W]; o += QKV_W
    wz = w_in[:, o:o + MIX_W]; o += MIX_W
    w_beta = w_in[:, o:o + N_HEADS]; o += N_HEADS
    w_dec = w_in[:, o:o + N_HEADS]; o += N_HEADS
    wf = w_in[:, o:o + QKV_W]; o += QKV_W
    wf = jnp.concatenate([wf[:, :MIX_W] * (LOG2E * HEAD_DIM ** -0.5), wf[:, MIX_W:]], axis=1)
    w_fg = w_in[:, o:o + N_HEADS]; o += N_HEADS
    wga = w_in[:, o:o + d]; o += d
    wgb = w_in[:, o:o + d]
    ws = jnp.zeros((d, LANES), F32)
    ws = ws.at[:, L_BETA:L_BETA + N_HEADS].set(w_beta).at[:, L_G:L_G + N_HEADS].set(w_dec)
    ws = ws.at[:, L_F:L_F + N_HEADS].set(w_fg)
    bf = lambda w: w.astype(BF16)

    tm = min(ROW_TILE, t)
    qkv, z, fx, gate_a, gate_b, small = _inproj(
        x, g_norm1.reshape(1, d), sc1, sh1, bf(wq), bf(wz), bf(wf), bf(wga), bf(wgb), bf(ws), tm)

    gcol, grow = _gates(small, _pad_lanes(a_log, L_G), _pad_lanes(dt_bias, L_G), _pad_lanes(b_fgate, L_F),
                        min(GATES_TILE, t))

    o_a = _delta(qkv, z, gcol, grow, conv_w, g_onorm.reshape(1, HEAD_DIM), min(DELTA_TILE, t))
    tq, tk = min(FOX_TQ, t), min(FOX_TK, t)
    f_rows = grow[L_F:L_F + N_HEADS].reshape(N_HEADS, t // tk, 1, tk)
    o_b = _fox(fx, f_rows, tq, tk)

    wr = jnp.zeros((d, LANES), F32).at[:, :N_EXPERTS].set(w_router)
    br = _pad_lanes(b_router, 0)
    x1, h2, ri, gates, cnt = _merge(o_a, o_b, gate_a, gate_b, x, bf(w_o_delta), bf(w_o_fox), bf(w_out), gt1,
                                    g_norm2.reshape(1, d), sc2, sh2, wr, br, tm)

    n_blocks = (t * TOP_K) // EXPERT_BLOCK + N_EXPERTS
    dest, blk, _ = _dest(ri, cnt, n_blocks, min(2048, t))
    dest_flat = dest[:, :TOP_K].reshape(t * TOP_K)
    blk_e, nvalid = blk[:n_blocks, 0], blk[:n_blocks, 1]

    xbuf = _dispatch(dest_flat, h2, n_blocks * EXPERT_BLOCK)
    ybuf = _experts(blk_e, nvalid, xbuf, w_gate_up, b_gate_up.reshape(N_EXPERTS, 1, 2 * FF),
                    w_down, b_down.reshape(N_EXPERTS, 1, d))
    return _combine(dest_flat, ybuf, x1, gates, gt2, g_final.reshape(1, d))


def kernel(x, c, w_ada, b_ada, g_norm1, w_in, conv_w, a_log, dt_bias, g_onorm, b_fgate, w_o_delta, w_o_fox, w_out,
           g_norm2, w_router, b_router, w_gate_up, b_gate_up, w_down, b_down, g_final):
    b, s, d = x.shape
    assert b == 1 and d == D_MODEL and w_ada.shape[0] == 1
    h = x[0]
    for l in range(w_ada.shape[0]):
        mod = _modulation(c[0], w_ada[l], b_ada[l])
        h = _layer(h, mod, g_norm1[l], w_in[l], conv_w[l], a_log[l], dt_bias[l], g_onorm[l], b_fgate[l],
                   w_o_delta[l], w_o_fox[l], w_out[l], g_norm2[l], w_router[l], b_router[l], w_gate_up[l],
                   b_gate_up[l], w_down[l], b_down[l], g_final)
    return h[None]
```

```python
import functools

import jax
import jax.numpy as jnp
from jax import lax
from jax.experimental import pallas as pl
from jax.experimental.pallas import tpu as pltpu

F32 = jnp.float32
BF16 = jnp.bfloat16
I32 = jnp.int32
HIGHEST = lax.Precision.HIGHEST

D_MODEL = 1024
HEAD_DIM = 128
N_HEADS = 4
MIX_W = N_HEADS * HEAD_DIM
QKV_W = 3 * MIX_W
CONV_K = 4
CHUNK = 64
STACK = N_HEADS * CHUNK
SUB = 16
N_EXPERTS = 32
TOP_K = 4
FF = D_MODEL
SWIGLU_LIMIT = 7.0
SWIGLU_ALPHA = 1.702
RMS_EPS = 1e-6
L2_EPS = 1e-6
N_MOD = 6
LANES = 128
NEG_BIG = -1e30
LOG2E = 1.4426950408889634

ROW_TILE = 512
GATES_TILE = 256
DELTA_TILE = 256
EXPERT_BLOCK = 512
ROUTE_TILE = 256
VMEM_LIMIT = 56 * 1024 * 1024


def _params(sem, vmem=VMEM_LIMIT, flags=None):
    return pltpu.CompilerParams(dimension_semantics=sem, vmem_limit_bytes=vmem, flags=flags)


def _softplus(x):
    return jnp.maximum(x, 0.0) + jnp.log(1.0 + jnp.exp(-jnp.abs(x)))


def _sigmoid(x):
    return 1.0 / (1.0 + jnp.exp(-x))


def _silu(x):
    return x * _sigmoid(x)


def _dot(a, b, precision=None):
    return jnp.dot(a, b, preferred_element_type=F32, precision=precision)


def _dot_nt(a, b, precision=None):
    return lax.dot_general(a, b, (((1,), (1,)), ((), ())), preferred_element_type=F32, precision=precision)


def _mod_kernel(c_ref, w_ref, b_ref, o_ref):
    o_ref[...] = jnp.sum(c_ref[...] * w_ref[...], axis=0, keepdims=True) + b_ref[...]


def _modulation(c, w_ada, b_ada):
    d, n = w_ada.shape
    tn = 1024
    return pl.pallas_call(
        _mod_kernel,
        grid=(n // tn,),
        in_specs=[pl.BlockSpec((d, 1), lambda j: (0, 0)),
                  pl.BlockSpec((d, tn), lambda j: (0, j)),
                  pl.BlockSpec((1, tn), lambda j: (0, j))],
        out_specs=pl.BlockSpec((1, tn), lambda j: (0, j)),
        out_shape=jax.ShapeDtypeStruct((1, n), F32),
        compiler_params=_params(("arbitrary",)),
        name="mod",
    )(c.reshape(d, 1), w_ada, b_ada.reshape(1, n))


def _inproj_kernel(x_ref, g_ref, sc_ref, sh_ref, wq_ref, wz_ref, wf_ref, wga_ref, wgb_ref, ws_ref,
                   oq_ref, oz_ref, of_ref, oga_ref, ogb_ref, os_ref):
    x = x_ref[...]
    y = x * lax.rsqrt(jnp.mean(x * x, axis=-1, keepdims=True) + RMS_EPS) * g_ref[...]
    h = (y * (1.0 + sc_ref[...]) + sh_ref[...]).astype(BF16)
    oq_ref[...] = _dot_nt(h, wq_ref[...])
    oz_ref[...] = _dot_nt(h, wz_ref[...]).astype(BF16)
    of_ref[...] = _dot_nt(h, wf_ref[...]).astype(BF16)
    oga_ref[...] = _dot_nt(h, wga_ref[...]).astype(BF16)
    ogb_ref[...] = _dot_nt(h, wgb_ref[...]).astype(BF16)
    os_ref[...] = _dot_nt(h, ws_ref[...])


def _inproj(x, g1, sc1, sh1, wq, wz, wf, wga, wgb, ws, tm):
    t, d = x.shape
    row = lambda i: (i, 0)
    fixed = lambda i: (0, 0)
    ws_list = [wq, wz, wf, wga, wgb, ws]
    out_dt = [F32, BF16, BF16, BF16, BF16, F32]
    return pl.pallas_call(
        _inproj_kernel,
        grid=(t // tm,),
        in_specs=[pl.BlockSpec((tm, d), row)] + [pl.BlockSpec((1, d), fixed)] * 3
                 + [pl.BlockSpec(w.shape, fixed) for w in ws_list],
        out_specs=[pl.BlockSpec((tm, w.shape[0]), row) for w in ws_list],
        out_shape=[jax.ShapeDtypeStruct((t, w.shape[0]), dt) for w, dt in zip(ws_list, out_dt)],
        compiler_params=_params(("arbitrary",)),
        name="inproj",
    )(x, g1, sc1, sh1, *ws_list)


L_BETA, L_G, L_F, L_GC, L_GL = 0, 4, 8, 12, 16


def _gates_kernel(s_ref, alog_ref, dtb_ref, bf_ref, col_ref, row_ref, carry_ref):
    i = pl.program_id(0)
    tm = s_ref.shape[0]

    @pl.when(i == 0)
    def _():
        carry_ref[...] = jnp.zeros_like(carry_ref)

    s = s_ref[...]
    lane = lax.broadcasted_iota(I32, s.shape, 1)
    beta = _sigmoid(s)
    g = -jnp.exp(alog_ref[...]) * _softplus(s + dtb_ref[...])
    logf = -_softplus(-(s + bf_ref[...]))
    is_g = (lane >= L_G) & (lane < L_G + N_HEADS)
    is_f = (lane >= L_F) & (lane < L_F + N_HEADS)
    g = jnp.where(is_g, g, 0.0)
    logf = jnp.where(is_f, logf, 0.0)

    r = lax.broadcasted_iota(I32, (tm, tm), 0)
    c = lax.broadcasted_iota(I32, (tm, tm), 1)
    same_chunk = (r // CHUNK) == (c // CHUNK)
    tri = (r >= c).astype(F32)
    tri_chunk = jnp.where(same_chunk & (r >= c), 1.0, 0.0)
    ones_chunk = jnp.where(same_chunk, 1.0, 0.0)
    f_cum = _dot(tri, logf, HIGHEST) + carry_ref[...]
    carry_ref[...] = f_cum[tm - 1:tm, :]
    gc = _dot(tri_chunk, g, HIGHEST)
    gl = _dot(ones_chunk, g, HIGHEST)

    out = jnp.where(lane < N_HEADS, beta, 0.0) + g + f_cum
    out = out + pltpu.roll(gc, L_GC - L_G, 1) + pltpu.roll(gl, L_GL - L_G, 1)
    col_ref[...] = out
    row_ref[...] = out.T[:row_ref.shape[0], :]


def _gates(small, alog_row, dtb_row, bf_row, tm):
    t = small.shape[0]
    n_rows = 24
    return pl.pallas_call(
        _gates_kernel,
        grid=(t // tm,),
        in_specs=[pl.BlockSpec((tm, LANES), lambda i: (i, 0))] + [pl.BlockSpec((1, LANES), lambda i: (0, 0))] * 3,
        out_specs=[pl.BlockSpec((tm, LANES), lambda i: (i, 0)), pl.BlockSpec((n_rows, tm), lambda i: (0, i))],
        out_shape=[jax.ShapeDtypeStruct((t, LANES), F32), jax.ShapeDtypeStruct((n_rows, t), F32)],
        scratch_shapes=[pltpu.VMEM((1, LANES), F32)],
        compiler_params=_params(("arbitrary",)),
        name="gates",
    )(small, alog_row, dtb_row, bf_row)


def _stack_heads(a, col0):
    return jnp.concatenate([a[:, col0 + h * HEAD_DIM: col0 + (h + 1) * HEAD_DIM] for h in range(N_HEADS)], axis=0)


def _stack_cols(a, lane0):
    return jnp.concatenate([a[:, lane0 + h: lane0 + h + 1] for h in range(N_HEADS)], axis=0)


def _delta_kernel(qkv_ref, z_ref, gcol_ref, grow_ref, cw_ref, gon_ref, o_ref, ext_ref, tail_ref, s_ref):
    i = pl.program_id(0)
    tb = qkv_ref.shape[0]
    pad = tail_ref.shape[0]

    @pl.when(i == 0)
    def _():
        tail_ref[...] = jnp.zeros_like(tail_ref)
        s_ref[...] = jnp.zeros_like(s_ref)

    ext_ref[0:pad, :] = tail_ref[...]
    ext_ref[pad:pad + tb, :] = qkv_ref[...]
    tail_ref[...] = qkv_ref[tb - pad:tb, :]
    conv = cw_ref[0:1, :] * ext_ref[pad - 3:pad - 3 + tb, :]
    for j in range(1, CONV_K):
        conv = conv + cw_ref[j:j + 1, :] * ext_ref[pad - 3 + j:pad - 3 + j + tb, :]
    act = _silu(conv)

    r = lax.broadcasted_iota(I32, (STACK, STACK), 0)
    c = lax.broadcasted_iota(I32, (STACK, STACK), 1)
    same = (r // CHUNK) == (c // CHUNK)
    m_incl = same & (r >= c)
    m_strict = same & (r > c)
    eye = jnp.where(r == c, 1.0, 0.0)
    m_sub = []
    size = SUB
    while size <= CHUNK:
        m_sub.append((r // size) == (c // size))
        size *= 2
    rb = lax.broadcasted_iota(I32, (STACK, N_HEADS * HEAD_DIM), 0) // CHUNK
    cb = lax.broadcasted_iota(I32, (STACK, N_HEADS * HEAD_DIM), 1) // HEAD_DIM
    head_match = rb == cb

    chunks = range(tb // CHUNK)
    pre = []
    for ch in chunks:
        r0 = ch * CHUNK
        a = act[r0:r0 + CHUNK, :]
        q = _stack_heads(a, 0)
        k = _stack_heads(a, MIX_W)
        v = _stack_heads(a, 2 * MIX_W)
        q = q * lax.rsqrt(jnp.sum(q * q, axis=-1, keepdims=True) + L2_EPS) * (HEAD_DIM ** -0.5)
        k = k * lax.rsqrt(jnp.sum(k * k, axis=-1, keepdims=True) + L2_EPS)

        gcols = gcol_ref[r0:r0 + CHUNK, :]
        beta = _stack_cols(gcols, L_BETA)
        gc = _stack_cols(gcols, L_GC)
        gc_row = jnp.concatenate(
            [grow_ref[L_GC + h:L_GC + h + 1, r0:r0 + CHUNK] for h in range(N_HEADS)], axis=1)
        gl_row = jnp.concatenate(
            [grow_ref[L_GL + h:L_GL + h + 1, r0:r0 + CHUNK] for h in range(N_HEADS)], axis=1)

        decay = jnp.exp(jnp.where(m_incl, gc - gc_row, NEG_BIG))
        kb16 = k.astype(BF16)
        kk = _dot_nt(kb16, kb16)
        lmat = jnp.where(m_strict, kk * beta * decay, 0.0)
        attn = (_dot_nt(q.astype(BF16), kb16) * decay).astype(BF16)
        egc = jnp.exp(gc)
        rhs = jnp.concatenate([v * beta, k * (beta * egc)], axis=1).astype(BF16)
        kt_dec = (k.T * jnp.exp(gl_row - gc_row)).astype(BF16)
        pre.append(dict(lmat=lmat, attn=attn, rhs=rhs, kt_dec=kt_dec, q_dec=q * egc))

    l_d = [jnp.where(m_sub[0], p["lmat"], 0.0) for p in pre]
    inv = [eye - l for l in l_d]
    l16 = [l.astype(BF16) for l in l_d]
    pw = [_dot(l, l).astype(BF16) for l in l16]
    n_sq = SUB.bit_length() - 3
    for s in range(n_sq + 1):
        inv = [x + _dot(x.astype(BF16), p) for x, p in zip(inv, pw)]
        if s < n_sq:
            pw = [_dot(p, p).astype(BF16) for p in pw]
    for lvl in range(1, len(m_sub)):
        m_off = m_sub[lvl] & ~m_sub[lvl - 1]
        off = [jnp.where(m_off, p["lmat"], 0.0).astype(BF16) for p in pre]
        inv16 = [x.astype(BF16) for x in inv]
        half = [_dot(x, o).astype(BF16) for x, o in zip(inv16, off)]
        inv = [x - _dot(h, x16) for x, h, x16 in zip(inv, half, inv16)]
    uws = [_dot(x.astype(BF16), p["rhs"]) for x, p in zip(inv, pre)]

    for ch in chunks:
        r0 = ch * CHUNK
        attn, kt_dec = pre[ch]["attn"], pre[ch]["kt_dec"]
        u, w = uws[ch][:, :HEAD_DIM], uws[ch][:, HEAD_DIM:]

        state16 = s_ref[...].astype(BF16)
        wq_s = _dot(jnp.concatenate([w, pre[ch]["q_dec"]], axis=0).astype(BF16), state16)
        ws_d = jnp.concatenate([wq_s[h * CHUNK:(h + 1) * CHUNK, h * HEAD_DIM:(h + 1) * HEAD_DIM]
                                for h in range(N_HEADS)], axis=0)
        qs_d = jnp.concatenate([wq_s[STACK + h * CHUNK:STACK + (h + 1) * CHUNK, h * HEAD_DIM:(h + 1) * HEAD_DIM]
                                for h in range(N_HEADS)], axis=0)
        v_new = u - ws_d
        v16 = v_new.astype(BF16)
        o = qs_d + _dot(attn, v16)

        v_bd =jnp.where(head_match, jnp.concatenate([v_new] * N_HEADS, axis=1), 0.0).astype(BF16)
        upd = _dot(kt_dec, v_bd)
        for h in range(N_HEADS):
            e_h = jnp.exp(grow_ref[L_GL + h:L_GL + h + 1, r0:r0 + 1])
            sl = slice(h * HEAD_DIM, (h + 1) * HEAD_DIM)
            s_ref[:, sl] = s_ref[:, sl] * e_h + upd[:, sl]

        for h in range(N_HEADS):
            oh = o[h * CHUNK:(h + 1) * CHUNK, :]
            oh = oh * lax.rsqrt(jnp.mean(oh * oh, axis=-1, keepdims=True) + RMS_EPS) * gon_ref[...]
            zh = z_ref[r0:r0 + CHUNK, h * HEAD_DIM:(h + 1) * HEAD_DIM].astype(F32)
            o_ref[r0:r0 + CHUNK, h * HEAD_DIM:(h + 1) * HEAD_DIM] = (oh * _silu(zh)).astype(o_ref.dtype)


def _delta(qkv, z, gcol, grow, conv_w, g_onorm, tb):
    t = qkv.shape[0]
    pad = 8
    return pl.pallas_call(
        _delta_kernel,
        grid=(t // tb,),
        in_specs=[pl.BlockSpec((tb, QKV_W), lambda i: (i, 0)),
                  pl.BlockSpec((tb, MIX_W), lambda i: (i, 0)),
                  pl.BlockSpec((tb, LANES), lambda i: (i, 0)),
                  pl.BlockSpec((grow.shape[0], tb), lambda i: (0, i)),
                  pl.BlockSpec((CONV_K, QKV_W), lambda i: (0, 0)),
                  pl.BlockSpec((1, HEAD_DIM), lambda i: (0, 0))],
        out_specs=pl.BlockSpec((tb, MIX_W), lambda i: (i, 0)),
        out_shape=jax.ShapeDtypeStruct((t, MIX_W), BF16),
        scratch_shapes=[pltpu.VMEM((tb + pad, QKV_W), F32),
                        pltpu.VMEM((pad, QKV_W), F32),
                        pltpu.VMEM((HEAD_DIM, N_HEADS * HEAD_DIM), F32)],
        compiler_params=_params(("arbitrary",)),
        name="delta",
    )(qkv, z, gcol, grow, conv_w, g_onorm)


FOX_GROUP = 4
FOX_TQ, FOX_TK = 256, 2048


def _fox_kernel(*refs, tk):
    g = FOX_GROUP
    q_ref, o_ref = refs[0], refs[-1]
    k_refs, v_refs, f_refs = (refs[1 + i * g:1 + (i + 1) * g] for i in range(3))
    qi = pl.program_id(1)
    tq = q_ref.shape[0]
    ones = jnp.ones((tk, HEAD_DIM), BF16)

    def step(j, carry, mask):
        off = pl.multiple_of(j * tk, tk)
        ss = []
        for h in range(g):
            q = q_ref[:, h * HEAD_DIM:(h + 1) * HEAD_DIM]
            s = _dot_nt(q, k_refs[h][pl.ds(off, tk), :]) - f_refs[h][j] * LOG2E
            if mask is not None:
                s = jnp.where(mask, s, NEG_BIG)
            ss.append(s)
        ps, ms, alphas = [], [], []
        for h in range(g):
            m = carry[h][0]
            m_new = jnp.maximum(m, jnp.max(ss[h], axis=-1, keepdims=True))
            alphas.append(jnp.exp2(m - m_new))
            ps.append(jnp.exp2(ss[h] - m_new).astype(BF16))
            ms.append(m_new)
        out = []
        for h in range(g):
            v_aug = jnp.concatenate([v_refs[h][pl.ds(off, tk), :], ones], axis=1)
            out.append((ms[h], alphas[h] * carry[h][1] + _dot(ps[h], v_aug)))
        return tuple(out)

    init = tuple((jnp.full((tq, 1), NEG_BIG, F32), jnp.zeros((tq, 2 * HEAD_DIM), F32)) for _ in range(g))
    n_full = (qi * tq) // tk
    carry = lax.fori_loop(0, n_full, lambda j, c: step(j, c, None), init)
    row = lax.broadcasted_iota(I32, (tq, tk), 0)
    col = lax.broadcasted_iota(I32, (tq, tk), 1)
    for d in range(max(tq // tk, 1)):
        j = n_full + d
        carry = step(j, carry, col + (j * tk - qi * tq) <= row)
    for h in range(g):
        acc = carry[h][1]
        o_ref[:, h * HEAD_DIM:(h + 1) * HEAD_DIM] = (acc[:, :HEAD_DIM] / acc[:, HEAD_DIM:]).astype(o_ref.dtype)


def _fox(fx, f_rows, tq, tk):
    t = fx.shape[0]
    g = FOX_GROUP
    heads = lambda hg, u: hg * g + u
    q_specs = [pl.BlockSpec((tq, g * HEAD_DIM), lambda hg, i: (i, hg))]
    k_specs = [pl.BlockSpec((t, HEAD_DIM), lambda hg, i, u=u: (0, N_HEADS + heads(hg, u)),
                            pipeline_mode=pl.Buffered(1)) for u in range(g)]
    v_specs = [pl.BlockSpec((t, HEAD_DIM), lambda hg, i, u=u: (0, 2 * N_HEADS + heads(hg, u)),
                            pipeline_mode=pl.Buffered(1)) for u in range(g)]
    f_specs = [pl.BlockSpec((None, t // tk, 1, tk), lambda hg, i, u=u: (heads(hg, u), 0, 0, 0)) for u in range(g)]
    return pl.pallas_call(
        functools.partial(_fox_kernel, tk=tk),
        grid=(N_HEADS // g, t // tq),
        in_specs=q_specs + k_specs + v_specs + f_specs,
        out_specs=pl.BlockSpec((tq, g * HEAD_DIM), lambda hg, i: (i, hg)),
        out_shape=jax.ShapeDtypeStruct((t, MIX_W), BF16),
        compiler_params=_params(("arbitrary", "arbitrary")),
        name="fox",
    )(*([fx] * (1 + 2 * g) + [f_rows] * g))


def _merge_kernel(oa_ref, ob_ref, ga_ref, gb_ref, x_ref, woa_ref, wob_ref, wout_ref, gt1_ref, g2_ref, sc2_ref,
                  sh2_ref, wr_ref, br_ref, x1_ref, h2_ref, ri_ref, gate_ref, cnt_ref, carry_ref):
    i = pl.program_id(0)
    tm = x_ref.shape[0]

    @pl.when(i == 0)
    def _():
        carry_ref[...] = jnp.zeros_like(carry_ref)

    ya = _dot(oa_ref[...], woa_ref[...])
    yb = _dot(ob_ref[...], wob_ref[...])
    merged = _sigmoid(ga_ref[...].astype(F32)) * ya + _sigmoid(gb_ref[...].astype(F32)) * yb
    x1 = x_ref[...] + gt1_ref[...] * _dot(merged.astype(BF16), wout_ref[...])
    x1_ref[...] = x1
    y = x1 * lax.rsqrt(jnp.mean(x1 * x1, axis=-1, keepdims=True) + RMS_EPS) * g2_ref[...]
    h2 = y * (1.0 + sc2_ref[...]) + sh2_ref[...]
    h2_ref[...] = h2

    logits = _dot(h2, wr_ref[...], HIGHEST) + br_ref[...]
    lane = lax.broadcasted_iota(I32, logits.shape, 1)
    lanef = lane.astype(F32)
    cur = jnp.where(lane < N_EXPERTS, logits, -jnp.inf)
    vals, idxs = [], []
    for _ in range(TOP_K):
        m = jnp.max(cur, axis=-1, keepdims=True)
        ix = jnp.min(jnp.where(cur == m, lanef, float(LANES)), axis=-1, keepdims=True)
        vals.append(m)
        idxs.append(ix)
        cur = jnp.where(lanef == ix, -jnp.inf, cur)
    exps = [jnp.exp(v - vals[0]) for v in vals]
    denom = exps[0] + exps[1] + exps[2] + exps[3]

    onehot = jnp.zeros(logits.shape, F32)
    for ix in idxs:
        onehot = onehot + jnp.where(lanef == ix, 1.0, 0.0)
    r = lax.broadcasted_iota(I32, (tm, tm), 0)
    c = lax.broadcasted_iota(I32, (tm, tm), 1)
    strict = jnp.where(r > c, 1.0, 0.0).astype(BF16)
    before = _dot(strict, onehot.astype(BF16)) + carry_ref[...]
    carry_ref[...] = carry_ref[...] + jnp.sum(onehot, axis=0, keepdims=True)
    cnt_ref[...] = carry_ref[...].astype(I32)

    ri = jnp.zeros(logits.shape, F32)
    gates = jnp.zeros(logits.shape, F32)
    for kk in range(TOP_K):
        rank = jnp.sum(jnp.where(lanef == idxs[kk], before, 0.0), axis=-1, keepdims=True)
        ri = ri + jnp.where(lane == kk, idxs[kk], 0.0) + jnp.where(lane == TOP_K + kk, rank, 0.0)
        gates = gates + jnp.where(lane == kk, exps[kk] / denom, 0.0)
    ri_ref[...] = ri.astype(I32)
    gate_ref[...] = gates


def _merge(o_a, o_b, gate_a, gate_b, x, woa, wob, wout, gt1, g2, sc2, sh2, wr, br, tm):
    t, d = x.shape
    row = lambda i: (i, 0)
    fixed = lambda i: (0, 0)
    return pl.pallas_call(
        _merge_kernel,
        grid=(t // tm,),
        in_specs=[pl.BlockSpec((tm, MIX_W), row), pl.BlockSpec((tm, MIX_W), row),
                  pl.BlockSpec((tm, d), row), pl.BlockSpec((tm, d), row), pl.BlockSpec((tm, d), row),
                  pl.BlockSpec((MIX_W, d), fixed), pl.BlockSpec((MIX_W, d), fixed), pl.BlockSpec((d, d), fixed),
                  pl.BlockSpec((1, d), fixed), pl.BlockSpec((1, d), fixed), pl.BlockSpec((1, d), fixed),
                  pl.BlockSpec((1, d), fixed), pl.BlockSpec((d, LANES), fixed), pl.BlockSpec((1, LANES), fixed)],
        out_specs=[pl.BlockSpec((tm, d), row), pl.BlockSpec((tm, d), row), pl.BlockSpec((tm, LANES), row),
                   pl.BlockSpec((tm, LANES), row), pl.BlockSpec((1, LANES), fixed)],
        out_shape=[jax.ShapeDtypeStruct((t, d), F32), jax.ShapeDtypeStruct((t, d), F32),
                   jax.ShapeDtypeStruct((t, LANES), I32), jax.ShapeDtypeStruct((t, LANES), F32),
                   jax.ShapeDtypeStruct((1, LANES), I32)],
        scratch_shapes=[pltpu.VMEM((1, LANES), F32)],
        compiler_params=_params(("arbitrary",)),
        name="merge",
    )(o_a, o_b, gate_a, gate_b, x, woa, wob, wout, gt1, g2, sc2, sh2, wr, br)


def _dest_kernel(ri_ref, cnt_ref, dest_ref, blk_ref, pend_ref):
    shift = EXPERT_BLOCK.bit_length() - 1
    cnt = jnp.broadcast_to(cnt_ref[...], (8, LANES))
    lane_row = lax.broadcasted_iota(I32, cnt.shape, 1)
    padded = jnp.where(lane_row < N_EXPERTS, ((cnt + (EXPERT_BLOCK - 1)) >> shift) << shift, 0)
    pend = padded
    s = 1
    while s < N_EXPERTS:
        pend = pend + jnp.where(lane_row >= s, pltpu.roll(pend, s, 1), 0)
        s *= 2
    pstart = (pend - padded)[0:1, :].astype(F32)
    pend = pend[0:1, :]
    pend_ref[...] = pend

    ri = ri_ref[...]
    lane = lax.broadcasted_iota(I32, ri.shape, 1)
    dest = jnp.zeros(ri.shape, F32)
    for kk in range(TOP_K):
        ix = ri[:, kk:kk + 1]
        rank = ri[:, TOP_K + kk:TOP_K + kk + 1].astype(F32)
        start = jnp.sum(jnp.where(lane == ix, pstart, 0.0), axis=-1, keepdims=True)
        dest = dest + jnp.where(lane == kk, start + rank, 0.0)
    dest_ref[...] = dest.astype(I32)

    bstart = lax.broadcasted_iota(I32, blk_ref.shape, 0) * EXPERT_BLOCK
    lane_b = lax.broadcasted_iota(I32, blk_ref.shape, 1)
    ended = jnp.where((lane_b < N_EXPERTS) & (pend <= bstart), 1.0, 0.0)
    e = jnp.minimum(jnp.sum(ended, axis=-1, keepdims=True), float(N_EXPERTS - 1))
    vend = pstart + cnt[0:1, :].astype(F32)
    vend_e = jnp.sum(jnp.where(lane_b.astype(F32) == e, vend, 0.0), axis=-1, keepdims=True)
    nvalid = jnp.clip(vend_e - bstart[:, 0:1].astype(F32), 0.0, float(EXPERT_BLOCK))
    blk_ref[...] = jnp.where(lane_b == 0, e, jnp.where(lane_b == 1, nvalid, 0.0)).astype(I32)


def _dest(ri, cnt, n_blocks, tm):
    t = ri.shape[0]
    nb_pad = -(-n_blocks // 8) * 8
    return pl.pallas_call(
        _dest_kernel,
        grid=(t // tm,),
        in_specs=[pl.BlockSpec((tm, LANES), lambda i: (i, 0)), pl.BlockSpec((1, LANES), lambda i: (0, 0))],
        out_specs=[pl.BlockSpec((tm, LANES), lambda i: (i, 0)), pl.BlockSpec((nb_pad, LANES), lambda i: (0, 0)),
                   pl.BlockSpec((1, LANES), lambda i: (0, 0))],
        out_shape=[jax.ShapeDtypeStruct((t, LANES), I32), jax.ShapeDtypeStruct((nb_pad, LANES), I32),
                   jax.ShapeDtypeStruct((1, LANES), I32)],
        compiler_params=_params(("arbitrary",)),
        name="dest",
    )(ri, cnt)


def _row_copy(src_ref, src_row, dst_ref, dst_row, sem):
    return pltpu.make_async_copy(src_ref.at[pl.ds(src_row, 1), :], dst_ref.at[pl.ds(dst_row, 1), :], sem)


ROW_UNROLL = 8


def _drain_rows(src_ref, dst_ref, sem, n_rows):
    def drain(_, carry):
        for _ in range(ROW_UNROLL * TOP_K):
            _row_copy(src_ref, 0, dst_ref, 0, sem).wait()
        return carry

    lax.fori_loop(0, n_rows // ROW_UNROLL, drain, 0)


def _dispatch_kernel(dest_ref, h_ref, xbuf_ref, sem):
    tm = h_ref.shape[0]

    def issue(g, carry):
        for u in range(ROW_UNROLL):
            r = g * ROW_UNROLL + u
            for kk in range(TOP_K):
                _row_copy(h_ref, r, xbuf_ref, dest_ref[r * TOP_K + kk], sem).start(priority=kk % 2)
        return carry

    lax.fori_loop(0, tm // ROW_UNROLL, issue, 0)
    _drain_rows(h_ref, xbuf_ref, sem, tm)


def _dispatch(dest_flat, h2, n_rows):
    t, d = h2.shape
    tm = ROUTE_TILE
    return pl.pallas_call(
        _dispatch_kernel,
        grid=(t // tm,),
        in_specs=[pl.BlockSpec((tm * TOP_K,), lambda i: (i,), memory_space=pltpu.SMEM),
                  pl.BlockSpec((tm, d), lambda i: (i, 0))],
        out_specs=pl.BlockSpec(memory_space=pl.ANY),
        out_shape=jax.ShapeDtypeStruct((n_rows, d), h2.dtype),
        scratch_shapes=[pltpu.SemaphoreType.DMA(())],
        compiler_params=_params(("arbitrary",)),
        name="dispatch",
    )(dest_flat, h2)


def _experts_kernel(blk_e_ref, nvalid_ref, x_ref, wgu_ref, bgu_ref, wd_ref, bd_ref, y_ref, wgu16_ref, wd16_ref):
    b = pl.program_id(0)
    nvalid = nvalid_ref[b]

    @pl.when((b == 0) | (blk_e_ref[b] != blk_e_ref[jnp.maximum(b - 1, 0)]))
    def _():
        wgu16_ref[...] = wgu_ref[...].astype(BF16)
        wd16_ref[...] = wd_ref[...].astype(BF16)

    @pl.when(nvalid > 0)
    def _():
        rows = lax.broadcasted_iota(I32, x_ref.shape, 0)
        x = jnp.where(rows < nvalid, x_ref[...], 0.0).astype(BF16)
        gu = _dot(x, wgu16_ref[...]) + bgu_ref[...]
        gate = jnp.minimum(gu[:, :FF], SWIGLU_LIMIT)
        up = jnp.clip(gu[:, FF:], -SWIGLU_LIMIT, SWIGLU_LIMIT)
        act = (up + 1.0) * (gate * _sigmoid(SWIGLU_ALPHA * gate))
        y_ref[...] = _dot(act.astype(BF16), wd16_ref[...]) + bd_ref[...]

    @pl.when(nvalid <= 0)
    def _():
        y_ref[...] = jnp.zeros_like(y_ref)


def _experts(blk_e, nvalid, xbuf, wgu, bgu, wd, bd):
    n_rows, d = xbuf.shape
    nb = n_rows // EXPERT_BLOCK
    grid_spec = pltpu.PrefetchScalarGridSpec(
        num_scalar_prefetch=2,
        grid=(nb,),
        in_specs=[pl.BlockSpec((EXPERT_BLOCK, d), lambda b, e, n: (b, 0)),
                  pl.BlockSpec((None, d, 2 * FF), lambda b, e, n: (e[b], 0, 0)),
                  pl.BlockSpec((None, 1, 2 * FF), lambda b, e, n: (e[b], 0, 0)),
                  pl.BlockSpec((None, FF, d), lambda b, e, n: (e[b], 0, 0)),
                  pl.BlockSpec((None, 1, d), lambda b, e, n: (e[b], 0, 0))],
        out_specs=pl.BlockSpec((EXPERT_BLOCK, d), lambda b, e, n: (b, 0)),
        scratch_shapes=[pltpu.VMEM((d, 2 * FF), BF16), pltpu.VMEM((FF, d), BF16)],
    )
    return pl.pallas_call(
        _experts_kernel,
        grid_spec=grid_spec,
        out_shape=jax.ShapeDtypeStruct((n_rows, d), F32),
        compiler_params=_params(("arbitrary",)),
        name="experts",
    )(blk_e, nvalid, xbuf, wgu, bgu, wd, bd)


def _combine_kernel(dest_ref, dest_next_ref, ybuf_ref, x1_ref, gate_ref, gt2_ref, gf_ref, o_ref, rows_ref, sems):
    i = pl.program_id(0)
    tm = x1_ref.shape[0]
    slot = i % 2

    def start_gather(idx_ref, s):
        def issue(g, carry):
            for u in range(ROW_UNROLL):
                r = g * ROW_UNROLL + u
                for kk in range(TOP_K):
                    _row_copy(ybuf_ref, idx_ref[r * TOP_K + kk], rows_ref.at[s, kk], r,
                              sems.at[s]).start(priority=kk % 2)
            return carry

        lax.fori_loop(0, tm // ROW_UNROLL, issue, 0)

    @pl.when(i == 0)
    def _():
        start_gather(dest_ref, 0)

    @pl.when(i + 1 < pl.num_programs(0))
    def _():
        start_gather(dest_next_ref, 1 - slot)

    _drain_rows(ybuf_ref, rows_ref.at[slot, 0], sems.at[slot], tm)

    gates = gate_ref[...]
    moe = gates[:, 0:1] * rows_ref[slot, 0]
    for kk in range(1, TOP_K):
        moe = moe + gates[:, kk:kk + 1] * rows_ref[slot, kk]
    xo = x1_ref[...] + gt2_ref[...] * moe
    o_ref[...] = xo * lax.rsqrt(jnp.mean(xo * xo, axis=-1, keepdims=True) + RMS_EPS) * gf_ref[...]


def _combine(dest_flat, ybuf, x1, gates, gt2, g_final):
    t, d = x1.shape
    tm = ROUTE_TILE
    n = t // tm
    return pl.pallas_call(
        _combine_kernel,
        grid=(n,),
        in_specs=[pl.BlockSpec((tm * TOP_K,), lambda i: (i,), memory_space=pltpu.SMEM),
                  pl.BlockSpec((tm * TOP_K,), lambda i: (jnp.minimum(i + 1, n - 1),), memory_space=pltpu.SMEM),
                  pl.BlockSpec(memory_space=pl.ANY),
                  pl.BlockSpec((tm, d), lambda i: (i, 0)),
                  pl.BlockSpec((tm, LANES), lambda i: (i, 0)),
                  pl.BlockSpec((1, d), lambda i: (0, 0)),
                  pl.BlockSpec((1, d), lambda i: (0, 0))],
        out_specs=pl.BlockSpec((tm, d), lambda i: (i, 0)),
        out_shape=jax.ShapeDtypeStruct((t, d), F32),
        scratch_shapes=[pltpu.VMEM((2, TOP_K, tm, d), F32), pltpu.SemaphoreType.DMA((2,))],
        compiler_params=_params(("arbitrary",)),
        name="combine",
    )(dest_flat, dest_flat, ybuf, x1, gates, gt2, g_final)


def _pad_lanes(v, lane0):
    return jnp.zeros((1, LANES), F32).at[0, lane0:lane0 + v.shape[0]].set(v.astype(F32))


def _layer(x, mod, g_norm1, w_in, conv_w, a_log, dt_bias, g_onorm, b_fgate, w_o_delta, w_o_fox, w_out,
           g_norm2, w_router, b_router, w_gate_up, b_gate_up, w_down, b_down, g_final):
    t, d = x.shape
    sh1, sc1, gt1, sh2, sc2, gt2 = [mod[:, i * d:(i + 1) * d] for i in range(N_MOD)]

    w_t = w_in.T
    o = 0
    wq = w_t[o:o + QKV_W]; o += QKV_W
    wz = w_t[o:o + MIX_W]; o += MIX_W
    w_beta = w_t[o:o + N_HEADS]; o += N_HEADS
    w_dec = w_t[o:o + N_HEADS]; o += N_HEADS
    wf = w_t[o:o + QKV_W]; o += QKV_W
    wf = jnp.concatenate([wf[:MIX_W] * (LOG2E * HEAD_DIM ** -0.5), wf[MIX_W:]], axis=0)
    w_fg = w_t[o:o + N_HEADS]; o += N_HEADS
    wga = w_t[o:o + d]; o += d
    wgb = w_t[o:o + d]
    ws = jnp.zeros((LANES, d), F32)
    ws = ws.at[L_BETA:L_BETA + N_HEADS].set(w_beta).at[L_G:L_G + N_HEADS].set(w_dec)
    ws = ws.at[L_F:L_F + N_HEADS].set(w_fg)
    bf = lambda w: w.astype(BF16)

    tm = min(ROW_TILE, t)
    qkv, z, fx, gate_a, gate_b, small = _inproj(
        x, g_norm1.reshape(1, d), sc1, sh1, bf(wq), bf(wz), bf(wf), bf(wga), bf(wgb), bf(ws), tm)

    gcol, grow = _gates(small, _pad_lanes(a_log, L_G), _pad_lanes(dt_bias, L_G), _pad_lanes(b_fgate, L_F),
                        min(GATES_TILE, t))

    o_a = _delta(qkv, z, gcol, grow, conv_w, g_onorm.reshape(1, HEAD_DIM), min(DELTA_TILE, t))
    tq, tk = min(FOX_TQ, t), min(FOX_TK, t)
    f_rows = grow[L_F:L_F + N_HEADS].reshape(N_HEADS, t // tk, 1, tk)
    o_b = _fox(fx, f_rows, tq, tk)

    wr = jnp.zeros((d, LANES), F32).at[:, :N_EXPERTS].set(w_router)
    br = _pad_lanes(b_router, 0)
    x1, h2, ri, gates, cnt = _merge(o_a, o_b, gate_a, gate_b, x, bf(w_o_delta), bf(w_o_fox), bf(w_out), gt1,
                                    g_norm2.reshape(1, d), sc2, sh2, wr, br, tm)

    n_blocks = (t * TOP_K) // EXPERT_BLOCK + N_EXPERTS
    dest, blk, _ = _dest(ri, cnt, n_blocks, min(2048, t))
    dest_flat = dest[:, :TOP_K].reshape(t * TOP_K)
    blk_e, nvalid = blk[:n_blocks, 0], blk[:n_blocks, 1]

    xbuf = _dispatch(dest_flat, h2, n_blocks * EXPERT_BLOCK)
    ybuf = _experts(blk_e, nvalid, xbuf, w_gate_up, b_gate_up.reshape(N_EXPERTS, 1, 2 * FF),
                    w_down, b_down.reshape(N_EXPERTS, 1, d))
    return _combine(dest_flat, ybuf, x1, gates, gt2, g_final.reshape(1, d))


def kernel(x, c, w_ada, b_ada, g_norm1, w_in, conv_w, a_log, dt_bias, g_onorm, b_fgate, w_o_delta, w_o_fox, w_out,
           g_norm2, w_router, b_router, w_gate_up, b_gate_up, w_down, b_down, g_final):
    b, s, d = x.shape
    assert b == 1 and d == D_MODEL and w_ada.shape[0] == 1
    h = x[0]
    for l in range(w_ada.shape[0]):
        mod = _modulation(c[0], w_ada[l], b_ada[l])
        h = _layer(h, mod, g_norm1[l], w_in[l], conv_w[l], a_log[l], dt_bias[l], g_onorm[l], b_fgate[l],
                   w_o_delta[l], w_o_fox[l], w_out[l], g_norm2[l], w_router[l], b_router[l], w_gate_up[l],
                   b_gate_up[l], w_down[l], b_down[l], g_final)
    return h[None]
```

```python
import functools

import jax
import jax.numpy as jnp
from jax import lax
from jax.experimental import pallas as pl
from jax.experimental.pallas import tpu as pltpu

F32 = jnp.float32
BF16 = jnp.bfloat16
I32 = jnp.int32
HIGHEST = lax.Precision.HIGHEST

D_MODEL = 1024
HEAD_DIM = 128
N_HEADS = 4
MIX_W = N_HEADS * HEAD_DIM
QKV_W = 3 * MIX_W
CONV_K = 4
CHUNK = 64
STACK = N_HEADS * CHUNK
SUB = 16
N_EXPERTS = 32
TOP_K = 4
FF = D_MODEL
SWIGLU_LIMIT = 7.0
SWIGLU_ALPHA = 1.702
RMS_EPS = 1e-6
L2_EPS = 1e-6
N_MOD = 6
LANES = 128
NEG_BIG = -1e30
LOG2E = 1.4426950408889634

ROW_TILE = 512
GATES_TILE = 256
DELTA_TILE = 256
EXPERT_BLOCK = 512
ROUTE_TILE = 256
VMEM_LIMIT = 56 * 1024 * 1024


def _params(sem, vmem=VMEM_LIMIT, flags=None):
    return pltpu.CompilerParams(dimension_semantics=sem, vmem_limit_bytes=vmem, flags=flags)


def _softplus(x):
    return jnp.maximum(x, 0.0) + jnp.log(1.0 + jnp.exp(-jnp.abs(x)))


def _sigmoid(x):
    return 1.0 / (1.0 + jnp.exp(-x))


def _silu(x):
    return x * _sigmoid(x)


def _dot(a, b, precision=None):
    return jnp.dot(a, b, preferred_element_type=F32, precision=precision)


def _dot_nt(a, b, precision=None):
    return lax.dot_general(a, b, (((1,), (1,)), ((), ())), preferred_element_type=F32, precision=precision)


TOKEN_TILE_ROWS = D_MODEL // LANES


def _store_token_tiles(ref, x):
    n = x.shape[0]
    for s in range(TOKEN_TILE_ROWS):
        ref[pl.ds(s, n, stride=TOKEN_TILE_ROWS), :] = x[:, s * LANES:(s + 1) * LANES]


def _load_token_slab(ref, s, n):
    return ref[pl.ds(s, n, stride=TOKEN_TILE_ROWS), :]


def _token_copy(src_ref, src_tok, dst_ref, dst_tok, sem):
    rows = lambda i: pl.ds(pl.multiple_of(i * TOKEN_TILE_ROWS, TOKEN_TILE_ROWS), TOKEN_TILE_ROWS)
    return pltpu.make_async_copy(src_ref.at[rows(src_tok), :], dst_ref.at[rows(dst_tok), :], sem)


def _mod_kernel(c_ref, w_ref, b_ref, o_ref):
    o_ref[...] = jnp.sum(c_ref[...] * w_ref[...], axis=0, keepdims=True) + b_ref[...]


def _modulation(c, w_ada, b_ada):
    d, n = w_ada.shape
    tn = 1024
    return pl.pallas_call(
        _mod_kernel,
        grid=(n // tn,),
        in_specs=[pl.BlockSpec((d, 1), lambda j: (0, 0)),
                  pl.BlockSpec((d, tn), lambda j: (0, j)),
                  pl.BlockSpec((1, tn), lambda j: (0, j))],
        out_specs=pl.BlockSpec((1, tn), lambda j: (0, j)),
        out_shape=jax.ShapeDtypeStruct((1, n), F32),
        compiler_params=_params(("arbitrary",)),
        name="mod",
    )(c.reshape(d, 1), w_ada, b_ada.reshape(1, n))


def _inproj_kernel(x_ref, g_ref, sc_ref, sh_ref, wq_ref, wz_ref, wf_ref, wga_ref, wgb_ref, ws_ref,
                   oq_ref, oz_ref, of_ref, oga_ref, ogb_ref, os_ref):
    x = x_ref[...]
    y = x * lax.rsqrt(jnp.mean(x * x, axis=-1, keepdims=True) + RMS_EPS) * g_ref[...]
    h = (y * (1.0 + sc_ref[...]) + sh_ref[...]).astype(BF16)
    oq_ref[...] = _dot_nt(h, wq_ref[...])
    oz_ref[...] = _dot_nt(h, wz_ref[...]).astype(BF16)
    of_ref[...] = _dot_nt(h, wf_ref[...]).astype(BF16)
    oga_ref[...] = _dot_nt(h, wga_ref[...]).astype(BF16)
    ogb_ref[...] = _dot_nt(h, wgb_ref[...]).astype(BF16)
    os_ref[...] = _dot_nt(h, ws_ref[...])


def _inproj(x, g1, sc1, sh1, wq, wz, wf, wga, wgb, ws, tm):
    t, d = x.shape
    row = lambda i: (i, 0)
    fixed = lambda i: (0, 0)
    ws_list = [wq, wz, wf, wga, wgb, ws]
    out_dt = [F32, BF16, BF16, BF16, BF16, F32]
    return pl.pallas_call(
        _inproj_kernel,
        grid=(t // tm,),
        in_specs=[pl.BlockSpec((tm, d), row)] + [pl.BlockSpec((1, d), fixed)] * 3
                 + [pl.BlockSpec(w.shape, fixed) for w in ws_list],
        out_specs=[pl.BlockSpec((tm, w.shape[0]), row) for w in ws_list],
        out_shape=[jax.ShapeDtypeStruct((t, w.shape[0]), dt) for w, dt in zip(ws_list, out_dt)],
        compiler_params=_params(("arbitrary",)),
        name="inproj",
    )(x, g1, sc1, sh1, *ws_list)


L_BETA, L_G, L_F, L_GC, L_GL = 0, 4, 8, 12, 16


def _gates_kernel(s_ref, alog_ref, dtb_ref, bf_ref, col_ref, row_ref, carry_ref):
    i = pl.program_id(0)
    tm = s_ref.shape[0]

    @pl.when(i == 0)
    def _():
        carry_ref[...] = jnp.zeros_like(carry_ref)

    s = s_ref[...]
    lane = lax.broadcasted_iota(I32, s.shape, 1)
    beta = _sigmoid(s)
    g = -jnp.exp(alog_ref[...]) * _softplus(s + dtb_ref[...])
    logf = -_softplus(-(s + bf_ref[...]))
    is_g = (lane >= L_G) & (lane < L_G + N_HEADS)
    is_f = (lane >= L_F) & (lane < L_F + N_HEADS)
    g = jnp.where(is_g, g, 0.0)
    logf = jnp.where(is_f, logf, 0.0)

    r = lax.broadcasted_iota(I32, (tm, tm), 0)
    c = lax.broadcasted_iota(I32, (tm, tm), 1)
    same_chunk = (r // CHUNK) == (c // CHUNK)
    tri = (r >= c).astype(F32)
    tri_chunk = jnp.where(same_chunk & (r >= c), 1.0, 0.0)
    ones_chunk = jnp.where(same_chunk, 1.0, 0.0)
    f_cum = _dot(tri, logf, HIGHEST) + carry_ref[...]
    carry_ref[...] = f_cum[tm - 1:tm, :]
    gc = _dot(tri_chunk, g, HIGHEST)
    gl = _dot(ones_chunk, g, HIGHEST)

    out = jnp.where(lane < N_HEADS, beta, 0.0) + g + f_cum
    out = out + pltpu.roll(gc, L_GC - L_G, 1) + pltpu.roll(gl, L_GL - L_G, 1)
    col_ref[...] = out
    row_ref[...] = out.T[:row_ref.shape[0], :]


def _gates(small, alog_row, dtb_row, bf_row, tm):
    t = small.shape[0]
    n_rows = 24
    return pl.pallas_call(
        _gates_kernel,
        grid=(t // tm,),
        in_specs=[pl.BlockSpec((tm, LANES), lambda i: (i, 0))] + [pl.BlockSpec((1, LANES), lambda i: (0, 0))] * 3,
        out_specs=[pl.BlockSpec((tm, LANES), lambda i: (i, 0)), pl.BlockSpec((n_rows, tm), lambda i: (0, i))],
        out_shape=[jax.ShapeDtypeStruct((t, LANES), F32), jax.ShapeDtypeStruct((n_rows, t), F32)],
        scratch_shapes=[pltpu.VMEM((1, LANES), F32)],
        compiler_params=_params(("arbitrary",)),
        name="gates",
    )(small, alog_row, dtb_row, bf_row)


def _stack_heads(a, col0):
    return jnp.concatenate([a[:, col0 + h * HEAD_DIM: col0 + (h + 1) * HEAD_DIM] for h in range(N_HEADS)], axis=0)


def _stack_cols(a, lane0):
    return jnp.concatenate([a[:, lane0 + h: lane0 + h + 1] for h in range(N_HEADS)], axis=0)


def _delta_kernel(qkv_ref, z_ref, gcol_ref, grow_ref, cw_ref, gon_ref, o_ref, ext_ref, tail_ref, s_ref):
    i = pl.program_id(0)
    tb = qkv_ref.shape[0]
    pad = tail_ref.shape[0]

    @pl.when(i == 0)
    def _():
        tail_ref[...] = jnp.zeros_like(tail_ref)
        s_ref[...] = jnp.zeros_like(s_ref)

    ext_ref[0:pad, :] = tail_ref[...]
    ext_ref[pad:pad + tb, :] = qkv_ref[...]
    tail_ref[...] = qkv_ref[tb - pad:tb, :]
    conv = cw_ref[0:1, :] * ext_ref[pad - 3:pad - 3 + tb, :]
    for j in range(1, CONV_K):
        conv = conv + cw_ref[j:j + 1, :] * ext_ref[pad - 3 + j:pad - 3 + j + tb, :]
    act = _silu(conv)

    r = lax.broadcasted_iota(I32, (STACK, STACK), 0)
    c = lax.broadcasted_iota(I32, (STACK, STACK), 1)
    same = (r // CHUNK) == (c // CHUNK)
    m_incl = same & (r >= c)
    m_strict = same & (r > c)
    eye = jnp.where(r == c, 1.0, 0.0)
    m_sub = []
    size = SUB
    while size <= CHUNK:
        m_sub.append((r // size) == (c // size))
        size *= 2
    rb = lax.broadcasted_iota(I32, (STACK, N_HEADS * HEAD_DIM), 0) // CHUNK
    cb = lax.broadcasted_iota(I32, (STACK, N_HEADS * HEAD_DIM), 1) // HEAD_DIM
    head_match = rb == cb

    chunks = range(tb // CHUNK)
    pre = []
    for ch in chunks:
        r0 = ch * CHUNK
        a = act[r0:r0 + CHUNK, :]
        q = _stack_heads(a, 0)
        k = _stack_heads(a, MIX_W)
        v = _stack_heads(a, 2 * MIX_W)
        q = q * lax.rsqrt(jnp.sum(q * q, axis=-1, keepdims=True) + L2_EPS) * (HEAD_DIM ** -0.5)
        k = k * lax.rsqrt(jnp.sum(k * k, axis=-1, keepdims=True) + L2_EPS)

        gcols = gcol_ref[r0:r0 + CHUNK, :]
        beta = _stack_cols(gcols, L_BETA)
        gc = _stack_cols(gcols, L_GC)
        gc_row = jnp.concatenate(
            [grow_ref[L_GC + h:L_GC + h + 1, r0:r0 + CHUNK] for h in range(N_HEADS)], axis=1)
        gl_row = jnp.concatenate(
            [grow_ref[L_GL + h:L_GL + h + 1, r0:r0 + CHUNK] for h in range(N_HEADS)], axis=1)

        decay = jnp.exp(jnp.where(m_incl, gc - gc_row, NEG_BIG))
        kb16 = k.astype(BF16)
        kk = _dot_nt(kb16, kb16)
        lmat = jnp.where(m_strict, kk * beta * decay, 0.0)
        attn = (_dot_nt(q.astype(BF16), kb16) * decay).astype(BF16)
        egc = jnp.exp(gc)
        rhs = jnp.concatenate([v * beta, k * (beta * egc)], axis=1).astype(BF16)
        kt_dec = (k.T * jnp.exp(gl_row - gc_row)).astype(BF16)
        pre.append(dict(lmat=lmat, attn=attn, rhs=rhs, kt_dec=kt_dec, q_dec=q * egc))

    l_d = [jnp.where(m_sub[0], p["lmat"], 0.0) for p in pre]
    inv = [eye - l for l in l_d]
    l16 = [l.astype(BF16) for l in l_d]
    pw = [_dot(l, l).astype(BF16) for l in l16]
    n_sq = SUB.bit_length() - 3
    for s in range(n_sq + 1):
        inv = [x + _dot(x.astype(BF16), p) for x, p in zip(inv, pw)]
        if s < n_sq:
            pw = [_dot(p, p).astype(BF16) for p in pw]
    for lvl in range(1, len(m_sub)):
        m_off = m_sub[lvl] & ~m_sub[lvl - 1]
        off = [jnp.where(m_off, p["lmat"], 0.0).astype(BF16) for p in pre]
        inv16 = [x.astype(BF16) for x in inv]
        half = [_dot(x, o).astype(BF16) for x, o in zip(inv16, off)]
        inv = [x - _dot(h, x16) for x, h, x16 in zip(inv, half, inv16)]
    uws = [_dot(x.astype(BF16), p["rhs"]) for x, p in zip(inv, pre)]

    for ch in chunks:
        r0 = ch * CHUNK
        attn, kt_dec = pre[ch]["attn"], pre[ch]["kt_dec"]
        u, w = uws[ch][:, :HEAD_DIM], uws[ch][:, HEAD_DIM:]

        state16 = s_ref[...].astype(BF16)
        wq_s = _dot(jnp.concatenate([w, pre[ch]["q_dec"]], axis=0).astype(BF16), state16)
        ws_d = jnp.concatenate([wq_s[h * CHUNK:(h + 1) * CHUNK, h * HEAD_DIM:(h + 1) * HEAD_DIM]
                                for h in range(N_HEADS)], axis=0)
        qs_d = jnp.concatenate([wq_s[STACK + h * CHUNK:STACK + (h + 1) * CHUNK, h * HEAD_DIM:(h + 1) * HEAD_DIM]
                                for h in range(N_HEADS)], axis=0)
        v_new = u - ws_d
        v16 = v_new.astype(BF16)
        o = qs_d + _dot(attn, v16)

        v_bd =jnp.where(head_match, jnp.concatenate([v_new] * N_HEADS, axis=1), 0.0).astype(BF16)
        upd = _dot(kt_dec, v_bd)
        for h in range(N_HEADS):
            e_h = jnp.exp(grow_ref[L_GL + h:L_GL + h + 1, r0:r0 + 1])
            sl = slice(h * HEAD_DIM, (h + 1) * HEAD_DIM)
            s_ref[:, sl] = s_ref[:, sl] * e_h + upd[:, sl]

        for h in range(N_HEADS):
            oh = o[h * CHUNK:(h + 1) * CHUNK, :]
            oh = oh * lax.rsqrt(jnp.mean(oh * oh, axis=-1, keepdims=True) + RMS_EPS) * gon_ref[...]
            zh = z_ref[r0:r0 + CHUNK, h * HEAD_DIM:(h + 1) * HEAD_DIM].astype(F32)
            o_ref[r0:r0 + CHUNK, h * HEAD_DIM:(h + 1) * HEAD_DIM] = (oh * _silu(zh)).astype(o_ref.dtype)


def _delta(qkv, z, gcol, grow, conv_w, g_onorm, tb):
    t = qkv.shape[0]
    pad = 8
    return pl.pallas_call(
        _delta_kernel,
        grid=(t // tb,),
        in_specs=[pl.BlockSpec((tb, QKV_W), lambda i: (i, 0)),
                  pl.BlockSpec((tb, MIX_W), lambda i: (i, 0)),
                  pl.BlockSpec((tb, LANES), lambda i: (i, 0)),
                  pl.BlockSpec((grow.shape[0], tb), lambda i: (0, i)),
                  pl.BlockSpec((CONV_K, QKV_W), lambda i: (0, 0)),
                  pl.BlockSpec((1, HEAD_DIM), lambda i: (0, 0))],
        out_specs=pl.BlockSpec((tb, MIX_W), lambda i: (i, 0)),
        out_shape=jax.ShapeDtypeStruct((t, MIX_W), BF16),
        scratch_shapes=[pltpu.VMEM((tb + pad, QKV_W), F32),
                        pltpu.VMEM((pad, QKV_W), F32),
                        pltpu.VMEM((HEAD_DIM, N_HEADS * HEAD_DIM), F32)],
        compiler_params=_params(("arbitrary",)),
        name="delta",
    )(qkv, z, gcol, grow, conv_w, g_onorm)


FOX_GROUP = 4
FOX_TQ, FOX_TK = 256, 2048


def _fox_kernel(*refs, tk):
    g = FOX_GROUP
    q_ref, o_ref = refs[0], refs[-1]
    k_refs, v_refs, f_refs = (refs[1 + i * g:1 + (i + 1) * g] for i in range(3))
    qi = pl.program_id(1)
    tq = q_ref.shape[0]
    ones = jnp.ones((tk, HEAD_DIM), BF16)

    def step(j, carry, mask):
        off = pl.multiple_of(j * tk, tk)
        ss = []
        for h in range(g):
            q = q_ref[:, h * HEAD_DIM:(h + 1) * HEAD_DIM]
            s = _dot_nt(q, k_refs[h][pl.ds(off, tk), :]) - f_refs[h][j] * LOG2E
            if mask is not None:
                s = jnp.where(mask, s, NEG_BIG)
            ss.append(s)
        ps, ms, alphas = [], [], []
        for h in range(g):
            m = carry[h][0]
            m_new = jnp.maximum(m, jnp.max(ss[h], axis=-1, keepdims=True))
            alphas.append(jnp.exp2(m - m_new))
            ps.append(jnp.exp2(ss[h] - m_new).astype(BF16))
            ms.append(m_new)
        out = []
        for h in range(g):
            v_aug = jnp.concatenate([v_refs[h][pl.ds(off, tk), :], ones], axis=1)
            out.append((ms[h], alphas[h] * carry[h][1] + _dot(ps[h], v_aug)))
        return tuple(out)

    init = tuple((jnp.full((tq, 1), NEG_BIG, F32), jnp.zeros((tq, 2 * HEAD_DIM), F32)) for _ in range(g))
    n_full = (qi * tq) // tk
    carry = lax.fori_loop(0, n_full, lambda j, c: step(j, c, None), init)
    row = lax.broadcasted_iota(I32, (tq, tk), 0)
    col = lax.broadcasted_iota(I32, (tq, tk), 1)
    for d in range(max(tq // tk, 1)):
        j = n_full + d
        carry = step(j, carry, col + (j * tk - qi * tq) <= row)
    for h in range(g):
        acc = carry[h][1]
        o_ref[:, h * HEAD_DIM:(h + 1) * HEAD_DIM] = (acc[:, :HEAD_DIM] / acc[:, HEAD_DIM:]).astype(o_ref.dtype)


def _fox(fx, f_rows, tq, tk):
    t = fx.shape[0]
    g = FOX_GROUP
    heads = lambda hg, u: hg * g + u
    q_specs = [pl.BlockSpec((tq, g * HEAD_DIM), lambda hg, i: (i, hg))]
    k_specs = [pl.BlockSpec((t, HEAD_DIM), lambda hg, i, u=u: (0, N_HEADS + heads(hg, u)),
                            pipeline_mode=pl.Buffered(1)) for u in range(g)]
    v_specs = [pl.BlockSpec((t, HEAD_DIM), lambda hg, i, u=u: (0, 2 * N_HEADS + heads(hg, u)),
                            pipeline_mode=pl.Buffered(1)) for u in range(g)]
    f_specs = [pl.BlockSpec((None, t // tk, 1, tk), lambda hg, i, u=u: (heads(hg, u), 0, 0, 0)) for u in range(g)]
    return pl.pallas_call(
        functools.partial(_fox_kernel, tk=tk),
        grid=(N_HEADS // g, t // tq),
        in_specs=q_specs + k_specs + v_specs + f_specs,
        out_specs=pl.BlockSpec((tq, g * HEAD_DIM), lambda hg, i: (i, hg)),
        out_shape=jax.ShapeDtypeStruct((t, MIX_W), BF16),
        compiler_params=_params(("arbitrary", "arbitrary")),
        name="fox",
    )(*([fx] * (1 + 2 * g) + [f_rows] * g))


def _merge_kernel(oa_ref, ob_ref, ga_ref, gb_ref, x_ref, woa_ref, wob_ref, wout_ref, gt1_ref, g2_ref, sc2_ref,
                  sh2_ref, wr_ref, br_ref, x1_ref, h2_ref, ri_ref, gate_ref, cnt_ref, carry_ref):
    i = pl.program_id(0)
    tm = x_ref.shape[0]

    @pl.when(i == 0)
    def _():
        carry_ref[...] = jnp.zeros_like(carry_ref)

    ya = _dot(oa_ref[...], woa_ref[...])
    yb = _dot(ob_ref[...], wob_ref[...])
    merged = _sigmoid(ga_ref[...].astype(F32)) * ya + _sigmoid(gb_ref[...].astype(F32)) * yb
    x1 = x_ref[...] + gt1_ref[...] * _dot(merged.astype(BF16), wout_ref[...])
    x1_ref[...] = x1
    y = x1 * lax.rsqrt(jnp.mean(x1 * x1, axis=-1, keepdims=True) + RMS_EPS) * g2_ref[...]
    h2 = y * (1.0 + sc2_ref[...]) + sh2_ref[...]
    _store_token_tiles(h2_ref, h2)

    logits = _dot(h2, wr_ref[...], HIGHEST) + br_ref[...]
    lane = lax.broadcasted_iota(I32, logits.shape, 1)
    lanef = lane.astype(F32)
    cur = jnp.where(lane < N_EXPERTS, logits, -jnp.inf)
    vals, idxs = [], []
    for _ in range(TOP_K):
        m = jnp.max(cur, axis=-1, keepdims=True)
        ix = jnp.min(jnp.where(cur == m, lanef, float(LANES)), axis=-1, keepdims=True)
        vals.append(m)
        idxs.append(ix)
        cur = jnp.where(lanef == ix, -jnp.inf, cur)
    exps = [jnp.exp(v - vals[0]) for v in vals]
    denom = exps[0] + exps[1] + exps[2] + exps[3]

    onehot = jnp.zeros(logits.shape, F32)
    for ix in idxs:
        onehot = onehot + jnp.where(lanef == ix, 1.0, 0.0)
    r = lax.broadcasted_iota(I32, (tm, tm), 0)
    c = lax.broadcasted_iota(I32, (tm, tm), 1)
    strict = jnp.where(r > c, 1.0, 0.0).astype(BF16)
    before = _dot(strict, onehot.astype(BF16)) + carry_ref[...]
    carry_ref[...] = carry_ref[...] + jnp.sum(onehot, axis=0, keepdims=True)
    cnt_ref[...] = carry_ref[...].astype(I32)

    ri = jnp.zeros(logits.shape, F32)
    gates = jnp.zeros(logits.shape, F32)
    for kk in range(TOP_K):
        rank = jnp.sum(jnp.where(lanef == idxs[kk], before, 0.0), axis=-1, keepdims=True)
        ri = ri + jnp.where(lane == kk, idxs[kk], 0.0) + jnp.where(lane == TOP_K + kk, rank, 0.0)
        gates = gates + jnp.where(lane == kk, exps[kk] / denom, 0.0)
    ri_ref[...] = ri.astype(I32)
    gate_ref[...] = gates


def _merge(o_a, o_b, gate_a, gate_b, x, woa, wob, wout, gt1, g2, sc2, sh2, wr, br, tm):
    t, d = x.shape
    row = lambda i: (i, 0)
    fixed = lambda i: (0, 0)
    return pl.pallas_call(
        _merge_kernel,
        grid=(t // tm,),
        in_specs=[pl.BlockSpec((tm, MIX_W), row), pl.BlockSpec((tm, MIX_W), row),
                  pl.BlockSpec((tm, d), row), pl.BlockSpec((tm, d), row), pl.BlockSpec((tm, d), row),
                  pl.BlockSpec((MIX_W, d), fixed), pl.BlockSpec((MIX_W, d), fixed), pl.BlockSpec((d, d), fixed),
                  pl.BlockSpec((1, d), fixed), pl.BlockSpec((1, d), fixed), pl.BlockSpec((1, d), fixed),
                  pl.BlockSpec((1, d), fixed), pl.BlockSpec((d, LANES), fixed), pl.BlockSpec((1, LANES), fixed)],
        out_specs=[pl.BlockSpec((tm, d), row), pl.BlockSpec((tm * TOKEN_TILE_ROWS, LANES), row),
                   pl.BlockSpec((tm, LANES), row),
                   pl.BlockSpec((tm, LANES), row), pl.BlockSpec((1, LANES), fixed)],
        out_shape=[jax.ShapeDtypeStruct((t, d), F32), jax.ShapeDtypeStruct((t * TOKEN_TILE_ROWS, LANES), F32),
                   jax.ShapeDtypeStruct((t, LANES), I32), jax.ShapeDtypeStruct((t, LANES), F32),
                   jax.ShapeDtypeStruct((1, LANES), I32)],
        scratch_shapes=[pltpu.VMEM((1, LANES), F32)],
        compiler_params=_params(("arbitrary",)),
        name="merge",
    )(o_a, o_b, gate_a, gate_b, x, woa, wob, wout, gt1, g2, sc2, sh2, wr, br)


def _dest_kernel(ri_ref, cnt_ref, dest_ref, blk_ref, pend_ref):
    shift = EXPERT_BLOCK.bit_length() - 1
    cnt = jnp.broadcast_to(cnt_ref[...], (8, LANES))
    lane_row = lax.broadcasted_iota(I32, cnt.shape, 1)
    padded = jnp.where(lane_row < N_EXPERTS, ((cnt + (EXPERT_BLOCK - 1)) >> shift) << shift, 0)
    pend = padded
    s = 1
    while s < N_EXPERTS:
        pend = pend + jnp.where(lane_row >= s, pltpu.roll(pend, s, 1), 0)
        s *= 2
    pstart = (pend - padded)[0:1, :].astype(F32)
    pend = pend[0:1, :]
    pend_ref[...] = pend

    ri = ri_ref[...]
    lane = lax.broadcasted_iota(I32, ri.shape, 1)
    dest = jnp.zeros(ri.shape, F32)
    for kk in range(TOP_K):
        ix = ri[:, kk:kk + 1]
        rank = ri[:, TOP_K + kk:TOP_K + kk + 1].astype(F32)
        start = jnp.sum(jnp.where(lane == ix, pstart, 0.0), axis=-1, keepdims=True)
        dest = dest + jnp.where(lane == kk, start + rank, 0.0)
    dest_ref[...] = dest.astype(I32)

    bstart = lax.broadcasted_iota(I32, blk_ref.shape, 0) * EXPERT_BLOCK
    lane_b = lax.broadcasted_iota(I32, blk_ref.shape, 1)
    ended = jnp.where((lane_b < N_EXPERTS) & (pend <= bstart), 1.0, 0.0)
    e = jnp.minimum(jnp.sum(ended, axis=-1, keepdims=True), float(N_EXPERTS - 1))
    vend = pstart + cnt[0:1, :].astype(F32)
    vend_e = jnp.sum(jnp.where(lane_b.astype(F32) == e, vend, 0.0), axis=-1, keepdims=True)
    nvalid = jnp.clip(vend_e - bstart[:, 0:1].astype(F32), 0.0, float(EXPERT_BLOCK))
    blk_ref[...] = jnp.where(lane_b == 0, e, jnp.where(lane_b == 1, nvalid, 0.0)).astype(I32)


def _dest(ri, cnt, n_blocks, tm):
    t = ri.shape[0]
    nb_pad = -(-n_blocks // 8) * 8
    return pl.pallas_call(
        _dest_kernel,
        grid=(t // tm,),
        in_specs=[pl.BlockSpec((tm, LANES), lambda i: (i, 0)), pl.BlockSpec((1, LANES), lambda i: (0, 0))],
        out_specs=[pl.BlockSpec((tm, LANES), lambda i: (i, 0)), pl.BlockSpec((nb_pad, LANES), lambda i: (0, 0)),
                   pl.BlockSpec((1, LANES), lambda i: (0, 0))],
        out_shape=[jax.ShapeDtypeStruct((t, LANES), I32), jax.ShapeDtypeStruct((nb_pad, LANES), I32),
                   jax.ShapeDtypeStruct((1, LANES), I32)],
        compiler_params=_params(("arbitrary",)),
        name="dest",
    )(ri, cnt)


ROW_UNROLL = 8


def _drain_tokens(src_ref, dst_ref, sem, n_tokens):
    def drain(_, carry):
        for _ in range(ROW_UNROLL * TOP_K):
            _token_copy(src_ref, 0, dst_ref, 0, sem).wait()
        return carry

    lax.fori_loop(0, n_tokens // ROW_UNROLL, drain, 0)


def _dispatch_kernel(dest_ref, h_ref, xbuf_ref, sem):
    tm = h_ref.shape[0] // TOKEN_TILE_ROWS

    def issue(g, carry):
        for u in range(ROW_UNROLL):
            r = g * ROW_UNROLL + u
            for kk in range(TOP_K):
                _token_copy(h_ref, r, xbuf_ref, dest_ref[r * TOP_K + kk], sem).start(priority=kk % 2)
        return carry

    lax.fori_loop(0, tm // ROW_UNROLL, issue, 0)
    _drain_tokens(h_ref, xbuf_ref, sem, tm)


def _dispatch(dest_flat, h2_tiles, n_rows):
    tm = ROUTE_TILE
    t = h2_tiles.shape[0] // TOKEN_TILE_ROWS
    return pl.pallas_call(
        _dispatch_kernel,
        grid=(t // tm,),
        in_specs=[pl.BlockSpec((tm * TOP_K,), lambda i: (i,), memory_space=pltpu.SMEM),
                  pl.BlockSpec((tm * TOKEN_TILE_ROWS, LANES), lambda i: (i, 0))],
        out_specs=pl.BlockSpec(memory_space=pl.ANY),
        out_shape=jax.ShapeDtypeStruct((n_rows * TOKEN_TILE_ROWS, LANES), h2_tiles.dtype),
        scratch_shapes=[pltpu.SemaphoreType.DMA(())],
        compiler_params=_params(("arbitrary",)),
        name="dispatch",
    )(dest_flat, h2_tiles)


def _experts_kernel(blk_e_ref, nvalid_ref, x_ref, wgu_ref, bgu_ref, wd_ref, bd_ref, y_ref, wgu16_ref, wd16_ref):
    b = pl.program_id(0)
    nvalid = nvalid_ref[b]

    @pl.when((b == 0) | (blk_e_ref[b] != blk_e_ref[jnp.maximum(b - 1, 0)]))
    def _():
        wgu16_ref[...] = wgu_ref[...].astype(BF16)
        wd16_ref[...] = wd_ref[...].astype(BF16)

    @pl.when(nvalid > 0)
    def _():
        n = x_ref.shape[0] // TOKEN_TILE_ROWS
        x = jnp.concatenate([_load_token_slab(x_ref, s, n) for s in range(TOKEN_TILE_ROWS)], axis=1)
        rows = lax.broadcasted_iota(I32, x.shape, 0)
        x = jnp.where(rows < nvalid, x, 0.0).astype(BF16)
        gu = _dot(x, wgu16_ref[...]) + bgu_ref[...]
        gate = jnp.minimum(gu[:, :FF], SWIGLU_LIMIT)
        up = jnp.clip(gu[:, FF:], -SWIGLU_LIMIT, SWIGLU_LIMIT)
        act = (up + 1.0) * (gate * _sigmoid(SWIGLU_ALPHA * gate))
        _store_token_tiles(y_ref, _dot(act.astype(BF16), wd16_ref[...]) + bd_ref[...])

    @pl.when(nvalid <= 0)
    def _():
        y_ref[...] = jnp.zeros_like(y_ref)


def _experts(blk_e, nvalid, xbuf, wgu, bgu, wd, bd):
    d = D_MODEL
    blk_rows = EXPERT_BLOCK * TOKEN_TILE_ROWS
    nb = xbuf.shape[0] // blk_rows
    grid_spec = pltpu.PrefetchScalarGridSpec(
        num_scalar_prefetch=2,
        grid=(nb,),
        in_specs=[pl.BlockSpec((blk_rows, LANES), lambda b, e, n: (b, 0)),
                  pl.BlockSpec((None, d, 2 * FF), lambda b, e, n: (e[b], 0, 0)),
                  pl.BlockSpec((None, 1, 2 * FF), lambda b, e, n: (e[b], 0, 0)),
                  pl.BlockSpec((None, FF, d), lambda b, e, n: (e[b], 0, 0)),
                  pl.BlockSpec((None, 1, d), lambda b, e, n: (e[b], 0, 0))],
        out_specs=pl.BlockSpec((blk_rows, LANES), lambda b, e, n: (b, 0)),
        scratch_shapes=[pltpu.VMEM((d, 2 * FF), BF16), pltpu.VMEM((FF, d), BF16)],
    )
    return pl.pallas_call(
        _experts_kernel,
        grid_spec=grid_spec,
        out_shape=jax.ShapeDtypeStruct(xbuf.shape, F32),
        compiler_params=_params(("arbitrary",)),
        name="experts",
    )(blk_e, nvalid, xbuf, wgu, bgu, wd, bd)


def _combine_kernel(dest_ref, dest_next_ref, ybuf_ref, x1_ref, gate_ref, gt2_ref, gf_ref, o_ref, rows_ref, sems):
    i = pl.program_id(0)
    tm = x1_ref.shape[0]
    slot = i % 2

    def start_gather(idx_ref, s):
        def issue(g, carry):
            for u in range(ROW_UNROLL):
                r = g * ROW_UNROLL + u
                for kk in range(TOP_K):
                    _token_copy(ybuf_ref, idx_ref[r * TOP_K + kk], rows_ref.at[s, kk], r,
                                sems.at[s]).start(priority=kk % 2)
            return carry

        lax.fori_loop(0, tm // ROW_UNROLL, issue, 0)

    @pl.when(i == 0)
    def _():
        start_gather(dest_ref, 0)

    @pl.when(i + 1 < pl.num_programs(0))
    def _():
        start_gather(dest_next_ref, 1 - slot)

    _drain_tokens(ybuf_ref, rows_ref.at[slot, 0], sems.at[slot], tm)

    gates = gate_ref[...]
    slabs = []
    for s in range(TOKEN_TILE_ROWS):
        moe = gates[:, 0:1] * _load_token_slab(rows_ref.at[slot, 0], s, tm)
        for kk in range(1, TOP_K):
            moe = moe + gates[:, kk:kk + 1] * _load_token_slab(rows_ref.at[slot, kk], s, tm)
        slabs.append(moe)
    xo = x1_ref[...] + gt2_ref[...] * jnp.concatenate(slabs, axis=1)
    o_ref[...] = xo * lax.rsqrt(jnp.mean(xo * xo, axis=-1, keepdims=True) + RMS_EPS) * gf_ref[...]


def _combine(dest_flat, ybuf, x1, gates, gt2, g_final):
    t, d = x1.shape
    tm = ROUTE_TILE
    n = t // tm
    return pl.pallas_call(
        _combine_kernel,
        grid=(n,),
        in_specs=[pl.BlockSpec((tm * TOP_K,), lambda i: (i,), memory_space=pltpu.SMEM),
                  pl.BlockSpec((tm * TOP_K,), lambda i: (jnp.minimum(i + 1, n - 1),), memory_space=pltpu.SMEM),
                  pl.BlockSpec(memory_space=pl.ANY),
                  pl.BlockSpec((tm, d), lambda i: (i, 0)),
                  pl.BlockSpec((tm, LANES), lambda i: (i, 0)),
                  pl.BlockSpec((1, d), lambda i: (0, 0)),
                  pl.BlockSpec((1, d), lambda i: (0, 0))],
        out_specs=pl.BlockSpec((tm, d), lambda i: (i, 0)),
        out_shape=jax.ShapeDtypeStruct((t, d), F32),
        scratch_shapes=[pltpu.VMEM((2, TOP_K, tm * TOKEN_TILE_ROWS, LANES), F32), pltpu.SemaphoreType.DMA((2,))],
        compiler_params=_params(("arbitrary",)),
        name="combine",
    )(dest_flat, dest_flat, ybuf, x1, gates, gt2, g_final)


def _pad_lanes(v, lane0):
    return jnp.zeros((1, LANES), F32).at[0, lane0:lane0 + v.shape[0]].set(v.astype(F32))


def _layer(x, mod, g_norm1, w_in, conv_w, a_log, dt_bias, g_onorm, b_fgate, w_o_delta, w_o_fox, w_out,
           g_norm2, w_router, b_router, w_gate_up, b_gate_up, w_down, b_down, g_final):
    t, d = x.shape
    sh1, sc1, gt1, sh2, sc2, gt2 = [mod[:, i * d:(i + 1) * d] for i in range(N_MOD)]

    w_t = w_in.T
    o = 0
    wq = w_t[o:o + QKV_W]; o += QKV_W
    wz = w_t[o:o + MIX_W]; o += MIX_W
    w_beta = w_t[o:o + N_HEADS]; o += N_HEADS
    w_dec = w_t[o:o + N_HEADS]; o += N_HEADS
    wf = w_t[o:o + QKV_W]; o += QKV_W
    wf = jnp.concatenate([wf[:MIX_W] * (LOG2E * HEAD_DIM ** -0.5), wf[MIX_W:]], axis=0)
    w_fg = w_t[o:o + N_HEADS]; o += N_HEADS
    wga = w_t[o:o + d]; o += d
    wgb = w_t[o:o + d]
    ws = jnp.zeros((LANES, d), F32)
    ws = ws.at[L_BETA:L_BETA + N_HEADS].set(w_beta).at[L_G:L_G + N_HEADS].set(w_dec)
    ws = ws.at[L_F:L_F + N_HEADS].set(w_fg)
    bf = lambda w: w.astype(BF16)

    tm = min(ROW_TILE, t)
    qkv, z, fx, gate_a, gate_b, small = _inproj(
        x, g_norm1.reshape(1, d), sc1, sh1, bf(wq), bf(wz), bf(wf), bf(wga), bf(wgb), bf(ws), tm)

    gcol, grow = _gates(small, _pad_lanes(a_log, L_G), _pad_lanes(dt_bias, L_G), _pad_lanes(b_fgate, L_F),
                        min(GATES_TILE, t))

    o_a = _delta(qkv, z, gcol, grow, conv_w, g_onorm.reshape(1, HEAD_DIM), min(DELTA_TILE, t))
    tq, tk = min(FOX_TQ, t), min(FOX_TK, t)
    f_rows = grow[L_F:L_F + N_HEADS].reshape(N_HEADS, t // tk, 1, tk)
    o_b = _fox(fx, f_rows, tq, tk)

    wr = jnp.zeros((d, LANES), F32).at[:, :N_EXPERTS].set(w_router)
    br = _pad_lanes(b_router, 0)
    x1, h2, ri, gates, cnt = _merge(o_a, o_b, gate_a, gate_b, x, bf(w_o_delta), bf(w_o_fox), bf(w_out), gt1,
                                    g_norm2.reshape(1, d), sc2, sh2, wr, br, tm)

    n_blocks = (t * TOP_K) // EXPERT_BLOCK + N_EXPERTS
    dest, blk, _ = _dest(ri, cnt, n_blocks, min(2048, t))
    dest_flat = dest[:, :TOP_K].reshape(t * TOP_K)
    blk_e, nvalid = blk[:n_blocks, 0], blk[:n_blocks, 1]

    xbuf = _dispatch(dest_flat, h2, n_blocks * EXPERT_BLOCK)
    ybuf = _experts(blk_e, nvalid, xbuf, w_gate_up, b_gate_up.reshape(N_EXPERTS, 1, 2 * FF),
                    w_down, b_down.reshape(N_EXPERTS, 1, d))
    return _combine(dest_flat, ybuf, x1, gates, gt2, g_final.reshape(1, d))


def kernel(x, c, w_ada, b_ada, g_norm1, w_in, conv_w, a_log, dt_bias, g_onorm, b_fgate, w_o_delta, w_o_fox, w_out,
           g_norm2, w_router, b_router, w_gate_up, b_gate_up, w_down, b_down, g_final):
    b, s, d = x.shape
    assert b == 1 and d == D_MODEL and w_ada.shape[0] == 1
    h = x[0]
    for l in range(w_ada.shape[0]):
        mod = _modulation(c[0], w_ada[l], b_ada[l])
        h = _layer(h, mod, g_norm1[l], w_in[l], conv_w[l], a_log[l], dt_bias[l], g_onorm[l], b_fgate[l],
                   w_o_delta[l], w_o_fox[l], w_out[l], g_norm2[l], w_router[l], b_router[l], w_gate_up[l],
                   b_gate_up[l], w_down[l], b_down[l], g_final)
    return h[None]
```

```python
import functools

import jax
import jax.numpy as jnp
from jax import lax
from jax.experimental import pallas as pl
from jax.experimental.pallas import tpu as pltpu

F32 = jnp.float32
BF16 = jnp.bfloat16
I32 = jnp.int32

D_MODEL = 1024
HEAD_DIM = 128
N_HEADS = 4
MIX_W = N_HEADS * HEAD_DIM
QKV_W = 3 * MIX_W
CONV_K = 4
CHUNK = 64
STACK = N_HEADS * CHUNK
SUB = 16
N_EXPERTS = 32
TOP_K = 4
FF = D_MODEL
SWIGLU_LIMIT = 7.0
SWIGLU_ALPHA = 1.702
RMS_EPS = 1e-6
L2_EPS = 1e-6
N_MOD = 6
LANES = 128
NEG_BIG = -1e30
LOG2E = 1.4426950408889634

ROW_TILE = 512
GATES_TILE = 256
DELTA_TILE = 256
EXPERT_BLOCK = 512
ROUTE_TILE = 256
VMEM_LIMIT = 56 * 1024 * 1024


def _params(sem, vmem=VMEM_LIMIT, flags=None):
    return pltpu.CompilerParams(dimension_semantics=sem, vmem_limit_bytes=vmem, flags=flags)


def _softplus(x):
    return jnp.maximum(x, 0.0) + jnp.log(1.0 + jnp.exp(-jnp.abs(x)))


def _sigmoid(x):
    return 1.0 / (1.0 + jnp.exp(-x))


def _silu(x):
    return x * _sigmoid(x)


def _dot(a, b, precision=None):
    return jnp.dot(a, b, preferred_element_type=F32, precision=precision)


def _dot_nt(a, b, precision=None):
    return lax.dot_general(a, b, (((1,), (1,)), ((), ())), preferred_element_type=F32, precision=precision)


def _split_bf16(x, terms):
    parts = []
    for _ in range(terms):
        p = x.astype(BF16)
        parts.append(p)
        x = x - p.astype(F32)
    return parts


def _dot_split(a, b):
    a_hi, a_lo = _split_bf16(a, 2)
    b_hi, b_lo = _split_bf16(b, 2)
    return _dot(a_hi, b_hi) + (_dot(a_hi, b_lo) + _dot(a_lo, b_hi))


def _dot_mask(mask01, x):
    return sum(_dot(mask01, p) for p in _split_bf16(x, 3))


TOKEN_TILE_ROWS = D_MODEL // LANES


def _store_token_tiles(ref, x):
    n = x.shape[0]
    for s in range(TOKEN_TILE_ROWS):
        ref[pl.ds(s, n, stride=TOKEN_TILE_ROWS), :] = x[:, s * LANES:(s + 1) * LANES]


def _load_token_slab(ref, s, n):
    return ref[pl.ds(s, n, stride=TOKEN_TILE_ROWS), :]


def _token_copy(src_ref, src_tok, dst_ref, dst_tok, sem):
    rows = lambda i: pl.ds(pl.multiple_of(i * TOKEN_TILE_ROWS, TOKEN_TILE_ROWS), TOKEN_TILE_ROWS)
    return pltpu.make_async_copy(src_ref.at[rows(src_tok), :], dst_ref.at[rows(dst_tok), :], sem)


def _mod_kernel(c_ref, w_ref, b_ref, o_ref):
    o_ref[...] = jnp.sum(c_ref[...] * w_ref[...], axis=0, keepdims=True) + b_ref[...]


def _modulation(c, w_ada, b_ada):
    d, n = w_ada.shape
    tn = 1024
    return pl.pallas_call(
        _mod_kernel,
        grid=(n // tn,),
        in_specs=[pl.BlockSpec((d, 1), lambda j: (0, 0)),
                  pl.BlockSpec((d, tn), lambda j: (0, j)),
                  pl.BlockSpec((1, tn), lambda j: (0, j))],
        out_specs=pl.BlockSpec((1, tn), lambda j: (0, j)),
        out_shape=jax.ShapeDtypeStruct((1, n), F32),
        compiler_params=_params(("arbitrary",)),
        name="mod",
    )(c.reshape(d, 1), w_ada, b_ada.reshape(1, n))


def _inproj_kernel(x_ref, g_ref, sc_ref, sh_ref, wq_ref, wz_ref, wf_ref, wga_ref, wgb_ref, ws_ref,
                   oq_ref, oz_ref, of_ref, oga_ref, ogb_ref, os_ref):
    x = x_ref[...]
    y = x * lax.rsqrt(jnp.mean(x * x, axis=-1, keepdims=True) + RMS_EPS) * g_ref[...]
    h = (y * (1.0 + sc_ref[...]) + sh_ref[...]).astype(BF16)
    oq_ref[...] = _dot_nt(h, wq_ref[...])
    oz_ref[...] = _dot_nt(h, wz_ref[...]).astype(BF16)
    of_ref[...] = _dot_nt(h, wf_ref[...]).astype(BF16)
    oga_ref[...] = _dot_nt(h, wga_ref[...]).astype(BF16)
    ogb_ref[...] = _dot_nt(h, wgb_ref[...]).astype(BF16)
    os_ref[...] = _dot_nt(h, ws_ref[...])


def _inproj(x, g1, sc1, sh1, wq, wz, wf, wga, wgb, ws, tm):
    t, d = x.shape
    row = lambda i: (i, 0)
    fixed = lambda i: (0, 0)
    ws_list = [wq, wz, wf, wga, wgb, ws]
    out_dt = [F32, BF16, BF16, BF16, BF16, F32]
    return pl.pallas_call(
        _inproj_kernel,
        grid=(t // tm,),
        in_specs=[pl.BlockSpec((tm, d), row)] + [pl.BlockSpec((1, d), fixed)] * 3
                 + [pl.BlockSpec(w.shape, fixed) for w in ws_list],
        out_specs=[pl.BlockSpec((tm, w.shape[0]), row) for w in ws_list],
        out_shape=[jax.ShapeDtypeStruct((t, w.shape[0]), dt) for w, dt in zip(ws_list, out_dt)],
        compiler_params=_params(("arbitrary",)),
        name="inproj",
    )(x, g1, sc1, sh1, *ws_list)


L_BETA, L_G, L_F, L_GC, L_GL = 0, 4, 8, 12, 16


def _gates_kernel(s_ref, alog_ref, dtb_ref, bf_ref, col_ref, row_ref, carry_ref):
    i = pl.program_id(0)
    tm = s_ref.shape[0]

    @pl.when(i == 0)
    def _():
        carry_ref[...] = jnp.zeros_like(carry_ref)

    s = s_ref[...]
    lane = lax.broadcasted_iota(I32, s.shape, 1)
    beta = _sigmoid(s)
    g = -jnp.exp(alog_ref[...]) * _softplus(s + dtb_ref[...])
    logf = -_softplus(-(s + bf_ref[...]))
    is_g = (lane >= L_G) & (lane < L_G + N_HEADS)
    is_f = (lane >= L_F) & (lane < L_F + N_HEADS)
    g = jnp.where(is_g, g, 0.0)
    logf = jnp.where(is_f, logf, 0.0)

    r = lax.broadcasted_iota(I32, (tm, tm), 0)
    c = lax.broadcasted_iota(I32, (tm, tm), 1)
    same_chunk = (r // CHUNK) == (c // CHUNK)
    tri = jnp.where(r >= c, 1.0, 0.0).astype(BF16)
    tri_chunk = jnp.where(same_chunk & (r >= c), 1.0, 0.0).astype(BF16)
    ones_chunk = jnp.where(same_chunk, 1.0, 0.0).astype(BF16)
    f_cum = _dot_mask(tri, logf) + carry_ref[...]
    carry_ref[...] = f_cum[tm - 1:tm, :]
    gc = _dot_mask(tri_chunk, g)
    gl = _dot_mask(ones_chunk, g)

    out = jnp.where(lane < N_HEADS, beta, 0.0) + g + f_cum
    out = out + pltpu.roll(gc, L_GC - L_G, 1) + pltpu.roll(gl, L_GL - L_G, 1)
    col_ref[...] = out
    row_ref[...] = out.T[:row_ref.shape[0], :]


def _gates(small, alog_row, dtb_row, bf_row, tm):
    t = small.shape[0]
    n_rows = 24
    return pl.pallas_call(
        _gates_kernel,
        grid=(t // tm,),
        in_specs=[pl.BlockSpec((tm, LANES), lambda i: (i, 0))] + [pl.BlockSpec((1, LANES), lambda i: (0, 0))] * 3,
        out_specs=[pl.BlockSpec((tm, LANES), lambda i: (i, 0)), pl.BlockSpec((n_rows, tm), lambda i: (0, i))],
        out_shape=[jax.ShapeDtypeStruct((t, LANES), F32), jax.ShapeDtypeStruct((n_rows, t), F32)],
        scratch_shapes=[pltpu.VMEM((1, LANES), F32)],
        compiler_params=_params(("arbitrary",)),
        name="gates",
    )(small, alog_row, dtb_row, bf_row)


def _stack_heads(a, col0):
    return jnp.concatenate([a[:, col0 + h * HEAD_DIM: col0 + (h + 1) * HEAD_DIM] for h in range(N_HEADS)], axis=0)


def _stack_cols(a, lane0):
    return jnp.concatenate([a[:, lane0 + h: lane0 + h + 1] for h in range(N_HEADS)], axis=0)


def _delta_kernel(qkv_ref, z_ref, gcol_ref, grow_ref, cw_ref, gon_ref, o_ref, ext_ref, tail_ref, s_ref):
    i = pl.program_id(0)
    tb = qkv_ref.shape[0]
    pad = tail_ref.shape[0]

    @pl.when(i == 0)
    def _():
        tail_ref[...] = jnp.zeros_like(tail_ref)
        s_ref[...] = jnp.zeros_like(s_ref)

    ext_ref[0:pad, :] = tail_ref[...]
    ext_ref[pad:pad + tb, :] = qkv_ref[...]
    tail_ref[...] = qkv_ref[tb - pad:tb, :]
    conv = cw_ref[0:1, :] * ext_ref[pad - 3:pad - 3 + tb, :]
    for j in range(1, CONV_K):
        conv = conv + cw_ref[j:j + 1, :] * ext_ref[pad - 3 + j:pad - 3 + j + tb, :]
    act = _silu(conv)

    r = lax.broadcasted_iota(I32, (STACK, STACK), 0)
    c = lax.broadcasted_iota(I32, (STACK, STACK), 1)
    same = (r // CHUNK) == (c // CHUNK)
    m_incl = same & (r >= c)
    m_strict = same & (r > c)
    eye = jnp.where(r == c, 1.0, 0.0)
    m_sub = []
    size = SUB
    while size <= CHUNK:
        m_sub.append((r // size) == (c // size))
        size *= 2
    rb = lax.broadcasted_iota(I32, (STACK, N_HEADS * HEAD_DIM), 0) // CHUNK
    cb = lax.broadcasted_iota(I32, (STACK, N_HEADS * HEAD_DIM), 1) // HEAD_DIM
    head_match = rb == cb

    chunks = range(tb // CHUNK)
    pre = []
    for ch in chunks:
        r0 = ch * CHUNK
        a = act[r0:r0 + CHUNK, :]
        q = _stack_heads(a, 0)
        k = _stack_heads(a, MIX_W)
        v = _stack_heads(a, 2 * MIX_W)
        q = q * lax.rsqrt(jnp.sum(q * q, axis=-1, keepdims=True) + L2_EPS) * (HEAD_DIM ** -0.5)
        k = k * lax.rsqrt(jnp.sum(k * k, axis=-1, keepdims=True) + L2_EPS)

        gcols = gcol_ref[r0:r0 + CHUNK, :]
        beta = _stack_cols(gcols, L_BETA)
        gc = _stack_cols(gcols, L_GC)
        gc_row = jnp.concatenate(
            [grow_ref[L_GC + h:L_GC + h + 1, r0:r0 + CHUNK] for h in range(N_HEADS)], axis=1)
        gl_row = jnp.concatenate(
            [grow_ref[L_GL + h:L_GL + h + 1, r0:r0 + CHUNK] for h in range(N_HEADS)], axis=1)

        decay = jnp.exp(jnp.where(m_incl, gc - gc_row, NEG_BIG))
        kb16 = k.astype(BF16)
        kk = _dot_nt(kb16, kb16)
        lmat = jnp.where(m_strict, kk * beta * decay, 0.0)
        attn = (_dot_nt(q.astype(BF16), kb16) * decay).astype(BF16)
        egc = jnp.exp(gc)
        rhs = jnp.concatenate([v * beta, k * (beta * egc)], axis=1).astype(BF16)
        kt_dec = (k.T * jnp.exp(gl_row - gc_row)).astype(BF16)
        pre.append(dict(lmat=lmat, attn=attn, rhs=rhs, kt_dec=kt_dec, q_dec=q * egc))

    l_d = [jnp.where(m_sub[0], p["lmat"], 0.0) for p in pre]
    inv = [eye - l for l in l_d]
    l16 = [l.astype(BF16) for l in l_d]
    pw = [_dot(l, l).astype(BF16) for l in l16]
    n_sq = SUB.bit_length() - 3
    for s in range(n_sq + 1):
        inv = [x + _dot(x.astype(BF16), p) for x, p in zip(inv, pw)]
        if s < n_sq:
            pw = [_dot(p, p).astype(BF16) for p in pw]
    for lvl in range(1, len(m_sub)):
        m_off = m_sub[lvl] & ~m_sub[lvl - 1]
        off = [jnp.where(m_off, p["lmat"], 0.0).astype(BF16) for p in pre]
        inv16 = [x.astype(BF16) for x in inv]
        half = [_dot(x, o).astype(BF16) for x, o in zip(inv16, off)]
        inv = [x - _dot(h, x16) for x, h, x16 in zip(inv, half, inv16)]
    uws = [_dot(x.astype(BF16), p["rhs"]) for x, p in zip(inv, pre)]

    for ch in chunks:
        r0 = ch * CHUNK
        attn, kt_dec = pre[ch]["attn"], pre[ch]["kt_dec"]
        u, w = uws[ch][:, :HEAD_DIM], uws[ch][:, HEAD_DIM:]

        state16 = s_ref[...].astype(BF16)
        wq_s = _dot(jnp.concatenate([w, pre[ch]["q_dec"]], axis=0).astype(BF16), state16)
        ws_d = jnp.concatenate([wq_s[h * CHUNK:(h + 1) * CHUNK, h * HEAD_DIM:(h + 1) * HEAD_DIM]
                                for h in range(N_HEADS)], axis=0)
        qs_d = jnp.concatenate([wq_s[STACK + h * CHUNK:STACK + (h + 1) * CHUNK, h * HEAD_DIM:(h + 1) * HEAD_DIM]
                                for h in range(N_HEADS)], axis=0)
        v_new = u - ws_d
        v16 = v_new.astype(BF16)
        o = qs_d + _dot(attn, v16)

        v_bd =jnp.where(head_match, jnp.concatenate([v_new] * N_HEADS, axis=1), 0.0).astype(BF16)
        upd = _dot(kt_dec, v_bd)
        for h in range(N_HEADS):
            e_h = jnp.exp(grow_ref[L_GL + h:L_GL + h + 1, r0:r0 + 1])
            sl = slice(h * HEAD_DIM, (h + 1) * HEAD_DIM)
            s_ref[:, sl] = s_ref[:, sl] * e_h + upd[:, sl]

        for h in range(N_HEADS):
            oh = o[h * CHUNK:(h + 1) * CHUNK, :]
            oh = oh * lax.rsqrt(jnp.mean(oh * oh, axis=-1, keepdims=True) + RMS_EPS) * gon_ref[...]
            zh = z_ref[r0:r0 + CHUNK, h * HEAD_DIM:(h + 1) * HEAD_DIM].astype(F32)
            o_ref[r0:r0 + CHUNK, h * HEAD_DIM:(h + 1) * HEAD_DIM] = (oh * _silu(zh)).astype(o_ref.dtype)


def _delta(qkv, z, gcol, grow, conv_w, g_onorm, tb):
    t = qkv.shape[0]
    pad = 8
    return pl.pallas_call(
        _delta_kernel,
        grid=(t // tb,),
        in_specs=[pl.BlockSpec((tb, QKV_W), lambda i: (i, 0)),
                  pl.BlockSpec((tb, MIX_W), lambda i: (i, 0)),
                  pl.BlockSpec((tb, LANES), lambda i: (i, 0)),
                  pl.BlockSpec((grow.shape[0], tb), lambda i: (0, i)),
                  pl.BlockSpec((CONV_K, QKV_W), lambda i: (0, 0)),
                  pl.BlockSpec((1, HEAD_DIM), lambda i: (0, 0))],
        out_specs=pl.BlockSpec((tb, MIX_W), lambda i: (i, 0)),
        out_shape=jax.ShapeDtypeStruct((t, MIX_W), BF16),
        scratch_shapes=[pltpu.VMEM((tb + pad, QKV_W), F32),
                        pltpu.VMEM((pad, QKV_W), F32),
                        pltpu.VMEM((HEAD_DIM, N_HEADS * HEAD_DIM), F32)],
        compiler_params=_params(("arbitrary",)),
        name="delta",
    )(qkv, z, gcol, grow, conv_w, g_onorm)


FOX_GROUP = 4
FOX_TQ, FOX_TK = 256, 2048
FOX_DIAG_PARTS = 2


def _fox_kernel(*refs, tk):
    g = FOX_GROUP
    q_ref, o_ref = refs[0], refs[-1]
    k_refs, v_refs, f_refs = (refs[1 + i * g:1 + (i + 1) * g] for i in range(3))
    qi = pl.program_id(1)
    tq = q_ref.shape[0]
    def step(off, width, f_blk, carry, mask):
        ones = jnp.ones((width, HEAD_DIM), BF16)
        ss = []
        for h in range(g):
            q = q_ref[:, h * HEAD_DIM:(h + 1) * HEAD_DIM]
            s = _dot_nt(q, k_refs[h][pl.ds(off, width), :]) - f_blk(h) * LOG2E
            if mask is not None:
                s = jnp.where(mask, s, NEG_BIG)
            ss.append(s)
        ps, ms, alphas = [], [], []
        for h in range(g):
            m = carry[h][0]
            m_new = jnp.maximum(m, jnp.max(ss[h], axis=-1, keepdims=True))
            alphas.append(jnp.exp2(m - m_new))
            ps.append(jnp.exp2(ss[h] - m_new).astype(BF16))
            ms.append(m_new)
        out = []
        for h in range(g):
            v_aug = jnp.concatenate([v_refs[h][pl.ds(off, width), :], ones], axis=1)
            out.append((ms[h], alphas[h] * carry[h][1] + _dot(ps[h], v_aug)))
        return tuple(out)

    def full_chunk(j, carry):
        return step(pl.multiple_of(j * tk, tk), tk, lambda h: f_refs[h][j], carry, None)

    init = tuple((jnp.full((tq, 1), NEG_BIG, F32), jnp.zeros((tq, 2 * HEAD_DIM), F32)) for _ in range(g))
    n_full = (qi * tq) // tk
    carry = lax.fori_loop(0, n_full, full_chunk, init)

    parts = max(1, min(FOX_DIAG_PARTS, tk // tq))
    width = tk // parts
    row = lax.broadcasted_iota(I32, (tq, width), 0)
    col = lax.broadcasted_iota(I32, (tq, width), 1)
    for d in range(max(tq // tk, 1)):
        j = n_full + d
        for p in range(parts):
            off = pl.multiple_of(j * tk + p * width, width)
            piece = functools.partial(step, off, width,
                                      lambda h, j=j, p=p: f_refs[h][j][:, p * width:(p + 1) * width],
                                      mask=col + (off - qi * tq) <= row)
            carry = piece(carry) if p == 0 else lax.cond(off < (qi + 1) * tq, piece, lambda c: c, carry)
    for h in range(g):
        acc = carry[h][1]
        o_ref[:, h * HEAD_DIM:(h + 1) * HEAD_DIM] = (acc[:, :HEAD_DIM] / acc[:, HEAD_DIM:]).astype(o_ref.dtype)


def _fox(fx, f_rows, tq, tk):
    t = fx.shape[0]
    g = FOX_GROUP
    heads = lambda hg, u: hg * g + u
    q_specs = [pl.BlockSpec((tq, g * HEAD_DIM), lambda hg, i: (i, hg))]
    k_specs = [pl.BlockSpec((t, HEAD_DIM), lambda hg, i, u=u: (0, N_HEADS + heads(hg, u)),
                            pipeline_mode=pl.Buffered(1)) for u in range(g)]
    v_specs = [pl.BlockSpec((t, HEAD_DIM), lambda hg, i, u=u: (0, 2 * N_HEADS + heads(hg, u)),
                            pipeline_mode=pl.Buffered(1)) for u in range(g)]
    f_specs = [pl.BlockSpec((None, t // tk, 1, tk), lambda hg, i, u=u: (heads(hg, u), 0, 0, 0)) for u in range(g)]
    return pl.pallas_call(
        functools.partial(_fox_kernel, tk=tk),
        grid=(N_HEADS // g, t // tq),
        in_specs=q_specs + k_specs + v_specs + f_specs,
        out_specs=pl.BlockSpec((tq, g * HEAD_DIM), lambda hg, i: (i, hg)),
        out_shape=jax.ShapeDtypeStruct((t, MIX_W), BF16),
        compiler_params=_params(("arbitrary", "arbitrary")),
        name="fox",
    )(*([fx] * (1 + 2 * g) + [f_rows] * g))


def _merge_kernel(oa_ref, ob_ref, ga_ref, gb_ref, x_ref, woa_ref, wob_ref, wout_ref, gt1_ref, g2_ref, sc2_ref,
                  sh2_ref, wr_ref, br_ref, x1_ref, h2_ref, ri_ref, gate_ref, cnt_ref, carry_ref):
    i = pl.program_id(0)
    tm = x_ref.shape[0]

    @pl.when(i == 0)
    def _():
        carry_ref[...] = jnp.zeros_like(carry_ref)

    ya = _dot(oa_ref[...], woa_ref[...])
    yb = _dot(ob_ref[...], wob_ref[...])
    merged = _sigmoid(ga_ref[...].astype(F32)) * ya + _sigmoid(gb_ref[...].astype(F32)) * yb
    x1 = x_ref[...] + gt1_ref[...] * _dot(merged.astype(BF16), wout_ref[...])
    x1_ref[...] = x1
    y = x1 * lax.rsqrt(jnp.mean(x1 * x1, axis=-1, keepdims=True) + RMS_EPS) * g2_ref[...]
    h2 = y * (1.0 + sc2_ref[...]) + sh2_ref[...]
    _store_token_tiles(h2_ref, h2)

    logits = _dot_split(h2, wr_ref[...]) + br_ref[...]
    lane = lax.broadcasted_iota(I32, logits.shape, 1)
    lanef = lane.astype(F32)
    cur = jnp.where(lane < N_EXPERTS, logits, -jnp.inf)
    vals, idxs = [], []
    for _ in range(TOP_K):
        m = jnp.max(cur, axis=-1, keepdims=True)
        ix = jnp.min(jnp.where(cur == m, lanef, float(LANES)), axis=-1, keepdims=True)
        vals.append(m)
        idxs.append(ix)
        cur = jnp.where(lanef == ix, -jnp.inf, cur)
    exps = [jnp.exp(v - vals[0]) for v in vals]
    denom = exps[0] + exps[1] + exps[2] + exps[3]

    onehot = jnp.zeros(logits.shape, F32)
    for ix in idxs:
        onehot = onehot + jnp.where(lanef == ix, 1.0, 0.0)
    r = lax.broadcasted_iota(I32, (tm, tm), 0)
    c = lax.broadcasted_iota(I32, (tm, tm), 1)
    strict = jnp.where(r > c, 1.0, 0.0).astype(BF16)
    before = _dot(strict, onehot.astype(BF16)) + carry_ref[...]
    carry_ref[...] = carry_ref[...] + jnp.sum(onehot, axis=0, keepdims=True)
    cnt_ref[...] = carry_ref[...].astype(I32)

    ri = jnp.zeros(logits.shape, F32)
    gates = jnp.zeros(logits.shape, F32)
    for kk in range(TOP_K):
        rank = jnp.sum(jnp.where(lanef == idxs[kk], before, 0.0), axis=-1, keepdims=True)
        ri = ri + jnp.where(lane == kk, idxs[kk], 0.0) + jnp.where(lane == TOP_K + kk, rank, 0.0)
        gates = gates + jnp.where(lane == kk, exps[kk] / denom, 0.0)
    ri_ref[...] = ri.astype(I32)
    gate_ref[...] = gates


def _merge(o_a, o_b, gate_a, gate_b, x, woa, wob, wout, gt1, g2, sc2, sh2, wr, br, tm):
    t, d = x.shape
    row = lambda i: (i, 0)
    fixed = lambda i: (0, 0)
    return pl.pallas_call(
        _merge_kernel,
        grid=(t // tm,),
        in_specs=[pl.BlockSpec((tm, MIX_W), row), pl.BlockSpec((tm, MIX_W), row),
                  pl.BlockSpec((tm, d), row), pl.BlockSpec((tm, d), row), pl.BlockSpec((tm, d), row),
                  pl.BlockSpec((MIX_W, d), fixed), pl.BlockSpec((MIX_W, d), fixed), pl.BlockSpec((d, d), fixed),
                  pl.BlockSpec((1, d), fixed), pl.BlockSpec((1, d), fixed), pl.BlockSpec((1, d), fixed),
                  pl.BlockSpec((1, d), fixed), pl.BlockSpec((d, LANES), fixed), pl.BlockSpec((1, LANES), fixed)],
        out_specs=[pl.BlockSpec((tm, d), row), pl.BlockSpec((tm * TOKEN_TILE_ROWS, LANES), row),
                   pl.BlockSpec((tm, LANES), row),
                   pl.BlockSpec((tm, LANES), row), pl.BlockSpec((1, LANES), fixed)],
        out_shape=[jax.ShapeDtypeStruct((t, d), F32), jax.ShapeDtypeStruct((t * TOKEN_TILE_ROWS, LANES), F32),
                   jax.ShapeDtypeStruct((t, LANES), I32), jax.ShapeDtypeStruct((t, LANES), F32),
                   jax.ShapeDtypeStruct((1, LANES), I32)],
        scratch_shapes=[pltpu.VMEM((1, LANES), F32)],
        compiler_params=_params(("arbitrary",)),
        name="merge",
    )(o_a, o_b, gate_a, gate_b, x, woa, wob, wout, gt1, g2, sc2, sh2, wr, br)


def _dest_kernel(ri_ref, cnt_ref, dest_ref, blk_ref, pend_ref):
    shift = EXPERT_BLOCK.bit_length() - 1
    cnt = jnp.broadcast_to(cnt_ref[...], (8, LANES))
    lane_row = lax.broadcasted_iota(I32, cnt.shape, 1)
    padded = jnp.where(lane_row < N_EXPERTS, ((cnt + (EXPERT_BLOCK - 1)) >> shift) << shift, 0)
    pend = padded
    s = 1
    while s < N_EXPERTS:
        pend = pend + jnp.where(lane_row >= s, pltpu.roll(pend, s, 1), 0)
        s *= 2
    pstart = (pend - padded)[0:1, :].astype(F32)
    pend = pend[0:1, :]
    pend_ref[...] = pend

    ri = ri_ref[...]
    lane = lax.broadcasted_iota(I32, ri.shape, 1)
    dest = jnp.zeros(ri.shape, F32)
    for kk in range(TOP_K):
        ix = ri[:, kk:kk + 1]
        rank = ri[:, TOP_K + kk:TOP_K + kk + 1].astype(F32)
        start = jnp.sum(jnp.where(lane == ix, pstart, 0.0), axis=-1, keepdims=True)
        dest = dest + jnp.where(lane == kk, start + rank, 0.0)
    dest_ref[...] = dest.astype(I32)

    bstart = lax.broadcasted_iota(I32, blk_ref.shape, 0) * EXPERT_BLOCK
    lane_b = lax.broadcasted_iota(I32, blk_ref.shape, 1)
    ended = jnp.where((lane_b < N_EXPERTS) & (pend <= bstart), 1.0, 0.0)
    e = jnp.minimum(jnp.sum(ended, axis=-1, keepdims=True), float(N_EXPERTS - 1))
    vend = pstart + cnt[0:1, :].astype(F32)
    vend_e = jnp.sum(jnp.where(lane_b.astype(F32) == e, vend, 0.0), axis=-1, keepdims=True)
    nvalid = jnp.clip(vend_e - bstart[:, 0:1].astype(F32), 0.0, float(EXPERT_BLOCK))
    blk_ref[...] = jnp.where(lane_b == 0, e, jnp.where(lane_b == 1, nvalid, 0.0)).astype(I32)


def _dest(ri, cnt, n_blocks, tm):
    t = ri.shape[0]
    nb_pad = -(-n_blocks // 8) * 8
    return pl.pallas_call(
        _dest_kernel,
        grid=(t // tm,),
        in_specs=[pl.BlockSpec((tm, LANES), lambda i: (i, 0)), pl.BlockSpec((1, LANES), lambda i: (0, 0))],
        out_specs=[pl.BlockSpec((tm, LANES), lambda i: (i, 0)), pl.BlockSpec((nb_pad, LANES), lambda i: (0, 0)),
                   pl.BlockSpec((1, LANES), lambda i: (0, 0))],
        out_shape=[jax.ShapeDtypeStruct((t, LANES), I32), jax.ShapeDtypeStruct((nb_pad, LANES), I32),
                   jax.ShapeDtypeStruct((1, LANES), I32)],
        compiler_params=_params(("arbitrary",)),
        name="dest",
    )(ri, cnt)


ROW_UNROLL = 8


def _drain_tokens(src_ref, dst_ref, sem, n_tokens):
    def drain(_, carry):
        for _ in range(ROW_UNROLL * TOP_K):
            _token_copy(src_ref, 0, dst_ref, 0, sem).wait()
        return carry

    lax.fori_loop(0, n_tokens // ROW_UNROLL, drain, 0)


def _dispatch_kernel(dest_ref, h_ref, xbuf_ref, sem):
    tm = h_ref.shape[0] // TOKEN_TILE_ROWS

    def issue(g, carry):
        for u in range(ROW_UNROLL):
            r = g * ROW_UNROLL + u
            for kk in range(TOP_K):
                _token_copy(h_ref, r, xbuf_ref, dest_ref[r * TOP_K + kk], sem).start(priority=kk % 2)
        return carry

    lax.fori_loop(0, tm // ROW_UNROLL, issue, 0)
    _drain_tokens(h_ref, xbuf_ref, sem, tm)


def _dispatch(dest_flat, h2_tiles, n_rows):
    tm = ROUTE_TILE
    t = h2_tiles.shape[0] // TOKEN_TILE_ROWS
    return pl.pallas_call(
        _dispatch_kernel,
        grid=(t // tm,),
        in_specs=[pl.BlockSpec((tm * TOP_K,), lambda i: (i,), memory_space=pltpu.SMEM),
                  pl.BlockSpec((tm * TOKEN_TILE_ROWS, LANES), lambda i: (i, 0))],
        out_specs=pl.BlockSpec(memory_space=pl.ANY),
        out_shape=jax.ShapeDtypeStruct((n_rows * TOKEN_TILE_ROWS, LANES), h2_tiles.dtype),
        scratch_shapes=[pltpu.SemaphoreType.DMA(())],
        compiler_params=_params(("arbitrary",)),
        name="dispatch",
    )(dest_flat, h2_tiles)


def _experts_kernel(blk_e_ref, nvalid_ref, x_ref, wgu_ref, bgu_ref, wd_ref, bd_ref, y_ref, wgu16_ref, wd16_ref):
    b = pl.program_id(0)
    nvalid = nvalid_ref[b]

    @pl.when((b == 0) | (blk_e_ref[b] != blk_e_ref[jnp.maximum(b - 1, 0)]))
    def _():
        wgu16_ref[...] = wgu_ref[...].astype(BF16)
        wd16_ref[...] = wd_ref[...].astype(BF16)

    @pl.when(nvalid > 0)
    def _():
        n = x_ref.shape[0] // TOKEN_TILE_ROWS
        x = jnp.concatenate([_load_token_slab(x_ref, s, n) for s in range(TOKEN_TILE_ROWS)], axis=1)
        rows = lax.broadcasted_iota(I32, x.shape, 0)
        x = jnp.where(rows < nvalid, x, 0.0).astype(BF16)
        gu = _dot(x, wgu16_ref[...]) + bgu_ref[...]
        gate = jnp.minimum(gu[:, :FF], SWIGLU_LIMIT)
        up = jnp.clip(gu[:, FF:], -SWIGLU_LIMIT, SWIGLU_LIMIT)
        act = (up + 1.0) * (gate * _sigmoid(SWIGLU_ALPHA * gate))
        _store_token_tiles(y_ref, _dot(act.astype(BF16), wd16_ref[...]) + bd_ref[...])

    @pl.when(nvalid <= 0)
    def _():
        y_ref[...] = jnp.zeros_like(y_ref)


def _experts(blk_e, nvalid, xbuf, wgu, bgu, wd, bd):
    d = D_MODEL
    blk_rows = EXPERT_BLOCK * TOKEN_TILE_ROWS
    nb = xbuf.shape[0] // blk_rows
    grid_spec = pltpu.PrefetchScalarGridSpec(
        num_scalar_prefetch=2,
        grid=(nb,),
        in_specs=[pl.BlockSpec((blk_rows, LANES), lambda b, e, n: (b, 0)),
                  pl.BlockSpec((None, d, 2 * FF), lambda b, e, n: (e[b], 0, 0)),
                  pl.BlockSpec((None, 1, 2 * FF), lambda b, e, n: (e[b], 0, 0)),
                  pl.BlockSpec((None, FF, d), lambda b, e, n: (e[b], 0, 0)),
                  pl.BlockSpec((None, 1, d), lambda b, e, n: (e[b], 0, 0))],
        out_specs=pl.BlockSpec((blk_rows, LANES), lambda b, e, n: (b, 0)),
        scratch_shapes=[pltpu.VMEM((d, 2 * FF), BF16), pltpu.VMEM((FF, d), BF16)],
    )
    return pl.pallas_call(
        _experts_kernel,
        grid_spec=grid_spec,
        out_shape=jax.ShapeDtypeStruct(xbuf.shape, F32),
        compiler_params=_params(("arbitrary",)),
        name="experts",
    )(blk_e, nvalid, xbuf, wgu, bgu, wd, bd)


def _combine_kernel(dest_ref, dest_next_ref, ybuf_ref, x1_ref, gate_ref, gt2_ref, gf_ref, o_ref, rows_ref, sems):
    i = pl.program_id(0)
    tm = x1_ref.shape[0]
    slot = i % 2

    def start_gather(idx_ref, s):
        def issue(g, carry):
            for u in range(ROW_UNROLL):
                r = g * ROW_UNROLL + u
                for kk in range(TOP_K):
                    _token_copy(ybuf_ref, idx_ref[r * TOP_K + kk], rows_ref.at[s, kk], r,
                                sems.at[s]).start(priority=kk % 2)
            return carry

        lax.fori_loop(0, tm // ROW_UNROLL, issue, 0)

    @pl.when(i == 0)
    def _():
        start_gather(dest_ref, 0)

    @pl.when(i + 1 < pl.num_programs(0))
    def _():
        start_gather(dest_next_ref, 1 - slot)

    _drain_tokens(ybuf_ref, rows_ref.at[slot, 0], sems.at[slot], tm)

    gates = gate_ref[...]
    slabs = []
    for s in range(TOKEN_TILE_ROWS):
        moe = gates[:, 0:1] * _load_token_slab(rows_ref.at[slot, 0], s, tm)
        for kk in range(1, TOP_K):
            moe = moe + gates[:, kk:kk + 1] * _load_token_slab(rows_ref.at[slot, kk], s, tm)
        slabs.append(moe)
    xo = x1_ref[...] + gt2_ref[...] * jnp.concatenate(slabs, axis=1)
    o_ref[...] = xo * lax.rsqrt(jnp.mean(xo * xo, axis=-1, keepdims=True) + RMS_EPS) * gf_ref[...]


def _combine(dest_flat, ybuf, x1, gates, gt2, g_final):
    t, d = x1.shape
    tm = ROUTE_TILE
    n = t // tm
    return pl.pallas_call(
        _combine_kernel,
        grid=(n,),
        in_specs=[pl.BlockSpec((tm * TOP_K,), lambda i: (i,), memory_space=pltpu.SMEM),
                  pl.BlockSpec((tm * TOP_K,), lambda i: (jnp.minimum(i + 1, n - 1),), memory_space=pltpu.SMEM),
                  pl.BlockSpec(memory_space=pl.ANY),
                  pl.BlockSpec((tm, d), lambda i: (i, 0)),
                  pl.BlockSpec((tm, LANES), lambda i: (i, 0)),
                  pl.BlockSpec((1, d), lambda i: (0, 0)),
                  pl.BlockSpec((1, d), lambda i: (0, 0))],
        out_specs=pl.BlockSpec((tm, d), lambda i: (i, 0)),
        out_shape=jax.ShapeDtypeStruct((t, d), F32),
        scratch_shapes=[pltpu.VMEM((2, TOP_K, tm * TOKEN_TILE_ROWS, LANES), F32), pltpu.SemaphoreType.DMA((2,))],
        compiler_params=_params(("arbitrary",)),
        name="combine",
    )(dest_flat, dest_flat, ybuf, x1, gates, gt2, g_final)


def _pad_lanes(v, lane0):
    return jnp.zeros((1, LANES), F32).at[0, lane0:lane0 + v.shape[0]].set(v.astype(F32))


def _layer(x, mod, g_norm1, w_in, conv_w, a_log, dt_bias, g_onorm, b_fgate, w_o_delta, w_o_fox, w_out,
           g_norm2, w_router, b_router, w_gate_up, b_gate_up, w_down, b_down, g_final):
    t, d = x.shape
    sh1, sc1, gt1, sh2, sc2, gt2 = [mod[:, i * d:(i + 1) * d] for i in range(N_MOD)]

    w_t = w_in.T
    o = 0
    wq = w_t[o:o + QKV_W]; o += QKV_W
    wz = w_t[o:o + MIX_W]; o += MIX_W
    w_beta = w_t[o:o + N_HEADS]; o += N_HEADS
    w_dec = w_t[o:o + N_HEADS]; o += N_HEADS
    wf = w_t[o:o + QKV_W]; o += QKV_W
    wf = jnp.concatenate([wf[:MIX_W] * (LOG2E * HEAD_DIM ** -0.5), wf[MIX_W:]], axis=0)
    w_fg = w_t[o:o + N_HEADS]; o += N_HEADS
    wga = w_t[o:o + d]; o += d
    wgb = w_t[o:o + d]
    ws = jnp.zeros((LANES, d), F32)
    ws = ws.at[L_BETA:L_BETA + N_HEADS].set(w_beta).at[L_G:L_G + N_HEADS].set(w_dec)
    ws = ws.at[L_F:L_F + N_HEADS].set(w_fg)
    bf = lambda w: w.astype(BF16)

    tm = min(ROW_TILE, t)
    qkv, z, fx, gate_a, gate_b, small = _inproj(
        x, g_norm1.reshape(1, d), sc1, sh1, bf(wq), bf(wz), bf(wf), bf(wga), bf(wgb), bf(ws), tm)

    gcol, grow = _gates(small, _pad_lanes(a_log, L_G), _pad_lanes(dt_bias, L_G), _pad_lanes(b_fgate, L_F),
                        min(GATES_TILE, t))

    o_a = _delta(qkv, z, gcol, grow, conv_w, g_onorm.reshape(1, HEAD_DIM), min(DELTA_TILE, t))
    tq, tk = min(FOX_TQ, t), min(FOX_TK, t)
    f_rows = grow[L_F:L_F + N_HEADS].reshape(N_HEADS, t // tk, 1, tk)
    o_b = _fox(fx, f_rows, tq, tk)

    wr = jnp.zeros((d, LANES), F32).at[:, :N_EXPERTS].set(w_router)
    br = _pad_lanes(b_router, 0)
    x1, h2, ri, gates, cnt = _merge(o_a, o_b, gate_a, gate_b, x, bf(w_o_delta), bf(w_o_fox), bf(w_out), gt1,
                                    g_norm2.reshape(1, d), sc2, sh2, wr, br, tm)

    n_blocks = (t * TOP_K) // EXPERT_BLOCK + N_EXPERTS
    dest, blk, _ = _dest(ri, cnt, n_blocks, min(2048, t))
    dest_flat = dest[:, :TOP_K].reshape(t * TOP_K)
    blk_e, nvalid = blk[:n_blocks, 0], blk[:n_blocks, 1]

    xbuf = _dispatch(dest_flat, h2, n_blocks * EXPERT_BLOCK)
    ybuf = _experts(blk_e, nvalid, xbuf, w_gate_up, b_gate_up.reshape(N_EXPERTS, 1, 2 * FF),
                    w_down, b_down.reshape(N_EXPERTS, 1, d))
    return _combine(dest_flat, ybuf, x1, gates, gt2, g_final.reshape(1, d))


def kernel(x, c, w_ada, b_ada, g_norm1, w_in, conv_w, a_log, dt_bias, g_onorm, b_fgate, w_o_delta, w_o_fox, w_out,
           g_norm2, w_router, b_router, w_gate_up, b_gate_up, w_down, b_down, g_final):
    b, s, d = x.shape
    assert b == 1 and d == D_MODEL and w_ada.shape[0] == 1
    h = x[0]
    for l in range(w_ada.shape[0]):
        mod = _modulation(c[0], w_ada[l], b_ada[l])
        h = _layer(h, mod, g_norm1[l], w_in[l], conv_w[l], a_log[l], dt_bias[l], g_onorm[l], b_fgate[l],
                   w_o_delta[l], w_o_fox[l], w_out[l], g_norm2[l], w_router[l], b_router[l], w_gate_up[l],
                   b_gate_up[l], w_down[l], b_down[l], g_final)
    return h[None]
```

```python
import functools

import jax
import jax.numpy as jnp
from jax import lax
from jax.experimental import pallas as pl
from jax.experimental.pallas import tpu as pltpu

F32 = jnp.float32
BF16 = jnp.bfloat16
I32 = jnp.int32

D_MODEL = 1024
HEAD_DIM = 128
N_HEADS = 4
MIX_W = N_HEADS * HEAD_DIM
QKV_W = 3 * MIX_W
CONV_K = 4
CHUNK = 64
STACK = N_HEADS * CHUNK
SUB = 16
N_EXPERTS = 32
TOP_K = 4
FF = D_MODEL
SWIGLU_LIMIT = 7.0
SWIGLU_ALPHA = 1.702
RMS_EPS = 1e-6
L2_EPS = 1e-6
N_MOD = 6
LANES = 128
NEG_BIG = -1e30
LOG2E = 1.4426950408889634

ROW_TILE = 512
GATES_TILE = 256
DELTA_TILE = 256
EXPERT_BLOCK = 512
ROUTE_TILE = 256
VMEM_LIMIT = 56 * 1024 * 1024


def _params(sem, vmem=VMEM_LIMIT, flags=None):
    return pltpu.CompilerParams(dimension_semantics=sem, vmem_limit_bytes=vmem, flags=flags)


def _softplus(x):
    return jnp.maximum(x, 0.0) + jnp.log(1.0 + jnp.exp(-jnp.abs(x)))


def _sigmoid(x):
    return 1.0 / (1.0 + jnp.exp(-x))


def _silu(x):
    return x * _sigmoid(x)


def _dot(a, b, precision=None):
    return jnp.dot(a, b, preferred_element_type=F32, precision=precision)


def _dot_nt(a, b, precision=None):
    return lax.dot_general(a, b, (((1,), (1,)), ((), ())), preferred_element_type=F32, precision=precision)


def _split_bf16(x, terms):
    parts = []
    for _ in range(terms):
        p = x.astype(BF16)
        parts.append(p)
        x = x - p.astype(F32)
    return parts


def _dot_split(a, b):
    a_hi, a_lo = _split_bf16(a, 2)
    b_hi, b_lo = _split_bf16(b, 2)
    return _dot(a_hi, b_hi) + (_dot(a_hi, b_lo) + _dot(a_lo, b_hi))


def _dot_mask(mask01, x):
    return sum(_dot(mask01, p) for p in _split_bf16(x, 3))


TOKEN_TILE_ROWS = D_MODEL // LANES


def _store_token_tiles(ref, x):
    n = x.shape[0]
    for s in range(TOKEN_TILE_ROWS):
        ref[pl.ds(s, n, stride=TOKEN_TILE_ROWS), :] = x[:, s * LANES:(s + 1) * LANES]


def _load_token_slab(ref, s, n):
    return ref[pl.ds(s, n, stride=TOKEN_TILE_ROWS), :]


def _token_copy(src_ref, src_tok, dst_ref, dst_tok, sem):
    rows = lambda i: pl.ds(pl.multiple_of(i * TOKEN_TILE_ROWS, TOKEN_TILE_ROWS), TOKEN_TILE_ROWS)
    return pltpu.make_async_copy(src_ref.at[rows(src_tok), :], dst_ref.at[rows(dst_tok), :], sem)


def _mod_kernel(c_ref, w_ref, b_ref, o_ref):
    o_ref[...] = jnp.sum(c_ref[...] * w_ref[...], axis=0, keepdims=True) + b_ref[...]


def _modulation(c, w_ada, b_ada):
    d, n = w_ada.shape
    tn = 1024
    return pl.pallas_call(
        _mod_kernel,
        grid=(n // tn,),
        in_specs=[pl.BlockSpec((d, 1), lambda j: (0, 0)),
                  pl.BlockSpec((d, tn), lambda j: (0, j)),
                  pl.BlockSpec((1, tn), lambda j: (0, j))],
        out_specs=pl.BlockSpec((1, tn), lambda j: (0, j)),
        out_shape=jax.ShapeDtypeStruct((1, n), F32),
        compiler_params=_params(("arbitrary",)),
        name="mod",
    )(c.reshape(d, 1), w_ada, b_ada.reshape(1, n))


def _inproj_kernel(x_ref, g_ref, sc_ref, sh_ref, wq_ref, wz_ref, wf_ref, wga_ref, wgb_ref, ws_ref,
                   oq_ref, oz_ref, of_ref, oga_ref, ogb_ref, os_ref):
    x = x_ref[...]
    y = x * lax.rsqrt(jnp.mean(x * x, axis=-1, keepdims=True) + RMS_EPS) * g_ref[...]
    h = (y * (1.0 + sc_ref[...]) + sh_ref[...]).astype(BF16)
    oq_ref[...] = _dot_nt(h, wq_ref[...])
    oz_ref[...] = _dot_nt(h, wz_ref[...]).astype(BF16)
    of_ref[...] = _dot_nt(h, wf_ref[...]).astype(BF16)
    oga_ref[...] = _dot_nt(h, wga_ref[...]).astype(BF16)
    ogb_ref[...] = _dot_nt(h, wgb_ref[...]).astype(BF16)
    os_ref[...] = _dot_nt(h, ws_ref[...])


def _inproj(x, g1, sc1, sh1, wq, wz, wf, wga, wgb, ws, tm):
    t, d = x.shape
    row = lambda i: (i, 0)
    fixed = lambda i: (0, 0)
    ws_list = [wq, wz, wf, wga, wgb, ws]
    out_dt = [F32, BF16, BF16, BF16, BF16, F32]
    return pl.pallas_call(
        _inproj_kernel,
        grid=(t // tm,),
        in_specs=[pl.BlockSpec((tm, d), row)] + [pl.BlockSpec((1, d), fixed)] * 3
                 + [pl.BlockSpec(w.shape, fixed) for w in ws_list],
        out_specs=[pl.BlockSpec((tm, w.shape[0]), row) for w in ws_list],
        out_shape=[jax.ShapeDtypeStruct((t, w.shape[0]), dt) for w, dt in zip(ws_list, out_dt)],
        compiler_params=_params(("arbitrary",)),
        name="inproj",
    )(x, g1, sc1, sh1, *ws_list)


L_BETA, L_G, L_F, L_GC, L_GL = 0, 4, 8, 12, 16


def _gates_kernel(s_ref, alog_ref, dtb_ref, bf_ref, col_ref, row_ref, carry_ref):
    i = pl.program_id(0)
    tm = s_ref.shape[0]

    @pl.when(i == 0)
    def _():
        carry_ref[...] = jnp.zeros_like(carry_ref)

    s = s_ref[...]
    lane = lax.broadcasted_iota(I32, s.shape, 1)
    beta = _sigmoid(s)
    g = -jnp.exp(alog_ref[...]) * _softplus(s + dtb_ref[...])
    logf = -_softplus(-(s + bf_ref[...]))
    is_g = (lane >= L_G) & (lane < L_G + N_HEADS)
    is_f = (lane >= L_F) & (lane < L_F + N_HEADS)
    g = jnp.where(is_g, g, 0.0)
    logf = jnp.where(is_f, logf, 0.0)

    r = lax.broadcasted_iota(I32, (tm, tm), 0)
    c = lax.broadcasted_iota(I32, (tm, tm), 1)
    same_chunk = (r // CHUNK) == (c // CHUNK)
    tri = jnp.where(r >= c, 1.0, 0.0).astype(BF16)
    tri_chunk = jnp.where(same_chunk & (r >= c), 1.0, 0.0).astype(BF16)
    ones_chunk = jnp.where(same_chunk, 1.0, 0.0).astype(BF16)
    f_cum = _dot_mask(tri, logf) + carry_ref[...]
    carry_ref[...] = f_cum[tm - 1:tm, :]
    gc = _dot_mask(tri_chunk, g)
    gl = _dot_mask(ones_chunk, g)

    out = jnp.where(lane < N_HEADS, beta, 0.0) + g + f_cum
    out = out + pltpu.roll(gc, L_GC - L_G, 1) + pltpu.roll(gl, L_GL - L_G, 1)
    col_ref[...] = out
    row_ref[...] = out.T[:row_ref.shape[0], :]


def _gates(small, alog_row, dtb_row, bf_row, tm):
    t = small.shape[0]
    n_rows = 24
    return pl.pallas_call(
        _gates_kernel,
        grid=(t // tm,),
        in_specs=[pl.BlockSpec((tm, LANES), lambda i: (i, 0))] + [pl.BlockSpec((1, LANES), lambda i: (0, 0))] * 3,
        out_specs=[pl.BlockSpec((tm, LANES), lambda i: (i, 0)), pl.BlockSpec((n_rows, tm), lambda i: (0, i))],
        out_shape=[jax.ShapeDtypeStruct((t, LANES), F32), jax.ShapeDtypeStruct((n_rows, t), F32)],
        scratch_shapes=[pltpu.VMEM((1, LANES), F32)],
        compiler_params=_params(("arbitrary",)),
        name="gates",
    )(small, alog_row, dtb_row, bf_row)


def _stack_heads(a, col0):
    return jnp.concatenate([a[:, col0 + h * HEAD_DIM: col0 + (h + 1) * HEAD_DIM] for h in range(N_HEADS)], axis=0)


def _stack_cols(a, lane0):
    return jnp.concatenate([a[:, lane0 + h: lane0 + h + 1] for h in range(N_HEADS)], axis=0)


def _delta_kernel(qkv_ref, z_ref, gcol_ref, grow_ref, cw_ref, gon_ref, o_ref, ext_ref, tail_ref, s_ref):
    i = pl.program_id(0)
    tb = qkv_ref.shape[0]
    pad = tail_ref.shape[0]

    @pl.when(i == 0)
    def _():
        tail_ref[...] = jnp.zeros_like(tail_ref)
        s_ref[...] = jnp.zeros_like(s_ref)

    ext_ref[0:pad, :] = tail_ref[...]
    ext_ref[pad:pad + tb, :] = qkv_ref[...]
    tail_ref[...] = qkv_ref[tb - pad:tb, :]
    conv = cw_ref[0:1, :] * ext_ref[pad - 3:pad - 3 + tb, :]
    for j in range(1, CONV_K):
        conv = conv + cw_ref[j:j + 1, :] * ext_ref[pad - 3 + j:pad - 3 + j + tb, :]
    act = _silu(conv)

    r = lax.broadcasted_iota(I32, (STACK, STACK), 0)
    c = lax.broadcasted_iota(I32, (STACK, STACK), 1)
    same = (r // CHUNK) == (c // CHUNK)
    m_incl = same & (r >= c)
    m_strict = same & (r > c)
    eye = jnp.where(r == c, 1.0, 0.0)
    m_sub = []
    size = SUB
    while size <= CHUNK:
        m_sub.append((r // size) == (c // size))
        size *= 2
    rb = lax.broadcasted_iota(I32, (STACK, N_HEADS * HEAD_DIM), 0) // CHUNK
    cb = lax.broadcasted_iota(I32, (STACK, N_HEADS * HEAD_DIM), 1) // HEAD_DIM
    head_match = rb == cb

    chunks = range(tb // CHUNK)
    pre = []
    for ch in chunks:
        r0 = ch * CHUNK
        a = act[r0:r0 + CHUNK, :]
        q = _stack_heads(a, 0)
        k = _stack_heads(a, MIX_W)
        v = _stack_heads(a, 2 * MIX_W)
        q = q * lax.rsqrt(jnp.sum(q * q, axis=-1, keepdims=True) + L2_EPS) * (HEAD_DIM ** -0.5)
        k = k * lax.rsqrt(jnp.sum(k * k, axis=-1, keepdims=True) + L2_EPS)

        gcols = gcol_ref[r0:r0 + CHUNK, :]
        beta = _stack_cols(gcols, L_BETA)
        gc = _stack_cols(gcols, L_GC)
        gc_row = jnp.concatenate(
            [grow_ref[L_GC + h:L_GC + h + 1, r0:r0 + CHUNK] for h in range(N_HEADS)], axis=1)
        gl_row = jnp.concatenate(
            [grow_ref[L_GL + h:L_GL + h + 1, r0:r0 + CHUNK] for h in range(N_HEADS)], axis=1)

        decay = jnp.exp(jnp.where(m_incl, gc - gc_row, NEG_BIG))
        kb16 = k.astype(BF16)
        kk = _dot_nt(kb16, kb16)
        lmat = jnp.where(m_strict, kk * beta * decay, 0.0)
        attn = (_dot_nt(q.astype(BF16), kb16) * decay).astype(BF16)
        egc = jnp.exp(gc)
        rhs = jnp.concatenate([v * beta, k * (beta * egc)], axis=1).astype(BF16)
        kt_dec = (k.T * jnp.exp(gl_row - gc_row)).astype(BF16)
        pre.append(dict(lmat=lmat, attn=attn, rhs=rhs, kt_dec=kt_dec, q_dec=q * egc))

    l_d = [jnp.where(m_sub[0], p["lmat"], 0.0) for p in pre]
    inv = [eye - l for l in l_d]
    l16 = [l.astype(BF16) for l in l_d]
    pw = [_dot(l, l).astype(BF16) for l in l16]
    n_sq = SUB.bit_length() - 3
    for s in range(n_sq + 1):
        inv = [x + _dot(x.astype(BF16), p) for x, p in zip(inv, pw)]
        if s < n_sq:
            pw = [_dot(p, p).astype(BF16) for p in pw]
    for lvl in range(1, len(m_sub)):
        m_off = m_sub[lvl] & ~m_sub[lvl - 1]
        off = [jnp.where(m_off, p["lmat"], 0.0).astype(BF16) for p in pre]
        inv16 = [x.astype(BF16) for x in inv]
        half = [_dot(x, o).astype(BF16) for x, o in zip(inv16, off)]
        inv = [x - _dot(h, x16) for x, h, x16 in zip(inv, half, inv16)]
    uws = [_dot(x.astype(BF16), p["rhs"]) for x, p in zip(inv, pre)]

    for ch in chunks:
        r0 = ch * CHUNK
        attn, kt_dec = pre[ch]["attn"], pre[ch]["kt_dec"]
        u, w = uws[ch][:, :HEAD_DIM], uws[ch][:, HEAD_DIM:]

        state16 = s_ref[...].astype(BF16)
        wq_s = _dot(jnp.concatenate([w, pre[ch]["q_dec"]], axis=0).astype(BF16), state16)
        ws_d = jnp.concatenate([wq_s[h * CHUNK:(h + 1) * CHUNK, h * HEAD_DIM:(h + 1) * HEAD_DIM]
                                for h in range(N_HEADS)], axis=0)
        qs_d = jnp.concatenate([wq_s[STACK + h * CHUNK:STACK + (h + 1) * CHUNK, h * HEAD_DIM:(h + 1) * HEAD_DIM]
                                for h in range(N_HEADS)], axis=0)
        v_new = u - ws_d
        v16 = v_new.astype(BF16)
        o = qs_d + _dot(attn, v16)

        v_bd =jnp.where(head_match, jnp.concatenate([v_new] * N_HEADS, axis=1), 0.0).astype(BF16)
        upd = _dot(kt_dec, v_bd)
        for h in range(N_HEADS):
            e_h = jnp.exp(grow_ref[L_GL + h:L_GL + h + 1, r0:r0 + 1])
            sl = slice(h * HEAD_DIM, (h + 1) * HEAD_DIM)
            s_ref[:, sl] = s_ref[:, sl] * e_h + upd[:, sl]

        for h in range(N_HEADS):
            oh = o[h * CHUNK:(h + 1) * CHUNK, :]
            oh = oh * lax.rsqrt(jnp.mean(oh * oh, axis=-1, keepdims=True) + RMS_EPS) * gon_ref[...]
            zh = z_ref[r0:r0 + CHUNK, h * HEAD_DIM:(h + 1) * HEAD_DIM].astype(F32)
            o_ref[r0:r0 + CHUNK, h * HEAD_DIM:(h + 1) * HEAD_DIM] = (oh * _silu(zh)).astype(o_ref.dtype)


def _delta(qkv, z, gcol, grow, conv_w, g_onorm, tb):
    t = qkv.shape[0]
    pad = 8
    return pl.pallas_call(
        _delta_kernel,
        grid=(t // tb,),
        in_specs=[pl.BlockSpec((tb, QKV_W), lambda i: (i, 0)),
                  pl.BlockSpec((tb, MIX_W), lambda i: (i, 0)),
                  pl.BlockSpec((tb, LANES), lambda i: (i, 0)),
                  pl.BlockSpec((grow.shape[0], tb), lambda i: (0, i)),
                  pl.BlockSpec((CONV_K, QKV_W), lambda i: (0, 0)),
                  pl.BlockSpec((1, HEAD_DIM), lambda i: (0, 0))],
        out_specs=pl.BlockSpec((tb, MIX_W), lambda i: (i, 0)),
        out_shape=jax.ShapeDtypeStruct((t, MIX_W), BF16),
        scratch_shapes=[pltpu.VMEM((tb + pad, QKV_W), F32),
                        pltpu.VMEM((pad, QKV_W), F32),
                        pltpu.VMEM((HEAD_DIM, N_HEADS * HEAD_DIM), F32)],
        compiler_params=_params(("arbitrary",)),
        name="delta",
    )(qkv, z, gcol, grow, conv_w, g_onorm)


FOX_GROUP = 4
FOX_TQ, FOX_TK = 256, 2048
FOX_DIAG_PARTS = 2


def _fox_kernel(*refs, tk):
    g = FOX_GROUP
    q_ref, o_ref = refs[0], refs[-1]
    k_refs, v_refs, f_refs = (refs[1 + i * g:1 + (i + 1) * g] for i in range(3))
    qi = pl.program_id(1)
    tq = q_ref.shape[0]
    def step(off, width, f_blk, carry, mask):
        ones = jnp.ones((width, HEAD_DIM), BF16)
        ss = []
        for h in range(g):
            q = q_ref[:, h * HEAD_DIM:(h + 1) * HEAD_DIM]
            s = _dot_nt(q, k_refs[h][pl.ds(off, width), :]) - f_blk(h) * LOG2E
            if mask is not None:
                s = jnp.where(mask, s, NEG_BIG)
            ss.append(s)
        ps, ms, alphas = [], [], []
        for h in range(g):
            m = carry[h][0]
            m_new = jnp.maximum(m, jnp.max(ss[h], axis=-1, keepdims=True))
            alphas.append(jnp.exp2(m - m_new))
            ps.append(jnp.exp2(ss[h] - m_new).astype(BF16))
            ms.append(m_new)
        out = []
        for h in range(g):
            v_aug = jnp.concatenate([v_refs[h][pl.ds(off, width), :], ones], axis=1)
            out.append((ms[h], alphas[h] * carry[h][1] + _dot(ps[h], v_aug)))
        return tuple(out)

    def full_chunk(j, carry):
        return step(pl.multiple_of(j * tk, tk), tk, lambda h: f_refs[h][j], carry, None)

    init = tuple((jnp.full((tq, 1), NEG_BIG, F32), jnp.zeros((tq, 2 * HEAD_DIM), F32)) for _ in range(g))
    n_full = (qi * tq) // tk
    carry = lax.fori_loop(0, n_full, full_chunk, init)

    parts = max(1, min(FOX_DIAG_PARTS, tk // tq))
    width = tk // parts
    row = lax.broadcasted_iota(I32, (tq, width), 0)
    col = lax.broadcasted_iota(I32, (tq, width), 1)
    for d in range(max(tq // tk, 1)):
        j = n_full + d
        for p in range(parts):
            off = pl.multiple_of(j * tk + p * width, width)
            piece = functools.partial(step, off, width,
                                      lambda h, j=j, p=p: f_refs[h][j][:, p * width:(p + 1) * width],
                                      mask=col + (off - qi * tq) <= row)
            carry = piece(carry) if p == 0 else lax.cond(off < (qi + 1) * tq, piece, lambda c: c, carry)
    for h in range(g):
        acc = carry[h][1]
        o_ref[:, h * HEAD_DIM:(h + 1) * HEAD_DIM] = (acc[:, :HEAD_DIM] / acc[:, HEAD_DIM:]).astype(o_ref.dtype)


def _fox(fx, f_rows, tq, tk):
    t = fx.shape[0]
    g = FOX_GROUP
    heads = lambda hg, u: hg * g + u
    q_specs = [pl.BlockSpec((tq, g * HEAD_DIM), lambda hg, i: (i, hg))]
    k_specs = [pl.BlockSpec((t, HEAD_DIM), lambda hg, i, u=u: (0, N_HEADS + heads(hg, u)),
                            pipeline_mode=pl.Buffered(1)) for u in range(g)]
    v_specs = [pl.BlockSpec((t, HEAD_DIM), lambda hg, i, u=u: (0, 2 * N_HEADS + heads(hg, u)),
                            pipeline_mode=pl.Buffered(1)) for u in range(g)]
    f_specs = [pl.BlockSpec((None, t // tk, 1, tk), lambda hg, i, u=u: (heads(hg, u), 0, 0, 0)) for u in range(g)]
    return pl.pallas_call(
        functools.partial(_fox_kernel, tk=tk),
        grid=(N_HEADS // g, t // tq),
        in_specs=q_specs + k_specs + v_specs + f_specs,
        out_specs=pl.BlockSpec((tq, g * HEAD_DIM), lambda hg, i: (i, hg)),
        out_shape=jax.ShapeDtypeStruct((t, MIX_W), BF16),
        compiler_params=_params(("arbitrary", "arbitrary")),
        name="fox",
    )(*([fx] * (1 + 2 * g) + [f_rows] * g))


def _merge_kernel(oa_ref, ob_ref, ga_ref, gb_ref, x_ref, woa_ref, wob_ref, wout_ref, gt1_ref, g2_ref, sc2_ref,
                  sh2_ref, wr_ref, br_ref, x1_ref, h2_ref, ri_ref, gate_ref, cnt_ref, carry_ref):
    i = pl.program_id(0)
    tm = x_ref.shape[0]

    @pl.when(i == 0)
    def _():
        carry_ref[...] = jnp.zeros_like(carry_ref)

    ya = _dot(oa_ref[...], woa_ref[...])
    yb = _dot(ob_ref[...], wob_ref[...])
    merged = _sigmoid(ga_ref[...].astype(F32)) * ya + _sigmoid(gb_ref[...].astype(F32)) * yb
    x1 = x_ref[...] + gt1_ref[...] * _dot(merged.astype(BF16), wout_ref[...])
    x1_ref[...] = x1
    y = x1 * lax.rsqrt(jnp.mean(x1 * x1, axis=-1, keepdims=True) + RMS_EPS) * g2_ref[...]
    h2 = y * (1.0 + sc2_ref[...]) + sh2_ref[...]
    _store_token_tiles(h2_ref, h2)

    logits = _dot_split(h2, wr_ref[...]) + br_ref[...]
    lane = lax.broadcasted_iota(I32, logits.shape, 1)
    lanef = lane.astype(F32)
    cur = jnp.where(lane < N_EXPERTS, logits, -jnp.inf)
    vals, idxs = [], []
    for _ in range(TOP_K):
        m = jnp.max(cur, axis=-1, keepdims=True)
        ix = jnp.min(jnp.where(cur == m, lanef, float(LANES)), axis=-1, keepdims=True)
        vals.append(m)
        idxs.append(ix)
        cur = jnp.where(lanef == ix, -jnp.inf, cur)
    exps = [jnp.exp(v - vals[0]) for v in vals]
    denom = exps[0] + exps[1] + exps[2] + exps[3]

    onehot = jnp.zeros(logits.shape, F32)
    for ix in idxs:
        onehot = onehot + jnp.where(lanef == ix, 1.0, 0.0)
    r = lax.broadcasted_iota(I32, (tm, tm), 0)
    c = lax.broadcasted_iota(I32, (tm, tm), 1)
    strict = jnp.where(r > c, 1.0, 0.0).astype(BF16)
    before = _dot(strict, onehot.astype(BF16)) + carry_ref[...]
    carry_ref[...] = carry_ref[...] + jnp.sum(onehot, axis=0, keepdims=True)
    cnt_ref[...] = carry_ref[...].astype(I32)

    ri = jnp.zeros(logits.shape, F32)
    gates = jnp.zeros(logits.shape, F32)
    for kk in range(TOP_K):
        rank = jnp.sum(jnp.where(lanef == idxs[kk], before, 0.0), axis=-1, keepdims=True)
        ri = ri + jnp.where(lane == kk, idxs[kk], 0.0) + jnp.where(lane == TOP_K + kk, rank, 0.0)
        gates = gates + jnp.where(lane == kk, exps[kk] / denom, 0.0)
    ri_ref[...] = ri.astype(I32)
    gate_ref[...] = gates


def _merge(o_a, o_b, gate_a, gate_b, x, woa, wob, wout, gt1, g2, sc2, sh2, wr, br, tm):
    t, d = x.shape
    row = lambda i: (i, 0)
    fixed = lambda i: (0, 0)
    return pl.pallas_call(
        _merge_kernel,
        grid=(t // tm,),
        in_specs=[pl.BlockSpec((tm, MIX_W), row), pl.BlockSpec((tm, MIX_W), row),
                  pl.BlockSpec((tm, d), row), pl.BlockSpec((tm, d), row), pl.BlockSpec((tm, d), row),
                  pl.BlockSpec((MIX_W, d), fixed), pl.BlockSpec((MIX_W, d), fixed), pl.BlockSpec((d, d), fixed),
                  pl.BlockSpec((1, d), fixed), pl.BlockSpec((1, d), fixed), pl.BlockSpec((1, d), fixed),
                  pl.BlockSpec((1, d), fixed), pl.BlockSpec((d, LANES), fixed), pl.BlockSpec((1, LANES), fixed)],
        out_specs=[pl.BlockSpec((tm, d), row), pl.BlockSpec((tm * TOKEN_TILE_ROWS, LANES), row),
                   pl.BlockSpec((tm, LANES), row),
                   pl.BlockSpec((tm, LANES), row), pl.BlockSpec((1, LANES), fixed)],
        out_shape=[jax.ShapeDtypeStruct((t, d), F32), jax.ShapeDtypeStruct((t * TOKEN_TILE_ROWS, LANES), F32),
                   jax.ShapeDtypeStruct((t, LANES), I32), jax.ShapeDtypeStruct((t, LANES), F32),
                   jax.ShapeDtypeStruct((1, LANES), I32)],
        scratch_shapes=[pltpu.VMEM((1, LANES), F32)],
        compiler_params=_params(("arbitrary",)),
        name="merge",
    )(o_a, o_b, gate_a, gate_b, x, woa, wob, wout, gt1, g2, sc2, sh2, wr, br)


def _dest_kernel(ri_ref, cnt_ref, dest_ref, blk_ref, pend_ref):
    shift = EXPERT_BLOCK.bit_length() - 1
    cnt = jnp.broadcast_to(cnt_ref[...], (8, LANES))
    lane_row = lax.broadcasted_iota(I32, cnt.shape, 1)
    padded = jnp.where(lane_row < N_EXPERTS, ((cnt + (EXPERT_BLOCK - 1)) >> shift) << shift, 0)
    pend = padded
    s = 1
    while s < N_EXPERTS:
        pend = pend + jnp.where(lane_row >= s, pltpu.roll(pend, s, 1), 0)
        s *= 2
    pstart = (pend - padded)[0:1, :].astype(F32)
    pend = pend[0:1, :]
    pend_ref[...] = pend

    ri = ri_ref[...]
    lane = lax.broadcasted_iota(I32, ri.shape, 1)
    dest = jnp.zeros(ri.shape, F32)
    for kk in range(TOP_K):
        ix = ri[:, kk:kk + 1]
        rank = ri[:, TOP_K + kk:TOP_K + kk + 1].astype(F32)
        start = jnp.sum(jnp.where(lane == ix, pstart, 0.0), axis=-1, keepdims=True)
        dest = dest + jnp.where(lane == kk, start + rank, 0.0)
    dest_ref[...] = dest.astype(I32)

    bstart = lax.broadcasted_iota(I32, blk_ref.shape, 0) * EXPERT_BLOCK
    lane_b = lax.broadcasted_iota(I32, blk_ref.shape, 1)
    ended = jnp.where((lane_b < N_EXPERTS) & (pend <= bstart), 1.0, 0.0)
    e = jnp.minimum(jnp.sum(ended, axis=-1, keepdims=True), float(N_EXPERTS - 1))
    vend = pstart + cnt[0:1, :].astype(F32)
    vend_e = jnp.sum(jnp.where(lane_b.astype(F32) == e, vend, 0.0), axis=-1, keepdims=True)
    nvalid = jnp.clip(vend_e - bstart[:, 0:1].astype(F32), 0.0, float(EXPERT_BLOCK))
    blk_ref[...] = jnp.where(lane_b == 0, e, jnp.where(lane_b == 1, nvalid, 0.0)).astype(I32)


def _dest(ri, cnt, n_blocks, tm):
    t = ri.shape[0]
    nb_pad = -(-n_blocks // 8) * 8
    return pl.pallas_call(
        _dest_kernel,
        grid=(t // tm,),
        in_specs=[pl.BlockSpec((tm, LANES), lambda i: (i, 0)), pl.BlockSpec((1, LANES), lambda i: (0, 0))],
        out_specs=[pl.BlockSpec((tm, LANES), lambda i: (i, 0)), pl.BlockSpec((nb_pad, LANES), lambda i: (0, 0)),
                   pl.BlockSpec((1, LANES), lambda i: (0, 0))],
        out_shape=[jax.ShapeDtypeStruct((t, LANES), I32), jax.ShapeDtypeStruct((nb_pad, LANES), I32),
                   jax.ShapeDtypeStruct((1, LANES), I32)],
        compiler_params=_params(("arbitrary",)),
        name="dest",
    )(ri, cnt)


ROW_UNROLL = 8


def _drain_tokens(src_ref, dst_ref, sem, n_tokens):
    def drain(_, carry):
        for _ in range(ROW_UNROLL * TOP_K):
            _token_copy(src_ref, 0, dst_ref, 0, sem).wait()
        return carry

    lax.fori_loop(0, n_tokens // ROW_UNROLL, drain, 0)


def _dispatch_kernel(dest_ref, h_ref, xbuf_ref, sem):
    tm = h_ref.shape[0] // TOKEN_TILE_ROWS

    def issue(g, carry):
        for u in range(ROW_UNROLL):
            r = g * ROW_UNROLL + u
            for kk in range(TOP_K):
                _token_copy(h_ref, r, xbuf_ref, dest_ref[r * TOP_K + kk], sem).start(priority=kk % 2)
        return carry

    lax.fori_loop(0, tm // ROW_UNROLL, issue, 0)
    _drain_tokens(h_ref, xbuf_ref, sem, tm)


def _dispatch(dest_flat, h2_tiles, n_rows):
    tm = ROUTE_TILE
    t = h2_tiles.shape[0] // TOKEN_TILE_ROWS
    return pl.pallas_call(
        _dispatch_kernel,
        grid=(t // tm,),
        in_specs=[pl.BlockSpec((tm * TOP_K,), lambda i: (i,), memory_space=pltpu.SMEM),
                  pl.BlockSpec((tm * TOKEN_TILE_ROWS, LANES), lambda i: (i, 0))],
        out_specs=pl.BlockSpec(memory_space=pl.ANY),
        out_shape=jax.ShapeDtypeStruct((n_rows * TOKEN_TILE_ROWS, LANES), h2_tiles.dtype),
        scratch_shapes=[pltpu.SemaphoreType.DMA(())],
        compiler_params=_params(("arbitrary",)),
        name="dispatch",
    )(dest_flat, h2_tiles)


def _experts_kernel(blk_e_ref, nvalid_ref, x_ref, wgu_ref, bgu_ref, wd_ref, bd_ref, y_ref, wgu16_ref, wd16_ref):
    b = pl.program_id(0)
    nvalid = nvalid_ref[b]

    @pl.when((b == 0) | (blk_e_ref[b] != blk_e_ref[jnp.maximum(b - 1, 0)]))
    def _():
        wgu16_ref[...] = wgu_ref[...].astype(BF16)
        wd16_ref[...] = wd_ref[...].astype(BF16)

    n_blk = x_ref.shape[0] // TOKEN_TILE_ROWS
    n_half = n_blk // 2

    def mlp(n):
        x = jnp.concatenate([_load_token_slab(x_ref, s, n) for s in range(TOKEN_TILE_ROWS)], axis=1)
        rows = lax.broadcasted_iota(I32, x.shape, 0)
        x = jnp.where(rows < nvalid, x, 0.0).astype(BF16)
        gu = _dot(x, wgu16_ref[...]) + bgu_ref[...]
        gate = jnp.minimum(gu[:, :FF], SWIGLU_LIMIT)
        up = jnp.clip(gu[:, FF:], -SWIGLU_LIMIT, SWIGLU_LIMIT)
        act = (up + 1.0) * (gate * _sigmoid(SWIGLU_ALPHA * gate))
        _store_token_tiles(y_ref, _dot(act.astype(BF16), wd16_ref[...]) + bd_ref[...])
        if n < n_blk:
            y_ref[n * TOKEN_TILE_ROWS:, :] = jnp.zeros(((n_blk - n) * TOKEN_TILE_ROWS, LANES), y_ref.dtype)

    @pl.when(nvalid > n_half)
    def _():
        mlp(n_blk)

    @pl.when((nvalid > 0) & (nvalid <= n_half))
    def _():
        mlp(n_half)

    @pl.when(nvalid <= 0)
    def _():
        y_ref[...] = jnp.zeros_like(y_ref)


def _experts(blk_e, nvalid, xbuf, wgu, bgu, wd, bd):
    d = D_MODEL
    blk_rows = EXPERT_BLOCK * TOKEN_TILE_ROWS
    nb = xbuf.shape[0] // blk_rows
    grid_spec = pltpu.PrefetchScalarGridSpec(
        num_scalar_prefetch=2,
        grid=(nb,),
        in_specs=[pl.BlockSpec((blk_rows, LANES), lambda b, e, n: (b, 0)),
                  pl.BlockSpec((None, d, 2 * FF), lambda b, e, n: (e[b], 0, 0)),
                  pl.BlockSpec((None, 1, 2 * FF), lambda b, e, n: (e[b], 0, 0)),
                  pl.BlockSpec((None, FF, d), lambda b, e, n: (e[b], 0, 0)),
                  pl.BlockSpec((None, 1, d), lambda b, e, n: (e[b], 0, 0))],
        out_specs=pl.BlockSpec((blk_rows, LANES), lambda b, e, n: (b, 0)),
        scratch_shapes=[pltpu.VMEM((d, 2 * FF), BF16), pltpu.VMEM((FF, d), BF16)],
    )
    return pl.pallas_call(
        _experts_kernel,
        grid_spec=grid_spec,
        out_shape=jax.ShapeDtypeStruct(xbuf.shape, F32),
        compiler_params=_params(("arbitrary",)),
        name="experts",
    )(blk_e, nvalid, xbuf, wgu, bgu, wd, bd)


def _combine_kernel(dest_ref, dest_next_ref, ybuf_ref, x1_ref, gate_ref, gt2_ref, gf_ref, o_ref, rows_ref, sems):
    i = pl.program_id(0)
    tm = x1_ref.shape[0]
    slot = i % 2

    def start_gather(idx_ref, s):
        def issue(g, carry):
            for u in range(ROW_UNROLL):
                r = g * ROW_UNROLL + u
                for kk in range(TOP_K):
                    _token_copy(ybuf_ref, idx_ref[r * TOP_K + kk], rows_ref.at[s, kk], r,
                                sems.at[s]).start(priority=kk % 2)
            return carry

        lax.fori_loop(0, tm // ROW_UNROLL, issue, 0)

    @pl.when(i == 0)
    def _():
        start_gather(dest_ref, 0)

    @pl.when(i + 1 < pl.num_programs(0))
    def _():
        start_gather(dest_next_ref, 1 - slot)

    _drain_tokens(ybuf_ref, rows_ref.at[slot, 0], sems.at[slot], tm)

    gates = gate_ref[...]
    slabs = []
    for s in range(TOKEN_TILE_ROWS):
        moe = gates[:, 0:1] * _load_token_slab(rows_ref.at[slot, 0], s, tm)
        for kk in range(1, TOP_K):
            moe = moe + gates[:, kk:kk + 1] * _load_token_slab(rows_ref.at[slot, kk], s, tm)
        slabs.append(moe)
    xo = x1_ref[...] + gt2_ref[...] * jnp.concatenate(slabs, axis=1)
    o_ref[...] = xo * lax.rsqrt(jnp.mean(xo * xo, axis=-1, keepdims=True) + RMS_EPS) * gf_ref[...]


def _combine(dest_flat, ybuf, x1, gates, gt2, g_final):
    t, d = x1.shape
    tm = ROUTE_TILE
    n = t // tm
    return pl.pallas_call(
        _combine_kernel,
        grid=(n,),
        in_specs=[pl.BlockSpec((tm * TOP_K,), lambda i: (i,), memory_space=pltpu.SMEM),
                  pl.BlockSpec((tm * TOP_K,), lambda i: (jnp.minimum(i + 1, n - 1),), memory_space=pltpu.SMEM),
                  pl.BlockSpec(memory_space=pl.ANY),
                  pl.BlockSpec((tm, d), lambda i: (i, 0)),
                  pl.BlockSpec((tm, LANES), lambda i: (i, 0)),
                  pl.BlockSpec((1, d), lambda i: (0, 0)),
                  pl.BlockSpec((1, d), lambda i: (0, 0))],
        out_specs=pl.BlockSpec((tm, d), lambda i: (i, 0)),
        out_shape=jax.ShapeDtypeStruct((t, d), F32),
        scratch_shapes=[pltpu.VMEM((2, TOP_K, tm * TOKEN_TILE_ROWS, LANES), F32), pltpu.SemaphoreType.DMA((2,))],
        compiler_params=_params(("arbitrary",)),
        name="combine",
    )(dest_flat, dest_flat, ybuf, x1, gates, gt2, g_final)


def _pad_lanes(v, lane0):
    return jnp.zeros((1, LANES), F32).at[0, lane0:lane0 + v.shape[0]].set(v.astype(F32))


def _layer(x, mod, g_norm1, w_in, conv_w, a_log, dt_bias, g_onorm, b_fgate, w_o_delta, w_o_fox, w_out,
           g_norm2, w_router, b_router, w_gate_up, b_gate_up, w_down, b_down, g_final):
    t, d = x.shape
    sh1, sc1, gt1, sh2, sc2, gt2 = [mod[:, i * d:(i + 1) * d] for i in range(N_MOD)]

    w_t = w_in.T
    o = 0
    wq = w_t[o:o + QKV_W]; o += QKV_W
    wz = w_t[o:o + MIX_W]; o += MIX_W
    w_beta = w_t[o:o + N_HEADS]; o += N_HEADS
    w_dec = w_t[o:o + N_HEADS]; o += N_HEADS
    wf = w_t[o:o + QKV_W]; o += QKV_W
    wf = jnp.concatenate([wf[:MIX_W] * (LOG2E * HEAD_DIM ** -0.5), wf[MIX_W:]], axis=0)
    w_fg = w_t[o:o + N_HEADS]; o += N_HEADS
    wga = w_t[o:o + d]; o += d
    wgb = w_t[o:o + d]
    ws = jnp.zeros((LANES, d), F32)
    ws = ws.at[L_BETA:L_BETA + N_HEADS].set(w_beta).at[L_G:L_G + N_HEADS].set(w_dec)
    ws = ws.at[L_F:L_F + N_HEADS].set(w_fg)
    bf = lambda w: w.astype(BF16)

    tm = min(ROW_TILE, t)
    qkv, z, fx, gate_a, gate_b, small = _inproj(
        x, g_norm1.reshape(1, d), sc1, sh1, bf(wq), bf(wz), bf(wf), bf(wga), bf(wgb), bf(ws), tm)

    gcol, grow = _gates(small, _pad_lanes(a_log, L_G), _pad_lanes(dt_bias, L_G), _pad_lanes(b_fgate, L_F),
                        min(GATES_TILE, t))

    o_a = _delta(qkv, z, gcol, grow, conv_w, g_onorm.reshape(1, HEAD_DIM), min(DELTA_TILE, t))
    tq, tk = min(FOX_TQ, t), min(FOX_TK, t)
    f_rows = grow[L_F:L_F + N_HEADS].reshape(N_HEADS, t // tk, 1, tk)
    o_b = _fox(fx, f_rows, tq, tk)

    wr = jnp.zeros((d, LANES), F32).at[:, :N_EXPERTS].set(w_router)
    br = _pad_lanes(b_router, 0)
    x1, h2, ri, gates, cnt = _merge(o_a, o_b, gate_a, gate_b, x, bf(w_o_delta), bf(w_o_fox), bf(w_out), gt1,
                                    g_norm2.reshape(1, d), sc2, sh2, wr, br, tm)

    n_blocks = (t * TOP_K) // EXPERT_BLOCK + N_EXPERTS
    dest, blk, _ = _dest(ri, cnt, n_blocks, min(2048, t))
    dest_flat = dest[:, :TOP_K].reshape(t * TOP_K)
    blk_e, nvalid = blk[:n_blocks, 0], blk[:n_blocks, 1]

    xbuf = _dispatch(dest_flat, h2, n_blocks * EXPERT_BLOCK)
    ybuf = _experts(blk_e, nvalid, xbuf, w_gate_up, b_gate_up.reshape(N_EXPERTS, 1, 2 * FF),
                    w_down, b_down.reshape(N_EXPERTS, 1, d))
    return _combine(dest_flat, ybuf, x1, gates, gt2, g_final.reshape(1, d))


def kernel(x, c, w_ada, b_ada, g_norm1, w_in, conv_w, a_log, dt_bias, g_onorm, b_fgate, w_o_delta, w_o_fox, w_out,
           g_norm2, w_router, b_router, w_gate_up, b_gate_up, w_down, b_down, g_final):
    b, s, d = x.shape
    assert b == 1 and d == D_MODEL and w_ada.shape[0] == 1
    h = x[0]
    for l in range(w_ada.shape[0]):
        mod = _modulation(c[0], w_ada[l], b_ada[l])
        h = _layer(h, mod, g_norm1[l], w_in[l], conv_w[l], a_log[l], dt_bias[l], g_onorm[l], b_fgate[l],
                   w_o_delta[l], w_o_fox[l], w_out[l], g_norm2[l], w_router[l], b_router[l], w_gate_up[l],
                   b_gate_up[l], w_down[l], b_down[l], g_final)
    return h[None]
```

```python
import functools

import jax
import jax.numpy as jnp
from jax import lax
from jax.experimental import pallas as pl
from jax.experimental.pallas import tpu as pltpu

F32 = jnp.float32
BF16 = jnp.bfloat16
I32 = jnp.int32

D_MODEL = 1024
HEAD_DIM = 128
N_HEADS = 4
MIX_W = N_HEADS * HEAD_DIM
QKV_W = 3 * MIX_W
CONV_K = 4
CHUNK = 64
STACK = N_HEADS * CHUNK
SUB = 16
N_EXPERTS = 32
TOP_K = 4
FF = D_MODEL
SWIGLU_LIMIT = 7.0
SWIGLU_ALPHA = 1.702
RMS_EPS = 1e-6
L2_EPS = 1e-6
N_MOD = 6
LANES = 128
NEG_BIG = -1e30
LOG2E = 1.4426950408889634

ROW_TILE = 512
GATES_TILE = 256
DELTA_TILE = 256
EXPERT_BLOCK = 512
ROUTE_TILE = 256
VMEM_LIMIT = 56 * 1024 * 1024


def _params(sem, vmem=VMEM_LIMIT, flags=None):
    return pltpu.CompilerParams(dimension_semantics=sem, vmem_limit_bytes=vmem, flags=flags)


def _softplus(x):
    return jnp.maximum(x, 0.0) + jnp.log(1.0 + jnp.exp(-jnp.abs(x)))


def _sigmoid(x):
    return 1.0 / (1.0 + jnp.exp(-x))


def _silu(x):
    return x * _sigmoid(x)


def _dot(a, b, precision=None):
    return jnp.dot(a, b, preferred_element_type=F32, precision=precision)


def _dot_nt(a, b, precision=None):
    return lax.dot_general(a, b, (((1,), (1,)), ((), ())), preferred_element_type=F32, precision=precision)


def _split_bf16(x, terms):
    parts = []
    for _ in range(terms):
        p = x.astype(BF16)
        parts.append(p)
        x = x - p.astype(F32)
    return parts


def _dot_split(a, b):
    a_hi, a_lo = _split_bf16(a, 2)
    b_hi, b_lo = _split_bf16(b, 2)
    return _dot(a_hi, b_hi) + (_dot(a_hi, b_lo) + _dot(a_lo, b_hi))


def _dot_mask(mask01, x):
    return sum(_dot(mask01, p) for p in _split_bf16(x, 3))


TOKEN_TILE_ROWS = D_MODEL // LANES


def _store_token_tiles(ref, x):
    n = x.shape[0]
    for s in range(TOKEN_TILE_ROWS):
        ref[pl.ds(s, n, stride=TOKEN_TILE_ROWS), :] = x[:, s * LANES:(s + 1) * LANES]


def _load_token_slab(ref, s, n):
    return ref[pl.ds(s, n, stride=TOKEN_TILE_ROWS), :]


def _token_copy(src_ref, src_tok, dst_ref, dst_tok, sem):
    rows = lambda i: pl.ds(pl.multiple_of(i * TOKEN_TILE_ROWS, TOKEN_TILE_ROWS), TOKEN_TILE_ROWS)
    return pltpu.make_async_copy(src_ref.at[rows(src_tok), :], dst_ref.at[rows(dst_tok), :], sem)


def _mod_kernel(c_ref, w_ref, b_ref, o_ref):
    o_ref[...] = jnp.sum(c_ref[...] * w_ref[...], axis=0, keepdims=True) + b_ref[...]


def _modulation(c, w_ada, b_ada):
    d, n = w_ada.shape
    tn = 1024
    return pl.pallas_call(
        _mod_kernel,
        grid=(n // tn,),
        in_specs=[pl.BlockSpec((d, 1), lambda j: (0, 0)),
                  pl.BlockSpec((d, tn), lambda j: (0, j)),
                  pl.BlockSpec((1, tn), lambda j: (0, j))],
        out_specs=pl.BlockSpec((1, tn), lambda j: (0, j)),
        out_shape=jax.ShapeDtypeStruct((1, n), F32),
        compiler_params=_params(("arbitrary",)),
        name="mod",
    )(c.reshape(d, 1), w_ada, b_ada.reshape(1, n))


def _inproj_kernel(x_ref, g_ref, sc_ref, sh_ref, wq_ref, wz_ref, wf_ref, wga_ref, wgb_ref, ws_ref,
                   oq_ref, oz_ref, of_ref, oga_ref, ogb_ref, os_ref):
    x = x_ref[...]
    y = x * lax.rsqrt(jnp.mean(x * x, axis=-1, keepdims=True) + RMS_EPS) * g_ref[...]
    h = (y * (1.0 + sc_ref[...]) + sh_ref[...]).astype(BF16)
    oq_ref[...] = _dot_nt(h, wq_ref[...])
    oz_ref[...] = _dot_nt(h, wz_ref[...]).astype(BF16)
    of_ref[...] = _dot_nt(h, wf_ref[...]).astype(BF16)
    oga_ref[...] = _dot_nt(h, wga_ref[...]).astype(BF16)
    ogb_ref[...] = _dot_nt(h, wgb_ref[...]).astype(BF16)
    os_ref[...] = _dot_nt(h, ws_ref[...])


def _inproj(x, g1, sc1, sh1, wq, wz, wf, wga, wgb, ws, tm):
    t, d = x.shape
    row = lambda i: (i, 0)
    fixed = lambda i: (0, 0)
    ws_list = [wq, wz, wf, wga, wgb, ws]
    out_dt = [F32, BF16, BF16, BF16, BF16, F32]
    return pl.pallas_call(
        _inproj_kernel,
        grid=(t // tm,),
        in_specs=[pl.BlockSpec((tm, d), row)] + [pl.BlockSpec((1, d), fixed)] * 3
                 + [pl.BlockSpec(w.shape, fixed) for w in ws_list],
        out_specs=[pl.BlockSpec((tm, w.shape[0]), row) for w in ws_list],
        out_shape=[jax.ShapeDtypeStruct((t, w.shape[0]), dt) for w, dt in zip(ws_list, out_dt)],
        compiler_params=_params(("arbitrary",)),
        name="inproj",
    )(x, g1, sc1, sh1, *ws_list)


L_BETA, L_G, L_F, L_GC, L_GL = 0, 4, 8, 12, 16


def _gates_kernel(s_ref, alog_ref, dtb_ref, bf_ref, col_ref, row_ref, carry_ref):
    i = pl.program_id(0)
    tm = s_ref.shape[0]

    @pl.when(i == 0)
    def _():
        carry_ref[...] = jnp.zeros_like(carry_ref)

    s = s_ref[...]
    lane = lax.broadcasted_iota(I32, s.shape, 1)
    beta = _sigmoid(s)
    g = -jnp.exp(alog_ref[...]) * _softplus(s + dtb_ref[...])
    logf = -_softplus(-(s + bf_ref[...]))
    is_g = (lane >= L_G) & (lane < L_G + N_HEADS)
    is_f = (lane >= L_F) & (lane < L_F + N_HEADS)
    g = jnp.where(is_g, g, 0.0)
    logf = jnp.where(is_f, logf, 0.0)

    r = lax.broadcasted_iota(I32, (tm, tm), 0)
    c = lax.broadcasted_iota(I32, (tm, tm), 1)
    same_chunk = (r // CHUNK) == (c // CHUNK)
    tri = jnp.where(r >= c, 1.0, 0.0).astype(BF16)
    tri_chunk = jnp.where(same_chunk & (r >= c), 1.0, 0.0).astype(BF16)
    ones_chunk = jnp.where(same_chunk, 1.0, 0.0).astype(BF16)
    f_cum = _dot_mask(tri, logf) + carry_ref[...]
    carry_ref[...] = f_cum[tm - 1:tm, :]
    gc = _dot_mask(tri_chunk, g)
    gl = _dot_mask(ones_chunk, g)

    out = jnp.where(lane < N_HEADS, beta, 0.0) + g + f_cum
    out = out + pltpu.roll(gc, L_GC - L_G, 1) + pltpu.roll(gl, L_GL - L_G, 1)
    col_ref[...] = out
    row_ref[...] = out.T[:row_ref.shape[0], :]


def _gates(small, alog_row, dtb_row, bf_row, tm):
    t = small.shape[0]
    n_rows = 24
    return pl.pallas_call(
        _gates_kernel,
        grid=(t // tm,),
        in_specs=[pl.BlockSpec((tm, LANES), lambda i: (i, 0))] + [pl.BlockSpec((1, LANES), lambda i: (0, 0))] * 3,
        out_specs=[pl.BlockSpec((tm, LANES), lambda i: (i, 0)), pl.BlockSpec((n_rows, tm), lambda i: (0, i))],
        out_shape=[jax.ShapeDtypeStruct((t, LANES), F32), jax.ShapeDtypeStruct((n_rows, t), F32)],
        scratch_shapes=[pltpu.VMEM((1, LANES), F32)],
        compiler_params=_params(("arbitrary",)),
        name="gates",
    )(small, alog_row, dtb_row, bf_row)


def _stack_heads(a, col0):
    return jnp.concatenate([a[:, col0 + h * HEAD_DIM: col0 + (h + 1) * HEAD_DIM] for h in range(N_HEADS)], axis=0)


def _stack_cols(a, lane0):
    return jnp.concatenate([a[:, lane0 + h: lane0 + h + 1] for h in range(N_HEADS)], axis=0)


def _delta_kernel(qkv_ref, z_ref, gcol_ref, grow_ref, cw_ref, gon_ref, o_ref, ext_ref, tail_ref, s_ref):
    i = pl.program_id(0)
    tb = qkv_ref.shape[0]
    pad = tail_ref.shape[0]

    @pl.when(i == 0)
    def _():
        tail_ref[...] = jnp.zeros_like(tail_ref)
        s_ref[...] = jnp.zeros_like(s_ref)

    ext_ref[0:pad, :] = tail_ref[...]
    ext_ref[pad:pad + tb, :] = qkv_ref[...]
    tail_ref[...] = qkv_ref[tb - pad:tb, :]
    conv = cw_ref[0:1, :] * ext_ref[pad - 3:pad - 3 + tb, :]
    for j in range(1, CONV_K):
        conv = conv + cw_ref[j:j + 1, :] * ext_ref[pad - 3 + j:pad - 3 + j + tb, :]
    act = _silu(conv)

    r = lax.broadcasted_iota(I32, (STACK, STACK), 0)
    c = lax.broadcasted_iota(I32, (STACK, STACK), 1)
    same = (r // CHUNK) == (c // CHUNK)
    m_incl = same & (r >= c)
    m_strict = same & (r > c)
    eye = jnp.where(r == c, 1.0, 0.0)
    m_sub = []
    size = SUB
    while size <= CHUNK:
        m_sub.append((r // size) == (c // size))
        size *= 2
    rb = lax.broadcasted_iota(I32, (STACK, N_HEADS * HEAD_DIM), 0) // CHUNK
    cb = lax.broadcasted_iota(I32, (STACK, N_HEADS * HEAD_DIM), 1) // HEAD_DIM
    head_match = rb == cb

    chunks = range(tb // CHUNK)
    pre = []
    for ch in chunks:
        r0 = ch * CHUNK
        a = act[r0:r0 + CHUNK, :]
        q = _stack_heads(a, 0)
        k = _stack_heads(a, MIX_W)
        v = _stack_heads(a, 2 * MIX_W)
        q = q * lax.rsqrt(jnp.sum(q * q, axis=-1, keepdims=True) + L2_EPS) * (HEAD_DIM ** -0.5)
        k = k * lax.rsqrt(jnp.sum(k * k, axis=-1, keepdims=True) + L2_EPS)

        gcols = gcol_ref[r0:r0 + CHUNK, :]
        beta = _stack_cols(gcols, L_BETA)
        gc = _stack_cols(gcols, L_GC)
        gc_row = jnp.concatenate(
            [grow_ref[L_GC + h:L_GC + h + 1, r0:r0 + CHUNK] for h in range(N_HEADS)], axis=1)
        gl_row = jnp.concatenate(
            [grow_ref[L_GL + h:L_GL + h + 1, r0:r0 + CHUNK] for h in range(N_HEADS)], axis=1)

        decay = jnp.exp(jnp.where(m_incl, gc - gc_row, NEG_BIG))
        kb16 = k.astype(BF16)
        kk = _dot_nt(kb16, kb16)
        lmat = jnp.where(m_strict, kk * beta * decay, 0.0)
        attn = (_dot_nt(q.astype(BF16), kb16) * decay).astype(BF16)
        egc = jnp.exp(gc)
        rhs = jnp.concatenate([v * beta, k * (beta * egc)], axis=1).astype(BF16)
        kt_dec = (k.T * jnp.exp(gl_row - gc_row)).astype(BF16)
        pre.append(dict(lmat=lmat, attn=attn, rhs=rhs, kt_dec=kt_dec, q_dec=q * egc))

    l_d = [jnp.where(m_sub[0], p["lmat"], 0.0) for p in pre]
    inv = [eye - l for l in l_d]
    l16 = [l.astype(BF16) for l in l_d]
    pw = [_dot(l, l).astype(BF16) for l in l16]
    n_sq = SUB.bit_length() - 3
    for s in range(n_sq + 1):
        inv = [x + _dot(x.astype(BF16), p) for x, p in zip(inv, pw)]
        if s < n_sq:
            pw = [_dot(p, p).astype(BF16) for p in pw]
    for lvl in range(1, len(m_sub)):
        m_off = m_sub[lvl] & ~m_sub[lvl - 1]
        off = [jnp.where(m_off, p["lmat"], 0.0).astype(BF16) for p in pre]
        inv16 = [x.astype(BF16) for x in inv]
        half = [_dot(x, o).astype(BF16) for x, o in zip(inv16, off)]
        inv = [x - _dot(h, x16) for x, h, x16 in zip(inv, half, inv16)]
    uws = [_dot(x.astype(BF16), p["rhs"]) for x, p in zip(inv, pre)]

    for ch in chunks:
        r0 = ch * CHUNK
        attn, kt_dec = pre[ch]["attn"], pre[ch]["kt_dec"]
        u, w = uws[ch][:, :HEAD_DIM], uws[ch][:, HEAD_DIM:]

        state16 = s_ref[...].astype(BF16)
        wq_s = _dot(jnp.concatenate([w, pre[ch]["q_dec"]], axis=0).astype(BF16), state16)
        ws_d = jnp.concatenate([wq_s[h * CHUNK:(h + 1) * CHUNK, h * HEAD_DIM:(h + 1) * HEAD_DIM]
                                for h in range(N_HEADS)], axis=0)
        qs_d = jnp.concatenate([wq_s[STACK + h * CHUNK:STACK + (h + 1) * CHUNK, h * HEAD_DIM:(h + 1) * HEAD_DIM]
                                for h in range(N_HEADS)], axis=0)
        v_new = u - ws_d
        v16 = v_new.astype(BF16)
        o = qs_d + _dot(attn, v16)

        v_bd =jnp.where(head_match, jnp.concatenate([v_new] * N_HEADS, axis=1), 0.0).astype(BF16)
        upd = _dot(kt_dec, v_bd)
        for h in range(N_HEADS):
            e_h = jnp.exp(grow_ref[L_GL + h:L_GL + h + 1, r0:r0 + 1])
            sl = slice(h * HEAD_DIM, (h + 1) * HEAD_DIM)
            s_ref[:, sl] = s_ref[:, sl] * e_h + upd[:, sl]

        for h in range(N_HEADS):
            oh = o[h * CHUNK:(h + 1) * CHUNK, :]
            oh = oh * lax.rsqrt(jnp.mean(oh * oh, axis=-1, keepdims=True) + RMS_EPS) * gon_ref[...]
            zh = z_ref[r0:r0 + CHUNK, h * HEAD_DIM:(h + 1) * HEAD_DIM].astype(F32)
            o_ref[r0:r0 + CHUNK, h * HEAD_DIM:(h + 1) * HEAD_DIM] = (oh * _silu(zh)).astype(o_ref.dtype)


def _delta(qkv, z, gcol, grow, conv_w, g_onorm, tb):
    t = qkv.shape[0]
    pad = 8
    return pl.pallas_call(
        _delta_kernel,
        grid=(t // tb,),
        in_specs=[pl.BlockSpec((tb, QKV_W), lambda i: (i, 0)),
                  pl.BlockSpec((tb, MIX_W), lambda i: (i, 0)),
                  pl.BlockSpec((tb, LANES), lambda i: (i, 0)),
                  pl.BlockSpec((grow.shape[0], tb), lambda i: (0, i)),
                  pl.BlockSpec((CONV_K, QKV_W), lambda i: (0, 0)),
                  pl.BlockSpec((1, HEAD_DIM), lambda i: (0, 0))],
        out_specs=pl.BlockSpec((tb, MIX_W), lambda i: (i, 0)),
        out_shape=jax.ShapeDtypeStruct((t, MIX_W), BF16),
        scratch_shapes=[pltpu.VMEM((tb + pad, QKV_W), F32),
                        pltpu.VMEM((pad, QKV_W), F32),
                        pltpu.VMEM((HEAD_DIM, N_HEADS * HEAD_DIM), F32)],
        compiler_params=_params(("arbitrary",)),
        name="delta",
    )(qkv, z, gcol, grow, conv_w, g_onorm)


FOX_GROUP = 4
FOX_TQ, FOX_TK = 256, 2048
FOX_DIAG_PARTS = 2


def _fox_kernel(*refs, tk):
    g = FOX_GROUP
    q_ref, o_ref = refs[0], refs[-1]
    k_refs, v_refs, f_refs = (refs[1 + i * g:1 + (i + 1) * g] for i in range(3))
    qi = pl.program_id(1)
    tq = q_ref.shape[0]
    def step(off, width, f_blk, carry, mask):
        ones = jnp.ones((width, HEAD_DIM), BF16)
        ss = []
        for h in range(g):
            q = q_ref[:, h * HEAD_DIM:(h + 1) * HEAD_DIM]
            s = _dot_nt(q, k_refs[h][pl.ds(off, width), :]) - f_blk(h) * LOG2E
            if mask is not None:
                s = jnp.where(mask, s, NEG_BIG)
            ss.append(s)
        ps, ms, alphas = [], [], []
        for h in range(g):
            m = carry[h][0]
            m_new = jnp.maximum(m, jnp.max(ss[h], axis=-1, keepdims=True))
            alphas.append(jnp.exp2(m - m_new))
            ps.append(jnp.exp2(ss[h] - m_new).astype(BF16))
            ms.append(m_new)
        out = []
        for h in range(g):
            v_aug = jnp.concatenate([v_refs[h][pl.ds(off, width), :], ones], axis=1)
            out.append((ms[h], alphas[h] * carry[h][1] + _dot(ps[h], v_aug)))
        return tuple(out)

    def full_chunk(j, carry):
        return step(pl.multiple_of(j * tk, tk), tk, lambda h: f_refs[h][j], carry, None)

    init = tuple((jnp.full((tq, 1), NEG_BIG, F32), jnp.zeros((tq, 2 * HEAD_DIM), F32)) for _ in range(g))
    n_full = (qi * tq) // tk
    carry = lax.fori_loop(0, n_full, full_chunk, init)

    parts = max(1, min(FOX_DIAG_PARTS, tk // tq))
    width = tk // parts
    row = lax.broadcasted_iota(I32, (tq, width), 0)
    col = lax.broadcasted_iota(I32, (tq, width), 1)
    for d in range(max(tq // tk, 1)):
        j = n_full + d
        for p in range(parts):
            off = pl.multiple_of(j * tk + p * width, width)
            piece = functools.partial(step, off, width,
                                      lambda h, j=j, p=p: f_refs[h][j][:, p * width:(p + 1) * width],
                                      mask=col + (off - qi * tq) <= row)
            carry = piece(carry) if p == 0 else lax.cond(off < (qi + 1) * tq, piece, lambda c: c, carry)
    for h in range(g):
        acc = carry[h][1]
        o_ref[:, h * HEAD_DIM:(h + 1) * HEAD_DIM] = (acc[:, :HEAD_DIM] / acc[:, HEAD_DIM:]).astype(o_ref.dtype)


def _fox(fx, f_rows, tq, tk):
    t = fx.shape[0]
    g = FOX_GROUP
    heads = lambda hg, u: hg * g + u
    q_specs = [pl.BlockSpec((tq, g * HEAD_DIM), lambda hg, i: (i, hg))]
    k_specs = [pl.BlockSpec((t, HEAD_DIM), lambda hg, i, u=u: (0, N_HEADS + heads(hg, u)),
                            pipeline_mode=pl.Buffered(1)) for u in range(g)]
    v_specs = [pl.BlockSpec((t, HEAD_DIM), lambda hg, i, u=u: (0, 2 * N_HEADS + heads(hg, u)),
                            pipeline_mode=pl.Buffered(1)) for u in range(g)]
    f_specs = [pl.BlockSpec((None, t // tk, 1, tk), lambda hg, i, u=u: (heads(hg, u), 0, 0, 0)) for u in range(g)]
    return pl.pallas_call(
        functools.partial(_fox_kernel, tk=tk),
        grid=(N_HEADS // g, t // tq),
        in_specs=q_specs + k_specs + v_specs + f_specs,
        out_specs=pl.BlockSpec((tq, g * HEAD_DIM), lambda hg, i: (i, hg)),
        out_shape=jax.ShapeDtypeStruct((t, MIX_W), BF16),
        compiler_params=_params(("arbitrary", "arbitrary")),
        name="fox",
    )(*([fx] * (1 + 2 * g) + [f_rows] * g))


def _merge_kernel(oa_ref, ob_ref, ga_ref, gb_ref, x_ref, woa_ref, wob_ref, wout_ref, gt1_ref, g2_ref, sc2_ref,
                  sh2_ref, wr_ref, br_ref, x1_ref, h2_ref, ri_ref, gate_ref, cnt_ref, carry_ref):
    i = pl.program_id(0)
    tm = x_ref.shape[0]

    @pl.when(i == 0)
    def _():
        carry_ref[...] = jnp.zeros_like(carry_ref)

    ya = _dot(oa_ref[...], woa_ref[...])
    yb = _dot(ob_ref[...], wob_ref[...])
    merged = _sigmoid(ga_ref[...].astype(F32)) * ya + _sigmoid(gb_ref[...].astype(F32)) * yb
    x1 = x_ref[...] + gt1_ref[...] * _dot(merged.astype(BF16), wout_ref[...])
    x1_ref[...] = x1
    y = x1 * lax.rsqrt(jnp.mean(x1 * x1, axis=-1, keepdims=True) + RMS_EPS) * g2_ref[...]
    h2 = y * (1.0 + sc2_ref[...]) + sh2_ref[...]
    _store_token_tiles(h2_ref, h2)

    logits = _dot_split(h2, wr_ref[...]) + br_ref[...]
    lane = lax.broadcasted_iota(I32, logits.shape, 1)
    lanef = lane.astype(F32)
    cur = jnp.where(lane < N_EXPERTS, logits, -jnp.inf)
    vals, idxs = [], []
    for _ in range(TOP_K):
        m = jnp.max(cur, axis=-1, keepdims=True)
        ix = jnp.min(jnp.where(cur == m, lanef, float(LANES)), axis=-1, keepdims=True)
        vals.append(m)
        idxs.append(ix)
        cur = jnp.where(lanef == ix, -jnp.inf, cur)
    exps = [jnp.exp(v - vals[0]) for v in vals]
    denom = exps[0] + exps[1] + exps[2] + exps[3]

    onehot = jnp.zeros(logits.shape, F32)
    for ix in idxs:
        onehot = onehot + jnp.where(lanef == ix, 1.0, 0.0)
    r = lax.broadcasted_iota(I32, (tm, tm), 0)
    c = lax.broadcasted_iota(I32, (tm, tm), 1)
    strict = jnp.where(r > c, 1.0, 0.0).astype(BF16)
    before = _dot(strict, onehot.astype(BF16)) + carry_ref[...]
    carry_ref[...] = carry_ref[...] + jnp.sum(onehot, axis=0, keepdims=True)
    cnt_ref[...] = carry_ref[...].astype(I32)

    ri = jnp.zeros(logits.shape, F32)
    gates = jnp.zeros(logits.shape, F32)
    for kk in range(TOP_K):
        rank = jnp.sum(jnp.where(lanef == idxs[kk], before, 0.0), axis=-1, keepdims=True)
        ri = ri + jnp.where(lane == kk, idxs[kk], 0.0) + jnp.where(lane == TOP_K + kk, rank, 0.0)
        gates = gates + jnp.where(lane == kk, exps[kk] / denom, 0.0)
    ri_ref[...] = ri.astype(I32)
    gate_ref[...] = gates


def _merge(o_a, o_b, gate_a, gate_b, x, woa, wob, wout, gt1, g2, sc2, sh2, wr, br, tm):
    t, d = x.shape
    row = lambda i: (i, 0)
    fixed = lambda i: (0, 0)
    return pl.pallas_call(
        _merge_kernel,
        grid=(t // tm,),
        in_specs=[pl.BlockSpec((tm, MIX_W), row), pl.BlockSpec((tm, MIX_W), row),
                  pl.BlockSpec((tm, d), row), pl.BlockSpec((tm, d), row), pl.BlockSpec((tm, d), row),
                  pl.BlockSpec((MIX_W, d), fixed), pl.BlockSpec((MIX_W, d), fixed), pl.BlockSpec((d, d), fixed),
                  pl.BlockSpec((1, d), fixed), pl.BlockSpec((1, d), fixed), pl.BlockSpec((1, d), fixed),
                  pl.BlockSpec((1, d), fixed), pl.BlockSpec((d, LANES), fixed), pl.BlockSpec((1, LANES), fixed)],
        out_specs=[pl.BlockSpec((tm, d), row), pl.BlockSpec((tm * TOKEN_TILE_ROWS, LANES), row),
                   pl.BlockSpec((tm, LANES), row),
                   pl.BlockSpec((tm, LANES), row), pl.BlockSpec((1, LANES), fixed)],
        out_shape=[jax.ShapeDtypeStruct((t, d), F32), jax.ShapeDtypeStruct((t * TOKEN_TILE_ROWS, LANES), F32),
                   jax.ShapeDtypeStruct((t, LANES), I32), jax.ShapeDtypeStruct((t, LANES), F32),
                   jax.ShapeDtypeStruct((1, LANES), I32)],
        scratch_shapes=[pltpu.VMEM((1, LANES), F32)],
        compiler_params=_params(("arbitrary",)),
        name="merge",
    )(o_a, o_b, gate_a, gate_b, x, woa, wob, wout, gt1, g2, sc2, sh2, wr, br)


def _dest_kernel(ri_ref, cnt_ref, dest_ref, blk_ref, pend_ref):
    shift = EXPERT_BLOCK.bit_length() - 1
    cnt = jnp.broadcast_to(cnt_ref[...], (8, LANES))
    lane_row = lax.broadcasted_iota(I32, cnt.shape, 1)
    padded = jnp.where(lane_row < N_EXPERTS, ((cnt + (EXPERT_BLOCK - 1)) >> shift) << shift, 0)
    pend = padded
    s = 1
    while s < N_EXPERTS:
        pend = pend + jnp.where(lane_row >= s, pltpu.roll(pend, s, 1), 0)
        s *= 2
    pstart = (pend - padded)[0:1, :].astype(F32)
    pend = pend[0:1, :]
    pend_ref[...] = pend

    ri = ri_ref[...]
    lane = lax.broadcasted_iota(I32, ri.shape, 1)
    dest = jnp.zeros(ri.shape, F32)
    for kk in range(TOP_K):
        ix = ri[:, kk:kk + 1]
        rank = ri[:, TOP_K + kk:TOP_K + kk + 1].astype(F32)
        start = jnp.sum(jnp.where(lane == ix, pstart, 0.0), axis=-1, keepdims=True)
        dest = dest + jnp.where(lane == kk, start + rank, 0.0)
    dest_ref[...] = dest.astype(I32)

    bstart = lax.broadcasted_iota(I32, blk_ref.shape, 0) * EXPERT_BLOCK
    lane_b = lax.broadcasted_iota(I32, blk_ref.shape, 1)
    ended = jnp.where((lane_b < N_EXPERTS) & (pend <= bstart), 1.0, 0.0)
    e = jnp.minimum(jnp.sum(ended, axis=-1, keepdims=True), float(N_EXPERTS - 1))
    vend = pstart + cnt[0:1, :].astype(F32)
    vend_e = jnp.sum(jnp.where(lane_b.astype(F32) == e, vend, 0.0), axis=-1, keepdims=True)
    nvalid = jnp.clip(vend_e - bstart[:, 0:1].astype(F32), 0.0, float(EXPERT_BLOCK))
    blk_ref[...] = jnp.where(lane_b == 0, e, jnp.where(lane_b == 1, nvalid, 0.0)).astype(I32)


def _dest(ri, cnt, n_blocks, tm):
    t = ri.shape[0]
    nb_pad = -(-n_blocks // 8) * 8
    return pl.pallas_call(
        _dest_kernel,
        grid=(t // tm,),
        in_specs=[pl.BlockSpec((tm, LANES), lambda i: (i, 0)), pl.BlockSpec((1, LANES), lambda i: (0, 0))],
        out_specs=[pl.BlockSpec((tm, LANES), lambda i: (i, 0)), pl.BlockSpec((nb_pad, LANES), lambda i: (0, 0)),
                   pl.BlockSpec((1, LANES), lambda i: (0, 0))],
        out_shape=[jax.ShapeDtypeStruct((t, LANES), I32), jax.ShapeDtypeStruct((nb_pad, LANES), I32),
                   jax.ShapeDtypeStruct((1, LANES), I32)],
        compiler_params=_params(("arbitrary",)),
        name="dest",
    )(ri, cnt)


ROW_UNROLL = 8


def _drain_tokens(src_ref, dst_ref, sem, n_tokens):
    def drain(_, carry):
        for _ in range(ROW_UNROLL * TOP_K):
            _token_copy(src_ref, 0, dst_ref, 0, sem).wait()
        return carry

    lax.fori_loop(0, n_tokens // ROW_UNROLL, drain, 0)


def _dispatch_kernel(dest_ref, h_ref, xbuf_ref, sem):
    tm = h_ref.shape[0] // TOKEN_TILE_ROWS

    def issue(g, carry):
        for u in range(ROW_UNROLL):
            r = g * ROW_UNROLL + u
            for kk in range(TOP_K):
                _token_copy(h_ref, r, xbuf_ref, dest_ref[r * TOP_K + kk], sem).start(priority=kk % 2)
        return carry

    lax.fori_loop(0, tm // ROW_UNROLL, issue, 0)
    _drain_tokens(h_ref, xbuf_ref, sem, tm)


def _dispatch(dest_flat, h2_tiles, n_rows):
    tm = ROUTE_TILE
    t = h2_tiles.shape[0] // TOKEN_TILE_ROWS
    return pl.pallas_call(
        _dispatch_kernel,
        grid=(t // tm,),
        in_specs=[pl.BlockSpec((tm * TOP_K,), lambda i: (i,), memory_space=pltpu.SMEM),
                  pl.BlockSpec((tm * TOKEN_TILE_ROWS, LANES), lambda i: (i, 0))],
        out_specs=pl.BlockSpec(memory_space=pl.ANY),
        out_shape=jax.ShapeDtypeStruct((n_rows * TOKEN_TILE_ROWS, LANES), h2_tiles.dtype),
        scratch_shapes=[pltpu.SemaphoreType.DMA(())],
        compiler_params=_params(("arbitrary",)),
        name="dispatch",
    )(dest_flat, h2_tiles)


def _experts_kernel(blk_e_ref, nvalid_ref, grp_ref, nxt_e_ref, x_ref, wgu_hbm, bgu_ref, wd_hbm, bd_ref, y_ref,
                    wgu32_ref, wd32_ref, wgu16_ref, wd16_ref, sems):
    b = pl.program_id(0)
    nvalid = nvalid_ref[b]
    e = blk_e_ref[b]
    slot = grp_ref[b] % 2
    first = (b == 0) | (e != blk_e_ref[jnp.maximum(b - 1, 0)])

    def weight_copies(expert, s):
        return (pltpu.make_async_copy(wgu_hbm.at[expert], wgu32_ref.at[s], sems.at[s, 0]),
                pltpu.make_async_copy(wd_hbm.at[expert], wd32_ref.at[s], sems.at[s, 1]))

    @pl.when(b == 0)
    def _():
        for cp in weight_copies(e, 0):
            cp.start()

    @pl.when(first)
    def _():
        for cp in weight_copies(e, slot):
            cp.wait()
        wgu16_ref[...] = wgu32_ref[slot].astype(BF16)
        wd16_ref[...] = wd32_ref[slot].astype(BF16)

    @pl.when(first & (nxt_e_ref[b] != e))
    def _():
        for cp in weight_copies(nxt_e_ref[b], 1 - slot):
            cp.start()

    @pl.when(nvalid > 0)
    def _():
        n = x_ref.shape[0] // TOKEN_TILE_ROWS
        x = jnp.concatenate([_load_token_slab(x_ref, s, n) for s in range(TOKEN_TILE_ROWS)], axis=1)
        rows = lax.broadcasted_iota(I32, x.shape, 0)
        x = jnp.where(rows < nvalid, x, 0.0).astype(BF16)
        gu = _dot(x, wgu16_ref[...]) + bgu_ref[...]
        gate = jnp.minimum(gu[:, :FF], SWIGLU_LIMIT)
        up = jnp.clip(gu[:, FF:], -SWIGLU_LIMIT, SWIGLU_LIMIT)
        act = (up + 1.0) * (gate * _sigmoid(SWIGLU_ALPHA * gate))
        _store_token_tiles(y_ref, _dot(act.astype(BF16), wd16_ref[...]) + bd_ref[...])

    @pl.when(nvalid <= 0)
    def _():
        y_ref[...] = jnp.zeros_like(y_ref)


def _experts(blk_e, nvalid, xbuf, wgu, bgu, wd, bd):
    d = D_MODEL
    blk_rows = EXPERT_BLOCK * TOKEN_TILE_ROWS
    nb = xbuf.shape[0] // blk_rows
    change = jnp.concatenate([jnp.zeros((1,), I32), (blk_e[1:] != blk_e[:-1]).astype(I32)])
    grp = jnp.cumsum(change)
    pos = jnp.where(change > 0, jnp.arange(nb, dtype=I32), nb)
    nxt_pos = lax.cummin(jnp.concatenate([pos[1:], jnp.full((1,), nb, I32)]), reverse=True)
    nxt_e = jnp.where(nxt_pos < nb, blk_e[jnp.minimum(nxt_pos, nb - 1)], blk_e)
    grid_spec = pltpu.PrefetchScalarGridSpec(
        num_scalar_prefetch=4,
        grid=(nb,),
        in_specs=[pl.BlockSpec((blk_rows, LANES), lambda b, e, *_: (b, 0)),
                  pl.BlockSpec(memory_space=pl.ANY),
                  pl.BlockSpec((None, 1, 2 * FF), lambda b, e, *_: (e[b], 0, 0)),
                  pl.BlockSpec(memory_space=pl.ANY),
                  pl.BlockSpec((None, 1, d), lambda b, e, *_: (e[b], 0, 0))],
        out_specs=pl.BlockSpec((blk_rows, LANES), lambda b, e, *_: (b, 0)),
        scratch_shapes=[pltpu.VMEM((2, d, 2 * FF), F32), pltpu.VMEM((2, FF, d), F32),
                        pltpu.VMEM((d, 2 * FF), BF16), pltpu.VMEM((FF, d), BF16),
                        pltpu.SemaphoreType.DMA((2, 2))],
    )
    return pl.pallas_call(
        _experts_kernel,
        grid_spec=grid_spec,
        out_shape=jax.ShapeDtypeStruct(xbuf.shape, F32),
        compiler_params=_params(("arbitrary",)),
        name="experts",
    )(blk_e, nvalid, grp, nxt_e, xbuf, wgu, bgu, wd, bd)


def _combine_kernel(dest_ref, dest_next_ref, ybuf_ref, x1_ref, gate_ref, gt2_ref, gf_ref, o_ref, rows_ref, sems):
    i = pl.program_id(0)
    tm = x1_ref.shape[0]
    slot = i % 2

    def start_gather(idx_ref, s):
        def issue(g, carry):
            for u in range(ROW_UNROLL):
                r = g * ROW_UNROLL + u
                for kk in range(TOP_K):
                    _token_copy(ybuf_ref, idx_ref[r * TOP_K + kk], rows_ref.at[s, kk], r,
                                sems.at[s]).start(priority=kk % 2)
            return carry

        lax.fori_loop(0, tm // ROW_UNROLL, issue, 0)

    @pl.when(i == 0)
    def _():
        start_gather(dest_ref, 0)

    @pl.when(i + 1 < pl.num_programs(0))
    def _():
        start_gather(dest_next_ref, 1 - slot)

    _drain_tokens(ybuf_ref, rows_ref.at[slot, 0], sems.at[slot], tm)

    gates = gate_ref[...]
    slabs = []
    for s in range(TOKEN_TILE_ROWS):
        moe = gates[:, 0:1] * _load_token_slab(rows_ref.at[slot, 0], s, tm)
        for kk in range(1, TOP_K):
            moe = moe + gates[:, kk:kk + 1] * _load_token_slab(rows_ref.at[slot, kk], s, tm)
        slabs.append(moe)
    xo = x1_ref[...] + gt2_ref[...] * jnp.concatenate(slabs, axis=1)
    o_ref[...] = xo * lax.rsqrt(jnp.mean(xo * xo, axis=-1, keepdims=True) + RMS_EPS) * gf_ref[...]


def _combine(dest_flat, ybuf, x1, gates, gt2, g_final):
    t, d = x1.shape
    tm = ROUTE_TILE
    n = t // tm
    return pl.pallas_call(
        _combine_kernel,
        grid=(n,),
        in_specs=[pl.BlockSpec((tm * TOP_K,), lambda i: (i,), memory_space=pltpu.SMEM),
                  pl.BlockSpec((tm * TOP_K,), lambda i: (jnp.minimum(i + 1, n - 1),), memory_space=pltpu.SMEM),
                  pl.BlockSpec(memory_space=pl.ANY),
                  pl.BlockSpec((tm, d), lambda i: (i, 0)),
                  pl.BlockSpec((tm, LANES), lambda i: (i, 0)),
                  pl.BlockSpec((1, d), lambda i: (0, 0)),
                  pl.BlockSpec((1, d), lambda i: (0, 0))],
        out_specs=pl.BlockSpec((tm, d), lambda i: (i, 0)),
        out_shape=jax.ShapeDtypeStruct((t, d), F32),
        scratch_shapes=[pltpu.VMEM((2, TOP_K, tm * TOKEN_TILE_ROWS, LANES), F32), pltpu.SemaphoreType.DMA((2,))],
        compiler_params=_params(("arbitrary",)),
        name="combine",
    )(dest_flat, dest_flat, ybuf, x1, gates, gt2, g_final)


def _pad_lanes(v, lane0):
    return jnp.zeros((1, LANES), F32).at[0, lane0:lane0 + v.shape[0]].set(v.astype(F32))


def _layer(x, mod, g_norm1, w_in, conv_w, a_log, dt_bias, g_onorm, b_fgate, w_o_delta, w_o_fox, w_out,
           g_norm2, w_router, b_router, w_gate_up, b_gate_up, w_down, b_down, g_final):
    t, d = x.shape
    sh1, sc1, gt1, sh2, sc2, gt2 = [mod[:, i * d:(i + 1) * d] for i in range(N_MOD)]

    w_t = w_in.T
    o = 0
    wq = w_t[o:o + QKV_W]; o += QKV_W
    wz = w_t[o:o + MIX_W]; o += MIX_W
    w_beta = w_t[o:o + N_HEADS]; o += N_HEADS
    w_dec = w_t[o:o + N_HEADS]; o += N_HEADS
    wf = w_t[o:o + QKV_W]; o += QKV_W
    wf = jnp.concatenate([wf[:MIX_W] * (LOG2E * HEAD_DIM ** -0.5), wf[MIX_W:]], axis=0)
    w_fg = w_t[o:o + N_HEADS]; o += N_HEADS
    wga = w_t[o:o + d]; o += d
    wgb = w_t[o:o + d]
    ws = jnp.zeros((LANES, d), F32)
    ws = ws.at[L_BETA:L_BETA + N_HEADS].set(w_beta).at[L_G:L_G + N_HEADS].set(w_dec)
    ws = ws.at[L_F:L_F + N_HEADS].set(w_fg)
    bf = lambda w: w.astype(BF16)

    tm = min(ROW_TILE, t)
    qkv, z, fx, gate_a, gate_b, small = _inproj(
        x, g_norm1.reshape(1, d), sc1, sh1, bf(wq), bf(wz), bf(wf), bf(wga), bf(wgb), bf(ws), tm)

    gcol, grow = _gates(small, _pad_lanes(a_log, L_G), _pad_lanes(dt_bias, L_G), _pad_lanes(b_fgate, L_F),
                        min(GATES_TILE, t))

    o_a = _delta(qkv, z, gcol, grow, conv_w, g_onorm.reshape(1, HEAD_DIM), min(DELTA_TILE, t))
    tq, tk = min(FOX_TQ, t), min(FOX_TK, t)
    f_rows = grow[L_F:L_F + N_HEADS].reshape(N_HEADS, t // tk, 1, tk)
    o_b = _fox(fx, f_rows, tq, tk)

    wr = jnp.zeros((d, LANES), F32).at[:, :N_EXPERTS].set(w_router)
    br = _pad_lanes(b_router, 0)
    x1, h2, ri, gates, cnt = _merge(o_a, o_b, gate_a, gate_b, x, bf(w_o_delta), bf(w_o_fox), bf(w_out), gt1,
                                    g_norm2.reshape(1, d), sc2, sh2, wr, br, tm)

    n_blocks = (t * TOP_K) // EXPERT_BLOCK + N_EXPERTS
    dest, blk, _ = _dest(ri, cnt, n_blocks, min(2048, t))
    dest_flat = dest[:, :TOP_K].reshape(t * TOP_K)
    blk_e, nvalid = blk[:n_blocks, 0], blk[:n_blocks, 1]

    xbuf = _dispatch(dest_flat, h2, n_blocks * EXPERT_BLOCK)
    ybuf = _experts(blk_e, nvalid, xbuf, w_gate_up, b_gate_up.reshape(N_EXPERTS, 1, 2 * FF),
                    w_down, b_down.reshape(N_EXPERTS, 1, d))
    return _combine(dest_flat, ybuf, x1, gates, gt2, g_final.reshape(1, d))


def kernel(x, c, w_ada, b_ada, g_norm1, w_in, conv_w, a_log, dt_bias, g_onorm, b_fgate, w_o_delta, w_o_fox, w_out,
           g_norm2, w_router, b_router, w_gate_up, b_gate_up, w_down, b_down, g_final):
    b, s, d = x.shape
    assert b == 1 and d == D_MODEL and w_ada.shape[0] == 1
    h = x[0]
    for l in range(w_ada.shape[0]):
        mod = _modulation(c[0], w_ada[l], b_ada[l])
        h = _layer(h, mod, g_norm1[l], w_in[l], conv_w[l], a_log[l], dt_bias[l], g_onorm[l], b_fgate[l],
                   w_o_delta[l], w_o_fox[l], w_out[l], g_norm2[l], w_router[l], b_router[l], w_gate_up[l],
                   b_gate_up[l], w_down[l], b_down[l], g_final)
    return h[None]
```

```python
import functools

import jax
import jax.numpy as jnp
from jax import lax
from jax.experimental import pallas as pl
from jax.experimental.pallas import tpu as pltpu

F32 = jnp.float32
BF16 = jnp.bfloat16
I32 = jnp.int32

D_MODEL = 1024
HEAD_DIM = 128
N_HEADS = 4
MIX_W = N_HEADS * HEAD_DIM
QKV_W = 3 * MIX_W
CONV_K = 4
CHUNK = 64
STACK = N_HEADS * CHUNK
SUB = 16
N_EXPERTS = 32
TOP_K = 4
FF = D_MODEL
SWIGLU_LIMIT = 7.0
SWIGLU_ALPHA = 1.702
RMS_EPS = 1e-6
L2_EPS = 1e-6
N_MOD = 6
LANES = 128
NEG_BIG = -1e30
LOG2E = 1.4426950408889634

ROW_TILE = 512
GATES_TILE = 256
DELTA_TILE = 256
EXPERT_BLOCK = 512
ROUTE_TILE = 256
VMEM_LIMIT = 56 * 1024 * 1024


def _params(sem, vmem=VMEM_LIMIT, flags=None):
    return pltpu.CompilerParams(dimension_semantics=sem, vmem_limit_bytes=vmem, flags=flags)


def _softplus(x):
    return jnp.maximum(x, 0.0) + jnp.log(1.0 + jnp.exp(-jnp.abs(x)))


def _sigmoid(x):
    return 1.0 / (1.0 + jnp.exp(-x))


def _silu(x):
    return x * _sigmoid(x)


def _dot(a, b, precision=None):
    return jnp.dot(a, b, preferred_element_type=F32, precision=precision)


def _dot_nt(a, b, precision=None):
    return lax.dot_general(a, b, (((1,), (1,)), ((), ())), preferred_element_type=F32, precision=precision)


def _split_bf16(x, terms):
    parts = []
    for _ in range(terms):
        p = x.astype(BF16)
        parts.append(p)
        x = x - p.astype(F32)
    return parts


def _dot_split(a, b):
    a_hi, a_lo = _split_bf16(a, 2)
    b_hi, b_lo = _split_bf16(b, 2)
    return _dot(a_hi, b_hi) + (_dot(a_hi, b_lo) + _dot(a_lo, b_hi))


def _dot_mask(mask01, x):
    return sum(_dot(mask01, p) for p in _split_bf16(x, 3))


TOKEN_TILE_ROWS = D_MODEL // LANES


def _store_token_tiles(ref, x):
    n = x.shape[0]
    for s in range(TOKEN_TILE_ROWS):
        ref[pl.ds(s, n, stride=TOKEN_TILE_ROWS), :] = x[:, s * LANES:(s + 1) * LANES]


def _load_token_slab(ref, s, n):
    return ref[pl.ds(s, n, stride=TOKEN_TILE_ROWS), :]


def _token_copy(src_ref, src_tok, dst_ref, dst_tok, sem):
    rows = lambda i: pl.ds(pl.multiple_of(i * TOKEN_TILE_ROWS, TOKEN_TILE_ROWS), TOKEN_TILE_ROWS)
    return pltpu.make_async_copy(src_ref.at[rows(src_tok), :], dst_ref.at[rows(dst_tok), :], sem)


def _mod_kernel(c_ref, w_ref, b_ref, o_ref):
    o_ref[...] = jnp.sum(c_ref[...] * w_ref[...], axis=0, keepdims=True) + b_ref[...]


def _modulation(c, w_ada, b_ada):
    d, n = w_ada.shape
    tn = 1024
    return pl.pallas_call(
        _mod_kernel,
        grid=(n // tn,),
        in_specs=[pl.BlockSpec((d, 1), lambda j: (0, 0)),
                  pl.BlockSpec((d, tn), lambda j: (0, j)),
                  pl.BlockSpec((1, tn), lambda j: (0, j))],
        out_specs=pl.BlockSpec((1, tn), lambda j: (0, j)),
        out_shape=jax.ShapeDtypeStruct((1, n), F32),
        compiler_params=_params(("arbitrary",)),
        name="mod",
    )(c.reshape(d, 1), w_ada, b_ada.reshape(1, n))


def _inproj_kernel(x_ref, g_ref, sc_ref, sh_ref, wq_ref, wz_ref, wf_ref, wga_ref, wgb_ref, ws_ref,
                   oq_ref, oz_ref, of_ref, oga_ref, ogb_ref, os_ref):
    x = x_ref[...]
    y = x * lax.rsqrt(jnp.mean(x * x, axis=-1, keepdims=True) + RMS_EPS) * g_ref[...]
    h = (y * (1.0 + sc_ref[...]) + sh_ref[...]).astype(BF16)
    oq_ref[...] = _dot_nt(h, wq_ref[...])
    oz_ref[...] = _dot_nt(h, wz_ref[...]).astype(BF16)
    of_ref[...] = _dot_nt(h, wf_ref[...]).astype(BF16)
    oga_ref[...] = _dot_nt(h, wga_ref[...]).astype(BF16)
    ogb_ref[...] = _dot_nt(h, wgb_ref[...]).astype(BF16)
    os_ref[...] = _dot_nt(h, ws_ref[...])


def _inproj(x, g1, sc1, sh1, wq, wz, wf, wga, wgb, ws, tm):
    t, d = x.shape
    row = lambda i: (i, 0)
    fixed = lambda i: (0, 0)
    ws_list = [wq, wz, wf, wga, wgb, ws]
    out_dt = [F32, BF16, BF16, BF16, BF16, F32]
    return pl.pallas_call(
        _inproj_kernel,
        grid=(t // tm,),
        in_specs=[pl.BlockSpec((tm, d), row)] + [pl.BlockSpec((1, d), fixed)] * 3
                 + [pl.BlockSpec(w.shape, fixed) for w in ws_list],
        out_specs=[pl.BlockSpec((tm, w.shape[0]), row) for w in ws_list],
        out_shape=[jax.ShapeDtypeStruct((t, w.shape[0]), dt) for w, dt in zip(ws_list, out_dt)],
        compiler_params=_params(("arbitrary",)),
        name="inproj",
    )(x, g1, sc1, sh1, *ws_list)


L_BETA, L_G, L_F, L_GC, L_GL = 0, 4, 8, 12, 16


def _gates_kernel(s_ref, alog_ref, dtb_ref, bf_ref, col_ref, row_ref, carry_ref):
    i = pl.program_id(0)
    tm = s_ref.shape[0]

    @pl.when(i == 0)
    def _():
        carry_ref[...] = jnp.zeros_like(carry_ref)

    s = s_ref[...]
    lane = lax.broadcasted_iota(I32, s.shape, 1)
    beta = _sigmoid(s)
    g = -jnp.exp(alog_ref[...]) * _softplus(s + dtb_ref[...])
    logf = -_softplus(-(s + bf_ref[...]))
    is_g = (lane >= L_G) & (lane < L_G + N_HEADS)
    is_f = (lane >= L_F) & (lane < L_F + N_HEADS)
    g = jnp.where(is_g, g, 0.0)
    logf = jnp.where(is_f, logf, 0.0)

    r = lax.broadcasted_iota(I32, (tm, tm), 0)
    c = lax.broadcasted_iota(I32, (tm, tm), 1)
    same_chunk = (r // CHUNK) == (c // CHUNK)
    tri = jnp.where(r >= c, 1.0, 0.0).astype(BF16)
    tri_chunk = jnp.where(same_chunk & (r >= c), 1.0, 0.0).astype(BF16)
    ones_chunk = jnp.where(same_chunk, 1.0, 0.0).astype(BF16)
    f_cum = _dot_mask(tri, logf) + carry_ref[...]
    carry_ref[...] = f_cum[tm - 1:tm, :]
    gc = _dot_mask(tri_chunk, g)
    gl = _dot_mask(ones_chunk, g)

    out = jnp.where(lane < N_HEADS, beta, 0.0) + g + f_cum
    out = out + pltpu.roll(gc, L_GC - L_G, 1) + pltpu.roll(gl, L_GL - L_G, 1)
    col_ref[...] = out
    row_ref[...] = out.T[:row_ref.shape[0], :]


def _gates(small, alog_row, dtb_row, bf_row, tm):
    t = small.shape[0]
    n_rows = 24
    return pl.pallas_call(
        _gates_kernel,
        grid=(t // tm,),
        in_specs=[pl.BlockSpec((tm, LANES), lambda i: (i, 0))] + [pl.BlockSpec((1, LANES), lambda i: (0, 0))] * 3,
        out_specs=[pl.BlockSpec((tm, LANES), lambda i: (i, 0)), pl.BlockSpec((n_rows, tm), lambda i: (0, i))],
        out_shape=[jax.ShapeDtypeStruct((t, LANES), F32), jax.ShapeDtypeStruct((n_rows, t), F32)],
        scratch_shapes=[pltpu.VMEM((1, LANES), F32)],
        compiler_params=_params(("arbitrary",)),
        name="gates",
    )(small, alog_row, dtb_row, bf_row)


def _stack_heads(a, col0):
    return jnp.concatenate([a[:, col0 + h * HEAD_DIM: col0 + (h + 1) * HEAD_DIM] for h in range(N_HEADS)], axis=0)


def _stack_cols(a, lane0):
    return jnp.concatenate([a[:, lane0 + h: lane0 + h + 1] for h in range(N_HEADS)], axis=0)


def _delta_kernel(qkv_ref, z_ref, gcol_ref, grow_ref, cw_ref, gon_ref, o_ref, ext_ref, tail_ref, s_ref):
    i = pl.program_id(0)
    tb = qkv_ref.shape[0]
    pad = tail_ref.shape[0]

    @pl.when(i == 0)
    def _():
        tail_ref[...] = jnp.zeros_like(tail_ref)
        s_ref[...] = jnp.zeros_like(s_ref)

    ext_ref[0:pad, :] = tail_ref[...]
    ext_ref[pad:pad + tb, :] = qkv_ref[...]
    tail_ref[...] = qkv_ref[tb - pad:tb, :]
    conv = cw_ref[0:1, :] * ext_ref[pad - 3:pad - 3 + tb, :]
    for j in range(1, CONV_K):
        conv = conv + cw_ref[j:j + 1, :] * ext_ref[pad - 3 + j:pad - 3 + j + tb, :]
    act = _silu(conv)

    r = lax.broadcasted_iota(I32, (STACK, STACK), 0)
    c = lax.broadcasted_iota(I32, (STACK, STACK), 1)
    same = (r // CHUNK) == (c // CHUNK)
    m_incl = same & (r >= c)
    m_strict = same & (r > c)
    eye = jnp.where(r == c, 1.0, 0.0)
    m_sub = []
    size = SUB
    while size <= CHUNK:
        m_sub.append((r // size) == (c // size))
        size *= 2
    rb = lax.broadcasted_iota(I32, (STACK, N_HEADS * HEAD_DIM), 0) // CHUNK
    cb = lax.broadcasted_iota(I32, (STACK, N_HEADS * HEAD_DIM), 1) // HEAD_DIM
    head_match = rb == cb

    chunks = range(tb // CHUNK)
    pre = []
    for ch in chunks:
        r0 = ch * CHUNK
        a = act[r0:r0 + CHUNK, :]
        q = _stack_heads(a, 0)
        k = _stack_heads(a, MIX_W)
        v = _stack_heads(a, 2 * MIX_W)
        q = q * lax.rsqrt(jnp.sum(q * q, axis=-1, keepdims=True) + L2_EPS) * (HEAD_DIM ** -0.5)
        k = k * lax.rsqrt(jnp.sum(k * k, axis=-1, keepdims=True) + L2_EPS)

        gcols = gcol_ref[r0:r0 + CHUNK, :]
        beta = _stack_cols(gcols, L_BETA)
        gc = _stack_cols(gcols, L_GC)
        gc_row = jnp.concatenate(
            [grow_ref[L_GC + h:L_GC + h + 1, r0:r0 + CHUNK] for h in range(N_HEADS)], axis=1)
        gl_row = jnp.concatenate(
            [grow_ref[L_GL + h:L_GL + h + 1, r0:r0 + CHUNK] for h in range(N_HEADS)], axis=1)

        decay = jnp.exp(jnp.where(m_incl, gc - gc_row, NEG_BIG))
        kb16 = k.astype(BF16)
        kk = _dot_nt(kb16, kb16)
        lmat = jnp.where(m_strict, kk * beta * decay, 0.0)
        attn = (_dot_nt(q.astype(BF16), kb16) * decay).astype(BF16)
        egc = jnp.exp(gc)
        rhs = jnp.concatenate([v * beta, k * (beta * egc)], axis=1).astype(BF16)
        kt_dec = (k.T * jnp.exp(gl_row - gc_row)).astype(BF16)
        pre.append(dict(lmat=lmat, attn=attn, rhs=rhs, kt_dec=kt_dec, q_dec=q * egc))

    l_d = [jnp.where(m_sub[0], p["lmat"], 0.0) for p in pre]
    inv = [eye - l for l in l_d]
    l16 = [l.astype(BF16) for l in l_d]
    pw = [_dot(l, l).astype(BF16) for l in l16]
    n_sq = SUB.bit_length() - 3
    for s in range(n_sq + 1):
        inv = [x + _dot(x.astype(BF16), p) for x, p in zip(inv, pw)]
        if s < n_sq:
            pw = [_dot(p, p).astype(BF16) for p in pw]
    for lvl in range(1, len(m_sub)):
        m_off = m_sub[lvl] & ~m_sub[lvl - 1]
        off = [jnp.where(m_off, p["lmat"], 0.0).astype(BF16) for p in pre]
        inv16 = [x.astype(BF16) for x in inv]
        half = [_dot(x, o).astype(BF16) for x, o in zip(inv16, off)]
        inv = [x - _dot(h, x16) for x, h, x16 in zip(inv, half, inv16)]
    uws = [_dot(x.astype(BF16), p["rhs"]) for x, p in zip(inv, pre)]

    for ch in chunks:
        r0 = ch * CHUNK
        attn, kt_dec = pre[ch]["attn"], pre[ch]["kt_dec"]
        u, w = uws[ch][:, :HEAD_DIM], uws[ch][:, HEAD_DIM:]

        state16 = s_ref[...].astype(BF16)
        wq_s = _dot(jnp.concatenate([w, pre[ch]["q_dec"]], axis=0).astype(BF16), state16)
        ws_d = jnp.concatenate([wq_s[h * CHUNK:(h + 1) * CHUNK, h * HEAD_DIM:(h + 1) * HEAD_DIM]
                                for h in range(N_HEADS)], axis=0)
        qs_d = jnp.concatenate([wq_s[STACK + h * CHUNK:STACK + (h + 1) * CHUNK, h * HEAD_DIM:(h + 1) * HEAD_DIM]
                                for h in range(N_HEADS)], axis=0)
        v_new = u - ws_d
        v16 = v_new.astype(BF16)
        o = qs_d + _dot(attn, v16)

        v_bd =jnp.where(head_match, jnp.concatenate([v_new] * N_HEADS, axis=1), 0.0).astype(BF16)
        upd = _dot(kt_dec, v_bd)
        for h in range(N_HEADS):
            e_h = jnp.exp(grow_ref[L_GL + h:L_GL + h + 1, r0:r0 + 1])
            sl = slice(h * HEAD_DIM, (h + 1) * HEAD_DIM)
            s_ref[:, sl] = s_ref[:, sl] * e_h + upd[:, sl]

        for h in range(N_HEADS):
            oh = o[h * CHUNK:(h + 1) * CHUNK, :]
            oh = oh * lax.rsqrt(jnp.mean(oh * oh, axis=-1, keepdims=True) + RMS_EPS) * gon_ref[...]
            zh = z_ref[r0:r0 + CHUNK, h * HEAD_DIM:(h + 1) * HEAD_DIM].astype(F32)
            o_ref[r0:r0 + CHUNK, h * HEAD_DIM:(h + 1) * HEAD_DIM] = (oh * _silu(zh)).astype(o_ref.dtype)


def _delta(qkv, z, gcol, grow, conv_w, g_onorm, tb):
    t = qkv.shape[0]
    pad = 8
    return pl.pallas_call(
        _delta_kernel,
        grid=(t // tb,),
        in_specs=[pl.BlockSpec((tb, QKV_W), lambda i: (i, 0)),
                  pl.BlockSpec((tb, MIX_W), lambda i: (i, 0)),
                  pl.BlockSpec((tb, LANES), lambda i: (i, 0)),
                  pl.BlockSpec((grow.shape[0], tb), lambda i: (0, i)),
                  pl.BlockSpec((CONV_K, QKV_W), lambda i: (0, 0)),
                  pl.BlockSpec((1, HEAD_DIM), lambda i: (0, 0))],
        out_specs=pl.BlockSpec((tb, MIX_W), lambda i: (i, 0)),
        out_shape=jax.ShapeDtypeStruct((t, MIX_W), BF16),
        scratch_shapes=[pltpu.VMEM((tb + pad, QKV_W), F32),
                        pltpu.VMEM((pad, QKV_W), F32),
                        pltpu.VMEM((HEAD_DIM, N_HEADS * HEAD_DIM), F32)],
        compiler_params=_params(("arbitrary",)),
        name="delta",
    )(qkv, z, gcol, grow, conv_w, g_onorm)


FOX_GROUP = 4
FOX_TQ, FOX_TK = 256, 2048
FOX_DIAG_PARTS = 2
FOX_SKIP_LOG2 = -200.0


def _fox_kernel(*refs, tk):
    g = FOX_GROUP
    q_ref, o_ref, kmax_ref = refs[0], refs[-2], refs[-1]
    k_refs, v_refs, f_refs = (refs[1 + i * g:1 + (i + 1) * g] for i in range(3))
    qi = pl.program_id(1)
    tq = q_ref.shape[0]
    t_keys = k_refs[0].shape[0]

    @pl.when(qi == 0)
    def _():
        rows = min(t_keys, 2048)
        for h in range(g):
            def norm_max(c, best, h=h):
                kc = k_refs[h][pl.ds(pl.multiple_of(c * rows, rows), rows), :].astype(F32)
                return jnp.maximum(best, jnp.sum(kc * kc, axis=-1, keepdims=True))

            best = lax.fori_loop(0, t_keys // rows, norm_max, jnp.zeros((rows, 1), F32))
            kmax_ref[h:h + 1, :] = jnp.broadcast_to(jnp.sqrt(jnp.max(best, axis=0, keepdims=True)), (1, LANES))

    def step(off, width, f_blk, carry, mask):
        ones = jnp.ones((width, HEAD_DIM), BF16)
        ss = []
        for h in range(g):
            q = q_ref[:, h * HEAD_DIM:(h + 1) * HEAD_DIM]
            s = _dot_nt(q, k_refs[h][pl.ds(off, width), :]) - f_blk(h) * LOG2E
            if mask is not None:
                s = jnp.where(mask, s, NEG_BIG)
            ss.append(s)
        ps, ms, alphas = [], [], []
        for h in range(g):
            m = carry[h][0]
            m_new = jnp.maximum(m, jnp.max(ss[h], axis=-1, keepdims=True))
            alphas.append(jnp.exp2(m - m_new))
            ps.append(jnp.exp2(ss[h] - m_new).astype(BF16))
            ms.append(m_new)
        out = []
        for h in range(g):
            v_aug = jnp.concatenate([v_refs[h][pl.ds(off, width), :], ones], axis=1)
            out.append((ms[h], alphas[h] * carry[h][1] + _dot(ps[h], v_aug)))
        return tuple(out)

    def head_chunk(h, j, mc):
        m, acc = mc
        off = pl.multiple_of(j * tk, tk)
        q = q_ref[:, h * HEAD_DIM:(h + 1) * HEAD_DIM]
        s = _dot_nt(q, k_refs[h][pl.ds(off, tk), :]) - f_refs[h][j] * LOG2E
        m_new = jnp.maximum(m, jnp.max(s, axis=-1, keepdims=True))
        p = jnp.exp2(s - m_new).astype(BF16)
        v_aug = jnp.concatenate([v_refs[h][pl.ds(off, tk), :], jnp.ones((tk, HEAD_DIM), BF16)], axis=1)
        return m_new, jnp.exp2(m - m_new) * acc + _dot(p, v_aug)

    init = tuple((jnp.full((tq, 1), NEG_BIG, F32), jnp.zeros((tq, 2 * HEAD_DIM), F32)) for _ in range(g))
    n_full = (qi * tq) // tk

    parts = max(1, min(FOX_DIAG_PARTS, tk // tq))
    width = tk // parts
    row = lax.broadcasted_iota(I32, (tq, width), 0)
    col = lax.broadcasted_iota(I32, (tq, width), 1)
    carry = init
    for d in reversed(range(max(tq // tk, 1))):
        j = n_full + d
        for p in reversed(range(parts)):
            off = pl.multiple_of(j * tk + p * width, width)
            piece = functools.partial(step, off, width,
                                      lambda h, j=j, p=p: f_refs[h][j][:, p * width:(p + 1) * width],
                                      mask=col + (off - qi * tq) <= row)
            carry = piece(carry) if p == 0 else lax.cond(off < (qi + 1) * tq, piece, lambda c: c, carry)

    qn = []
    for h in range(g):
        q = q_ref[:, h * HEAD_DIM:(h + 1) * HEAD_DIM].astype(F32)
        qn.append(jnp.sqrt(jnp.sum(q * q, axis=-1, keepdims=True)) * kmax_ref[h:h + 1, 0:1] * 1.01 + 1.0)

    def reaches(j, carry):
        jc = jnp.maximum(j, 0)
        flags = []
        for h in range(g):
            bound = qn[h] - f_refs[h][jc][:, tk - 1:tk] * LOG2E
            flags.append((jnp.max(bound - carry[h][0]) >= FOX_SKIP_LOG2).astype(I32))
        return tuple(flags)

    def chunk_all_heads(state):
        j, _, carry = state
        carry = step(pl.multiple_of(j * tk, tk), tk, lambda h: f_refs[h][j], carry, None)
        return j - 1, reaches(j - 1, carry), carry

    def chunk_live_heads(state):
        j, live, carry = state
        carry = tuple(lax.cond(live[h] > 0, functools.partial(head_chunk, h, j), lambda mc: mc, carry[h])
                      for h in range(g))
        return j - 1, reaches(j - 1, carry), carry

    state = (n_full - 1, reaches(n_full - 1, carry), carry)
    state = lax.while_loop(lambda st: (st[0] >= 0) & (sum(st[1]) == g), chunk_all_heads, state)
    _, _, carry = lax.while_loop(lambda st: (st[0] >= 0) & (sum(st[1]) > 0), chunk_live_heads, state)
    for h in range(g):
        acc = carry[h][1]
        o_ref[:, h * HEAD_DIM:(h + 1) * HEAD_DIM] = (acc[:, :HEAD_DIM] / acc[:, HEAD_DIM:]).astype(o_ref.dtype)


def _fox(fx, f_rows, tq, tk):
    t = fx.shape[0]
    g = FOX_GROUP
    heads = lambda hg, u: hg * g + u
    q_specs = [pl.BlockSpec((tq, g * HEAD_DIM), lambda hg, i: (i, hg))]
    k_specs = [pl.BlockSpec((t, HEAD_DIM), lambda hg, i, u=u: (0, N_HEADS + heads(hg, u)),
                            pipeline_mode=pl.Buffered(1)) for u in range(g)]
    v_specs = [pl.BlockSpec((t, HEAD_DIM), lambda hg, i, u=u: (0, 2 * N_HEADS + heads(hg, u)),
                            pipeline_mode=pl.Buffered(1)) for u in range(g)]
    f_specs = [pl.BlockSpec((None, t // tk, 1, tk), lambda hg, i, u=u: (heads(hg, u), 0, 0, 0)) for u in range(g)]
    return pl.pallas_call(
        functools.partial(_fox_kernel, tk=tk),
        grid=(N_HEADS // g, t // tq),
        in_specs=q_specs + k_specs + v_specs + f_specs,
        out_specs=pl.BlockSpec((tq, g * HEAD_DIM), lambda hg, i: (i, hg)),
        out_shape=jax.ShapeDtypeStruct((t, MIX_W), BF16),
        scratch_shapes=[pltpu.VMEM((8, LANES), F32)],
        compiler_params=_params(("arbitrary", "arbitrary")),
        name="fox",
    )(*([fx] * (1 + 2 * g) + [f_rows] * g))


def _merge_kernel(oa_ref, ob_ref, ga_ref, gb_ref, x_ref, woa_ref, wob_ref, wout_ref, gt1_ref, g2_ref, sc2_ref,
                  sh2_ref, wr_ref, br_ref, x1_ref, h2_ref, ri_ref, gate_ref, cnt_ref, carry_ref):
    i = pl.program_id(0)
    tm = x_ref.shape[0]

    @pl.when(i == 0)
    def _():
        carry_ref[...] = jnp.zeros_like(carry_ref)

    ya = _dot(oa_ref[...], woa_ref[...])
    yb = _dot(ob_ref[...], wob_ref[...])
    merged = _sigmoid(ga_ref[...].astype(F32)) * ya + _sigmoid(gb_ref[...].astype(F32)) * yb
    x1 = x_ref[...] + gt1_ref[...] * _dot(merged.astype(BF16), wout_ref[...])
    x1_ref[...] = x1
    y = x1 * lax.rsqrt(jnp.mean(x1 * x1, axis=-1, keepdims=True) + RMS_EPS) * g2_ref[...]
    h2 = y * (1.0 + sc2_ref[...]) + sh2_ref[...]
    _store_token_tiles(h2_ref, h2)

    logits = _dot_split(h2, wr_ref[...]) + br_ref[...]
    lane = lax.broadcasted_iota(I32, logits.shape, 1)
    lanef = lane.astype(F32)
    cur = jnp.where(lane < N_EXPERTS, logits, -jnp.inf)
    vals, idxs = [], []
    for _ in range(TOP_K):
        m = jnp.max(cur, axis=-1, keepdims=True)
        ix = jnp.min(jnp.where(cur == m, lanef, float(LANES)), axis=-1, keepdims=True)
        vals.append(m)
        idxs.append(ix)
        cur = jnp.where(lanef == ix, -jnp.inf, cur)
    exps = [jnp.exp(v - vals[0]) for v in vals]
    denom = exps[0] + exps[1] + exps[2] + exps[3]

    onehot = jnp.zeros(logits.shape, F32)
    for ix in idxs:
        onehot = onehot + jnp.where(lanef == ix, 1.0, 0.0)
    r = lax.broadcasted_iota(I32, (tm, tm), 0)
    c = lax.broadcasted_iota(I32, (tm, tm), 1)
    strict = jnp.where(r > c, 1.0, 0.0).astype(BF16)
    before = _dot(strict, onehot.astype(BF16)) + carry_ref[...]
    carry_ref[...] = carry_ref[...] + jnp.sum(onehot, axis=0, keepdims=True)
    cnt_ref[...] = carry_ref[...].astype(I32)

    ri = jnp.zeros(logits.shape, F32)
    gates = jnp.zeros(logits.shape, F32)
    for kk in range(TOP_K):
        rank = jnp.sum(jnp.where(lanef == idxs[kk], before, 0.0), axis=-1, keepdims=True)
        ri = ri + jnp.where(lane == kk, idxs[kk], 0.0) + jnp.where(lane == TOP_K + kk, rank, 0.0)
        gates = gates + jnp.where(lane == kk, exps[kk] / denom, 0.0)
    ri_ref[...] = ri.astype(I32)
    gate_ref[...] = gates


def _merge(o_a, o_b, gate_a, gate_b, x, woa, wob, wout, gt1, g2, sc2, sh2, wr, br, tm):
    t, d = x.shape
    row = lambda i: (i, 0)
    fixed = lambda i: (0, 0)
    return pl.pallas_call(
        _merge_kernel,
        grid=(t // tm,),
        in_specs=[pl.BlockSpec((tm, MIX_W), row), pl.BlockSpec((tm, MIX_W), row),
                  pl.BlockSpec((tm, d), row), pl.BlockSpec((tm, d), row), pl.BlockSpec((tm, d), row),
                  pl.BlockSpec((MIX_W, d), fixed), pl.BlockSpec((MIX_W, d), fixed), pl.BlockSpec((d, d), fixed),
                  pl.BlockSpec((1, d), fixed), pl.BlockSpec((1, d), fixed), pl.BlockSpec((1, d), fixed),
                  pl.BlockSpec((1, d), fixed), pl.BlockSpec((d, LANES), fixed), pl.BlockSpec((1, LANES), fixed)],
        out_specs=[pl.BlockSpec((tm, d), row), pl.BlockSpec((tm * TOKEN_TILE_ROWS, LANES), row),
                   pl.BlockSpec((tm, LANES), row),
                   pl.BlockSpec((tm, LANES), row), pl.BlockSpec((1, LANES), fixed)],
        out_shape=[jax.ShapeDtypeStruct((t, d), F32), jax.ShapeDtypeStruct((t * TOKEN_TILE_ROWS, LANES), F32),
                   jax.ShapeDtypeStruct((t, LANES), I32), jax.ShapeDtypeStruct((t, LANES), F32),
                   jax.ShapeDtypeStruct((1, LANES), I32)],
        scratch_shapes=[pltpu.VMEM((1, LANES), F32)],
        compiler_params=_params(("arbitrary",)),
        name="merge",
    )(o_a, o_b, gate_a, gate_b, x, woa, wob, wout, gt1, g2, sc2, sh2, wr, br)


def _dest_kernel(ri_ref, cnt_ref, dest_ref, blk_ref, pend_ref):
    shift = EXPERT_BLOCK.bit_length() - 1
    cnt = jnp.broadcast_to(cnt_ref[...], (8, LANES))
    lane_row = lax.broadcasted_iota(I32, cnt.shape, 1)
    padded = jnp.where(lane_row < N_EXPERTS, ((cnt + (EXPERT_BLOCK - 1)) >> shift) << shift, 0)
    pend = padded
    s = 1
    while s < N_EXPERTS:
        pend = pend + jnp.where(lane_row >= s, pltpu.roll(pend, s, 1), 0)
        s *= 2
    pstart = (pend - padded)[0:1, :].astype(F32)
    pend = pend[0:1, :]
    pend_ref[...] = pend

    ri = ri_ref[...]
    lane = lax.broadcasted_iota(I32, ri.shape, 1)
    dest = jnp.zeros(ri.shape, F32)
    for kk in range(TOP_K):
        ix = ri[:, kk:kk + 1]
        rank = ri[:, TOP_K + kk:TOP_K + kk + 1].astype(F32)
        start = jnp.sum(jnp.where(lane == ix, pstart, 0.0), axis=-1, keepdims=True)
        dest = dest + jnp.where(lane == kk, start + rank, 0.0)
    dest_ref[...] = dest.astype(I32)

    bstart = lax.broadcasted_iota(I32, blk_ref.shape, 0) * EXPERT_BLOCK
    lane_b = lax.broadcasted_iota(I32, blk_ref.shape, 1)
    ended = jnp.where((lane_b < N_EXPERTS) & (pend <= bstart), 1.0, 0.0)
    e = jnp.minimum(jnp.sum(ended, axis=-1, keepdims=True), float(N_EXPERTS - 1))
    vend = pstart + cnt[0:1, :].astype(F32)
    vend_e = jnp.sum(jnp.where(lane_b.astype(F32) == e, vend, 0.0), axis=-1, keepdims=True)
    nvalid = jnp.clip(vend_e - bstart[:, 0:1].astype(F32), 0.0, float(EXPERT_BLOCK))
    blk_ref[...] = jnp.where(lane_b == 0, e, jnp.where(lane_b == 1, nvalid, 0.0)).astype(I32)


def _dest(ri, cnt, n_blocks, tm):
    t = ri.shape[0]
    nb_pad = -(-n_blocks // 8) * 8
    return pl.pallas_call(
        _dest_kernel,
        grid=(t // tm,),
        in_specs=[pl.BlockSpec((tm, LANES), lambda i: (i, 0)), pl.BlockSpec((1, LANES), lambda i: (0, 0))],
        out_specs=[pl.BlockSpec((tm, LANES), lambda i: (i, 0)), pl.BlockSpec((nb_pad, LANES), lambda i: (0, 0)),
                   pl.BlockSpec((1, LANES), lambda i: (0, 0))],
        out_shape=[jax.ShapeDtypeStruct((t, LANES), I32), jax.ShapeDtypeStruct((nb_pad, LANES), I32),
                   jax.ShapeDtypeStruct((1, LANES), I32)],
        compiler_params=_params(("arbitrary",)),
        name="dest",
    )(ri, cnt)


ROW_UNROLL = 8


def _drain_tokens(src_ref, dst_ref, sem, n_tokens):
    def drain(_, carry):
        for _ in range(ROW_UNROLL * TOP_K):
            _token_copy(src_ref, 0, dst_ref, 0, sem).wait()
        return carry

    lax.fori_loop(0, n_tokens // ROW_UNROLL, drain, 0)


def _dispatch_kernel(dest_ref, h_ref, xbuf_ref, sem):
    tm = h_ref.shape[0] // TOKEN_TILE_ROWS

    def issue(g, carry):
        for u in range(ROW_UNROLL):
            r = g * ROW_UNROLL + u
            for kk in range(TOP_K):
                _token_copy(h_ref, r, xbuf_ref, dest_ref[r * TOP_K + kk], sem).start(priority=kk % 2)
        return carry

    lax.fori_loop(0, tm // ROW_UNROLL, issue, 0)
    _drain_tokens(h_ref, xbuf_ref, sem, tm)


def _dispatch(dest_flat, h2_tiles, n_rows):
    tm = ROUTE_TILE
    t = h2_tiles.shape[0] // TOKEN_TILE_ROWS
    return pl.pallas_call(
        _dispatch_kernel,
        grid=(t // tm,),
        in_specs=[pl.BlockSpec((tm * TOP_K,), lambda i: (i,), memory_space=pltpu.SMEM),
                  pl.BlockSpec((tm * TOKEN_TILE_ROWS, LANES), lambda i: (i, 0))],
        out_specs=pl.BlockSpec(memory_space=pl.ANY),
        out_shape=jax.ShapeDtypeStruct((n_rows * TOKEN_TILE_ROWS, LANES), h2_tiles.dtype),
        scratch_shapes=[pltpu.SemaphoreType.DMA(())],
        compiler_params=_params(("arbitrary",)),
        name="dispatch",
    )(dest_flat, h2_tiles)


def _experts_kernel(blk_e_ref, nvalid_ref, grp_ref, nxt_e_ref, x_ref, wgu_hbm, bgu_ref, wd_hbm, bd_ref, y_ref,
                    wgu32_ref, wd32_ref, wgu16_ref, wd16_ref, sems):
    b = pl.program_id(0)
    nvalid = nvalid_ref[b]
    e = blk_e_ref[b]
    slot = grp_ref[b] % 2
    first = (b == 0) | (e != blk_e_ref[jnp.maximum(b - 1, 0)])

    def weight_copies(expert, s):
        return (pltpu.make_async_copy(wgu_hbm.at[expert], wgu32_ref.at[s], sems.at[s, 0]),
                pltpu.make_async_copy(wd_hbm.at[expert], wd32_ref.at[s], sems.at[s, 1]))

    @pl.when(b == 0)
    def _():
        for cp in weight_copies(e, 0):
            cp.start()

    @pl.when(first)
    def _():
        for cp in weight_copies(e, slot):
            cp.wait()
        wgu16_ref[...] = wgu32_ref[slot].astype(BF16)
        wd16_ref[...] = wd32_ref[slot].astype(BF16)

    @pl.when(first & (nxt_e_ref[b] != e))
    def _():
        for cp in weight_copies(nxt_e_ref[b], 1 - slot):
            cp.start()

    @pl.when(nvalid > 0)
    def _():
        n = x_ref.shape[0] // TOKEN_TILE_ROWS
        x = jnp.concatenate([_load_token_slab(x_ref, s, n) for s in range(TOKEN_TILE_ROWS)], axis=1)
        rows = lax.broadcasted_iota(I32, x.shape, 0)
        x = jnp.where(rows < nvalid, x, 0.0).astype(BF16)
        gu = _dot(x, wgu16_ref[...]) + bgu_ref[...]
        gate = jnp.minimum(gu[:, :FF], SWIGLU_LIMIT)
        up = jnp.clip(gu[:, FF:], -SWIGLU_LIMIT, SWIGLU_LIMIT)
        act = (up + 1.0) * (gate * _sigmoid(SWIGLU_ALPHA * gate))
        _store_token_tiles(y_ref, _dot(act.astype(BF16), wd16_ref[...]) + bd_ref[...])

    @pl.when(nvalid <= 0)
    def _():
        y_ref[...] = jnp.zeros_like(y_ref)


def _experts(blk_e, nvalid, xbuf, wgu, bgu, wd, bd):
    d = D_MODEL
    blk_rows = EXPERT_BLOCK * TOKEN_TILE_ROWS
    nb = xbuf.shape[0] // blk_rows
    change = jnp.concatenate([jnp.zeros((1,), I32), (blk_e[1:] != blk_e[:-1]).astype(I32)])
    grp = jnp.cumsum(change)
    pos = jnp.where(change > 0, jnp.arange(nb, dtype=I32), nb)
    nxt_pos = lax.cummin(jnp.concatenate([pos[1:], jnp.full((1,), nb, I32)]), reverse=True)
    nxt_e = jnp.where(nxt_pos < nb, blk_e[jnp.minimum(nxt_pos, nb - 1)], blk_e)
    grid_spec = pltpu.PrefetchScalarGridSpec(
        num_scalar_prefetch=4,
        grid=(nb,),
        in_specs=[pl.BlockSpec((blk_rows, LANES), lambda b, e, *_: (b, 0)),
                  pl.BlockSpec(memory_space=pl.ANY),
                  pl.BlockSpec((None, 1, 2 * FF), lambda b, e, *_: (e[b], 0, 0)),
                  pl.BlockSpec(memory_space=pl.ANY),
                  pl.BlockSpec((None, 1, d), lambda b, e, *_: (e[b], 0, 0))],
        out_specs=pl.BlockSpec((blk_rows, LANES), lambda b, e, *_: (b, 0)),
        scratch_shapes=[pltpu.VMEM((2, d, 2 * FF), F32), pltpu.VMEM((2, FF, d), F32),
                        pltpu.VMEM((d, 2 * FF), BF16), pltpu.VMEM((FF, d), BF16),
                        pltpu.SemaphoreType.DMA((2, 2))],
    )
    return pl.pallas_call(
        _experts_kernel,
        grid_spec=grid_spec,
        out_shape=jax.ShapeDtypeStruct(xbuf.shape, F32),
        compiler_params=_params(("arbitrary",)),
        name="experts",
    )(blk_e, nvalid, grp, nxt_e, xbuf, wgu, bgu, wd, bd)


def _combine_kernel(dest_ref, dest_next_ref, ybuf_ref, x1_ref, gate_ref, gt2_ref, gf_ref, o_ref, rows_ref, sems):
    i = pl.program_id(0)
    tm = x1_ref.shape[0]
    slot = i % 2

    def start_gather(idx_ref, s):
        def issue(g, carry):
            for u in range(ROW_UNROLL):
                r = g * ROW_UNROLL + u
                for kk in range(TOP_K):
                    _token_copy(ybuf_ref, idx_ref[r * TOP_K + kk], rows_ref.at[s, kk], r,
                                sems.at[s]).start(priority=kk % 2)
            return carry

        lax.fori_loop(0, tm // ROW_UNROLL, issue, 0)

    @pl.when(i == 0)
    def _():
        start_gather(dest_ref, 0)

    @pl.when(i + 1 < pl.num_programs(0))
    def _():
        start_gather(dest_next_ref, 1 - slot)

    _drain_tokens(ybuf_ref, rows_ref.at[slot, 0], sems.at[slot], tm)

    gates = gate_ref[...]
    slabs = []
    for s in range(TOKEN_TILE_ROWS):
        moe = gates[:, 0:1] * _load_token_slab(rows_ref.at[slot, 0], s, tm)
        for kk in range(1, TOP_K):
            moe = moe + gates[:, kk:kk + 1] * _load_token_slab(rows_ref.at[slot, kk], s, tm)
        slabs.append(moe)
    xo = x1_ref[...] + gt2_ref[...] * jnp.concatenate(slabs, axis=1)
    o_ref[...] = xo * lax.rsqrt(jnp.mean(xo * xo, axis=-1, keepdims=True) + RMS_EPS) * gf_ref[...]


def _combine(dest_flat, ybuf, x1, gates, gt2, g_final):
    t, d = x1.shape
    tm = ROUTE_TILE
    n = t // tm
    return pl.pallas_call(
        _combine_kernel,
        grid=(n,),
        in_specs=[pl.BlockSpec((tm * TOP_K,), lambda i: (i,), memory_space=pltpu.SMEM),
                  pl.BlockSpec((tm * TOP_K,), lambda i: (jnp.minimum(i + 1, n - 1),), memory_space=pltpu.SMEM),
                  pl.BlockSpec(memory_space=pl.ANY),
                  pl.BlockSpec((tm, d), lambda i: (i, 0)),
                  pl.BlockSpec((tm, LANES), lambda i: (i, 0)),
                  pl.BlockSpec((1, d), lambda i: (0, 0)),
                  pl.BlockSpec((1, d), lambda i: (0, 0))],
        out_specs=pl.BlockSpec((tm, d), lambda i: (i, 0)),
        out_shape=jax.ShapeDtypeStruct((t, d), F32),
        scratch_shapes=[pltpu.VMEM((2, TOP_K, tm * TOKEN_TILE_ROWS, LANES), F32), pltpu.SemaphoreType.DMA((2,))],
        compiler_params=_params(("arbitrary",)),
        name="combine",
    )(dest_flat, dest_flat, ybuf, x1, gates, gt2, g_final)


def _pad_lanes(v, lane0):
    return jnp.zeros((1, LANES), F32).at[0, lane0:lane0 + v.shape[0]].set(v.astype(F32))


def _layer(x, mod, g_norm1, w_in, conv_w, a_log, dt_bias, g_onorm, b_fgate, w_o_delta, w_o_fox, w_out,
           g_norm2, w_router, b_router, w_gate_up, b_gate_up, w_down, b_down, g_final):
    t, d = x.shape
    sh1, sc1, gt1, sh2, sc2, gt2 = [mod[:, i * d:(i + 1) * d] for i in range(N_MOD)]

    w_t = w_in.T
    o = 0
    wq = w_t[o:o + QKV_W]; o += QKV_W
    wz = w_t[o:o + MIX_W]; o += MIX_W
    w_beta = w_t[o:o + N_HEADS]; o += N_HEADS
    w_dec = w_t[o:o + N_HEADS]; o += N_HEADS
    wf = w_t[o:o + QKV_W]; o += QKV_W
    wf = jnp.concatenate([wf[:MIX_W] * (LOG2E * HEAD_DIM ** -0.5), wf[MIX_W:]], axis=0)
    w_fg = w_t[o:o + N_HEADS]; o += N_HEADS
    wga = w_t[o:o + d]; o += d
    wgb = w_t[o:o + d]
    ws = jnp.zeros((LANES, d), F32)
    ws = ws.at[L_BETA:L_BETA + N_HEADS].set(w_beta).at[L_G:L_G + N_HEADS].set(w_dec)
    ws = ws.at[L_F:L_F + N_HEADS].set(w_fg)
    bf = lambda w: w.astype(BF16)

    tm = min(ROW_TILE, t)
    qkv, z, fx, gate_a, gate_b, small = _inproj(
        x, g_norm1.reshape(1, d), sc1, sh1, bf(wq), bf(wz), bf(wf), bf(wga), bf(wgb), bf(ws), tm)

    gcol, grow = _gates(small, _pad_lanes(a_log, L_G), _pad_lanes(dt_bias, L_G), _pad_lanes(b_fgate, L_F),
                        min(GATES_TILE, t))

    o_a = _delta(qkv, z, gcol, grow, conv_w, g_onorm.reshape(1, HEAD_DIM), min(DELTA_TILE, t))
    tq, tk = min(FOX_TQ, t), min(FOX_TK, t)
    f_rows = grow[L_F:L_F + N_HEADS].reshape(N_HEADS, t // tk, 1, tk)
    o_b = _fox(fx, f_rows, tq, tk)

    wr = jnp.zeros((d, LANES), F32).at[:, :N_EXPERTS].set(w_router)
    br = _pad_lanes(b_router, 0)
    x1, h2, ri, gates, cnt = _merge(o_a, o_b, gate_a, gate_b, x, bf(w_o_delta), bf(w_o_fox), bf(w_out), gt1,
                                    g_norm2.reshape(1, d), sc2, sh2, wr, br, tm)

    n_blocks = (t * TOP_K) // EXPERT_BLOCK + N_EXPERTS
    dest, blk, _ = _dest(ri, cnt, n_blocks, min(2048, t))
    dest_flat = dest[:, :TOP_K].reshape(t * TOP_K)
    blk_e, nvalid = blk[:n_blocks, 0], blk[:n_blocks, 1]

    xbuf = _dispatch(dest_flat, h2, n_blocks * EXPERT_BLOCK)
    ybuf = _experts(blk_e, nvalid, xbuf, w_gate_up, b_gate_up.reshape(N_EXPERTS, 1, 2 * FF),
                    w_down, b_down.reshape(N_EXPERTS, 1, d))
    return _combine(dest_flat, ybuf, x1, gates, gt2, g_final.reshape(1, d))


def kernel(x, c, w_ada, b_ada, g_norm1, w_in, conv_w, a_log, dt_bias, g_onorm, b_fgate, w_o_delta, w_o_fox, w_out,
           g_norm2, w_router, b_router, w_gate_up, b_gate_up, w_down, b_down, g_final):
    b, s, d = x.shape
    assert b == 1 and d == D_MODEL and w_ada.shape[0] == 1
    h = x[0]
    for l in range(w_ada.shape[0]):
        mod = _modulation(c[0], w_ada[l], b_ada[l])
        h = _layer(h, mod, g_norm1[l], w_in[l], conv_w[l], a_log[l], dt_bias[l], g_onorm[l], b_fgate[l],
                   w_o_delta[l], w_o_fox[l], w_out[l], g_norm2[l], w_router[l], b_router[l], w_gate_up[l],
                   b_gate_up[l], w_down[l], b_down[l], g_final)
    return h[None]
```

```python
import functools

import jax
import jax.numpy as jnp
from jax import lax
from jax.experimental import pallas as pl
from jax.experimental.pallas import tpu as pltpu

F32 = jnp.float32
BF16 = jnp.bfloat16
I32 = jnp.int32

D_MODEL = 1024
HEAD_DIM = 128
N_HEADS = 4
MIX_W = N_HEADS * HEAD_DIM
QKV_W = 3 * MIX_W
CONV_K = 4
CHUNK = 64
STACK = N_HEADS * CHUNK
SUB = 16
N_EXPERTS = 32
TOP_K = 4
FF = D_MODEL
SWIGLU_LIMIT = 7.0
SWIGLU_ALPHA = 1.702
RMS_EPS = 1e-6
L2_EPS = 1e-6
N_MOD = 6
LANES = 128
NEG_BIG = -1e30
LOG2E = 1.4426950408889634

ROW_TILE = 512
GATES_TILE = 256
DELTA_TILE = 256
EXPERT_BLOCK = 512
ROUTE_TILE = 256
VMEM_LIMIT = 56 * 1024 * 1024


def _params(sem, vmem=VMEM_LIMIT, flags=None):
    return pltpu.CompilerParams(dimension_semantics=sem, vmem_limit_bytes=vmem, flags=flags)


def _softplus(x):
    return jnp.maximum(x, 0.0) + jnp.log(1.0 + jnp.exp(-jnp.abs(x)))


def _sigmoid(x):
    return 1.0 / (1.0 + jnp.exp(-x))


def _silu(x):
    return x * _sigmoid(x)


def _dot(a, b, precision=None):
    return jnp.dot(a, b, preferred_element_type=F32, precision=precision)


def _dot_nt(a, b, precision=None):
    return lax.dot_general(a, b, (((1,), (1,)), ((), ())), preferred_element_type=F32, precision=precision)


def _split_bf16(x, terms):
    parts = []
    for _ in range(terms):
        p = x.astype(BF16)
        parts.append(p)
        x = x - p.astype(F32)
    return parts


def _dot_split(a, b):
    a_hi, a_lo = _split_bf16(a, 2)
    b_hi, b_lo = _split_bf16(b, 2)
    return _dot(a_hi, b_hi) + (_dot(a_hi, b_lo) + _dot(a_lo, b_hi))


def _dot_mask(mask01, x):
    return sum(_dot(mask01, p) for p in _split_bf16(x, 3))


TOKEN_TILE_ROWS = D_MODEL // LANES


def _store_token_tiles(ref, x):
    n = x.shape[0]
    for s in range(TOKEN_TILE_ROWS):
        ref[pl.ds(s, n, stride=TOKEN_TILE_ROWS), :] = x[:, s * LANES:(s + 1) * LANES]


def _load_token_slab(ref, s, n):
    return ref[pl.ds(s, n, stride=TOKEN_TILE_ROWS), :]


def _token_copy(src_ref, src_tok, dst_ref, dst_tok, sem):
    rows = lambda i: pl.ds(pl.multiple_of(i * TOKEN_TILE_ROWS, TOKEN_TILE_ROWS), TOKEN_TILE_ROWS)
    return pltpu.make_async_copy(src_ref.at[rows(src_tok), :], dst_ref.at[rows(dst_tok), :], sem)


def _mod_kernel(c_ref, w_ref, b_ref, o_ref):
    o_ref[...] = jnp.sum(c_ref[...] * w_ref[...], axis=0, keepdims=True) + b_ref[...]


def _modulation(c, w_ada, b_ada):
    d, n = w_ada.shape
    tn = 1024
    return pl.pallas_call(
        _mod_kernel,
        grid=(n // tn,),
        in_specs=[pl.BlockSpec((d, 1), lambda j: (0, 0)),
                  pl.BlockSpec((d, tn), lambda j: (0, j)),
                  pl.BlockSpec((1, tn), lambda j: (0, j))],
        out_specs=pl.BlockSpec((1, tn), lambda j: (0, j)),
        out_shape=jax.ShapeDtypeStruct((1, n), F32),
        compiler_params=_params(("arbitrary",)),
        name="mod",
    )(c.reshape(d, 1), w_ada, b_ada.reshape(1, n))


def _inproj_kernel(x_ref, g_ref, sc_ref, sh_ref, wq_ref, wz_ref, wf_ref, wga_ref, wgb_ref, ws_ref,
                   oq_ref, oz_ref, of_ref, oga_ref, ogb_ref, os_ref):
    x = x_ref[...]
    y = x * lax.rsqrt(jnp.mean(x * x, axis=-1, keepdims=True) + RMS_EPS) * g_ref[...]
    h = (y * (1.0 + sc_ref[...]) + sh_ref[...]).astype(BF16)
    oq_ref[...] = _dot_nt(h, wq_ref[...])
    oz_ref[...] = _dot_nt(h, wz_ref[...]).astype(BF16)
    of_ref[...] = _dot_nt(h, wf_ref[...]).astype(BF16)
    oga_ref[...] = _dot_nt(h, wga_ref[...]).astype(BF16)
    ogb_ref[...] = _dot_nt(h, wgb_ref[...]).astype(BF16)
    os_ref[...] = _dot_nt(h, ws_ref[...])


def _inproj(x, g1, sc1, sh1, wq, wz, wf, wga, wgb, ws, tm):
    t, d = x.shape
    row = lambda i: (i, 0)
    fixed = lambda i: (0, 0)
    ws_list = [wq, wz, wf, wga, wgb, ws]
    out_dt = [F32, BF16, BF16, BF16, BF16, F32]
    return pl.pallas_call(
        _inproj_kernel,
        grid=(t // tm,),
        in_specs=[pl.BlockSpec((tm, d), row)] + [pl.BlockSpec((1, d), fixed)] * 3
                 + [pl.BlockSpec(w.shape, fixed) for w in ws_list],
        out_specs=[pl.BlockSpec((tm, w.shape[0]), row) for w in ws_list],
        out_shape=[jax.ShapeDtypeStruct((t, w.shape[0]), dt) for w, dt in zip(ws_list, out_dt)],
        compiler_params=_params(("arbitrary",)),
        name="inproj",
    )(x, g1, sc1, sh1, *ws_list)


L_BETA, L_G, L_F, L_GC, L_GL = 0, 4, 8, 12, 16


def _gates_kernel(s_ref, alog_ref, dtb_ref, bf_ref, col_ref, row_ref, carry_ref):
    i = pl.program_id(0)
    tm = s_ref.shape[0]

    @pl.when(i == 0)
    def _():
        carry_ref[...] = jnp.zeros_like(carry_ref)

    s = s_ref[...]
    lane = lax.broadcasted_iota(I32, s.shape, 1)
    beta = _sigmoid(s)
    g = -jnp.exp(alog_ref[...]) * _softplus(s + dtb_ref[...])
    logf = -_softplus(-(s + bf_ref[...]))
    is_g = (lane >= L_G) & (lane < L_G + N_HEADS)
    is_f = (lane >= L_F) & (lane < L_F + N_HEADS)
    g = jnp.where(is_g, g, 0.0)
    logf = jnp.where(is_f, logf, 0.0)

    r = lax.broadcasted_iota(I32, (tm, tm), 0)
    c = lax.broadcasted_iota(I32, (tm, tm), 1)
    same_chunk = (r // CHUNK) == (c // CHUNK)
    tri = jnp.where(r >= c, 1.0, 0.0).astype(BF16)
    tri_chunk = jnp.where(same_chunk & (r >= c), 1.0, 0.0).astype(BF16)
    ones_chunk = jnp.where(same_chunk, 1.0, 0.0).astype(BF16)
    f_cum = _dot_mask(tri, logf) + carry_ref[...]
    carry_ref[...] = f_cum[tm - 1:tm, :]
    gc = _dot_mask(tri_chunk, g)
    gl = _dot_mask(ones_chunk, g)

    out = jnp.where(lane < N_HEADS, beta, 0.0) + g + f_cum
    out = out + pltpu.roll(gc, L_GC - L_G, 1) + pltpu.roll(gl, L_GL - L_G, 1)
    col_ref[...] = out
    row_ref[...] = out.T[:row_ref.shape[0], :]


def _gates(small, alog_row, dtb_row, bf_row, tm):
    t = small.shape[0]
    n_rows = 24
    return pl.pallas_call(
        _gates_kernel,
        grid=(t // tm,),
        in_specs=[pl.BlockSpec((tm, LANES), lambda i: (i, 0))] + [pl.BlockSpec((1, LANES), lambda i: (0, 0))] * 3,
        out_specs=[pl.BlockSpec((tm, LANES), lambda i: (i, 0)), pl.BlockSpec((n_rows, tm), lambda i: (0, i))],
        out_shape=[jax.ShapeDtypeStruct((t, LANES), F32), jax.ShapeDtypeStruct((n_rows, t), F32)],
        scratch_shapes=[pltpu.VMEM((1, LANES), F32)],
        compiler_params=_params(("arbitrary",)),
        name="gates",
    )(small, alog_row, dtb_row, bf_row)


def _stack_heads(a, col0):
    return jnp.concatenate([a[:, col0 + h * HEAD_DIM: col0 + (h + 1) * HEAD_DIM] for h in range(N_HEADS)], axis=0)


def _stack_cols(a, lane0):
    return jnp.concatenate([a[:, lane0 + h: lane0 + h + 1] for h in range(N_HEADS)], axis=0)


def _delta_kernel(qkv_ref, z_ref, gcol_ref, grow_ref, cw_ref, gon_ref, o_ref, ext_ref, tail_ref, s_ref):
    i = pl.program_id(0)
    tb = qkv_ref.shape[0]
    pad = tail_ref.shape[0]

    @pl.when(i == 0)
    def _():
        tail_ref[...] = jnp.zeros_like(tail_ref)
        s_ref[...] = jnp.zeros_like(s_ref)

    ext_ref[0:pad, :] = tail_ref[...]
    ext_ref[pad:pad + tb, :] = qkv_ref[...]
    tail_ref[...] = qkv_ref[tb - pad:tb, :]
    conv = cw_ref[0:1, :] * ext_ref[pad - 3:pad - 3 + tb, :]
    for j in range(1, CONV_K):
        conv = conv + cw_ref[j:j + 1, :] * ext_ref[pad - 3 + j:pad - 3 + j + tb, :]
    act = _silu(conv)

    r = lax.broadcasted_iota(I32, (STACK, STACK), 0)
    c = lax.broadcasted_iota(I32, (STACK, STACK), 1)
    same = (r // CHUNK) == (c // CHUNK)
    m_incl = same & (r >= c)
    m_strict = same & (r > c)
    eye = jnp.where(r == c, 1.0, 0.0)
    m_sub = []
    size = SUB
    while size <= CHUNK:
        m_sub.append((r // size) == (c // size))
        size *= 2
    rb = lax.broadcasted_iota(I32, (STACK, N_HEADS * HEAD_DIM), 0) // CHUNK
    cb = lax.broadcasted_iota(I32, (STACK, N_HEADS * HEAD_DIM), 1) // HEAD_DIM
    head_match = rb == cb

    chunks = range(tb // CHUNK)
    pre = []
    for ch in chunks:
        r0 = ch * CHUNK
        a = act[r0:r0 + CHUNK, :]
        q = _stack_heads(a, 0)
        k = _stack_heads(a, MIX_W)
        v = _stack_heads(a, 2 * MIX_W)
        q = q * lax.rsqrt(jnp.sum(q * q, axis=-1, keepdims=True) + L2_EPS) * (HEAD_DIM ** -0.5)
        k = k * lax.rsqrt(jnp.sum(k * k, axis=-1, keepdims=True) + L2_EPS)

        gcols = gcol_ref[r0:r0 + CHUNK, :]
        beta = _stack_cols(gcols, L_BETA)
        gc = _stack_cols(gcols, L_GC)
        gc_row = jnp.concatenate(
            [grow_ref[L_GC + h:L_GC + h + 1, r0:r0 + CHUNK] for h in range(N_HEADS)], axis=1)
        gl_row = jnp.concatenate(
            [grow_ref[L_GL + h:L_GL + h + 1, r0:r0 + CHUNK] for h in range(N_HEADS)], axis=1)

        decay = jnp.exp(jnp.where(m_incl, gc - gc_row, NEG_BIG))
        kb16 = k.astype(BF16)
        kk = _dot_nt(kb16, kb16)
        lmat = jnp.where(m_strict, kk * beta * decay, 0.0)
        attn = (_dot_nt(q.astype(BF16), kb16) * decay).astype(BF16)
        egc = jnp.exp(gc)
        rhs = jnp.concatenate([v * beta, k * (beta * egc)], axis=1).astype(BF16)
        kt_dec = (k.T * jnp.exp(gl_row - gc_row)).astype(BF16)
        pre.append(dict(lmat=lmat, attn=attn, rhs=rhs, kt_dec=kt_dec, q_dec=q * egc))

    l_d = [jnp.where(m_sub[0], p["lmat"], 0.0) for p in pre]
    inv = [eye - l for l in l_d]
    l16 = [l.astype(BF16) for l in l_d]
    pw = [_dot(l, l).astype(BF16) for l in l16]
    n_sq = SUB.bit_length() - 3
    for s in range(n_sq + 1):
        inv = [x + _dot(x.astype(BF16), p) for x, p in zip(inv, pw)]
        if s < n_sq:
            pw = [_dot(p, p).astype(BF16) for p in pw]
    for lvl in range(1, len(m_sub)):
        m_off = m_sub[lvl] & ~m_sub[lvl - 1]
        off = [jnp.where(m_off, p["lmat"], 0.0).astype(BF16) for p in pre]
        inv16 = [x.astype(BF16) for x in inv]
        half = [_dot(x, o).astype(BF16) for x, o in zip(inv16, off)]
        inv = [x - _dot(h, x16) for x, h, x16 in zip(inv, half, inv16)]
    uws = [_dot(x.astype(BF16), p["rhs"]) for x, p in zip(inv, pre)]

    for ch in chunks:
        r0 = ch * CHUNK
        attn, kt_dec = pre[ch]["attn"], pre[ch]["kt_dec"]
        u, w = uws[ch][:, :HEAD_DIM], uws[ch][:, HEAD_DIM:]

        state16 = s_ref[...].astype(BF16)
        wq_s = _dot(jnp.concatenate([w, pre[ch]["q_dec"]], axis=0).astype(BF16), state16)
        ws_d = jnp.concatenate([wq_s[h * CHUNK:(h + 1) * CHUNK, h * HEAD_DIM:(h + 1) * HEAD_DIM]
                                for h in range(N_HEADS)], axis=0)
        qs_d = jnp.concatenate([wq_s[STACK + h * CHUNK:STACK + (h + 1) * CHUNK, h * HEAD_DIM:(h + 1) * HEAD_DIM]
                                for h in range(N_HEADS)], axis=0)
        v_new = u - ws_d
        v16 = v_new.astype(BF16)
        o = qs_d + _dot(attn, v16)

        v_bd =jnp.where(head_match, jnp.concatenate([v_new] * N_HEADS, axis=1), 0.0).astype(BF16)
        upd = _dot(kt_dec, v_bd)
        for h in range(N_HEADS):
            e_h = jnp.exp(grow_ref[L_GL + h:L_GL + h + 1, r0:r0 + 1])
            sl = slice(h * HEAD_DIM, (h + 1) * HEAD_DIM)
            s_ref[:, sl] = s_ref[:, sl] * e_h + upd[:, sl]

        for h in range(N_HEADS):
            oh = o[h * CHUNK:(h + 1) * CHUNK, :]
            oh = oh * lax.rsqrt(jnp.mean(oh * oh, axis=-1, keepdims=True) + RMS_EPS) * gon_ref[...]
            zh = z_ref[r0:r0 + CHUNK, h * HEAD_DIM:(h + 1) * HEAD_DIM].astype(F32)
            o_ref[r0:r0 + CHUNK, h * HEAD_DIM:(h + 1) * HEAD_DIM] = (oh * _silu(zh)).astype(o_ref.dtype)


def _delta(qkv, z, gcol, grow, conv_w, g_onorm, tb):
    t = qkv.shape[0]
    pad = 8
    return pl.pallas_call(
        _delta_kernel,
        grid=(t // tb,),
        in_specs=[pl.BlockSpec((tb, QKV_W), lambda i: (i, 0)),
                  pl.BlockSpec((tb, MIX_W), lambda i: (i, 0)),
                  pl.BlockSpec((tb, LANES), lambda i: (i, 0)),
                  pl.BlockSpec((grow.shape[0], tb), lambda i: (0, i)),
                  pl.BlockSpec((CONV_K, QKV_W), lambda i: (0, 0)),
                  pl.BlockSpec((1, HEAD_DIM), lambda i: (0, 0))],
        out_specs=pl.BlockSpec((tb, MIX_W), lambda i: (i, 0)),
        out_shape=jax.ShapeDtypeStruct((t, MIX_W), BF16),
        scratch_shapes=[pltpu.VMEM((tb + pad, QKV_W), F32),
                        pltpu.VMEM((pad, QKV_W), F32),
                        pltpu.VMEM((HEAD_DIM, N_HEADS * HEAD_DIM), F32)],
        compiler_params=_params(("arbitrary",)),
        name="delta",
    )(qkv, z, gcol, grow, conv_w, g_onorm)


FOX_GROUP = 4
FOX_TQ, FOX_TK = 256, 2048
FOX_DIAG_PARTS = 2
FOX_ROW_STREAMS = 2
FOX_SKIP_LOG2 = -200.0


def _fox_kernel(*refs, tk):
    g = FOX_GROUP
    q_ref, o_ref, kmax_ref = refs[0], refs[-2], refs[-1]
    k_refs, v_refs, f_refs = (refs[1 + i * g:1 + (i + 1) * g] for i in range(3))
    qi = pl.program_id(1)
    tq = q_ref.shape[0]
    t_keys = k_refs[0].shape[0]

    @pl.when(qi == 0)
    def _():
        rows = min(t_keys, 2048)
        for h in range(g):
            def norm_max(c, best, h=h):
                kc = k_refs[h][pl.ds(pl.multiple_of(c * rows, rows), rows), :].astype(F32)
                return jnp.maximum(best, jnp.sum(kc * kc, axis=-1, keepdims=True))

            best = lax.fori_loop(0, t_keys // rows, norm_max, jnp.zeros((rows, 1), F32))
            kmax_ref[h:h + 1, :] = jnp.broadcast_to(jnp.sqrt(jnp.max(best, axis=0, keepdims=True)), (1, LANES))

    def step(off, width, f_blk, carry, mask):
        ones = jnp.ones((width, HEAD_DIM), BF16)
        ss = []
        for h in range(g):
            q = q_ref[:, h * HEAD_DIM:(h + 1) * HEAD_DIM]
            s = _dot_nt(q, k_refs[h][pl.ds(off, width), :]) - f_blk(h) * LOG2E
            if mask is not None:
                s = jnp.where(mask, s, NEG_BIG)
            ss.append(s)
        ps, ms, alphas = [], [], []
        for h in range(g):
            m = carry[h][0]
            m_new = jnp.maximum(m, jnp.max(ss[h], axis=-1, keepdims=True))
            alphas.append(jnp.exp2(m - m_new))
            ps.append(jnp.exp2(ss[h] - m_new).astype(BF16))
            ms.append(m_new)
        out = []
        for h in range(g):
            v_aug = jnp.concatenate([v_refs[h][pl.ds(off, width), :], ones], axis=1)
            out.append((ms[h], alphas[h] * carry[h][1] + _dot(ps[h], v_aug)))
        return tuple(out)

    def head_chunk(h, j, mc):
        m, acc = mc
        off = pl.multiple_of(j * tk, tk)
        k_blk = k_refs[h][pl.ds(off, tk), :]
        v_aug = jnp.concatenate([v_refs[h][pl.ds(off, tk), :], jnp.ones((tk, HEAD_DIM), BF16)], axis=1)
        f_blk = f_refs[h][j] * LOG2E
        part = tq // FOX_ROW_STREAMS
        rows = [slice(i * part, (i + 1) * part) for i in range(FOX_ROW_STREAMS)]
        ss = [_dot_nt(q_ref[r, h * HEAD_DIM:(h + 1) * HEAD_DIM], k_blk) - f_blk for r in rows]
        m_new = [jnp.maximum(m[r], jnp.max(s, axis=-1, keepdims=True)) for r, s in zip(rows, ss)]
        ps = [jnp.exp2(s - mn).astype(BF16) for s, mn in zip(ss, m_new)]
        accs = [jnp.exp2(m[r] - mn) * acc[r] + _dot(p, v_aug) for r, mn, p in zip(rows, m_new, ps)]
        return jnp.concatenate(m_new, axis=0), jnp.concatenate(accs, axis=0)

    init = tuple((jnp.full((tq, 1), NEG_BIG, F32), jnp.zeros((tq, 2 * HEAD_DIM), F32)) for _ in range(g))
    n_full = (qi * tq) // tk

    parts = max(1, min(FOX_DIAG_PARTS, tk // tq))
    width = tk // parts
    row = lax.broadcasted_iota(I32, (tq, width), 0)
    col = lax.broadcasted_iota(I32, (tq, width), 1)
    carry = init
    for d in reversed(range(max(tq // tk, 1))):
        j = n_full + d
        for p in reversed(range(parts)):
            off = pl.multiple_of(j * tk + p * width, width)
            piece = functools.partial(step, off, width,
                                      lambda h, j=j, p=p: f_refs[h][j][:, p * width:(p + 1) * width],
                                      mask=col + (off - qi * tq) <= row)
            carry = piece(carry) if p == 0 else lax.cond(off < (qi + 1) * tq, piece, lambda c: c, carry)

    qn = []
    for h in range(g):
        q = q_ref[:, h * HEAD_DIM:(h + 1) * HEAD_DIM].astype(F32)
        qn.append(jnp.sqrt(jnp.sum(q * q, axis=-1, keepdims=True)) * kmax_ref[h:h + 1, 0:1] * 1.01 + 1.0)

    def reaches(j, carry):
        jc = jnp.maximum(j, 0)
        flags = []
        for h in range(g):
            bound = qn[h] - f_refs[h][jc][:, tk - 1:tk] * LOG2E
            flags.append((jnp.max(bound - carry[h][0]) >= FOX_SKIP_LOG2).astype(I32))
        return tuple(flags)

    def chunk_all_heads(state):
        j, _, carry = state
        carry = step(pl.multiple_of(j * tk, tk), tk, lambda h: f_refs[h][j], carry, None)
        return j - 1, reaches(j - 1, carry), carry

    def chunk_live_heads(state):
        j, live, carry = state
        carry = tuple(lax.cond(live[h] > 0, functools.partial(head_chunk, h, j), lambda mc: mc, carry[h])
                      for h in range(g))
        return j - 1, reaches(j - 1, carry), carry

    state = (n_full - 1, reaches(n_full - 1, carry), carry)
    state = lax.while_loop(lambda st: (st[0] >= 0) & (sum(st[1]) == g), chunk_all_heads, state)
    _, _, carry = lax.while_loop(lambda st: (st[0] >= 0) & (sum(st[1]) > 0), chunk_live_heads, state)
    for h in range(g):
        acc = carry[h][1]
        o_ref[:, h * HEAD_DIM:(h + 1) * HEAD_DIM] = (acc[:, :HEAD_DIM] / acc[:, HEAD_DIM:]).astype(o_ref.dtype)


def _fox(fx, f_rows, tq, tk):
    t = fx.shape[0]
    g = FOX_GROUP
    heads = lambda hg, u: hg * g + u
    q_specs = [pl.BlockSpec((tq, g * HEAD_DIM), lambda hg, i: (i, hg))]
    k_specs = [pl.BlockSpec((t, HEAD_DIM), lambda hg, i, u=u: (0, N_HEADS + heads(hg, u)),
                            pipeline_mode=pl.Buffered(1)) for u in range(g)]
    v_specs = [pl.BlockSpec((t, HEAD_DIM), lambda hg, i, u=u: (0, 2 * N_HEADS + heads(hg, u)),
                            pipeline_mode=pl.Buffered(1)) for u in range(g)]
    f_specs = [pl.BlockSpec((None, t // tk, 1, tk), lambda hg, i, u=u: (heads(hg, u), 0, 0, 0)) for u in range(g)]
    return pl.pallas_call(
        functools.partial(_fox_kernel, tk=tk),
        grid=(N_HEADS // g, t // tq),
        in_specs=q_specs + k_specs + v_specs + f_specs,
        out_specs=pl.BlockSpec((tq, g * HEAD_DIM), lambda hg, i: (i, hg)),
        out_shape=jax.ShapeDtypeStruct((t, MIX_W), BF16),
        scratch_shapes=[pltpu.VMEM((8, LANES), F32)],
        compiler_params=_params(("arbitrary", "arbitrary")),
        name="fox",
    )(*([fx] * (1 + 2 * g) + [f_rows] * g))


def _merge_kernel(oa_ref, ob_ref, ga_ref, gb_ref, x_ref, woa_ref, wob_ref, wout_ref, gt1_ref, g2_ref, sc2_ref,
                  sh2_ref, wr_ref, br_ref, x1_ref, h2_ref, ri_ref, gate_ref, cnt_ref, carry_ref):
    i = pl.program_id(0)
    tm = x_ref.shape[0]

    @pl.when(i == 0)
    def _():
        carry_ref[...] = jnp.zeros_like(carry_ref)

    ya = _dot(oa_ref[...], woa_ref[...])
    yb = _dot(ob_ref[...], wob_ref[...])
    merged = _sigmoid(ga_ref[...].astype(F32)) * ya + _sigmoid(gb_ref[...].astype(F32)) * yb
    x1 = x_ref[...] + gt1_ref[...] * _dot(merged.astype(BF16), wout_ref[...])
    x1_ref[...] = x1
    y = x1 * lax.rsqrt(jnp.mean(x1 * x1, axis=-1, keepdims=True) + RMS_EPS) * g2_ref[...]
    h2 = y * (1.0 + sc2_ref[...]) + sh2_ref[...]
    _store_token_tiles(h2_ref, h2)

    logits = _dot_split(h2, wr_ref[...]) + br_ref[...]
    lane = lax.broadcasted_iota(I32, logits.shape, 1)
    lanef = lane.astype(F32)
    cur = jnp.where(lane < N_EXPERTS, logits, -jnp.inf)
    vals, idxs = [], []
    for _ in range(TOP_K):
        m = jnp.max(cur, axis=-1, keepdims=True)
        ix = jnp.min(jnp.where(cur == m, lanef, float(LANES)), axis=-1, keepdims=True)
        vals.append(m)
        idxs.append(ix)
        cur = jnp.where(lanef == ix, -jnp.inf, cur)
    exps = [jnp.exp(v - vals[0]) for v in vals]
    denom = exps[0] + exps[1] + exps[2] + exps[3]

    onehot = jnp.zeros(logits.shape, F32)
    for ix in idxs:
        onehot = onehot + jnp.where(lanef == ix, 1.0, 0.0)
    r = lax.broadcasted_iota(I32, (tm, tm), 0)
    c = lax.broadcasted_iota(I32, (tm, tm), 1)
    strict = jnp.where(r > c, 1.0, 0.0).astype(BF16)
    before = _dot(strict, onehot.astype(BF16)) + carry_ref[...]
    carry_ref[...] = carry_ref[...] + jnp.sum(onehot, axis=0, keepdims=True)
    cnt_ref[...] = carry_ref[...].astype(I32)

    ri = jnp.zeros(logits.shape, F32)
    gates = jnp.zeros(logits.shape, F32)
    for kk in range(TOP_K):
        rank = jnp.sum(jnp.where(lanef == idxs[kk], before, 0.0), axis=-1, keepdims=True)
        ri = ri + jnp.where(lane == kk, idxs[kk], 0.0) + jnp.where(lane == TOP_K + kk, rank, 0.0)
        gates = gates + jnp.where(lane == kk, exps[kk] / denom, 0.0)
    ri_ref[...] = ri.astype(I32)
    gate_ref[...] = gates


def _merge(o_a, o_b, gate_a, gate_b, x, woa, wob, wout, gt1, g2, sc2, sh2, wr, br, tm):
    t, d = x.shape
    row = lambda i: (i, 0)
    fixed = lambda i: (0, 0)
    return pl.pallas_call(
        _merge_kernel,
        grid=(t // tm,),
        in_specs=[pl.BlockSpec((tm, MIX_W), row), pl.BlockSpec((tm, MIX_W), row),
                  pl.BlockSpec((tm, d), row), pl.BlockSpec((tm, d), row), pl.BlockSpec((tm, d), row),
                  pl.BlockSpec((MIX_W, d), fixed), pl.BlockSpec((MIX_W, d), fixed), pl.BlockSpec((d, d), fixed),
                  pl.BlockSpec((1, d), fixed), pl.BlockSpec((1, d), fixed), pl.BlockSpec((1, d), fixed),
                  pl.BlockSpec((1, d), fixed), pl.BlockSpec((d, LANES), fixed), pl.BlockSpec((1, LANES), fixed)],
        out_specs=[pl.BlockSpec((tm, d), row), pl.BlockSpec((tm * TOKEN_TILE_ROWS, LANES), row),
                   pl.BlockSpec((tm, LANES), row),
                   pl.BlockSpec((tm, LANES), row), pl.BlockSpec((1, LANES), fixed)],
        out_shape=[jax.ShapeDtypeStruct((t, d), F32), jax.ShapeDtypeStruct((t * TOKEN_TILE_ROWS, LANES), F32),
                   jax.ShapeDtypeStruct((t, LANES), I32), jax.ShapeDtypeStruct((t, LANES), F32),
                   jax.ShapeDtypeStruct((1, LANES), I32)],
        scratch_shapes=[pltpu.VMEM((1, LANES), F32)],
        compiler_params=_params(("arbitrary",)),
        name="merge",
    )(o_a, o_b, gate_a, gate_b, x, woa, wob, wout, gt1, g2, sc2, sh2, wr, br)


def _dest_kernel(ri_ref, cnt_ref, dest_ref, blk_ref, pend_ref):
    shift = EXPERT_BLOCK.bit_length() - 1
    cnt = jnp.broadcast_to(cnt_ref[...], (8, LANES))
    lane_row = lax.broadcasted_iota(I32, cnt.shape, 1)
    padded = jnp.where(lane_row < N_EXPERTS, ((cnt + (EXPERT_BLOCK - 1)) >> shift) << shift, 0)
    pend = padded
    s = 1
    while s < N_EXPERTS:
        pend = pend + jnp.where(lane_row >= s, pltpu.roll(pend, s, 1), 0)
        s *= 2
    pstart = (pend - padded)[0:1, :].astype(F32)
    pend = pend[0:1, :]
    pend_ref[...] = pend

    ri = ri_ref[...]
    lane = lax.broadcasted_iota(I32, ri.shape, 1)
    dest = jnp.zeros(ri.shape, F32)
    for kk in range(TOP_K):
        ix = ri[:, kk:kk + 1]
        rank = ri[:, TOP_K + kk:TOP_K + kk + 1].astype(F32)
        start = jnp.sum(jnp.where(lane == ix, pstart, 0.0), axis=-1, keepdims=True)
        dest = dest + jnp.where(lane == kk, start + rank, 0.0)
    dest_ref[...] = dest.astype(I32)

    bstart = lax.broadcasted_iota(I32, blk_ref.shape, 0) * EXPERT_BLOCK
    lane_b = lax.broadcasted_iota(I32, blk_ref.shape, 1)
    ended = jnp.where((lane_b < N_EXPERTS) & (pend <= bstart), 1.0, 0.0)
    e = jnp.minimum(jnp.sum(ended, axis=-1, keepdims=True), float(N_EXPERTS - 1))
    vend = pstart + cnt[0:1, :].astype(F32)
    vend_e = jnp.sum(jnp.where(lane_b.astype(F32) == e, vend, 0.0), axis=-1, keepdims=True)
    nvalid = jnp.clip(vend_e - bstart[:, 0:1].astype(F32), 0.0, float(EXPERT_BLOCK))
    blk_ref[...] = jnp.where(lane_b == 0, e, jnp.where(lane_b == 1, nvalid, 0.0)).astype(I32)


def _dest(ri, cnt, n_blocks, tm):
    t = ri.shape[0]
    nb_pad = -(-n_blocks // 8) * 8
    return pl.pallas_call(
        _dest_kernel,
        grid=(t // tm,),
        in_specs=[pl.BlockSpec((tm, LANES), lambda i: (i, 0)), pl.BlockSpec((1, LANES), lambda i: (0, 0))],
        out_specs=[pl.BlockSpec((tm, LANES), lambda i: (i, 0)), pl.BlockSpec((nb_pad, LANES), lambda i: (0, 0)),
                   pl.BlockSpec((1, LANES), lambda i: (0, 0))],
        out_shape=[jax.ShapeDtypeStruct((t, LANES), I32), jax.ShapeDtypeStruct((nb_pad, LANES), I32),
                   jax.ShapeDtypeStruct((1, LANES), I32)],
        compiler_params=_params(("arbitrary",)),
        name="dest",
    )(ri, cnt)


ROW_UNROLL = 8


def _drain_tokens(src_ref, dst_ref, sem, n_tokens):
    def drain(_, carry):
        for _ in range(ROW_UNROLL * TOP_K):
            _token_copy(src_ref, 0, dst_ref, 0, sem).wait()
        return carry

    lax.fori_loop(0, n_tokens // ROW_UNROLL, drain, 0)


def _dispatch_kernel(dest_ref, h_ref, xbuf_ref, sem):
    tm = h_ref.shape[0] // TOKEN_TILE_ROWS

    def issue(g, carry):
        for u in range(ROW_UNROLL):
            r = g * ROW_UNROLL + u
            for kk in range(TOP_K):
                _token_copy(h_ref, r, xbuf_ref, dest_ref[r * TOP_K + kk], sem).start(priority=kk % 2)
        return carry

    lax.fori_loop(0, tm // ROW_UNROLL, issue, 0)
    _drain_tokens(h_ref, xbuf_ref, sem, tm)


def _dispatch(dest_flat, h2_tiles, n_rows):
    tm = ROUTE_TILE
    t = h2_tiles.shape[0] // TOKEN_TILE_ROWS
    return pl.pallas_call(
        _dispatch_kernel,
        grid=(t // tm,),
        in_specs=[pl.BlockSpec((tm * TOP_K,), lambda i: (i,), memory_space=pltpu.SMEM),
                  pl.BlockSpec((tm * TOKEN_TILE_ROWS, LANES), lambda i: (i, 0))],
        out_specs=pl.BlockSpec(memory_space=pl.ANY),
        out_shape=jax.ShapeDtypeStruct((n_rows * TOKEN_TILE_ROWS, LANES), h2_tiles.dtype),
        scratch_shapes=[pltpu.SemaphoreType.DMA(())],
        compiler_params=_params(("arbitrary",)),
        name="dispatch",
    )(dest_flat, h2_tiles)


def _experts_kernel(blk_e_ref, nvalid_ref, grp_ref, nxt_e_ref, x_ref, wgu_hbm, bgu_ref, wd_hbm, bd_ref, y_ref,
                    wgu32_ref, wd32_ref, wgu16_ref, wd16_ref, sems):
    b = pl.program_id(0)
    nvalid = nvalid_ref[b]
    e = blk_e_ref[b]
    slot = grp_ref[b] % 2
    first = (b == 0) | (e != blk_e_ref[jnp.maximum(b - 1, 0)])

    def weight_copies(expert, s):
        return (pltpu.make_async_copy(wgu_hbm.at[expert], wgu32_ref.at[s], sems.at[s, 0]),
                pltpu.make_async_copy(wd_hbm.at[expert], wd32_ref.at[s], sems.at[s, 1]))

    @pl.when(b == 0)
    def _():
        for cp in weight_copies(e, 0):
            cp.start()

    @pl.when(first)
    def _():
        for cp in weight_copies(e, slot):
            cp.wait()
        wgu16_ref[...] = wgu32_ref[slot].astype(BF16)
        wd16_ref[...] = wd32_ref[slot].astype(BF16)

    @pl.when(first & (nxt_e_ref[b] != e))
    def _():
        for cp in weight_copies(nxt_e_ref[b], 1 - slot):
            cp.start()

    @pl.when(nvalid > 0)
    def _():
        n = x_ref.shape[0] // TOKEN_TILE_ROWS
        x = jnp.concatenate([_load_token_slab(x_ref, s, n) for s in range(TOKEN_TILE_ROWS)], axis=1)
        rows = lax.broadcasted_iota(I32, x.shape, 0)
        x = jnp.where(rows < nvalid, x, 0.0).astype(BF16)
        gu = _dot(x, wgu16_ref[...]) + bgu_ref[...]
        gate = jnp.minimum(gu[:, :FF], SWIGLU_LIMIT)
        up = jnp.clip(gu[:, FF:], -SWIGLU_LIMIT, SWIGLU_LIMIT)
        act = (up + 1.0) * (gate * _sigmoid(SWIGLU_ALPHA * gate))
        _store_token_tiles(y_ref, _dot(act.astype(BF16), wd16_ref[...]) + bd_ref[...])

    @pl.when(nvalid <= 0)
    def _():
        y_ref[...] = jnp.zeros_like(y_ref)


def _experts(blk_e, nvalid, xbuf, wgu, bgu, wd, bd):
    d = D_MODEL
    blk_rows = EXPERT_BLOCK * TOKEN_TILE_ROWS
    nb = xbuf.shape[0] // blk_rows
    change = jnp.concatenate([jnp.zeros((1,), I32), (blk_e[1:] != blk_e[:-1]).astype(I32)])
    grp = jnp.cumsum(change)
    pos = jnp.where(change > 0, jnp.arange(nb, dtype=I32), nb)
    nxt_pos = lax.cummin(jnp.concatenate([pos[1:], jnp.full((1,), nb, I32)]), reverse=True)
    nxt_e = jnp.where(nxt_pos < nb, blk_e[jnp.minimum(nxt_pos, nb - 1)], blk_e)
    grid_spec = pltpu.PrefetchScalarGridSpec(
        num_scalar_prefetch=4,
        grid=(nb,),
        in_specs=[pl.BlockSpec((blk_rows, LANES), lambda b, e, *_: (b, 0)),
                  pl.BlockSpec(memory_space=pl.ANY),
                  pl.BlockSpec((None, 1, 2 * FF), lambda b, e, *_: (e[b], 0, 0)),
                  pl.BlockSpec(memory_space=pl.ANY),
                  pl.BlockSpec((None, 1, d), lambda b, e, *_: (e[b], 0, 0))],
        out_specs=pl.BlockSpec((blk_rows, LANES), lambda b, e, *_: (b, 0)),
        scratch_shapes=[pltpu.VMEM((2, d, 2 * FF), F32), pltpu.VMEM((2, FF, d), F32),
                        pltpu.VMEM((d, 2 * FF), BF16), pltpu.VMEM((FF, d), BF16),
                        pltpu.SemaphoreType.DMA((2, 2))],
    )
    return pl.pallas_call(
        _experts_kernel,
        grid_spec=grid_spec,
        out_shape=jax.ShapeDtypeStruct(xbuf.shape, F32),
        compiler_params=_params(("arbitrary",)),
        name="experts",
    )(blk_e, nvalid, grp, nxt_e, xbuf, wgu, bgu, wd, bd)


def _combine_kernel(dest_ref, dest_next_ref, ybuf_ref, x1_ref, gate_ref, gt2_ref, gf_ref, o_ref, rows_ref, sems):
    i = pl.program_id(0)
    tm = x1_ref.shape[0]
    slot = i % 2

    def start_gather(idx_ref, s):
        def issue(g, carry):
            for u in range(ROW_UNROLL):
                r = g * ROW_UNROLL + u
                for kk in range(TOP_K):
                    _token_copy(ybuf_ref, idx_ref[r * TOP_K + kk], rows_ref.at[s, kk], r,
                                sems.at[s]).start(priority=kk % 2)
            return carry

        lax.fori_loop(0, tm // ROW_UNROLL, issue, 0)

    @pl.when(i == 0)
    def _():
        start_gather(dest_ref, 0)

    @pl.when(i + 1 < pl.num_programs(0))
    def _():
        start_gather(dest_next_ref, 1 - slot)

    _drain_tokens(ybuf_ref, rows_ref.at[slot, 0], sems.at[slot], tm)

    gates = gate_ref[...]
    slabs = []
    for s in range(TOKEN_TILE_ROWS):
        moe = gates[:, 0:1] * _load_token_slab(rows_ref.at[slot, 0], s, tm)
        for kk in range(1, TOP_K):
            moe = moe + gates[:, kk:kk + 1] * _load_token_slab(rows_ref.at[slot, kk], s, tm)
        slabs.append(moe)
    xo = x1_ref[...] + gt2_ref[...] * jnp.concatenate(slabs, axis=1)
    o_ref[...] = xo * lax.rsqrt(jnp.mean(xo * xo, axis=-1, keepdims=True) + RMS_EPS) * gf_ref[...]


def _combine(dest_flat, ybuf, x1, gates, gt2, g_final):
    t, d = x1.shape
    tm = ROUTE_TILE
    n = t // tm
    return pl.pallas_call(
        _combine_kernel,
        grid=(n,),
        in_specs=[pl.BlockSpec((tm * TOP_K,), lambda i: (i,), memory_space=pltpu.SMEM),
                  pl.BlockSpec((tm * TOP_K,), lambda i: (jnp.minimum(i + 1, n - 1),), memory_space=pltpu.SMEM),
                  pl.BlockSpec(memory_space=pl.ANY),
                  pl.BlockSpec((tm, d), lambda i: (i, 0)),
                  pl.BlockSpec((tm, LANES), lambda i: (i, 0)),
                  pl.BlockSpec((1, d), lambda i: (0, 0)),
                  pl.BlockSpec((1, d), lambda i: (0, 0))],
        out_specs=pl.BlockSpec((tm, d), lambda i: (i, 0)),
        out_shape=jax.ShapeDtypeStruct((t, d), F32),
        scratch_shapes=[pltpu.VMEM((2, TOP_K, tm * TOKEN_TILE_ROWS, LANES), F32), pltpu.SemaphoreType.DMA((2,))],
        compiler_params=_params(("arbitrary",)),
        name="combine",
    )(dest_flat, dest_flat, ybuf, x1, gates, gt2, g_final)


def _pad_lanes(v, lane0):
    return jnp.zeros((1, LANES), F32).at[0, lane0:lane0 + v.shape[0]].set(v.astype(F32))


def _layer(x, mod, g_norm1, w_in, conv_w, a_log, dt_bias, g_onorm, b_fgate, w_o_delta, w_o_fox, w_out,
           g_norm2, w_router, b_router, w_gate_up, b_gate_up, w_down, b_down, g_final):
    t, d = x.shape
    sh1, sc1, gt1, sh2, sc2, gt2 = [mod[:, i * d:(i + 1) * d] for i in range(N_MOD)]

    w_t = w_in.T
    o = 0
    wq = w_t[o:o + QKV_W]; o += QKV_W
    wz = w_t[o:o + MIX_W]; o += MIX_W
    w_beta = w_t[o:o + N_HEADS]; o += N_HEADS
    w_dec = w_t[o:o + N_HEADS]; o += N_HEADS
    wf = w_t[o:o + QKV_W]; o += QKV_W
    wf = jnp.concatenate([wf[:MIX_W] * (LOG2E * HEAD_DIM ** -0.5), wf[MIX_W:]], axis=0)
    w_fg = w_t[o:o + N_HEADS]; o += N_HEADS
    wga = w_t[o:o + d]; o += d
    wgb = w_t[o:o + d]
    ws = jnp.zeros((LANES, d), F32)
    ws = ws.at[L_BETA:L_BETA + N_HEADS].set(w_beta).at[L_G:L_G + N_HEADS].set(w_dec)
    ws = ws.at[L_F:L_F + N_HEADS].set(w_fg)
    bf = lambda w: w.astype(BF16)

    tm = min(ROW_TILE, t)
    qkv, z, fx, gate_a, gate_b, small = _inproj(
        x, g_norm1.reshape(1, d), sc1, sh1, bf(wq), bf(wz), bf(wf), bf(wga), bf(wgb), bf(ws), tm)

    gcol, grow = _gates(small, _pad_lanes(a_log, L_G), _pad_lanes(dt_bias, L_G), _pad_lanes(b_fgate, L_F),
                        min(GATES_TILE, t))

    o_a = _delta(qkv, z, gcol, grow, conv_w, g_onorm.reshape(1, HEAD_DIM), min(DELTA_TILE, t))
    tq, tk = min(FOX_TQ, t), min(FOX_TK, t)
    f_rows = grow[L_F:L_F + N_HEADS].reshape(N_HEADS, t // tk, 1, tk)
    o_b = _fox(fx, f_rows, tq, tk)

    wr = jnp.zeros((d, LANES), F32).at[:, :N_EXPERTS].set(w_router)
    br = _pad_lanes(b_router, 0)
    x1, h2, ri, gates, cnt = _merge(o_a, o_b, gate_a, gate_b, x, bf(w_o_delta), bf(w_o_fox), bf(w_out), gt1,
                                    g_norm2.reshape(1, d), sc2, sh2, wr, br, tm)

    n_blocks = (t * TOP_K) // EXPERT_BLOCK + N_EXPERTS
    dest, blk, _ = _dest(ri, cnt, n_blocks, min(2048, t))
    dest_flat = dest[:, :TOP_K].reshape(t * TOP_K)
    blk_e, nvalid = blk[:n_blocks, 0], blk[:n_blocks, 1]

    xbuf = _dispatch(dest_flat, h2, n_blocks * EXPERT_BLOCK)
    ybuf = _experts(blk_e, nvalid, xbuf, w_gate_up, b_gate_up.reshape(N_EXPERTS, 1, 2 * FF),
                    w_down, b_down.reshape(N_EXPERTS, 1, d))
    return _combine(dest_flat, ybuf, x1, gates, gt2, g_final.reshape(1, d))


def kernel(x, c, w_ada, b_ada, g_norm1, w_in, conv_w, a_log, dt_bias, g_onorm, b_fgate, w_o_delta, w_o_fox, w_out,
           g_norm2, w_router, b_router, w_gate_up, b_gate_up, w_down, b_down, g_final):
    b, s, d = x.shape
    assert b == 1 and d == D_MODEL and w_ada.shape[0] == 1
    h = x[0]
    for l in range(w_ada.shape[0]):
        mod = _modulation(c[0], w_ada[l], b_ada[l])
        h = _layer(h, mod, g_norm1[l], w_in[l], conv_w[l], a_log[l], dt_bias[l], g_onorm[l], b_fgate[l],
                   w_o_delta[l], w_o_fox[l], w_out[l], g_norm2[l], w_router[l], b_router[l], w_gate_up[l],
                   b_gate_up[l], w_down[l], b_down[l], g_final)
    return h[None]
```

```python
import functools

import jax
import jax.numpy as jnp
from jax import lax
from jax.experimental import pallas as pl
from jax.experimental.pallas import tpu as pltpu

F32 = jnp.float32
BF16 = jnp.bfloat16
I32 = jnp.int32

D_MODEL = 1024
HEAD_DIM = 128
N_HEADS = 4
MIX_W = N_HEADS * HEAD_DIM
QKV_W = 3 * MIX_W
CONV_K = 4
CHUNK = 64
STACK = N_HEADS * CHUNK
SUB = 16
N_EXPERTS = 32
TOP_K = 4
FF = D_MODEL
SWIGLU_LIMIT = 7.0
SWIGLU_ALPHA = 1.702
RMS_EPS = 1e-6
L2_EPS = 1e-6
N_MOD = 6
LANES = 128
NEG_BIG = -1e30
LOG2E = 1.4426950408889634

ROW_TILE = 512
GATES_TILE = 256
DELTA_TILE = 256
EXPERT_BLOCK = 512
ROUTE_TILE = 256
VMEM_LIMIT = 56 * 1024 * 1024


def _params(sem, vmem=VMEM_LIMIT, flags=None):
    return pltpu.CompilerParams(dimension_semantics=sem, vmem_limit_bytes=vmem, flags=flags)


def _softplus(x):
    return jnp.maximum(x, 0.0) + jnp.log(1.0 + jnp.exp(-jnp.abs(x)))


def _sigmoid(x):
    return 1.0 / (1.0 + jnp.exp(-x))


def _silu(x):
    return x * _sigmoid(x)


def _dot(a, b, precision=None):
    return jnp.dot(a, b, preferred_element_type=F32, precision=precision)


def _dot_nt(a, b, precision=None):
    return lax.dot_general(a, b, (((1,), (1,)), ((), ())), preferred_element_type=F32, precision=precision)


def _split_bf16(x, terms):
    parts = []
    for _ in range(terms):
        p = x.astype(BF16)
        parts.append(p)
        x = x - p.astype(F32)
    return parts


def _dot_split(a, b):
    a_hi, a_lo = _split_bf16(a, 2)
    b_hi, b_lo = _split_bf16(b, 2)
    return _dot(a_hi, b_hi) + (_dot(a_hi, b_lo) + _dot(a_lo, b_hi))


def _dot_mask(mask01, x):
    return sum(_dot(mask01, p) for p in _split_bf16(x, 3))


TOKEN_TILE_ROWS = D_MODEL // LANES


def _store_token_tiles(ref, x):
    n = x.shape[0]
    for s in range(TOKEN_TILE_ROWS):
        ref[pl.ds(s, n, stride=TOKEN_TILE_ROWS), :] = x[:, s * LANES:(s + 1) * LANES]


def _load_token_slab(ref, s, n):
    return ref[pl.ds(s, n, stride=TOKEN_TILE_ROWS), :]


def _token_copy(src_ref, src_tok, dst_ref, dst_tok, sem):
    rows = lambda i: pl.ds(pl.multiple_of(i * TOKEN_TILE_ROWS, TOKEN_TILE_ROWS), TOKEN_TILE_ROWS)
    return pltpu.make_async_copy(src_ref.at[rows(src_tok), :], dst_ref.at[rows(dst_tok), :], sem)


def _mod_kernel(c_ref, w_ref, b_ref, o_ref):
    o_ref[...] = jnp.sum(c_ref[...] * w_ref[...], axis=0, keepdims=True) + b_ref[...]


def _modulation(c, w_ada, b_ada):
    d, n = w_ada.shape
    tn = 1024
    return pl.pallas_call(
        _mod_kernel,
        grid=(n // tn,),
        in_specs=[pl.BlockSpec((d, 1), lambda j: (0, 0)),
                  pl.BlockSpec((d, tn), lambda j: (0, j)),
                  pl.BlockSpec((1, tn), lambda j: (0, j))],
        out_specs=pl.BlockSpec((1, tn), lambda j: (0, j)),
        out_shape=jax.ShapeDtypeStruct((1, n), F32),
        compiler_params=_params(("arbitrary",)),
        name="mod",
    )(c.reshape(d, 1), w_ada, b_ada.reshape(1, n))


def _inproj_kernel(x_ref, g_ref, sc_ref, sh_ref, wq_ref, wz_ref, wf_ref, wga_ref, wgb_ref, ws_ref,
                   oq_ref, oz_ref, of_ref, oga_ref, ogb_ref, os_ref):
    x = x_ref[...]
    y = x * lax.rsqrt(jnp.mean(x * x, axis=-1, keepdims=True) + RMS_EPS) * g_ref[...]
    h = (y * (1.0 + sc_ref[...]) + sh_ref[...]).astype(BF16)
    oq_ref[...] = _dot_nt(h, wq_ref[...])
    oz_ref[...] = _dot_nt(h, wz_ref[...]).astype(BF16)
    of_ref[...] = _dot_nt(h, wf_ref[...]).astype(BF16)
    oga_ref[...] = _dot_nt(h, wga_ref[...]).astype(BF16)
    ogb_ref[...] = _dot_nt(h, wgb_ref[...]).astype(BF16)
    os_ref[...] = _dot_nt(h, ws_ref[...])


def _inproj(x, g1, sc1, sh1, wq, wz, wf, wga, wgb, ws, tm):
    t, d = x.shape
    row = lambda i: (i, 0)
    fixed = lambda i: (0, 0)
    ws_list = [wq, wz, wf, wga, wgb, ws]
    out_dt = [F32, BF16, BF16, BF16, BF16, F32]
    return pl.pallas_call(
        _inproj_kernel,
        grid=(t // tm,),
        in_specs=[pl.BlockSpec((tm, d), row)] + [pl.BlockSpec((1, d), fixed)] * 3
                 + [pl.BlockSpec(w.shape, fixed) for w in ws_list],
        out_specs=[pl.BlockSpec((tm, w.shape[0]), row) for w in ws_list],
        out_shape=[jax.ShapeDtypeStruct((t, w.shape[0]), dt) for w, dt in zip(ws_list, out_dt)],
        compiler_params=_params(("arbitrary",)),
        name="inproj",
    )(x, g1, sc1, sh1, *ws_list)


L_BETA, L_G, L_F, L_GC, L_GL = 0, 4, 8, 12, 16


def _gates_kernel(s_ref, alog_ref, dtb_ref, bf_ref, col_ref, row_ref, carry_ref):
    i = pl.program_id(0)
    tm = s_ref.shape[0]

    @pl.when(i == 0)
    def _():
        carry_ref[...] = jnp.zeros_like(carry_ref)

    s = s_ref[...]
    lane = lax.broadcasted_iota(I32, s.shape, 1)
    beta = _sigmoid(s)
    g = -jnp.exp(alog_ref[...]) * _softplus(s + dtb_ref[...])
    logf = -_softplus(-(s + bf_ref[...]))
    is_g = (lane >= L_G) & (lane < L_G + N_HEADS)
    is_f = (lane >= L_F) & (lane < L_F + N_HEADS)
    g = jnp.where(is_g, g, 0.0)
    logf = jnp.where(is_f, logf, 0.0)

    r = lax.broadcasted_iota(I32, (tm, tm), 0)
    c = lax.broadcasted_iota(I32, (tm, tm), 1)
    same_chunk = (r // CHUNK) == (c // CHUNK)
    tri = jnp.where(r >= c, 1.0, 0.0).astype(BF16)
    tri_chunk = jnp.where(same_chunk & (r >= c), 1.0, 0.0).astype(BF16)
    ones_chunk = jnp.where(same_chunk, 1.0, 0.0).astype(BF16)
    f_cum = _dot_mask(tri, logf) + carry_ref[...]
    carry_ref[...] = f_cum[tm - 1:tm, :]
    gc = _dot_mask(tri_chunk, g)
    gl = _dot_mask(ones_chunk, g)

    out = jnp.where(lane < N_HEADS, beta, 0.0) + g + f_cum
    out = out + pltpu.roll(gc, L_GC - L_G, 1) + pltpu.roll(gl, L_GL - L_G, 1)
    col_ref[...] = out
    row_ref[...] = out.T[:row_ref.shape[0], :]


def _gates(small, alog_row, dtb_row, bf_row, tm):
    t = small.shape[0]
    n_rows = 24
    return pl.pallas_call(
        _gates_kernel,
        grid=(t // tm,),
        in_specs=[pl.BlockSpec((tm, LANES), lambda i: (i, 0))] + [pl.BlockSpec((1, LANES), lambda i: (0, 0))] * 3,
        out_specs=[pl.BlockSpec((tm, LANES), lambda i: (i, 0)), pl.BlockSpec((n_rows, tm), lambda i: (0, i))],
        out_shape=[jax.ShapeDtypeStruct((t, LANES), F32), jax.ShapeDtypeStruct((n_rows, t), F32)],
        scratch_shapes=[pltpu.VMEM((1, LANES), F32)],
        compiler_params=_params(("arbitrary",)),
        name="gates",
    )(small, alog_row, dtb_row, bf_row)


def _stack_heads(a, col0):
    return jnp.concatenate([a[:, col0 + h * HEAD_DIM: col0 + (h + 1) * HEAD_DIM] for h in range(N_HEADS)], axis=0)


def _stack_cols(a, lane0):
    return jnp.concatenate([a[:, lane0 + h: lane0 + h + 1] for h in range(N_HEADS)], axis=0)


def _delta_kernel(qkv_ref, z_ref, gcol_ref, grow_ref, cw_ref, gon_ref, o_ref, ext_ref, tail_ref, s_ref):
    i = pl.program_id(0)
    tb = qkv_ref.shape[0]
    pad = tail_ref.shape[0]

    @pl.when(i == 0)
    def _():
        tail_ref[...] = jnp.zeros_like(tail_ref)
        s_ref[...] = jnp.zeros_like(s_ref)

    ext_ref[0:pad, :] = tail_ref[...]
    ext_ref[pad:pad + tb, :] = qkv_ref[...]
    tail_ref[...] = qkv_ref[tb - pad:tb, :]
    conv = cw_ref[0:1, :] * ext_ref[pad - 3:pad - 3 + tb, :]
    for j in range(1, CONV_K):
        conv = conv + cw_ref[j:j + 1, :] * ext_ref[pad - 3 + j:pad - 3 + j + tb, :]
    act = _silu(conv)

    r = lax.broadcasted_iota(I32, (STACK, STACK), 0)
    c = lax.broadcasted_iota(I32, (STACK, STACK), 1)
    same = (r // CHUNK) == (c // CHUNK)
    m_incl = same & (r >= c)
    m_strict = same & (r > c)
    eye = jnp.where(r == c, 1.0, 0.0)
    m_sub = []
    size = SUB
    while size <= CHUNK:
        m_sub.append((r // size) == (c // size))
        size *= 2
    rb = lax.broadcasted_iota(I32, (STACK, N_HEADS * HEAD_DIM), 0) // CHUNK
    cb = lax.broadcasted_iota(I32, (STACK, N_HEADS * HEAD_DIM), 1) // HEAD_DIM
    head_match = rb == cb

    chunks = range(tb // CHUNK)
    pre = []
    for ch in chunks:
        r0 = ch * CHUNK
        a = act[r0:r0 + CHUNK, :]
        q = _stack_heads(a, 0)
        k = _stack_heads(a, MIX_W)
        v = _stack_heads(a, 2 * MIX_W)
        q = q * lax.rsqrt(jnp.sum(q * q, axis=-1, keepdims=True) + L2_EPS) * (HEAD_DIM ** -0.5)
        k = k * lax.rsqrt(jnp.sum(k * k, axis=-1, keepdims=True) + L2_EPS)

        gcols = gcol_ref[r0:r0 + CHUNK, :]
        beta = _stack_cols(gcols, L_BETA)
        gc = _stack_cols(gcols, L_GC)
        gc_row = jnp.concatenate(
            [grow_ref[L_GC + h:L_GC + h + 1, r0:r0 + CHUNK] for h in range(N_HEADS)], axis=1)
        gl_row = jnp.concatenate(
            [grow_ref[L_GL + h:L_GL + h + 1, r0:r0 + CHUNK] for h in range(N_HEADS)], axis=1)

        decay = jnp.exp(jnp.where(m_incl, gc - gc_row, NEG_BIG))
        kb16 = k.astype(BF16)
        kk = _dot_nt(kb16, kb16)
        lmat = jnp.where(m_strict, kk * beta * decay, 0.0)
        attn = (_dot_nt(q.astype(BF16), kb16) * decay).astype(BF16)
        egc = jnp.exp(gc)
        rhs = jnp.concatenate([v * beta, k * (beta * egc)], axis=1).astype(BF16)
        kt_dec = (k.T * jnp.exp(gl_row - gc_row)).astype(BF16)
        pre.append(dict(lmat=lmat, attn=attn, rhs=rhs, kt_dec=kt_dec, q_dec=q * egc))

    l_d = [jnp.where(m_sub[0], p["lmat"], 0.0) for p in pre]
    inv = [eye - l for l in l_d]
    l16 = [l.astype(BF16) for l in l_d]
    pw = [_dot(l, l).astype(BF16) for l in l16]
    n_sq = SUB.bit_length() - 3
    for s in range(n_sq + 1):
        inv = [x + _dot(x.astype(BF16), p) for x, p in zip(inv, pw)]
        if s < n_sq:
            pw = [_dot(p, p).astype(BF16) for p in pw]
    for lvl in range(1, len(m_sub)):
        m_off = m_sub[lvl] & ~m_sub[lvl - 1]
        off = [jnp.where(m_off, p["lmat"], 0.0).astype(BF16) for p in pre]
        inv16 = [x.astype(BF16) for x in inv]
        half = [_dot(x, o).astype(BF16) for x, o in zip(inv16, off)]
        inv = [x - _dot(h, x16) for x, h, x16 in zip(inv, half, inv16)]
    uws = [_dot(x.astype(BF16), p["rhs"]) for x, p in zip(inv, pre)]

    for ch in chunks:
        r0 = ch * CHUNK
        attn, kt_dec = pre[ch]["attn"], pre[ch]["kt_dec"]
        u, w = uws[ch][:, :HEAD_DIM], uws[ch][:, HEAD_DIM:]

        state16 = s_ref[...].astype(BF16)
        wq_s = _dot(jnp.concatenate([w, pre[ch]["q_dec"]], axis=0).astype(BF16), state16)
        ws_d = jnp.concatenate([wq_s[h * CHUNK:(h + 1) * CHUNK, h * HEAD_DIM:(h + 1) * HEAD_DIM]
                                for h in range(N_HEADS)], axis=0)
        qs_d = jnp.concatenate([wq_s[STACK + h * CHUNK:STACK + (h + 1) * CHUNK, h * HEAD_DIM:(h + 1) * HEAD_DIM]
                                for h in range(N_HEADS)], axis=0)
        v_new = u - ws_d
        v16 = v_new.astype(BF16)
        o = qs_d + _dot(attn, v16)

        v_bd =jnp.where(head_match, jnp.concatenate([v_new] * N_HEADS, axis=1), 0.0).astype(BF16)
        upd = _dot(kt_dec, v_bd)
        for h in range(N_HEADS):
            e_h = jnp.exp(grow_ref[L_GL + h:L_GL + h + 1, r0:r0 + 1])
            sl = slice(h * HEAD_DIM, (h + 1) * HEAD_DIM)
            s_ref[:, sl] = s_ref[:, sl] * e_h + upd[:, sl]

        for h in range(N_HEADS):
            oh = o[h * CHUNK:(h + 1) * CHUNK, :]
            oh = oh * lax.rsqrt(jnp.mean(oh * oh, axis=-1, keepdims=True) + RMS_EPS) * gon_ref[...]
            zh = z_ref[r0:r0 + CHUNK, h * HEAD_DIM:(h + 1) * HEAD_DIM].astype(F32)
            o_ref[r0:r0 + CHUNK, h * HEAD_DIM:(h + 1) * HEAD_DIM] = (oh * _silu(zh)).astype(o_ref.dtype)


def _delta(qkv, z, gcol, grow, conv_w, g_onorm, tb):
    t = qkv.shape[0]
    pad = 8
    return pl.pallas_call(
        _delta_kernel,
        grid=(t // tb,),
        in_specs=[pl.BlockSpec((tb, QKV_W), lambda i: (i, 0)),
                  pl.BlockSpec((tb, MIX_W), lambda i: (i, 0)),
                  pl.BlockSpec((tb, LANES), lambda i: (i, 0)),
                  pl.BlockSpec((grow.shape[0], tb), lambda i: (0, i)),
                  pl.BlockSpec((CONV_K, QKV_W), lambda i: (0, 0)),
                  pl.BlockSpec((1, HEAD_DIM), lambda i: (0, 0))],
        out_specs=pl.BlockSpec((tb, MIX_W), lambda i: (i, 0)),
        out_shape=jax.ShapeDtypeStruct((t, MIX_W), BF16),
        scratch_shapes=[pltpu.VMEM((tb + pad, QKV_W), F32),
                        pltpu.VMEM((pad, QKV_W), F32),
                        pltpu.VMEM((HEAD_DIM, N_HEADS * HEAD_DIM), F32)],
        compiler_params=_params(("arbitrary",)),
        name="delta",
    )(qkv, z, gcol, grow, conv_w, g_onorm)


FOX_GROUP = 4
FOX_TQ, FOX_TK = 256, 2048
FOX_DIAG_PARTS = 2
FOX_ROW_STREAMS = 2
FOX_SKIP_LOG2 = -200.0


def _fox_kernel(*refs, tk):
    g = FOX_GROUP
    q_ref, o_ref, kmax_ref = refs[0], refs[-2], refs[-1]
    k_refs, v_refs, f_refs = (refs[1 + i * g:1 + (i + 1) * g] for i in range(3))
    qi = pl.program_id(1)
    tq = q_ref.shape[0]
    t_keys = k_refs[0].shape[0]

    @pl.when(qi == 0)
    def _():
        rows = min(t_keys, 2048)
        for h in range(g):
            def norm_max(c, best, h=h):
                kc = k_refs[h][pl.ds(pl.multiple_of(c * rows, rows), rows), :].astype(F32)
                return jnp.maximum(best, jnp.sum(kc * kc, axis=-1, keepdims=True))

            best = lax.fori_loop(0, t_keys // rows, norm_max, jnp.zeros((rows, 1), F32))
            kmax_ref[h:h + 1, :] = jnp.broadcast_to(jnp.sqrt(jnp.max(best, axis=0, keepdims=True)), (1, LANES))

    def step(off, width, f_blk, carry, mask):
        ones = jnp.ones((width, HEAD_DIM), BF16)
        ss = []
        for h in range(g):
            q = q_ref[:, h * HEAD_DIM:(h + 1) * HEAD_DIM]
            s = _dot_nt(q, k_refs[h][pl.ds(off, width), :]) - f_blk(h) * LOG2E
            if mask is not None:
                s = jnp.where(mask, s, NEG_BIG)
            ss.append(s)
        ps, ms, alphas = [], [], []
        for h in range(g):
            m = carry[h][0]
            m_new = jnp.maximum(m, jnp.max(ss[h], axis=-1, keepdims=True))
            alphas.append(jnp.exp2(m - m_new))
            ps.append(jnp.exp2(ss[h] - m_new).astype(BF16))
            ms.append(m_new)
        out = []
        for h in range(g):
            v_aug = jnp.concatenate([v_refs[h][pl.ds(off, width), :], ones], axis=1)
            out.append((ms[h], alphas[h] * carry[h][1] + _dot(ps[h], v_aug)))
        return tuple(out)

    def head_chunk(h, j, mc):
        m, acc = mc
        off = pl.multiple_of(j * tk, tk)
        k_blk = k_refs[h][pl.ds(off, tk), :]
        v_aug = jnp.concatenate([v_refs[h][pl.ds(off, tk), :], jnp.ones((tk, HEAD_DIM), BF16)], axis=1)
        f_blk = f_refs[h][j] * LOG2E
        part = tq // FOX_ROW_STREAMS
        rows = [slice(i * part, (i + 1) * part) for i in range(FOX_ROW_STREAMS)]
        ss = [_dot_nt(q_ref[r, h * HEAD_DIM:(h + 1) * HEAD_DIM], k_blk) - f_blk for r in rows]
        m_new = [jnp.maximum(m[r], jnp.max(s, axis=-1, keepdims=True)) for r, s in zip(rows, ss)]
        ps = [jnp.exp2(s - mn).astype(BF16) for s, mn in zip(ss, m_new)]
        accs = [jnp.exp2(m[r] - mn) * acc[r] + _dot(p, v_aug) for r, mn, p in zip(rows, m_new, ps)]
        return jnp.concatenate(m_new, axis=0), jnp.concatenate(accs, axis=0)

    init = tuple((jnp.full((tq, 1), NEG_BIG, F32), jnp.zeros((tq, 2 * HEAD_DIM), F32)) for _ in range(g))
    n_full = (qi * tq) // tk

    parts = max(1, min(FOX_DIAG_PARTS, tk // tq))
    width = tk // parts
    row = lax.broadcasted_iota(I32, (tq, width), 0)
    col = lax.broadcasted_iota(I32, (tq, width), 1)
    carry = init
    for d in reversed(range(max(tq // tk, 1))):
        j = n_full + d
        for p in reversed(range(parts)):
            off = pl.multiple_of(j * tk + p * width, width)
            piece = functools.partial(step, off, width,
                                      lambda h, j=j, p=p: f_refs[h][j][:, p * width:(p + 1) * width],
                                      mask=col + (off - qi * tq) <= row)
            carry = piece(carry) if p == 0 else lax.cond(off < (qi + 1) * tq, piece, lambda c: c, carry)

    qn = []
    for h in range(g):
        q = q_ref[:, h * HEAD_DIM:(h + 1) * HEAD_DIM].astype(F32)
        qn.append(jnp.sqrt(jnp.sum(q * q, axis=-1, keepdims=True)) * kmax_ref[h:h + 1, 0:1] * 1.01 + 1.0)

    def reaches(j, carry):
        jc = jnp.maximum(j, 0)
        flags = []
        for h in range(g):
            bound = qn[h] - f_refs[h][jc][:, tk - 1:tk] * LOG2E
            flags.append((jnp.max(bound - carry[h][0]) >= FOX_SKIP_LOG2).astype(I32))
        return tuple(flags)

    def chunk_all_heads(state):
        j, _, carry = state
        carry = step(pl.multiple_of(j * tk, tk), tk, lambda h: f_refs[h][j], carry, None)
        return j - 1, reaches(j - 1, carry), carry

    def chunk_live_heads(state):
        j, live, carry = state
        carry = tuple(lax.cond(live[h] > 0, functools.partial(head_chunk, h, j), lambda mc: mc, carry[h])
                      for h in range(g))
        return j - 1, reaches(j - 1, carry), carry

    state = (n_full - 1, reaches(n_full - 1, carry), carry)
    state = lax.while_loop(lambda st: (st[0] >= 0) & (sum(st[1]) == g), chunk_all_heads, state)
    _, _, carry = lax.while_loop(lambda st: (st[0] >= 0) & (sum(st[1]) > 0), chunk_live_heads, state)
    for h in range(g):
        acc = carry[h][1]
        o_ref[:, h * HEAD_DIM:(h + 1) * HEAD_DIM] = (acc[:, :HEAD_DIM] / acc[:, HEAD_DIM:]).astype(o_ref.dtype)


def _fox(fx, f_rows, tq, tk):
    t = fx.shape[0]
    g = FOX_GROUP
    heads = lambda hg, u: hg * g + u
    q_specs = [pl.BlockSpec((tq, g * HEAD_DIM), lambda hg, i: (i, hg))]
    k_specs = [pl.BlockSpec((t, HEAD_DIM), lambda hg, i, u=u: (0, N_HEADS + heads(hg, u)),
                            pipeline_mode=pl.Buffered(1)) for u in range(g)]
    v_specs = [pl.BlockSpec((t, HEAD_DIM), lambda hg, i, u=u: (0, 2 * N_HEADS + heads(hg, u)),
                            pipeline_mode=pl.Buffered(1)) for u in range(g)]
    f_specs = [pl.BlockSpec((None, t // tk, 1, tk), lambda hg, i, u=u: (heads(hg, u), 0, 0, 0)) for u in range(g)]
    return pl.pallas_call(
        functools.partial(_fox_kernel, tk=tk),
        grid=(N_HEADS // g, t // tq),
        in_specs=q_specs + k_specs + v_specs + f_specs,
        out_specs=pl.BlockSpec((tq, g * HEAD_DIM), lambda hg, i: (i, hg)),
        out_shape=jax.ShapeDtypeStruct((t, MIX_W), BF16),
        scratch_shapes=[pltpu.VMEM((8, LANES), F32)],
        compiler_params=_params(("arbitrary", "arbitrary")),
        name="fox",
    )(*([fx] * (1 + 2 * g) + [f_rows] * g))


def _merge_kernel(oa_ref, ob_ref, ga_ref, gb_ref, x_ref, woa_ref, wob_ref, wout_ref, gt1_ref, g2_ref, sc2_ref,
                  sh2_ref, wr_ref, br_ref, x1_ref, h2_ref, ri_ref, gate_ref, cnt_ref, carry_ref):
    i = pl.program_id(0)
    tm = x_ref.shape[0]

    @pl.when(i == 0)
    def _():
        carry_ref[...] = jnp.zeros_like(carry_ref)

    ya = _dot(oa_ref[...], woa_ref[...])
    yb = _dot(ob_ref[...], wob_ref[...])
    merged = _sigmoid(ga_ref[...].astype(F32)) * ya + _sigmoid(gb_ref[...].astype(F32)) * yb
    x1 = x_ref[...] + gt1_ref[...] * _dot(merged.astype(BF16), wout_ref[...])
    x1_ref[...] = x1
    y = x1 * lax.rsqrt(jnp.mean(x1 * x1, axis=-1, keepdims=True) + RMS_EPS) * g2_ref[...]
    h2 = y * (1.0 + sc2_ref[...]) + sh2_ref[...]
    _store_token_tiles(h2_ref, h2)

    logits = _dot_split(h2, wr_ref[...]) + br_ref[...]
    lane = lax.broadcasted_iota(I32, logits.shape, 1)
    lanef = lane.astype(F32)
    cur = jnp.where(lane < N_EXPERTS, logits, -jnp.inf)
    vals, idxs = [], []
    for _ in range(TOP_K):
        m = jnp.max(cur, axis=-1, keepdims=True)
        ix = jnp.min(jnp.where(cur == m, lanef, float(LANES)), axis=-1, keepdims=True)
        vals.append(m)
        idxs.append(ix)
        cur = jnp.where(lanef == ix, -jnp.inf, cur)
    exps = [jnp.exp(v - vals[0]) for v in vals]
    denom = exps[0] + exps[1] + exps[2] + exps[3]

    onehot = jnp.zeros(logits.shape, F32)
    for ix in idxs:
        onehot = onehot + jnp.where(lanef == ix, 1.0, 0.0)
    r = lax.broadcasted_iota(I32, (tm, tm), 0)
    c = lax.broadcasted_iota(I32, (tm, tm), 1)
    strict = jnp.where(r > c, 1.0, 0.0).astype(BF16)
    before = _dot(strict, onehot.astype(BF16)) + carry_ref[...]
    carry_ref[...] = carry_ref[...] + jnp.sum(onehot, axis=0, keepdims=True)
    cnt_ref[...] = carry_ref[...].astype(I32)

    ri = jnp.zeros(logits.shape, F32)
    gates = jnp.zeros(logits.shape, F32)
    for kk in range(TOP_K):
        rank = jnp.sum(jnp.where(lanef == idxs[kk], before, 0.0), axis=-1, keepdims=True)
        ri = ri + jnp.where(lane == kk, idxs[kk], 0.0) + jnp.where(lane == TOP_K + kk, rank, 0.0)
        gates = gates + jnp.where(lane == kk, exps[kk] / denom, 0.0)
    ri_ref[...] = ri.astype(I32)
    gate_ref[...] = gates


def _merge(o_a, o_b, gate_a, gate_b, x, woa, wob, wout, gt1, g2, sc2, sh2, wr, br, tm):
    t, d = x.shape
    row = lambda i: (i, 0)
    fixed = lambda i: (0, 0)
    return pl.pallas_call(
        _merge_kernel,
        grid=(t // tm,),
        in_specs=[pl.BlockSpec((tm, MIX_W), row), pl.BlockSpec((tm, MIX_W), row),
                  pl.BlockSpec((tm, d), row), pl.BlockSpec((tm, d), row), pl.BlockSpec((tm, d), row),
                  pl.BlockSpec((MIX_W, d), fixed), pl.BlockSpec((MIX_W, d), fixed), pl.BlockSpec((d, d), fixed),
                  pl.BlockSpec((1, d), fixed), pl.BlockSpec((1, d), fixed), pl.BlockSpec((1, d), fixed),
                  pl.BlockSpec((1, d), fixed), pl.BlockSpec((d, LANES), fixed), pl.BlockSpec((1, LANES), fixed)],
        out_specs=[pl.BlockSpec((tm, d), row), pl.BlockSpec((tm * TOKEN_TILE_ROWS, LANES), row),
                   pl.BlockSpec((tm, LANES), row),
                   pl.BlockSpec((tm, LANES), row), pl.BlockSpec((1, LANES), fixed)],
        out_shape=[jax.ShapeDtypeStruct((t, d), F32), jax.ShapeDtypeStruct((t * TOKEN_TILE_ROWS, LANES), F32),
                   jax.ShapeDtypeStruct((t, LANES), I32), jax.ShapeDtypeStruct((t, LANES), F32),
                   jax.ShapeDtypeStruct((1, LANES), I32)],
        scratch_shapes=[pltpu.VMEM((1, LANES), F32)],
        compiler_params=_params(("arbitrary",)),
        name="merge",
    )(o_a, o_b, gate_a, gate_b, x, woa, wob, wout, gt1, g2, sc2, sh2, wr, br)


def _dest_kernel(ri_ref, cnt_ref, dest_ref, blk_ref, pend_ref):
    shift = EXPERT_BLOCK.bit_length() - 1
    cnt = jnp.broadcast_to(cnt_ref[...], (8, LANES))
    lane_row = lax.broadcasted_iota(I32, cnt.shape, 1)
    padded = jnp.where(lane_row < N_EXPERTS, ((cnt + (EXPERT_BLOCK - 1)) >> shift) << shift, 0)
    pend = padded
    s = 1
    while s < N_EXPERTS:
        pend = pend + jnp.where(lane_row >= s, pltpu.roll(pend, s, 1), 0)
        s *= 2
    pstart = (pend - padded)[0:1, :].astype(F32)
    pend = pend[0:1, :]
    pend_ref[...] = pend

    ri = ri_ref[...]
    lane = lax.broadcasted_iota(I32, ri.shape, 1)
    dest = jnp.zeros(ri.shape, F32)
    for kk in range(TOP_K):
        ix = ri[:, kk:kk + 1]
        rank = ri[:, TOP_K + kk:TOP_K + kk + 1].astype(F32)
        start = jnp.sum(jnp.where(lane == ix, pstart, 0.0), axis=-1, keepdims=True)
        dest = dest + jnp.where(lane == kk, start + rank, 0.0)
    dest_ref[...] = dest.astype(I32)

    bstart = lax.broadcasted_iota(I32, blk_ref.shape, 0) * EXPERT_BLOCK
    lane_b = lax.broadcasted_iota(I32, blk_ref.shape, 1)
    ended = jnp.where((lane_b < N_EXPERTS) & (pend <= bstart), 1.0, 0.0)
    e = jnp.minimum(jnp.sum(ended, axis=-1, keepdims=True), float(N_EXPERTS - 1))
    vend = pstart + cnt[0:1, :].astype(F32)
    vend_e = jnp.sum(jnp.where(lane_b.astype(F32) == e, vend, 0.0), axis=-1, keepdims=True)
    nvalid = jnp.clip(vend_e - bstart[:, 0:1].astype(F32), 0.0, float(EXPERT_BLOCK))
    blk_ref[...] = jnp.where(lane_b == 0, e, jnp.where(lane_b == 1, nvalid, 0.0)).astype(I32)


def _dest(ri, cnt, n_blocks, tm):
    t = ri.shape[0]
    nb_pad = -(-n_blocks // 8) * 8
    return pl.pallas_call(
        _dest_kernel,
        grid=(t // tm,),
        in_specs=[pl.BlockSpec((tm, LANES), lambda i: (i, 0)), pl.BlockSpec((1, LANES), lambda i: (0, 0))],
        out_specs=[pl.BlockSpec((tm, LANES), lambda i: (i, 0)), pl.BlockSpec((nb_pad, LANES), lambda i: (0, 0)),
                   pl.BlockSpec((1, LANES), lambda i: (0, 0))],
        out_shape=[jax.ShapeDtypeStruct((t, LANES), I32), jax.ShapeDtypeStruct((nb_pad, LANES), I32),
                   jax.ShapeDtypeStruct((1, LANES), I32)],
        compiler_params=_params(("arbitrary",)),
        name="dest",
    )(ri, cnt)


ROW_UNROLL = 8


def _drain_tokens(src_ref, dst_ref, sem, n_tokens):
    def drain(_, carry):
        for _ in range(ROW_UNROLL * TOP_K):
            _token_copy(src_ref, 0, dst_ref, 0, sem).wait()
        return carry

    lax.fori_loop(0, n_tokens // ROW_UNROLL, drain, 0)


def _dispatch_kernel(dest_ref, h_ref, xbuf_ref, sem):
    tm = h_ref.shape[0] // TOKEN_TILE_ROWS

    def issue(g, carry):
        for u in range(ROW_UNROLL):
            r = g * ROW_UNROLL + u
            for kk in range(TOP_K):
                _token_copy(h_ref, r, xbuf_ref, dest_ref[r * TOP_K + kk], sem).start(priority=kk % 2)
        return carry

    lax.fori_loop(0, tm // ROW_UNROLL, issue, 0)
    _drain_tokens(h_ref, xbuf_ref, sem, tm)


def _dispatch(dest_flat, h2_tiles, n_rows):
    tm = ROUTE_TILE
    t = h2_tiles.shape[0] // TOKEN_TILE_ROWS
    return pl.pallas_call(
        _dispatch_kernel,
        grid=(t // tm,),
        in_specs=[pl.BlockSpec((tm * TOP_K,), lambda i: (i,), memory_space=pltpu.SMEM),
                  pl.BlockSpec((tm * TOKEN_TILE_ROWS, LANES), lambda i: (i, 0))],
        out_specs=pl.BlockSpec(memory_space=pl.ANY),
        out_shape=jax.ShapeDtypeStruct((n_rows * TOKEN_TILE_ROWS, LANES), h2_tiles.dtype),
        scratch_shapes=[pltpu.SemaphoreType.DMA(())],
        compiler_params=_params(("arbitrary",)),
        name="dispatch",
    )(dest_flat, h2_tiles)


def _experts_kernel(blk_e_ref, nvalid_ref, grp_ref, nxt_e_ref, x_ref, wgu_hbm, bgu_ref, wd_hbm, bd_ref, y_ref,
                    wgu32_ref, wd32_ref, wgu16_ref, wd16_ref, sems):
    b = pl.program_id(0)
    nvalid = nvalid_ref[b]
    e = blk_e_ref[b]
    slot = grp_ref[b] % 2
    first = (b == 0) | (e != blk_e_ref[jnp.maximum(b - 1, 0)])

    def weight_copies(expert, s):
        return (pltpu.make_async_copy(wgu_hbm.at[expert], wgu32_ref.at[s], sems.at[s, 0]),
                pltpu.make_async_copy(wd_hbm.at[expert], wd32_ref.at[s], sems.at[s, 1]))

    @pl.when(b == 0)
    def _():
        for cp in weight_copies(e, 0):
            cp.start()

    @pl.when(first)
    def _():
        for cp in weight_copies(e, slot):
            cp.wait()
        wgu16_ref[...] = wgu32_ref[slot].astype(BF16)
        wd16_ref[...] = wd32_ref[slot].astype(BF16)

    @pl.when(first & (nxt_e_ref[b] != e))
    def _():
        for cp in weight_copies(nxt_e_ref[b], 1 - slot):
            cp.start()

    n_blk = x_ref.shape[0] // TOKEN_TILE_ROWS
    n_half = n_blk // 2

    def mlp(n):
        x = jnp.concatenate([_load_token_slab(x_ref, s, n) for s in range(TOKEN_TILE_ROWS)], axis=1)
        rows = lax.broadcasted_iota(I32, x.shape, 0)
        x = jnp.where(rows < nvalid, x, 0.0).astype(BF16)
        gu = _dot(x, wgu16_ref[...]) + bgu_ref[...]
        gate = jnp.minimum(gu[:, :FF], SWIGLU_LIMIT)
        up = jnp.clip(gu[:, FF:], -SWIGLU_LIMIT, SWIGLU_LIMIT)
        act = (up + 1.0) * (gate * _sigmoid(SWIGLU_ALPHA * gate))
        _store_token_tiles(y_ref, _dot(act.astype(BF16), wd16_ref[...]) + bd_ref[...])
        if n < n_blk:
            y_ref[n * TOKEN_TILE_ROWS:, :] = jnp.zeros(((n_blk - n) * TOKEN_TILE_ROWS, LANES), y_ref.dtype)

    @pl.when(nvalid > n_half)
    def _():
        mlp(n_blk)

    @pl.when((nvalid > 0) & (nvalid <= n_half))
    def _():
        mlp(n_half)

    @pl.when(nvalid <= 0)
    def _():
        y_ref[...] = jnp.zeros_like(y_ref)


def _experts(blk_e, nvalid, xbuf, wgu, bgu, wd, bd):
    d = D_MODEL
    blk_rows = EXPERT_BLOCK * TOKEN_TILE_ROWS
    nb = xbuf.shape[0] // blk_rows
    change = jnp.concatenate([jnp.zeros((1,), I32), (blk_e[1:] != blk_e[:-1]).astype(I32)])
    grp = jnp.cumsum(change)
    pos = jnp.where(change > 0, jnp.arange(nb, dtype=I32), nb)
    nxt_pos = lax.cummin(jnp.concatenate([pos[1:], jnp.full((1,), nb, I32)]), reverse=True)
    nxt_e = jnp.where(nxt_pos < nb, blk_e[jnp.minimum(nxt_pos, nb - 1)], blk_e)
    grid_spec = pltpu.PrefetchScalarGridSpec(
        num_scalar_prefetch=4,
        grid=(nb,),
        in_specs=[pl.BlockSpec((blk_rows, LANES), lambda b, e, *_: (b, 0)),
                  pl.BlockSpec(memory_space=pl.ANY),
                  pl.BlockSpec((None, 1, 2 * FF), lambda b, e, *_: (e[b], 0, 0)),
                  pl.BlockSpec(memory_space=pl.ANY),
                  pl.BlockSpec((None, 1, d), lambda b, e, *_: (e[b], 0, 0))],
        out_specs=pl.BlockSpec((blk_rows, LANES), lambda b, e, *_: (b, 0)),
        scratch_shapes=[pltpu.VMEM((2, d, 2 * FF), F32), pltpu.VMEM((2, FF, d), F32),
                        pltpu.VMEM((d, 2 * FF), BF16), pltpu.VMEM((FF, d), BF16),
                        pltpu.SemaphoreType.DMA((2, 2))],
    )
    return pl.pallas_call(
        _experts_kernel,
        grid_spec=grid_spec,
        out_shape=jax.ShapeDtypeStruct(xbuf.shape, F32),
        compiler_params=_params(("arbitrary",)),
        name="experts",
    )(blk_e, nvalid, grp, nxt_e, xbuf, wgu, bgu, wd, bd)


def _combine_kernel(dest_ref, dest_next_ref, ybuf_ref, x1_ref, gate_ref, gt2_ref, gf_ref, o_ref, rows_ref, sems):
    i = pl.program_id(0)
    tm = x1_ref.shape[0]
    slot = i % 2

    def start_gather(idx_ref, s):
        def issue(g, carry):
            for u in range(ROW_UNROLL):
                r = g * ROW_UNROLL + u
                for kk in range(TOP_K):
                    _token_copy(ybuf_ref, idx_ref[r * TOP_K + kk], rows_ref.at[s, kk], r,
                                sems.at[s]).start(priority=kk % 2)
            return carry

        lax.fori_loop(0, tm // ROW_UNROLL, issue, 0)

    @pl.when(i == 0)
    def _():
        start_gather(dest_ref, 0)

    @pl.when(i + 1 < pl.num_programs(0))
    def _():
        start_gather(dest_next_ref, 1 - slot)

    _drain_tokens(ybuf_ref, rows_ref.at[slot, 0], sems.at[slot], tm)

    gates = gate_ref[...]
    slabs = []
    for s in range(TOKEN_TILE_ROWS):
        moe = gates[:, 0:1] * _load_token_slab(rows_ref.at[slot, 0], s, tm)
        for kk in range(1, TOP_K):
            moe = moe + gates[:, kk:kk + 1] * _load_token_slab(rows_ref.at[slot, kk], s, tm)
        slabs.append(moe)
    xo = x1_ref[...] + gt2_ref[...] * jnp.concatenate(slabs, axis=1)
    o_ref[...] = xo * lax.rsqrt(jnp.mean(xo * xo, axis=-1, keepdims=True) + RMS_EPS) * gf_ref[...]


def _combine(dest_flat, ybuf, x1, gates, gt2, g_final):
    t, d = x1.shape
    tm = ROUTE_TILE
    n = t // tm
    return pl.pallas_call(
        _combine_kernel,
        grid=(n,),
        in_specs=[pl.BlockSpec((tm * TOP_K,), lambda i: (i,), memory_space=pltpu.SMEM),
                  pl.BlockSpec((tm * TOP_K,), lambda i: (jnp.minimum(i + 1, n - 1),), memory_space=pltpu.SMEM),
                  pl.BlockSpec(memory_space=pl.ANY),
                  pl.BlockSpec((tm, d), lambda i: (i, 0)),
                  pl.BlockSpec((tm, LANES), lambda i: (i, 0)),
                  pl.BlockSpec((1, d), lambda i: (0, 0)),
                  pl.BlockSpec((1, d), lambda i: (0, 0))],
        out_specs=pl.BlockSpec((tm, d), lambda i: (i, 0)),
        out_shape=jax.ShapeDtypeStruct((t, d), F32),
        scratch_shapes=[pltpu.VMEM((2, TOP_K, tm * TOKEN_TILE_ROWS, LANES), F32), pltpu.SemaphoreType.DMA((2,))],
        compiler_params=_params(("arbitrary",)),
        name="combine",
    )(dest_flat, dest_flat, ybuf, x1, gates, gt2, g_final)


def _pad_lanes(v, lane0):
    return jnp.zeros((1, LANES), F32).at[0, lane0:lane0 + v.shape[0]].set(v.astype(F32))


def _layer(x, mod, g_norm1, w_in, conv_w, a_log, dt_bias, g_onorm, b_fgate, w_o_delta, w_o_fox, w_out,
           g_norm2, w_router, b_router, w_gate_up, b_gate_up, w_down, b_down, g_final):
    t, d = x.shape
    sh1, sc1, gt1, sh2, sc2, gt2 = [mod[:, i * d:(i + 1) * d] for i in range(N_MOD)]

    w_t = w_in.T
    o = 0
    wq = w_t[o:o + QKV_W]; o += QKV_W
    wz = w_t[o:o + MIX_W]; o += MIX_W
    w_beta = w_t[o:o + N_HEADS]; o += N_HEADS
    w_dec = w_t[o:o + N_HEADS]; o += N_HEADS
    wf = w_t[o:o + QKV_W]; o += QKV_W
    wf = jnp.concatenate([wf[:MIX_W] * (LOG2E * HEAD_DIM ** -0.5), wf[MIX_W:]], axis=0)
    w_fg = w_t[o:o + N_HEADS]; o += N_HEADS
    wga = w_t[o:o + d]; o += d
    wgb = w_t[o:o + d]
    ws = jnp.zeros((LANES, d), F32)
    ws = ws.at[L_BETA:L_BETA + N_HEADS].set(w_beta).at[L_G:L_G + N_HEADS].set(w_dec)
    ws = ws.at[L_F:L_F + N_HEADS].set(w_fg)
    bf = lambda w: w.astype(BF16)

    tm = min(ROW_TILE, t)
    qkv, z, fx, gate_a, gate_b, small = _inproj(
        x, g_norm1.reshape(1, d), sc1, sh1, bf(wq), bf(wz), bf(wf), bf(wga), bf(wgb), bf(ws), tm)

    gcol, grow = _gates(small, _pad_lanes(a_log, L_G), _pad_lanes(dt_bias, L_G), _pad_lanes(b_fgate, L_F),
                        min(GATES_TILE, t))

    o_a = _delta(qkv, z, gcol, grow, conv_w, g_onorm.reshape(1, HEAD_DIM), min(DELTA_TILE, t))
    tq, tk = min(FOX_TQ, t), min(FOX_TK, t)
    f_rows = grow[L_F:L_F + N_HEADS].reshape(N_HEADS, t // tk, 1, tk)
    o_b = _fox(fx, f_rows, tq, tk)

    wr = jnp.zeros((d, LANES), F32).at[:, :N_EXPERTS].set(w_router)
    br = _pad_lanes(b_router, 0)
    x1, h2, ri, gates, cnt = _merge(o_a, o_b, gate_a, gate_b, x, bf(w_o_delta), bf(w_o_fox), bf(w_out), gt1,
                                    g_norm2.reshape(1, d), sc2, sh2, wr, br, tm)

    n_blocks = (t * TOP_K) // EXPERT_BLOCK + N_EXPERTS
    dest, blk, _ = _dest(ri, cnt, n_blocks, min(2048, t))
    dest_flat = dest[:, :TOP_K].reshape(t * TOP_K)
    blk_e, nvalid = blk[:n_blocks, 0], blk[:n_blocks, 1]

    xbuf = _dispatch(dest_flat, h2, n_blocks * EXPERT_BLOCK)
    ybuf = _experts(blk_e, nvalid, xbuf, w_gate_up, b_gate_up.reshape(N_EXPERTS, 1, 2 * FF),
                    w_down, b_down.reshape(N_EXPERTS, 1, d))
    return _combine(dest_flat, ybuf, x1, gates, gt2, g_final.reshape(1, d))


def kernel(x, c, w_ada, b_ada, g_norm1, w_in, conv_w, a_log, dt_bias, g_onorm, b_fgate, w_o_delta, w_o_fox, w_out,
           g_norm2, w_router, b_router, w_gate_up, b_gate_up, w_down, b_down, g_final):
    b, s, d = x.shape
    assert b == 1 and d == D_MODEL and w_ada.shape[0] == 1
    h = x[0]
    for l in range(w_ada.shape[0]):
        mod = _modulation(c[0], w_ada[l], b_ada[l])
        h = _layer(h, mod, g_norm1[l], w_in[l], conv_w[l], a_log[l], dt_bias[l], g_onorm[l], b_fgate[l],
                   w_o_delta[l], w_o_fox[l], w_out[l], g_norm2[l], w_router[l], b_router[l], w_gate_up[l],
                   b_gate_up[l], w_down[l], b_down[l], g_final)
    return h[None]
```

```python
import functools

import jax
import jax.numpy as jnp
from jax import lax
from jax.experimental import pallas as pl
from jax.experimental.pallas import tpu as pltpu

F32 = jnp.float32
BF16 = jnp.bfloat16
I32 = jnp.int32

D_MODEL = 1024
HEAD_DIM = 128
N_HEADS = 4
MIX_W = N_HEADS * HEAD_DIM
QKV_W = 3 * MIX_W
CONV_K = 4
CHUNK = 64
STACK = N_HEADS * CHUNK
SUB = 16
N_EXPERTS = 32
TOP_K = 4
FF = D_MODEL
SWIGLU_LIMIT = 7.0
SWIGLU_ALPHA = 1.702
RMS_EPS = 1e-6
L2_EPS = 1e-6
N_MOD = 6
LANES = 128
NEG_BIG = -1e30
LOG2E = 1.4426950408889634

SUBLANES = 8
MOD_TILE = 1024
ROW_TILE = 512
GATES_TILE = 256
GATE_ROWS = 24
DELTA_TILE = 256
EXPERT_BLOCK = 512
ROUTE_TILE = 256
VMEM_LIMIT = 56 * 1024 * 1024


def _params(sem):
    return pltpu.CompilerParams(dimension_semantics=sem, vmem_limit_bytes=VMEM_LIMIT)


def _softplus(x):
    return jnp.maximum(x, 0.0) + jnp.log(1.0 + jnp.exp(-jnp.abs(x)))


def _sigmoid(x):
    return 1.0 / (1.0 + jnp.exp(-x))


def _silu(x):
    return x * _sigmoid(x)


def _dot(a, b):
    return jnp.dot(a, b, preferred_element_type=F32)


def _dot_nt(a, b):
    return lax.dot_general(a, b, (((1,), (1,)), ((), ())), preferred_element_type=F32)


def _split_bf16(x, terms):
    parts = []
    for _ in range(terms):
        p = x.astype(BF16)
        parts.append(p)
        x = x - p.astype(F32)
    return parts


def _dot_split(a, b):
    a_hi, a_lo = _split_bf16(a, 2)
    b_hi, b_lo = _split_bf16(b, 2)
    return _dot(a_hi, b_hi) + (_dot(a_hi, b_lo) + _dot(a_lo, b_hi))


def _dot_mask(mask01, x):
    return sum(_dot(mask01, p) for p in _split_bf16(x, 3))


TOKEN_TILE_ROWS = D_MODEL // LANES


def _store_token_tiles(ref, x):
    n = x.shape[0]
    for s in range(TOKEN_TILE_ROWS):
        ref[pl.ds(s, n, stride=TOKEN_TILE_ROWS), :] = x[:, s * LANES:(s + 1) * LANES]


def _load_token_slab(ref, s, n):
    return ref[pl.ds(s, n, stride=TOKEN_TILE_ROWS), :]


def _token_copy(src_ref, src_tok, dst_ref, dst_tok, sem):
    rows = lambda i: pl.ds(pl.multiple_of(i * TOKEN_TILE_ROWS, TOKEN_TILE_ROWS), TOKEN_TILE_ROWS)
    return pltpu.make_async_copy(src_ref.at[rows(src_tok), :], dst_ref.at[rows(dst_tok), :], sem)


def _mod_kernel(c_ref, w_ref, b_ref, o_ref):
    o_ref[...] = jnp.sum(c_ref[...] * w_ref[...], axis=0, keepdims=True) + b_ref[...]


def _modulation(c, w_ada, b_ada):
    d, n = w_ada.shape
    tn = MOD_TILE
    return pl.pallas_call(
        _mod_kernel,
        grid=(n // tn,),
        in_specs=[pl.BlockSpec((d, 1), lambda j: (0, 0)),
                  pl.BlockSpec((d, tn), lambda j: (0, j)),
                  pl.BlockSpec((1, tn), lambda j: (0, j))],
        out_specs=pl.BlockSpec((1, tn), lambda j: (0, j)),
        out_shape=jax.ShapeDtypeStruct((1, n), F32),
        compiler_params=_params(("arbitrary",)),
        name="mod",
    )(c.reshape(d, 1), w_ada, b_ada.reshape(1, n))


def _inproj_kernel(x_ref, g_ref, sc_ref, sh_ref, wq_ref, wz_ref, wf_ref, wga_ref, wgb_ref, ws_ref,
                   oq_ref, oz_ref, of_ref, oga_ref, ogb_ref, os_ref):
    x = x_ref[...]
    y = x * lax.rsqrt(jnp.mean(x * x, axis=-1, keepdims=True) + RMS_EPS) * g_ref[...]
    h = (y * (1.0 + sc_ref[...]) + sh_ref[...]).astype(BF16)
    oq_ref[...] = _dot_nt(h, wq_ref[...])
    oz_ref[...] = _dot_nt(h, wz_ref[...]).astype(BF16)
    of_ref[...] = _dot_nt(h, wf_ref[...]).astype(BF16)
    oga_ref[...] = _dot_nt(h, wga_ref[...]).astype(BF16)
    ogb_ref[...] = _dot_nt(h, wgb_ref[...]).astype(BF16)
    os_ref[...] = _dot_nt(h, ws_ref[...])


def _inproj(x, g1, sc1, sh1, wq, wz, wf, wga, wgb, ws, tm):
    t, d = x.shape
    row = lambda i: (i, 0)
    fixed = lambda i: (0, 0)
    ws_list = [wq, wz, wf, wga, wgb, ws]
    out_dt = [F32, BF16, BF16, BF16, BF16, F32]
    return pl.pallas_call(
        _inproj_kernel,
        grid=(t // tm,),
        in_specs=[pl.BlockSpec((tm, d), row)] + [pl.BlockSpec((1, d), fixed)] * 3
                 + [pl.BlockSpec(w.shape, fixed) for w in ws_list],
        out_specs=[pl.BlockSpec((tm, w.shape[0]), row) for w in ws_list],
        out_shape=[jax.ShapeDtypeStruct((t, w.shape[0]), dt) for w, dt in zip(ws_list, out_dt)],
        compiler_params=_params(("arbitrary",)),
        name="inproj",
    )(x, g1, sc1, sh1, *ws_list)


L_BETA, L_G, L_F, L_GC, L_GL = 0, 4, 8, 12, 16


def _gates_kernel(s_ref, alog_ref, dtb_ref, bf_ref, col_ref, row_ref, carry_ref):
    i = pl.program_id(0)
    tm = s_ref.shape[0]

    @pl.when(i == 0)
    def _():
        carry_ref[...] = jnp.zeros_like(carry_ref)

    s = s_ref[...]
    lane = lax.broadcasted_iota(I32, s.shape, 1)
    beta = _sigmoid(s)
    g = -jnp.exp(alog_ref[...]) * _softplus(s + dtb_ref[...])
    logf = -_softplus(-(s + bf_ref[...]))
    is_g = (lane >= L_G) & (lane < L_G + N_HEADS)
    is_f = (lane >= L_F) & (lane < L_F + N_HEADS)
    g = jnp.where(is_g, g, 0.0)
    logf = jnp.where(is_f, logf, 0.0)

    r = lax.broadcasted_iota(I32, (tm, tm), 0)
    c = lax.broadcasted_iota(I32, (tm, tm), 1)
    same_chunk = (r // CHUNK) == (c // CHUNK)
    tri = jnp.where(r >= c, 1.0, 0.0).astype(BF16)
    tri_chunk = jnp.where(same_chunk & (r >= c), 1.0, 0.0).astype(BF16)
    ones_chunk = jnp.where(same_chunk, 1.0, 0.0).astype(BF16)
    f_cum = _dot_mask(tri, logf) + carry_ref[...]
    carry_ref[...] = f_cum[tm - 1:tm, :]
    gc = _dot_mask(tri_chunk, g)
    gl = _dot_mask(ones_chunk, g)

    out = jnp.where(lane < N_HEADS, beta, 0.0) + g + f_cum
    out = out + pltpu.roll(gc, L_GC - L_G, 1) + pltpu.roll(gl, L_GL - L_G, 1)
    col_ref[...] = out
    row_ref[...] = out.T[:row_ref.shape[0], :]


def _gates(small, alog_row, dtb_row, bf_row, tm):
    t = small.shape[0]
    n_rows = GATE_ROWS
    return pl.pallas_call(
        _gates_kernel,
        grid=(t // tm,),
        in_specs=[pl.BlockSpec((tm, LANES), lambda i: (i, 0))] + [pl.BlockSpec((1, LANES), lambda i: (0, 0))] * 3,
        out_specs=[pl.BlockSpec((tm, LANES), lambda i: (i, 0)), pl.BlockSpec((n_rows, tm), lambda i: (0, i))],
        out_shape=[jax.ShapeDtypeStruct((t, LANES), F32), jax.ShapeDtypeStruct((n_rows, t), F32)],
        scratch_shapes=[pltpu.VMEM((1, LANES), F32)],
        compiler_params=_params(("arbitrary",)),
        name="gates",
    )(small, alog_row, dtb_row, bf_row)


def _stack_heads(a, col0):
    return jnp.concatenate([a[:, col0 + h * HEAD_DIM: col0 + (h + 1) * HEAD_DIM] for h in range(N_HEADS)], axis=0)


def _stack_cols(a, lane0):
    return jnp.concatenate([a[:, lane0 + h: lane0 + h + 1] for h in range(N_HEADS)], axis=0)


def _delta_kernel(qkv_ref, z_ref, gcol_ref, grow_ref, cw_ref, gon_ref, o_ref, ext_ref, tail_ref, s_ref):
    i = pl.program_id(0)
    tb = qkv_ref.shape[0]
    pad = tail_ref.shape[0]

    @pl.when(i == 0)
    def _():
        tail_ref[...] = jnp.zeros_like(tail_ref)
        s_ref[...] = jnp.zeros_like(s_ref)

    ext_ref[0:pad, :] = tail_ref[...]
    ext_ref[pad:pad + tb, :] = qkv_ref[...]
    tail_ref[...] = qkv_ref[tb - pad:tb, :]
    conv = cw_ref[0:1, :] * ext_ref[pad - 3:pad - 3 + tb, :]
    for j in range(1, CONV_K):
        conv = conv + cw_ref[j:j + 1, :] * ext_ref[pad - 3 + j:pad - 3 + j + tb, :]
    act = _silu(conv)

    r = lax.broadcasted_iota(I32, (STACK, STACK), 0)
    c = lax.broadcasted_iota(I32, (STACK, STACK), 1)
    same = (r // CHUNK) == (c // CHUNK)
    m_incl = same & (r >= c)
    m_strict = same & (r > c)
    eye = jnp.where(r == c, 1.0, 0.0)
    m_sub = []
    size = SUB
    while size <= CHUNK:
        m_sub.append((r // size) == (c // size))
        size *= 2
    rb = lax.broadcasted_iota(I32, (STACK, N_HEADS * HEAD_DIM), 0) // CHUNK
    cb = lax.broadcasted_iota(I32, (STACK, N_HEADS * HEAD_DIM), 1) // HEAD_DIM
    head_match = rb == cb

    chunks = range(tb // CHUNK)
    pre = []
    for ch in chunks:
        r0 = ch * CHUNK
        a = act[r0:r0 + CHUNK, :]
        q = _stack_heads(a, 0)
        k = _stack_heads(a, MIX_W)
        v = _stack_heads(a, 2 * MIX_W)
        q = q * lax.rsqrt(jnp.sum(q * q, axis=-1, keepdims=True) + L2_EPS) * (HEAD_DIM ** -0.5)
        k = k * lax.rsqrt(jnp.sum(k * k, axis=-1, keepdims=True) + L2_EPS)

        gcols = gcol_ref[r0:r0 + CHUNK, :]
        beta = _stack_cols(gcols, L_BETA)
        gc = _stack_cols(gcols, L_GC)
        gc_row = jnp.concatenate(
            [grow_ref[L_GC + h:L_GC + h + 1, r0:r0 + CHUNK] for h in range(N_HEADS)], axis=1)
        gl_row = jnp.concatenate(
            [grow_ref[L_GL + h:L_GL + h + 1, r0:r0 + CHUNK] for h in range(N_HEADS)], axis=1)

        decay = jnp.exp(jnp.where(m_incl, gc - gc_row, NEG_BIG))
        kb16 = k.astype(BF16)
        kk = _dot_nt(kb16, kb16)
        lmat = jnp.where(m_strict, kk * beta * decay, 0.0)
        attn = (_dot_nt(q.astype(BF16), kb16) * decay).astype(BF16)
        egc = jnp.exp(gc)
        rhs = jnp.concatenate([v * beta, k * (beta * egc)], axis=1).astype(BF16)
        kt_dec = (k.T * jnp.exp(gl_row - gc_row)).astype(BF16)
        pre.append(dict(lmat=lmat, attn=attn, rhs=rhs, kt_dec=kt_dec, q_dec=q * egc))

    l_d = [jnp.where(m_sub[0], p["lmat"], 0.0) for p in pre]
    inv = [eye - l for l in l_d]
    l16 = [l.astype(BF16) for l in l_d]
    pw = [_dot(l, l).astype(BF16) for l in l16]
    n_sq = SUB.bit_length() - 3
    for s in range(n_sq + 1):
        inv = [x + _dot(x.astype(BF16), p) for x, p in zip(inv, pw)]
        if s < n_sq:
            pw = [_dot(p, p).astype(BF16) for p in pw]
    for lvl in range(1, len(m_sub)):
        m_off = m_sub[lvl] & ~m_sub[lvl - 1]
        off = [jnp.where(m_off, p["lmat"], 0.0).astype(BF16) for p in pre]
        inv16 = [x.astype(BF16) for x in inv]
        half = [_dot(x, o).astype(BF16) for x, o in zip(inv16, off)]
        inv = [x - _dot(h, x16) for x, h, x16 in zip(inv, half, inv16)]
    uws = [_dot(x.astype(BF16), p["rhs"]) for x, p in zip(inv, pre)]

    for ch in chunks:
        r0 = ch * CHUNK
        attn, kt_dec = pre[ch]["attn"], pre[ch]["kt_dec"]
        u, w = uws[ch][:, :HEAD_DIM], uws[ch][:, HEAD_DIM:]

        state16 = s_ref[...].astype(BF16)
        wq_s = _dot(jnp.concatenate([w, pre[ch]["q_dec"]], axis=0).astype(BF16), state16)
        ws_d = jnp.concatenate([wq_s[h * CHUNK:(h + 1) * CHUNK, h * HEAD_DIM:(h + 1) * HEAD_DIM]
                                for h in range(N_HEADS)], axis=0)
        qs_d = jnp.concatenate([wq_s[STACK + h * CHUNK:STACK + (h + 1) * CHUNK, h * HEAD_DIM:(h + 1) * HEAD_DIM]
                                for h in range(N_HEADS)], axis=0)
        v_new = u - ws_d
        v16 = v_new.astype(BF16)
        o = qs_d + _dot(attn, v16)

        v_bd = jnp.where(head_match, jnp.concatenate([v_new] * N_HEADS, axis=1), 0.0).astype(BF16)
        upd = _dot(kt_dec, v_bd)
        for h in range(N_HEADS):
            e_h = jnp.exp(grow_ref[L_GL + h:L_GL + h + 1, r0:r0 + 1])
            sl = slice(h * HEAD_DIM, (h + 1) * HEAD_DIM)
            s_ref[:, sl] = s_ref[:, sl] * e_h + upd[:, sl]

        for h in range(N_HEADS):
            oh = o[h * CHUNK:(h + 1) * CHUNK, :]
            oh = oh * lax.rsqrt(jnp.mean(oh * oh, axis=-1, keepdims=True) + RMS_EPS) * gon_ref[...]
            zh = z_ref[r0:r0 + CHUNK, h * HEAD_DIM:(h + 1) * HEAD_DIM].astype(F32)
            o_ref[r0:r0 + CHUNK, h * HEAD_DIM:(h + 1) * HEAD_DIM] = (oh * _silu(zh)).astype(o_ref.dtype)


def _delta(qkv, z, gcol, grow, conv_w, g_onorm, tb):
    t = qkv.shape[0]
    pad = SUBLANES
    return pl.pallas_call(
        _delta_kernel,
        grid=(t // tb,),
        in_specs=[pl.BlockSpec((tb, QKV_W), lambda i: (i, 0)),
                  pl.BlockSpec((tb, MIX_W), lambda i: (i, 0)),
                  pl.BlockSpec((tb, LANES), lambda i: (i, 0)),
                  pl.BlockSpec((grow.shape[0], tb), lambda i: (0, i)),
                  pl.BlockSpec((CONV_K, QKV_W), lambda i: (0, 0)),
                  pl.BlockSpec((1, HEAD_DIM), lambda i: (0, 0))],
        out_specs=pl.BlockSpec((tb, MIX_W), lambda i: (i, 0)),
        out_shape=jax.ShapeDtypeStruct((t, MIX_W), BF16),
        scratch_shapes=[pltpu.VMEM((tb + pad, QKV_W), F32),
                        pltpu.VMEM((pad, QKV_W), F32),
                        pltpu.VMEM((HEAD_DIM, N_HEADS * HEAD_DIM), F32)],
        compiler_params=_params(("arbitrary",)),
        name="delta",
    )(qkv, z, gcol, grow, conv_w, g_onorm)


FOX_GROUP = 4
FOX_TQ, FOX_TK = 256, 2048
FOX_DIAG_PARTS = 2
FOX_ROW_STREAMS = 2
FOX_BOUND_SCALE, FOX_BOUND_SLACK = 1.01, 1.0
FOX_SKIP_LOG2 = -200.0


def _fox_kernel(*refs, tk):
    g = FOX_GROUP
    q_ref, o_ref, kmax_ref = refs[0], refs[-2], refs[-1]
    k_refs, v_refs, f_refs = (refs[1 + i * g:1 + (i + 1) * g] for i in range(3))
    qi = pl.program_id(1)
    tq = q_ref.shape[0]
    t_keys = k_refs[0].shape[0]

    @pl.when(qi == 0)
    def _():
        rows = min(t_keys, tk)
        for h in range(g):
            def norm_max(c, best, h=h):
                kc = k_refs[h][pl.ds(pl.multiple_of(c * rows, rows), rows), :].astype(F32)
                return jnp.maximum(best, jnp.sum(kc * kc, axis=-1, keepdims=True))

            best = lax.fori_loop(0, t_keys // rows, norm_max, jnp.zeros((rows, 1), F32))
            kmax_ref[h:h + 1, :] = jnp.broadcast_to(jnp.sqrt(jnp.max(best, axis=0, keepdims=True)), (1, LANES))

    def step(off, width, f_blk, carry, mask):
        ones = jnp.ones((width, HEAD_DIM), BF16)
        ss = []
        for h in range(g):
            q = q_ref[:, h * HEAD_DIM:(h + 1) * HEAD_DIM]
            s = _dot_nt(q, k_refs[h][pl.ds(off, width), :]) - f_blk(h) * LOG2E
            if mask is not None:
                s = jnp.where(mask, s, NEG_BIG)
            ss.append(s)
        ps, ms, alphas = [], [], []
        for h in range(g):
            m = carry[h][0]
            m_new = jnp.maximum(m, jnp.max(ss[h], axis=-1, keepdims=True))
            alphas.append(jnp.exp2(m - m_new))
            ps.append(jnp.exp2(ss[h] - m_new).astype(BF16))
            ms.append(m_new)
        out = []
        for h in range(g):
            v_aug = jnp.concatenate([v_refs[h][pl.ds(off, width), :], ones], axis=1)
            out.append((ms[h], alphas[h] * carry[h][1] + _dot(ps[h], v_aug)))
        return tuple(out)

    def head_chunk(h, j, mc):
        m, acc = mc
        off = pl.multiple_of(j * tk, tk)
        k_blk = k_refs[h][pl.ds(off, tk), :]
        v_aug = jnp.concatenate([v_refs[h][pl.ds(off, tk), :], jnp.ones((tk, HEAD_DIM), BF16)], axis=1)
        f_blk = f_refs[h][j] * LOG2E
        part = tq // FOX_ROW_STREAMS
        rows = [slice(i * part, (i + 1) * part) for i in range(FOX_ROW_STREAMS)]
        ss = [_dot_nt(q_ref[r, h * HEAD_DIM:(h + 1) * HEAD_DIM], k_blk) - f_blk for r in rows]
        m_new = [jnp.maximum(m[r], jnp.max(s, axis=-1, keepdims=True)) for r, s in zip(rows, ss)]
        ps = [jnp.exp2(s - mn).astype(BF16) for s, mn in zip(ss, m_new)]
        accs = [jnp.exp2(m[r] - mn) * acc[r] + _dot(p, v_aug) for r, mn, p in zip(rows, m_new, ps)]
        return jnp.concatenate(m_new, axis=0), jnp.concatenate(accs, axis=0)

    init = tuple((jnp.full((tq, 1), NEG_BIG, F32), jnp.zeros((tq, 2 * HEAD_DIM), F32)) for _ in range(g))
    n_full = (qi * tq) // tk

    parts = max(1, min(FOX_DIAG_PARTS, tk // tq))
    width = tk // parts
    row = lax.broadcasted_iota(I32, (tq, width), 0)
    col = lax.broadcasted_iota(I32, (tq, width), 1)
    carry = init
    for d in reversed(range(max(tq // tk, 1))):
        j = n_full + d
        for p in reversed(range(parts)):
            off = pl.multiple_of(j * tk + p * width, width)
            piece = functools.partial(step, off, width,
                                      lambda h, j=j, p=p: f_refs[h][j][:, p * width:(p + 1) * width],
                                      mask=col + (off - qi * tq) <= row)
            carry = piece(carry) if p == 0 else lax.cond(off < (qi + 1) * tq, piece, lambda c: c, carry)

    qn = []
    for h in range(g):
        q = q_ref[:, h * HEAD_DIM:(h + 1) * HEAD_DIM].astype(F32)
        qk_max = jnp.sqrt(jnp.sum(q * q, axis=-1, keepdims=True)) * kmax_ref[h:h + 1, 0:1]
        qn.append(qk_max * FOX_BOUND_SCALE + FOX_BOUND_SLACK)

    def reaches(j, carry):
        jc = jnp.maximum(j, 0)
        flags = []
        for h in range(g):
            bound = qn[h] - f_refs[h][jc][:, tk - 1:tk] * LOG2E
            flags.append((jnp.max(bound - carry[h][0]) >= FOX_SKIP_LOG2).astype(I32))
        return tuple(flags)

    def chunk_all_heads(state):
        j, _, carry = state
        carry = step(pl.multiple_of(j * tk, tk), tk, lambda h: f_refs[h][j], carry, None)
        return j - 1, reaches(j - 1, carry), carry

    def chunk_live_heads(state):
        j, live, carry = state
        carry = tuple(lax.cond(live[h] > 0, functools.partial(head_chunk, h, j), lambda mc: mc, carry[h])
                      for h in range(g))
        return j - 1, reaches(j - 1, carry), carry

    state = (n_full - 1, reaches(n_full - 1, carry), carry)
    state = lax.while_loop(lambda st: (st[0] >= 0) & (sum(st[1]) == g), chunk_all_heads, state)
    _, _, carry = lax.while_loop(lambda st: (st[0] >= 0) & (sum(st[1]) > 0), chunk_live_heads, state)
    for h in range(g):
        acc = carry[h][1]
        o_ref[:, h * HEAD_DIM:(h + 1) * HEAD_DIM] = (acc[:, :HEAD_DIM] / acc[:, HEAD_DIM:]).astype(o_ref.dtype)


def _fox(fx, f_rows, tq, tk):
    t = fx.shape[0]
    g = FOX_GROUP
    heads = lambda hg, u: hg * g + u
    q_specs = [pl.BlockSpec((tq, g * HEAD_DIM), lambda hg, i: (i, hg))]
    k_specs = [pl.BlockSpec((t, HEAD_DIM), lambda hg, i, u=u: (0, N_HEADS + heads(hg, u)),
                            pipeline_mode=pl.Buffered(1)) for u in range(g)]
    v_specs = [pl.BlockSpec((t, HEAD_DIM), lambda hg, i, u=u: (0, 2 * N_HEADS + heads(hg, u)),
                            pipeline_mode=pl.Buffered(1)) for u in range(g)]
    f_specs = [pl.BlockSpec((None, t // tk, 1, tk), lambda hg, i, u=u: (heads(hg, u), 0, 0, 0)) for u in range(g)]
    return pl.pallas_call(
        functools.partial(_fox_kernel, tk=tk),
        grid=(N_HEADS // g, t // tq),
        in_specs=q_specs + k_specs + v_specs + f_specs,
        out_specs=pl.BlockSpec((tq, g * HEAD_DIM), lambda hg, i: (i, hg)),
        out_shape=jax.ShapeDtypeStruct((t, MIX_W), BF16),
        scratch_shapes=[pltpu.VMEM((8, LANES), F32)],
        compiler_params=_params(("arbitrary", "arbitrary")),
        name="fox",
    )(*([fx] * (1 + 2 * g) + [f_rows] * g))


def _merge_kernel(oa_ref, ob_ref, ga_ref, gb_ref, x_ref, woa_ref, wob_ref, wout_ref, gt1_ref, g2_ref, sc2_ref,
                  sh2_ref, wr_ref, br_ref, x1_ref, h2_ref, ri_ref, gate_ref, cnt_ref, carry_ref):
    i = pl.program_id(0)
    tm = x_ref.shape[0]

    @pl.when(i == 0)
    def _():
        carry_ref[...] = jnp.zeros_like(carry_ref)

    ya = _dot(oa_ref[...], woa_ref[...])
    yb = _dot(ob_ref[...], wob_ref[...])
    merged = _sigmoid(ga_ref[...].astype(F32)) * ya + _sigmoid(gb_ref[...].astype(F32)) * yb
    x1 = x_ref[...] + gt1_ref[...] * _dot(merged.astype(BF16), wout_ref[...])
    x1_ref[...] = x1
    y = x1 * lax.rsqrt(jnp.mean(x1 * x1, axis=-1, keepdims=True) + RMS_EPS) * g2_ref[...]
    h2 = y * (1.0 + sc2_ref[...]) + sh2_ref[...]
    _store_token_tiles(h2_ref, h2)

    logits = _dot_split(h2, wr_ref[...]) + br_ref[...]
    lane = lax.broadcasted_iota(I32, logits.shape, 1)
    lanef = lane.astype(F32)
    cur = jnp.where(lane < N_EXPERTS, logits, -jnp.inf)
    vals, idxs = [], []
    for _ in range(TOP_K):
        m = jnp.max(cur, axis=-1, keepdims=True)
        ix = jnp.min(jnp.where(cur == m, lanef, float(LANES)), axis=-1, keepdims=True)
        vals.append(m)
        idxs.append(ix)
        cur = jnp.where(lanef == ix, -jnp.inf, cur)
    exps = [jnp.exp(v - vals[0]) for v in vals]
    denom = exps[0] + exps[1] + exps[2] + exps[3]

    onehot = jnp.zeros(logits.shape, F32)
    for ix in idxs:
        onehot = onehot + jnp.where(lanef == ix, 1.0, 0.0)
    r = lax.broadcasted_iota(I32, (tm, tm), 0)
    c = lax.broadcasted_iota(I32, (tm, tm), 1)
    strict = jnp.where(r > c, 1.0, 0.0).astype(BF16)
    before = _dot(strict, onehot.astype(BF16)) + carry_ref[...]
    carry_ref[...] = carry_ref[...] + jnp.sum(onehot, axis=0, keepdims=True)
    cnt_ref[...] = carry_ref[...].astype(I32)

    ri = jnp.zeros(logits.shape, F32)
    gates = jnp.zeros(logits.shape, F32)
    for kk in range(TOP_K):
        rank = jnp.sum(jnp.where(lanef == idxs[kk], before, 0.0), axis=-1, keepdims=True)
        ri = ri + jnp.where(lane == kk, idxs[kk], 0.0) + jnp.where(lane == TOP_K + kk, rank, 0.0)
        gates = gates + jnp.where(lane == kk, exps[kk] / denom, 0.0)
    ri_ref[...] = ri.astype(I32)
    gate_ref[...] = gates


def _merge(o_a, o_b, gate_a, gate_b, x, woa, wob, wout, gt1, g2, sc2, sh2, wr, br, tm):
    t, d = x.shape
    row = lambda i: (i, 0)
    fixed = lambda i: (0, 0)
    return pl.pallas_call(
        _merge_kernel,
        grid=(t // tm,),
        in_specs=[pl.BlockSpec((tm, MIX_W), row), pl.BlockSpec((tm, MIX_W), row),
                  pl.BlockSpec((tm, d), row), pl.BlockSpec((tm, d), row), pl.BlockSpec((tm, d), row),
                  pl.BlockSpec((MIX_W, d), fixed), pl.BlockSpec((MIX_W, d), fixed), pl.BlockSpec((d, d), fixed),
                  pl.BlockSpec((1, d), fixed), pl.BlockSpec((1, d), fixed), pl.BlockSpec((1, d), fixed),
                  pl.BlockSpec((1, d), fixed), pl.BlockSpec((d, LANES), fixed), pl.BlockSpec((1, LANES), fixed)],
        out_specs=[pl.BlockSpec((tm, d), row), pl.BlockSpec((tm * TOKEN_TILE_ROWS, LANES), row),
                   pl.BlockSpec((tm, LANES), row),
                   pl.BlockSpec((tm, LANES), row), pl.BlockSpec((1, LANES), fixed)],
        out_shape=[jax.ShapeDtypeStruct((t, d), F32), jax.ShapeDtypeStruct((t * TOKEN_TILE_ROWS, LANES), F32),
                   jax.ShapeDtypeStruct((t, LANES), I32), jax.ShapeDtypeStruct((t, LANES), F32),
                   jax.ShapeDtypeStruct((1, LANES), I32)],
        scratch_shapes=[pltpu.VMEM((1, LANES), F32)],
        compiler_params=_params(("arbitrary",)),
        name="merge",
    )(o_a, o_b, gate_a, gate_b, x, woa, wob, wout, gt1, g2, sc2, sh2, wr, br)


def _dest_kernel(ri_ref, cnt_ref, dest_ref, blk_ref, pend_ref):
    shift = EXPERT_BLOCK.bit_length() - 1
    cnt = jnp.broadcast_to(cnt_ref[...], (8, LANES))
    lane_row = lax.broadcasted_iota(I32, cnt.shape, 1)
    padded = jnp.where(lane_row < N_EXPERTS, ((cnt + (EXPERT_BLOCK - 1)) >> shift) << shift, 0)
    pend = padded
    s = 1
    while s < N_EXPERTS:
        pend = pend + jnp.where(lane_row >= s, pltpu.roll(pend, s, 1), 0)
        s *= 2
    pstart = (pend - padded)[0:1, :].astype(F32)
    pend = pend[0:1, :]
    pend_ref[...] = pend

    ri = ri_ref[...]
    lane = lax.broadcasted_iota(I32, ri.shape, 1)
    dest = jnp.zeros(ri.shape, F32)
    for kk in range(TOP_K):
        ix = ri[:, kk:kk + 1]
        rank = ri[:, TOP_K + kk:TOP_K + kk + 1].astype(F32)
        start = jnp.sum(jnp.where(lane == ix, pstart, 0.0), axis=-1, keepdims=True)
        dest = dest + jnp.where(lane == kk, start + rank, 0.0)
    dest_ref[...] = dest.astype(I32)

    bstart = lax.broadcasted_iota(I32, blk_ref.shape, 0) * EXPERT_BLOCK
    lane_b = lax.broadcasted_iota(I32, blk_ref.shape, 1)
    ended = jnp.where((lane_b < N_EXPERTS) & (pend <= bstart), 1.0, 0.0)
    e = jnp.minimum(jnp.sum(ended, axis=-1, keepdims=True), float(N_EXPERTS - 1))
    vend = pstart + cnt[0:1, :].astype(F32)
    vend_e = jnp.sum(jnp.where(lane_b.astype(F32) == e, vend, 0.0), axis=-1, keepdims=True)
    nvalid = jnp.clip(vend_e - bstart[:, 0:1].astype(F32), 0.0, float(EXPERT_BLOCK))
    blk_ref[...] = jnp.where(lane_b == 0, e, jnp.where(lane_b == 1, nvalid, 0.0)).astype(I32)


def _dest(ri, cnt, n_blocks, tm):
    t = ri.shape[0]
    nb_pad = -(-n_blocks // 8) * 8
    return pl.pallas_call(
        _dest_kernel,
        grid=(t // tm,),
        in_specs=[pl.BlockSpec((tm, LANES), lambda i: (i, 0)), pl.BlockSpec((1, LANES), lambda i: (0, 0))],
        out_specs=[pl.BlockSpec((tm, LANES), lambda i: (i, 0)), pl.BlockSpec((nb_pad, LANES), lambda i: (0, 0)),
                   pl.BlockSpec((1, LANES), lambda i: (0, 0))],
        out_shape=[jax.ShapeDtypeStruct((t, LANES), I32), jax.ShapeDtypeStruct((nb_pad, LANES), I32),
                   jax.ShapeDtypeStruct((1, LANES), I32)],
        compiler_params=_params(("arbitrary",)),
        name="dest",
    )(ri, cnt)


ROW_UNROLL = 8


def _drain_tokens(src_ref, dst_ref, sem, n_tokens):
    def drain(_, carry):
        for _ in range(ROW_UNROLL * TOP_K):
            _token_copy(src_ref, 0, dst_ref, 0, sem).wait()
        return carry

    lax.fori_loop(0, n_tokens // ROW_UNROLL, drain, 0)


def _dispatch_kernel(dest_ref, h_ref, xbuf_ref, sem):
    tm = h_ref.shape[0] // TOKEN_TILE_ROWS

    def issue(g, carry):
        for u in range(ROW_UNROLL):
            r = g * ROW_UNROLL + u
            for kk in range(TOP_K):
                _token_copy(h_ref, r, xbuf_ref, dest_ref[r * TOP_K + kk], sem).start(priority=kk % 2)
        return carry

    lax.fori_loop(0, tm // ROW_UNROLL, issue, 0)
    _drain_tokens(h_ref, xbuf_ref, sem, tm)


def _dispatch(dest_flat, h2_tiles, n_rows):
    tm = ROUTE_TILE
    t = h2_tiles.shape[0] // TOKEN_TILE_ROWS
    return pl.pallas_call(
        _dispatch_kernel,
        grid=(t // tm,),
        in_specs=[pl.BlockSpec((tm * TOP_K,), lambda i: (i,), memory_space=pltpu.SMEM),
                  pl.BlockSpec((tm * TOKEN_TILE_ROWS, LANES), lambda i: (i, 0))],
        out_specs=pl.BlockSpec(memory_space=pl.ANY),
        out_shape=jax.ShapeDtypeStruct((n_rows * TOKEN_TILE_ROWS, LANES), h2_tiles.dtype),
        scratch_shapes=[pltpu.SemaphoreType.DMA(())],
        compiler_params=_params(("arbitrary",)),
        name="dispatch",
    )(dest_flat, h2_tiles)


def _experts_kernel(blk_e_ref, nvalid_ref, grp_ref, nxt_e_ref, x_ref, wgu_hbm, bgu_ref, wd_hbm, bd_ref, y_ref,
                    wgu32_ref, wd32_ref, wgu16_ref, wd16_ref, sems):
    b = pl.program_id(0)
    nvalid = nvalid_ref[b]
    e = blk_e_ref[b]
    slot = grp_ref[b] % 2
    first = (b == 0) | (e != blk_e_ref[jnp.maximum(b - 1, 0)])

    def weight_copies(expert, s):
        return (pltpu.make_async_copy(wgu_hbm.at[expert], wgu32_ref.at[s], sems.at[s, 0]),
                pltpu.make_async_copy(wd_hbm.at[expert], wd32_ref.at[s], sems.at[s, 1]))

    @pl.when(b == 0)
    def _():
        for cp in weight_copies(e, 0):
            cp.start()

    @pl.when(first)
    def _():
        for cp in weight_copies(e, slot):
            cp.wait()
        wgu16_ref[...] = wgu32_ref[slot].astype(BF16)
        wd16_ref[...] = wd32_ref[slot].astype(BF16)

    @pl.when(first & (nxt_e_ref[b] != e))
    def _():
        for cp in weight_copies(nxt_e_ref[b], 1 - slot):
            cp.start()

    n_blk = x_ref.shape[0] // TOKEN_TILE_ROWS
    n_half = n_blk // 2

    def mlp(n):
        x = jnp.concatenate([_load_token_slab(x_ref, s, n) for s in range(TOKEN_TILE_ROWS)], axis=1)
        rows = lax.broadcasted_iota(I32, x.shape, 0)
        x = jnp.where(rows < nvalid, x, 0.0).astype(BF16)
        gu = _dot(x, wgu16_ref[...]) + bgu_ref[...]
        gate = jnp.minimum(gu[:, :FF], SWIGLU_LIMIT)
        up = jnp.clip(gu[:, FF:], -SWIGLU_LIMIT, SWIGLU_LIMIT)
        act = (up + 1.0) * (gate * _sigmoid(SWIGLU_ALPHA * gate))
        _store_token_tiles(y_ref, _dot(act.astype(BF16), wd16_ref[...]) + bd_ref[...])
        if n < n_blk:
            y_ref[n * TOKEN_TILE_ROWS:, :] = jnp.zeros(((n_blk - n) * TOKEN_TILE_ROWS, LANES), y_ref.dtype)

    @pl.when(nvalid > n_half)
    def _():
        mlp(n_blk)

    @pl.when((nvalid > 0) & (nvalid <= n_half))
    def _():
        mlp(n_half)

    @pl.when(nvalid <= 0)
    def _():
        y_ref[...] = jnp.zeros_like(y_ref)


def _experts(blk_e, nvalid, xbuf, wgu, bgu, wd, bd):
    d = D_MODEL
    blk_rows = EXPERT_BLOCK * TOKEN_TILE_ROWS
    nb = xbuf.shape[0] // blk_rows
    change = jnp.concatenate([jnp.zeros((1,), I32), (blk_e[1:] != blk_e[:-1]).astype(I32)])
    grp = jnp.cumsum(change)
    pos = jnp.where(change > 0, jnp.arange(nb, dtype=I32), nb)
    nxt_pos = lax.cummin(jnp.concatenate([pos[1:], jnp.full((1,), nb, I32)]), reverse=True)
    nxt_e = jnp.where(nxt_pos < nb, blk_e[jnp.minimum(nxt_pos, nb - 1)], blk_e)
    grid_spec = pltpu.PrefetchScalarGridSpec(
        num_scalar_prefetch=4,
        grid=(nb,),
        in_specs=[pl.BlockSpec((blk_rows, LANES), lambda b, e, *_: (b, 0)),
                  pl.BlockSpec(memory_space=pl.ANY),
                  pl.BlockSpec((None, 1, 2 * FF), lambda b, e, *_: (e[b], 0, 0)),
                  pl.BlockSpec(memory_space=pl.ANY),
                  pl.BlockSpec((None, 1, d), lambda b, e, *_: (e[b], 0, 0))],
        out_specs=pl.BlockSpec((blk_rows, LANES), lambda b, e, *_: (b, 0)),
        scratch_shapes=[pltpu.VMEM((2, d, 2 * FF), F32), pltpu.VMEM((2, FF, d), F32),
                        pltpu.VMEM((d, 2 * FF), BF16), pltpu.VMEM((FF, d), BF16),
                        pltpu.SemaphoreType.DMA((2, 2))],
    )
    return pl.pallas_call(
        _experts_kernel,
        grid_spec=grid_spec,
        out_shape=jax.ShapeDtypeStruct(xbuf.shape, F32),
        compiler_params=_params(("arbitrary",)),
        name="experts",
    )(blk_e, nvalid, grp, nxt_e, xbuf, wgu, bgu, wd, bd)


def _combine_kernel(dest_ref, dest_next_ref, ybuf_ref, x1_ref, gate_ref, gt2_ref, gf_ref, o_ref, rows_ref, sems):
    i = pl.program_id(0)
    tm = x1_ref.shape[0]
    slot = i % 2

    def start_gather(idx_ref, s):
        def issue(g, carry):
            for u in range(ROW_UNROLL):
                r = g * ROW_UNROLL + u
                for kk in range(TOP_K):
                    _token_copy(ybuf_ref, idx_ref[r * TOP_K + kk], rows_ref.at[s, kk], r,
                                sems.at[s]).start(priority=kk % 2)
            return carry

        lax.fori_loop(0, tm // ROW_UNROLL, issue, 0)

    @pl.when(i == 0)
    def _():
        start_gather(dest_ref, 0)

    @pl.when(i + 1 < pl.num_programs(0))
    def _():
        start_gather(dest_next_ref, 1 - slot)

    _drain_tokens(ybuf_ref, rows_ref.at[slot, 0], sems.at[slot], tm)

    gates = gate_ref[...]
    slabs = []
    for s in range(TOKEN_TILE_ROWS):
        moe = gates[:, 0:1] * _load_token_slab(rows_ref.at[slot, 0], s, tm)
        for kk in range(1, TOP_K):
            moe = moe + gates[:, kk:kk + 1] * _load_token_slab(rows_ref.at[slot, kk], s, tm)
        slabs.append(moe)
    xo = x1_ref[...] + gt2_ref[...] * jnp.concatenate(slabs, axis=1)
    o_ref[...] = xo * lax.rsqrt(jnp.mean(xo * xo, axis=-1, keepdims=True) + RMS_EPS) * gf_ref[...]


def _combine(dest_flat, ybuf, x1, gates, gt2, g_final):
    t, d = x1.shape
    tm = ROUTE_TILE
    n = t // tm
    return pl.pallas_call(
        _combine_kernel,
        grid=(n,),
        in_specs=[pl.BlockSpec((tm * TOP_K,), lambda i: (i,), memory_space=pltpu.SMEM),
                  pl.BlockSpec((tm * TOP_K,), lambda i: (jnp.minimum(i + 1, n - 1),), memory_space=pltpu.SMEM),
                  pl.BlockSpec(memory_space=pl.ANY),
                  pl.BlockSpec((tm, d), lambda i: (i, 0)),
                  pl.BlockSpec((tm, LANES), lambda i: (i, 0)),
                  pl.BlockSpec((1, d), lambda i: (0, 0)),
                  pl.BlockSpec((1, d), lambda i: (0, 0))],
        out_specs=pl.BlockSpec((tm, d), lambda i: (i, 0)),
        out_shape=jax.ShapeDtypeStruct((t, d), F32),
        scratch_shapes=[pltpu.VMEM((2, TOP_K, tm * TOKEN_TILE_ROWS, LANES), F32), pltpu.SemaphoreType.DMA((2,))],
        compiler_params=_params(("arbitrary",)),
        name="combine",
    )(dest_flat, dest_flat, ybuf, x1, gates, gt2, g_final)


def _pad_lanes(v, lane0):
    return jnp.zeros((1, LANES), F32).at[0, lane0:lane0 + v.shape[0]].set(v.astype(F32))


def _layer(x, mod, g_norm1, w_in, conv_w, a_log, dt_bias, g_onorm, b_fgate, w_o_delta, w_o_fox, w_out,
           g_norm2, w_router, b_router, w_gate_up, b_gate_up, w_down, b_down, g_final):
    t, d = x.shape
    sh1, sc1, gt1, sh2, sc2, gt2 = [mod[:, i * d:(i + 1) * d] for i in range(N_MOD)]

    w_t = w_in.T
    o = 0
    wq = w_t[o:o + QKV_W]; o += QKV_W
    wz = w_t[o:o + MIX_W]; o += MIX_W
    w_beta = w_t[o:o + N_HEADS]; o += N_HEADS
    w_dec = w_t[o:o + N_HEADS]; o += N_HEADS
    wf = w_t[o:o + QKV_W]; o += QKV_W
    wf = jnp.concatenate([wf[:MIX_W] * (LOG2E * HEAD_DIM ** -0.5), wf[MIX_W:]], axis=0)
    w_fg = w_t[o:o + N_HEADS]; o += N_HEADS
    wga = w_t[o:o + d]; o += d
    wgb = w_t[o:o + d]
    ws = jnp.zeros((LANES, d), F32)
    ws = ws.at[L_BETA:L_BETA + N_HEADS].set(w_beta).at[L_G:L_G + N_HEADS].set(w_dec)
    ws = ws.at[L_F:L_F + N_HEADS].set(w_fg)
    bf = lambda w: w.astype(BF16)

    tm = min(ROW_TILE, t)
    qkv, z, fx, gate_a, gate_b, small = _inproj(
        x, g_norm1.reshape(1, d), sc1, sh1, bf(wq), bf(wz), bf(wf), bf(wga), bf(wgb), bf(ws), tm)

    gcol, grow = _gates(small, _pad_lanes(a_log, L_G), _pad_lanes(dt_bias, L_G), _pad_lanes(b_fgate, L_F),
                        min(GATES_TILE, t))

    o_a = _delta(qkv, z, gcol, grow, conv_w, g_onorm.reshape(1, HEAD_DIM), min(DELTA_TILE, t))
    tq, tk = min(FOX_TQ, t), min(FOX_TK, t)
    f_rows = grow[L_F:L_F + N_HEADS].reshape(N_HEADS, t // tk, 1, tk)
    o_b = _fox(fx, f_rows, tq, tk)

    wr = jnp.zeros((d, LANES), F32).at[:, :N_EXPERTS].set(w_router)
    br = _pad_lanes(b_router, 0)
    x1, h2, ri, gates, cnt = _merge(o_a, o_b, gate_a, gate_b, x, bf(w_o_delta), bf(w_o_fox), bf(w_out), gt1,
                                    g_norm2.reshape(1, d), sc2, sh2, wr, br, tm)

    n_blocks = (t * TOP_K) // EXPERT_BLOCK + N_EXPERTS
    dest, blk, _ = _dest(ri, cnt, n_blocks, min(2048, t))
    dest_flat = dest[:, :TOP_K].reshape(t * TOP_K)
    blk_e, nvalid = blk[:n_blocks, 0], blk[:n_blocks, 1]

    xbuf = _dispatch(dest_flat, h2, n_blocks * EXPERT_BLOCK)
    ybuf = _experts(blk_e, nvalid, xbuf, w_gate_up, b_gate_up.reshape(N_EXPERTS, 1, 2 * FF),
                    w_down, b_down.reshape(N_EXPERTS, 1, d))
    return _combine(dest_flat, ybuf, x1, gates, gt2, g_final.reshape(1, d))


def kernel(x, c, w_ada, b_ada, g_norm1, w_in, conv_w, a_log, dt_bias, g_onorm, b_fgate, w_o_delta, w_o_fox, w_out,
           g_norm2, w_router, b_router, w_gate_up, b_gate_up, w_down, b_down, g_final):
    b, s, d = x.shape
    assert b == 1 and d == D_MODEL and w_ada.shape[0] == 1
    h = x[0]
    for l in range(w_ada.shape[0]):
        mod = _modulation(c[0], w_ada[l], b_ada[l])
        h = _layer(h, mod, g_norm1[l], w_in[l], conv_w[l], a_log[l], dt_bias[l], g_onorm[l], b_fgate[l],
                   w_o_delta[l], w_o_fox[l], w_out[l], g_norm2[l], w_router[l], b_router[l], w_gate_up[l],
                   b_gate_up[l], w_down[l], b_down[l], g_final)
    return h[None]
```

```python
import functools

import jax
import jax.numpy as jnp
from jax import lax
from jax.experimental import pallas as pl
from jax.experimental.pallas import tpu as pltpu

F32 = jnp.float32
BF16 = jnp.bfloat16
I32 = jnp.int32

D_MODEL = 1024
HEAD_DIM = 128
N_HEADS = 4
MIX_W = N_HEADS * HEAD_DIM
QKV_W = 3 * MIX_W
CONV_K = 4
CHUNK = 64
STACK = N_HEADS * CHUNK
SUB = 16
N_EXPERTS = 32
TOP_K = 4
FF = D_MODEL
SWIGLU_LIMIT = 7.0
SWIGLU_ALPHA = 1.702
RMS_EPS = 1e-6
L2_EPS = 1e-6
N_MOD = 6
LANES = 128
NEG_BIG = -1e30
LOG2E = 1.4426950408889634

SUBLANES = 8
MOD_TILE = 1024
ROW_TILE = 512
GATES_TILE = 256
GATE_ROWS = 24
DELTA_TILE = 256
EXPERT_BLOCK = 512
ROUTE_TILE = 256
VMEM_LIMIT = 56 * 1024 * 1024


def _params(sem):
    return pltpu.CompilerParams(dimension_semantics=sem, vmem_limit_bytes=VMEM_LIMIT)


def _softplus(x):
    return jnp.maximum(x, 0.0) + jnp.log(1.0 + jnp.exp(-jnp.abs(x)))


def _sigmoid(x):
    return 1.0 / (1.0 + jnp.exp(-x))


def _silu(x):
    return x * _sigmoid(x)


def _dot(a, b):
    return jnp.dot(a, b, preferred_element_type=F32)


def _dot_nt(a, b):
    return lax.dot_general(a, b, (((1,), (1,)), ((), ())), preferred_element_type=F32)


def _split_bf16(x, terms):
    parts = []
    for _ in range(terms):
        p = x.astype(BF16)
        parts.append(p)
        x = x - p.astype(F32)
    return parts


def _dot_split(a, b):
    a_hi, a_lo = _split_bf16(a, 2)
    b_hi, b_lo = _split_bf16(b, 2)
    return _dot(a_hi, b_hi) + (_dot(a_hi, b_lo) + _dot(a_lo, b_hi))


def _dot_mask(mask01, x):
    return sum(_dot(mask01, p) for p in _split_bf16(x, 3))


TOKEN_TILE_ROWS = D_MODEL // LANES


def _store_token_tiles(ref, x):
    n = x.shape[0]
    for s in range(TOKEN_TILE_ROWS):
        ref[pl.ds(s, n, stride=TOKEN_TILE_ROWS), :] = x[:, s * LANES:(s + 1) * LANES]


def _load_token_slab(ref, s, n):
    return ref[pl.ds(s, n, stride=TOKEN_TILE_ROWS), :]


def _token_copy(src_ref, src_tok, dst_ref, dst_tok, sem):
    rows = lambda i: pl.ds(pl.multiple_of(i * TOKEN_TILE_ROWS, TOKEN_TILE_ROWS), TOKEN_TILE_ROWS)
    return pltpu.make_async_copy(src_ref.at[rows(src_tok), :], dst_ref.at[rows(dst_tok), :], sem)


def _mod_kernel(c_ref, w_ref, b_ref, o_ref):
    o_ref[...] = jnp.sum(c_ref[...] * w_ref[...], axis=0, keepdims=True) + b_ref[...]


def _modulation(c, w_ada, b_ada):
    d, n = w_ada.shape
    tn = MOD_TILE
    return pl.pallas_call(
        _mod_kernel,
        grid=(n // tn,),
        in_specs=[pl.BlockSpec((d, 1), lambda j: (0, 0)),
                  pl.BlockSpec((d, tn), lambda j: (0, j)),
                  pl.BlockSpec((1, tn), lambda j: (0, j))],
        out_specs=pl.BlockSpec((1, tn), lambda j: (0, j)),
        out_shape=jax.ShapeDtypeStruct((1, n), F32),
        compiler_params=_params(("arbitrary",)),
        name="mod",
    )(c.reshape(d, 1), w_ada, b_ada.reshape(1, n))


def _inproj_kernel(x_ref, g_ref, sc_ref, sh_ref, wq_ref, wz_ref, wf_ref, wga_ref, wgb_ref, ws_ref,
                   oq_ref, oz_ref, of_ref, oga_ref, ogb_ref, os_ref):
    x = x_ref[...]
    y = x * lax.rsqrt(jnp.mean(x * x, axis=-1, keepdims=True) + RMS_EPS) * g_ref[...]
    h = (y * (1.0 + sc_ref[...]) + sh_ref[...]).astype(BF16)
    oq_ref[...] = _dot_nt(h, wq_ref[...])
    oz_ref[...] = _dot_nt(h, wz_ref[...]).astype(BF16)
    of_ref[...] = _dot_nt(h, wf_ref[...]).astype(BF16)
    oga_ref[...] = _dot_nt(h, wga_ref[...]).astype(BF16)
    ogb_ref[...] = _dot_nt(h, wgb_ref[...]).astype(BF16)
    os_ref[...] = _dot_nt(h, ws_ref[...])


def _inproj(x, g1, sc1, sh1, wq, wz, wf, wga, wgb, ws, tm):
    t, d = x.shape
    row = lambda i: (i, 0)
    fixed = lambda i: (0, 0)
    ws_list = [wq, wz, wf, wga, wgb, ws]
    out_dt = [F32, BF16, BF16, BF16, BF16, F32]
    return pl.pallas_call(
        _inproj_kernel,
        grid=(t // tm,),
        in_specs=[pl.BlockSpec((tm, d), row)] + [pl.BlockSpec((1, d), fixed)] * 3
                 + [pl.BlockSpec(w.shape, fixed) for w in ws_list],
        out_specs=[pl.BlockSpec((tm, w.shape[0]), row) for w in ws_list],
        out_shape=[jax.ShapeDtypeStruct((t, w.shape[0]), dt) for w, dt in zip(ws_list, out_dt)],
        compiler_params=_params(("arbitrary",)),
        name="inproj",
    )(x, g1, sc1, sh1, *ws_list)


L_BETA, L_G, L_F, L_GC, L_GL = 0, 4, 8, 12, 16


def _gates_kernel(s_ref, alog_ref, dtb_ref, bf_ref, col_ref, row_ref, carry_ref):
    i = pl.program_id(0)
    tm = s_ref.shape[0]

    @pl.when(i == 0)
    def _():
        carry_ref[...] = jnp.zeros_like(carry_ref)

    s = s_ref[...]
    lane = lax.broadcasted_iota(I32, s.shape, 1)
    beta = _sigmoid(s)
    g = -jnp.exp(alog_ref[...]) * _softplus(s + dtb_ref[...])
    logf = -_softplus(-(s + bf_ref[...]))
    is_g = (lane >= L_G) & (lane < L_G + N_HEADS)
    is_f = (lane >= L_F) & (lane < L_F + N_HEADS)
    g = jnp.where(is_g, g, 0.0)
    logf = jnp.where(is_f, logf, 0.0)

    r = lax.broadcasted_iota(I32, (tm, tm), 0)
    c = lax.broadcasted_iota(I32, (tm, tm), 1)
    same_chunk = (r // CHUNK) == (c // CHUNK)
    tri = jnp.where(r >= c, 1.0, 0.0).astype(BF16)
    tri_chunk = jnp.where(same_chunk & (r >= c), 1.0, 0.0).astype(BF16)
    ones_chunk = jnp.where(same_chunk, 1.0, 0.0).astype(BF16)
    f_cum = _dot_mask(tri, logf) + carry_ref[...]
    carry_ref[...] = f_cum[tm - 1:tm, :]
    gc = _dot_mask(tri_chunk, g)
    gl = _dot_mask(ones_chunk, g)

    out = jnp.where(lane < N_HEADS, beta, 0.0) + g + f_cum
    out = out + pltpu.roll(gc, L_GC - L_G, 1) + pltpu.roll(gl, L_GL - L_G, 1)
    col_ref[...] = out
    row_ref[...] = out.T[:row_ref.shape[0], :]


def _gates(small, alog_row, dtb_row, bf_row, tm):
    t = small.shape[0]
    n_rows = GATE_ROWS
    return pl.pallas_call(
        _gates_kernel,
        grid=(t // tm,),
        in_specs=[pl.BlockSpec((tm, LANES), lambda i: (i, 0))] + [pl.BlockSpec((1, LANES), lambda i: (0, 0))] * 3,
        out_specs=[pl.BlockSpec((tm, LANES), lambda i: (i, 0)), pl.BlockSpec((n_rows, tm), lambda i: (0, i))],
        out_shape=[jax.ShapeDtypeStruct((t, LANES), F32), jax.ShapeDtypeStruct((n_rows, t), F32)],
        scratch_shapes=[pltpu.VMEM((1, LANES), F32)],
        compiler_params=_params(("arbitrary",)),
        name="gates",
    )(small, alog_row, dtb_row, bf_row)


def _stack_heads(a, col0):
    return jnp.concatenate([a[:, col0 + h * HEAD_DIM: col0 + (h + 1) * HEAD_DIM] for h in range(N_HEADS)], axis=0)


def _stack_cols(a, lane0):
    return jnp.concatenate([a[:, lane0 + h: lane0 + h + 1] for h in range(N_HEADS)], axis=0)


def _delta_kernel(qkv_ref, z_ref, gcol_ref, grow_ref, cw_ref, gon_ref, o_ref, ext_ref, tail_ref, s_ref):
    i = pl.program_id(0)
    tb = qkv_ref.shape[0]
    pad = tail_ref.shape[0]

    @pl.when(i == 0)
    def _():
        tail_ref[...] = jnp.zeros_like(tail_ref)
        s_ref[...] = jnp.zeros_like(s_ref)

    ext_ref[0:pad, :] = tail_ref[...]
    ext_ref[pad:pad + tb, :] = qkv_ref[...]
    tail_ref[...] = qkv_ref[tb - pad:tb, :]
    conv = cw_ref[0:1, :] * ext_ref[pad - 3:pad - 3 + tb, :]
    for j in range(1, CONV_K):
        conv = conv + cw_ref[j:j + 1, :] * ext_ref[pad - 3 + j:pad - 3 + j + tb, :]
    act = _silu(conv)

    r = lax.broadcasted_iota(I32, (STACK, STACK), 0)
    c = lax.broadcasted_iota(I32, (STACK, STACK), 1)
    same = (r // CHUNK) == (c // CHUNK)
    m_incl = same & (r >= c)
    m_strict = same & (r > c)
    eye = jnp.where(r == c, 1.0, 0.0)
    m_sub = []
    size = SUB
    while size <= CHUNK:
        m_sub.append((r // size) == (c // size))
        size *= 2
    rb = lax.broadcasted_iota(I32, (STACK, N_HEADS * HEAD_DIM), 0) // CHUNK
    cb = lax.broadcasted_iota(I32, (STACK, N_HEADS * HEAD_DIM), 1) // HEAD_DIM
    head_match = rb == cb

    chunks = range(tb // CHUNK)
    pre = []
    for ch in chunks:
        r0 = ch * CHUNK
        a = act[r0:r0 + CHUNK, :]
        q = _stack_heads(a, 0)
        k = _stack_heads(a, MIX_W)
        v = _stack_heads(a, 2 * MIX_W)
        q = q * lax.rsqrt(jnp.sum(q * q, axis=-1, keepdims=True) + L2_EPS) * (HEAD_DIM ** -0.5)
        k = k * lax.rsqrt(jnp.sum(k * k, axis=-1, keepdims=True) + L2_EPS)

        gcols = gcol_ref[r0:r0 + CHUNK, :]
        beta = _stack_cols(gcols, L_BETA)
        gc = _stack_cols(gcols, L_GC)
        gc_row = jnp.concatenate(
            [grow_ref[L_GC + h:L_GC + h + 1, r0:r0 + CHUNK] for h in range(N_HEADS)], axis=1)
        gl_row = jnp.concatenate(
            [grow_ref[L_GL + h:L_GL + h + 1, r0:r0 + CHUNK] for h in range(N_HEADS)], axis=1)

        decay = jnp.exp(jnp.where(m_incl, gc - gc_row, NEG_BIG))
        kb16 = k.astype(BF16)
        kk = _dot_nt(kb16, kb16)
        lmat = jnp.where(m_strict, kk * beta * decay, 0.0)
        attn = (_dot_nt(q.astype(BF16), kb16) * decay).astype(BF16)
        egc = jnp.exp(gc)
        rhs = jnp.concatenate([v * beta, k * (beta * egc)], axis=1).astype(BF16)
        kt_dec = (k.T * jnp.exp(gl_row - gc_row)).astype(BF16)
        pre.append(dict(lmat=lmat, attn=attn, rhs=rhs, kt_dec=kt_dec, q_dec=q * egc))

    l_d = [jnp.where(m_sub[0], p["lmat"], 0.0) for p in pre]
    inv = [eye - l for l in l_d]
    l16 = [l.astype(BF16) for l in l_d]
    pw = [_dot(l, l).astype(BF16) for l in l16]
    n_sq = SUB.bit_length() - 3
    for s in range(n_sq + 1):
        inv = [x + _dot(x.astype(BF16), p) for x, p in zip(inv, pw)]
        if s < n_sq:
            pw = [_dot(p, p).astype(BF16) for p in pw]
    for lvl in range(1, len(m_sub)):
        m_off = m_sub[lvl] & ~m_sub[lvl - 1]
        off = [jnp.where(m_off, p["lmat"], 0.0).astype(BF16) for p in pre]
        inv16 = [x.astype(BF16) for x in inv]
        half = [_dot(x, o).astype(BF16) for x, o in zip(inv16, off)]
        inv = [x - _dot(h, x16) for x, h, x16 in zip(inv, half, inv16)]
    uws = [_dot(x.astype(BF16), p["rhs"]) for x, p in zip(inv, pre)]

    for ch in chunks:
        r0 = ch * CHUNK
        attn, kt_dec = pre[ch]["attn"], pre[ch]["kt_dec"]
        u, w = uws[ch][:, :HEAD_DIM], uws[ch][:, HEAD_DIM:]

        state16 = s_ref[...].astype(BF16)
        wq_s = _dot(jnp.concatenate([w, pre[ch]["q_dec"]], axis=0).astype(BF16), state16)
        ws_d = jnp.concatenate([wq_s[h * CHUNK:(h + 1) * CHUNK, h * HEAD_DIM:(h + 1) * HEAD_DIM]
                                for h in range(N_HEADS)], axis=0)
        qs_d = jnp.concatenate([wq_s[STACK + h * CHUNK:STACK + (h + 1) * CHUNK, h * HEAD_DIM:(h + 1) * HEAD_DIM]
                                for h in range(N_HEADS)], axis=0)
        v_new = u - ws_d
        v16 = v_new.astype(BF16)
        o = qs_d + _dot(attn, v16)

        v_bd = jnp.where(head_match, jnp.concatenate([v_new] * N_HEADS, axis=1), 0.0).astype(BF16)
        upd = _dot(kt_dec, v_bd)
        for h in range(N_HEADS):
            e_h = jnp.exp(grow_ref[L_GL + h:L_GL + h + 1, r0:r0 + 1])
            sl = slice(h * HEAD_DIM, (h + 1) * HEAD_DIM)
            s_ref[:, sl] = s_ref[:, sl] * e_h + upd[:, sl]

        for h in range(N_HEADS):
            oh = o[h * CHUNK:(h + 1) * CHUNK, :]
            oh = oh * lax.rsqrt(jnp.mean(oh * oh, axis=-1, keepdims=True) + RMS_EPS) * gon_ref[...]
            zh = z_ref[r0:r0 + CHUNK, h * HEAD_DIM:(h + 1) * HEAD_DIM].astype(F32)
            o_ref[r0:r0 + CHUNK, h * HEAD_DIM:(h + 1) * HEAD_DIM] = (oh * _silu(zh)).astype(o_ref.dtype)


def _delta(qkv, z, gcol, grow, conv_w, g_onorm, tb):
    t = qkv.shape[0]
    pad = SUBLANES
    return pl.pallas_call(
        _delta_kernel,
        grid=(t // tb,),
        in_specs=[pl.BlockSpec((tb, QKV_W), lambda i: (i, 0)),
                  pl.BlockSpec((tb, MIX_W), lambda i: (i, 0)),
                  pl.BlockSpec((tb, LANES), lambda i: (i, 0)),
                  pl.BlockSpec((grow.shape[0], tb), lambda i: (0, i)),
                  pl.BlockSpec((CONV_K, QKV_W), lambda i: (0, 0)),
                  pl.BlockSpec((1, HEAD_DIM), lambda i: (0, 0))],
        out_specs=pl.BlockSpec((tb, MIX_W), lambda i: (i, 0)),
        out_shape=jax.ShapeDtypeStruct((t, MIX_W), BF16),
        scratch_shapes=[pltpu.VMEM((tb + pad, QKV_W), F32),
                        pltpu.VMEM((pad, QKV_W), F32),
                        pltpu.VMEM((HEAD_DIM, N_HEADS * HEAD_DIM), F32)],
        compiler_params=_params(("arbitrary",)),
        name="delta",
    )(qkv, z, gcol, grow, conv_w, g_onorm)


FOX_GROUP = 4
FOX_TQ, FOX_TK = 256, 2048
FOX_DIAG_PARTS = 4
FOX_ROW_STREAMS = 2
FOX_BOUND_SCALE, FOX_BOUND_SLACK = 1.01, 1.0
FOX_SKIP_LOG2 = -200.0


def _fox_kernel(*refs, tk):
    g = FOX_GROUP
    q_ref, o_ref, kmax_ref = refs[0], refs[-2], refs[-1]
    k_refs, v_refs, f_refs = (refs[1 + i * g:1 + (i + 1) * g] for i in range(3))
    qi = pl.program_id(1)
    tq = q_ref.shape[0]
    t_keys = k_refs[0].shape[0]

    @pl.when(qi == 0)
    def _():
        rows = min(t_keys, tk)
        for h in range(g):
            def norm_max(c, best, h=h):
                kc = k_refs[h][pl.ds(pl.multiple_of(c * rows, rows), rows), :].astype(F32)
                return jnp.maximum(best, jnp.sum(kc * kc, axis=-1, keepdims=True))

            best = lax.fori_loop(0, t_keys // rows, norm_max, jnp.zeros((rows, 1), F32))
            kmax_ref[h:h + 1, :] = jnp.broadcast_to(jnp.sqrt(jnp.max(best, axis=0, keepdims=True)), (1, LANES))

    def step(off, width, f_blk, carry, mask):
        ones = jnp.ones((width, HEAD_DIM), BF16)
        ss = []
        for h in range(g):
            q = q_ref[:, h * HEAD_DIM:(h + 1) * HEAD_DIM]
            s = _dot_nt(q, k_refs[h][pl.ds(off, width), :]) - f_blk(h) * LOG2E
            if mask is not None:
                s = jnp.where(mask, s, NEG_BIG)
            ss.append(s)
        ps, ms, alphas = [], [], []
        for h in range(g):
            m = carry[h][0]
            m_new = jnp.maximum(m, jnp.max(ss[h], axis=-1, keepdims=True))
            alphas.append(jnp.exp2(m - m_new))
            ps.append(jnp.exp2(ss[h] - m_new).astype(BF16))
            ms.append(m_new)
        out = []
        for h in range(g):
            v_aug = jnp.concatenate([v_refs[h][pl.ds(off, width), :], ones], axis=1)
            out.append((ms[h], alphas[h] * carry[h][1] + _dot(ps[h], v_aug)))
        return tuple(out)

    def head_chunk(h, j, mc):
        m, acc = mc
        off = pl.multiple_of(j * tk, tk)
        k_blk = k_refs[h][pl.ds(off, tk), :]
        v_aug = jnp.concatenate([v_refs[h][pl.ds(off, tk), :], jnp.ones((tk, HEAD_DIM), BF16)], axis=1)
        f_blk = f_refs[h][j] * LOG2E
        part = tq // FOX_ROW_STREAMS
        rows = [slice(i * part, (i + 1) * part) for i in range(FOX_ROW_STREAMS)]
        ss = [_dot_nt(q_ref[r, h * HEAD_DIM:(h + 1) * HEAD_DIM], k_blk) - f_blk for r in rows]
        m_new = [jnp.maximum(m[r], jnp.max(s, axis=-1, keepdims=True)) for r, s in zip(rows, ss)]
        ps = [jnp.exp2(s - mn).astype(BF16) for s, mn in zip(ss, m_new)]
        accs = [jnp.exp2(m[r] - mn) * acc[r] + _dot(p, v_aug) for r, mn, p in zip(rows, m_new, ps)]
        return jnp.concatenate(m_new, axis=0), jnp.concatenate(accs, axis=0)

    init = tuple((jnp.full((tq, 1), NEG_BIG, F32), jnp.zeros((tq, 2 * HEAD_DIM), F32)) for _ in range(g))
    n_full = (qi * tq) // tk

    parts = max(1, min(FOX_DIAG_PARTS, tk // tq))
    width = tk // parts
    row = lax.broadcasted_iota(I32, (tq, width), 0)
    col = lax.broadcasted_iota(I32, (tq, width), 1)
    carry = init
    for d in reversed(range(max(tq // tk, 1))):
        j = n_full + d
        for p in reversed(range(parts)):
            off = pl.multiple_of(j * tk + p * width, width)
            piece = functools.partial(step, off, width,
                                      lambda h, j=j, p=p: f_refs[h][j][:, p * width:(p + 1) * width],
                                      mask=col + (off - qi * tq) <= row)
            carry = piece(carry) if p == 0 else lax.cond(off < (qi + 1) * tq, piece, lambda c: c, carry)

    qn = []
    for h in range(g):
        q = q_ref[:, h * HEAD_DIM:(h + 1) * HEAD_DIM].astype(F32)
        qk_max = jnp.sqrt(jnp.sum(q * q, axis=-1, keepdims=True)) * kmax_ref[h:h + 1, 0:1]
        qn.append(qk_max * FOX_BOUND_SCALE + FOX_BOUND_SLACK)

    def reaches(j, carry):
        jc = jnp.maximum(j, 0)
        flags = []
        for h in range(g):
            bound = qn[h] - f_refs[h][jc][:, tk - 1:tk] * LOG2E
            flags.append((jnp.max(bound - carry[h][0]) >= FOX_SKIP_LOG2).astype(I32))
        return tuple(flags)

    def chunk_all_heads(state):
        j, _, carry = state
        carry = step(pl.multiple_of(j * tk, tk), tk, lambda h: f_refs[h][j], carry, None)
        return j - 1, reaches(j - 1, carry), carry

    def chunk_live_heads(state):
        j, live, carry = state
        carry = tuple(lax.cond(live[h] > 0, functools.partial(head_chunk, h, j), lambda mc: mc, carry[h])
                      for h in range(g))
        return j - 1, reaches(j - 1, carry), carry

    state = (n_full - 1, reaches(n_full - 1, carry), carry)
    state = lax.while_loop(lambda st: (st[0] >= 0) & (sum(st[1]) == g), chunk_all_heads, state)
    _, _, carry = lax.while_loop(lambda st: (st[0] >= 0) & (sum(st[1]) > 0), chunk_live_heads, state)
    for h in range(g):
        acc = carry[h][1]
        o_ref[:, h * HEAD_DIM:(h + 1) * HEAD_DIM] = (acc[:, :HEAD_DIM] / acc[:, HEAD_DIM:]).astype(o_ref.dtype)


def _fox(fx, f_rows, tq, tk):
    t = fx.shape[0]
    g = FOX_GROUP
    heads = lambda hg, u: hg * g + u
    q_specs = [pl.BlockSpec((tq, g * HEAD_DIM), lambda hg, i: (i, hg))]
    k_specs = [pl.BlockSpec((t, HEAD_DIM), lambda hg, i, u=u: (0, N_HEADS + heads(hg, u)),
                            pipeline_mode=pl.Buffered(1)) for u in range(g)]
    v_specs = [pl.BlockSpec((t, HEAD_DIM), lambda hg, i, u=u: (0, 2 * N_HEADS + heads(hg, u)),
                            pipeline_mode=pl.Buffered(1)) for u in range(g)]
    f_specs = [pl.BlockSpec((None, t // tk, 1, tk), lambda hg, i, u=u: (heads(hg, u), 0, 0, 0)) for u in range(g)]
    return pl.pallas_call(
        functools.partial(_fox_kernel, tk=tk),
        grid=(N_HEADS // g, t // tq),
        in_specs=q_specs + k_specs + v_specs + f_specs,
        out_specs=pl.BlockSpec((tq, g * HEAD_DIM), lambda hg, i: (i, hg)),
        out_shape=jax.ShapeDtypeStruct((t, MIX_W), BF16),
        scratch_shapes=[pltpu.VMEM((8, LANES), F32)],
        compiler_params=_params(("arbitrary", "arbitrary")),
        name="fox",
    )(*([fx] * (1 + 2 * g) + [f_rows] * g))


def _merge_kernel(oa_ref, ob_ref, ga_ref, gb_ref, x_ref, woa_ref, wob_ref, wout_ref, gt1_ref, g2_ref, sc2_ref,
                  sh2_ref, wr_ref, br_ref, x1_ref, h2_ref, ri_ref, gate_ref, cnt_ref, carry_ref):
    i = pl.program_id(0)
    tm = x_ref.shape[0]

    @pl.when(i == 0)
    def _():
        carry_ref[...] = jnp.zeros_like(carry_ref)

    ya = _dot(oa_ref[...], woa_ref[...])
    yb = _dot(ob_ref[...], wob_ref[...])
    merged = _sigmoid(ga_ref[...].astype(F32)) * ya + _sigmoid(gb_ref[...].astype(F32)) * yb
    x1 = x_ref[...] + gt1_ref[...] * _dot(merged.astype(BF16), wout_ref[...])
    x1_ref[...] = x1
    y = x1 * lax.rsqrt(jnp.mean(x1 * x1, axis=-1, keepdims=True) + RMS_EPS) * g2_ref[...]
    h2 = y * (1.0 + sc2_ref[...]) + sh2_ref[...]
    _store_token_tiles(h2_ref, h2)

    logits = _dot_split(h2, wr_ref[...]) + br_ref[...]
    lane = lax.broadcasted_iota(I32, logits.shape, 1)
    lanef = lane.astype(F32)
    cur = jnp.where(lane < N_EXPERTS, logits, -jnp.inf)
    vals, idxs = [], []
    for _ in range(TOP_K):
        m = jnp.max(cur, axis=-1, keepdims=True)
        ix = jnp.min(jnp.where(cur == m, lanef, float(LANES)), axis=-1, keepdims=True)
        vals.append(m)
        idxs.append(ix)
        cur = jnp.where(lanef == ix, -jnp.inf, cur)
    exps = [jnp.exp(v - vals[0]) for v in vals]
    denom = exps[0] + exps[1] + exps[2] + exps[3]

    onehot = jnp.zeros(logits.shape, F32)
    for ix in idxs:
        onehot = onehot + jnp.where(lanef == ix, 1.0, 0.0)
    r = lax.broadcasted_iota(I32, (tm, tm), 0)
    c = lax.broadcasted_iota(I32, (tm, tm), 1)
    strict = jnp.where(r > c, 1.0, 0.0).astype(BF16)
    before = _dot(strict, onehot.astype(BF16)) + carry_ref[...]
    carry_ref[...] = carry_ref[...] + jnp.sum(onehot, axis=0, keepdims=True)
    cnt_ref[...] = carry_ref[...].astype(I32)

    ri = jnp.zeros(logits.shape, F32)
    gates = jnp.zeros(logits.shape, F32)
    for kk in range(TOP_K):
        rank = jnp.sum(jnp.where(lanef == idxs[kk], before, 0.0), axis=-1, keepdims=True)
        ri = ri + jnp.where(lane == kk, idxs[kk], 0.0) + jnp.where(lane == TOP_K + kk, rank, 0.0)
        gates = gates + jnp.where(lane == kk, exps[kk] / denom, 0.0)
    ri_ref[...] = ri.astype(I32)
    gate_ref[...] = gates


def _merge(o_a, o_b, gate_a, gate_b, x, woa, wob, wout, gt1, g2, sc2, sh2, wr, br, tm):
    t, d = x.shape
    row = lambda i: (i, 0)
    fixed = lambda i: (0, 0)
    return pl.pallas_call(
        _merge_kernel,
        grid=(t // tm,),
        in_specs=[pl.BlockSpec((tm, MIX_W), row), pl.BlockSpec((tm, MIX_W), row),
                  pl.BlockSpec((tm, d), row), pl.BlockSpec((tm, d), row), pl.BlockSpec((tm, d), row),
                  pl.BlockSpec((MIX_W, d), fixed), pl.BlockSpec((MIX_W, d), fixed), pl.BlockSpec((d, d), fixed),
                  pl.BlockSpec((1, d), fixed), pl.BlockSpec((1, d), fixed), pl.BlockSpec((1, d), fixed),
                  pl.BlockSpec((1, d), fixed), pl.BlockSpec((d, LANES), fixed), pl.BlockSpec((1, LANES), fixed)],
        out_specs=[pl.BlockSpec((tm, d), row), pl.BlockSpec((tm * TOKEN_TILE_ROWS, LANES), row),
                   pl.BlockSpec((tm, LANES), row),
                   pl.BlockSpec((tm, LANES), row), pl.BlockSpec((1, LANES), fixed)],
        out_shape=[jax.ShapeDtypeStruct((t, d), F32), jax.ShapeDtypeStruct((t * TOKEN_TILE_ROWS, LANES), F32),
                   jax.ShapeDtypeStruct((t, LANES), I32), jax.ShapeDtypeStruct((t, LANES), F32),
                   jax.ShapeDtypeStruct((1, LANES), I32)],
        scratch_shapes=[pltpu.VMEM((1, LANES), F32)],
        compiler_params=_params(("arbitrary",)),
        name="merge",
    )(o_a, o_b, gate_a, gate_b, x, woa, wob, wout, gt1, g2, sc2, sh2, wr, br)


def _dest_kernel(ri_ref, cnt_ref, dest_ref, blk_ref, pend_ref):
    shift = EXPERT_BLOCK.bit_length() - 1
    cnt = jnp.broadcast_to(cnt_ref[...], (8, LANES))
    lane_row = lax.broadcasted_iota(I32, cnt.shape, 1)
    padded = jnp.where(lane_row < N_EXPERTS, ((cnt + (EXPERT_BLOCK - 1)) >> shift) << shift, 0)
    pend = padded
    s = 1
    while s < N_EXPERTS:
        pend = pend + jnp.where(lane_row >= s, pltpu.roll(pend, s, 1), 0)
        s *= 2
    pstart = (pend - padded)[0:1, :].astype(F32)
    pend = pend[0:1, :]
    pend_ref[...] = pend

    ri = ri_ref[...]
    lane = lax.broadcasted_iota(I32, ri.shape, 1)
    dest = jnp.zeros(ri.shape, F32)
    for kk in range(TOP_K):
        ix = ri[:, kk:kk + 1]
        rank = ri[:, TOP_K + kk:TOP_K + kk + 1].astype(F32)
        start = jnp.sum(jnp.where(lane == ix, pstart, 0.0), axis=-1, keepdims=True)
        dest = dest + jnp.where(lane == kk, start + rank, 0.0)
    dest_ref[...] = dest.astype(I32)

    bstart = lax.broadcasted_iota(I32, blk_ref.shape, 0) * EXPERT_BLOCK
    lane_b = lax.broadcasted_iota(I32, blk_ref.shape, 1)
    ended = jnp.where((lane_b < N_EXPERTS) & (pend <= bstart), 1.0, 0.0)
    e = jnp.minimum(jnp.sum(ended, axis=-1, keepdims=True), float(N_EXPERTS - 1))
    vend = pstart + cnt[0:1, :].astype(F32)
    vend_e = jnp.sum(jnp.where(lane_b.astype(F32) == e, vend, 0.0), axis=-1, keepdims=True)
    nvalid = jnp.clip(vend_e - bstart[:, 0:1].astype(F32), 0.0, float(EXPERT_BLOCK))
    blk_ref[...] = jnp.where(lane_b == 0, e, jnp.where(lane_b == 1, nvalid, 0.0)).astype(I32)


def _dest(ri, cnt, n_blocks, tm):
    t = ri.shape[0]
    nb_pad = -(-n_blocks // 8) * 8
    return pl.pallas_call(
        _dest_kernel,
        grid=(t // tm,),
        in_specs=[pl.BlockSpec((tm, LANES), lambda i: (i, 0)), pl.BlockSpec((1, LANES), lambda i: (0, 0))],
        out_specs=[pl.BlockSpec((tm, LANES), lambda i: (i, 0)), pl.BlockSpec((nb_pad, LANES), lambda i: (0, 0)),
                   pl.BlockSpec((1, LANES), lambda i: (0, 0))],
        out_shape=[jax.ShapeDtypeStruct((t, LANES), I32), jax.ShapeDtypeStruct((nb_pad, LANES), I32),
                   jax.ShapeDtypeStruct((1, LANES), I32)],
        compiler_params=_params(("arbitrary",)),
        name="dest",
    )(ri, cnt)


ROW_UNROLL = 8


def _drain_tokens(src_ref, dst_ref, sem, n_tokens):
    def drain(_, carry):
        for _ in range(ROW_UNROLL * TOP_K):
            _token_copy(src_ref, 0, dst_ref, 0, sem).wait()
        return carry

    lax.fori_loop(0, n_tokens // ROW_UNROLL, drain, 0)


def _dispatch_kernel(dest_ref, nvalid_ref, h_ref, xbuf_ref, zero_ref, sem, fill_sem):
    tm = h_ref.shape[0] // TOKEN_TILE_ROWS

    @pl.when(pl.program_id(0) == 0)
    def _():
        zero_ref[...] = jnp.zeros_like(zero_ref)

        def padding_copies(b, act):
            pad = EXPERT_BLOCK - nvalid_ref[b]

            @pl.when(pad > 0)
            def _():
                first = b * EXPERT_BLOCK + nvalid_ref[b]
                size = EXPERT_BLOCK
                while size >= 1:
                    take = (pad & size) != 0
                    rows = size * TOKEN_TILE_ROWS
                    dst0 = pl.multiple_of(first * TOKEN_TILE_ROWS, TOKEN_TILE_ROWS)

                    @pl.when(take)
                    def _(rows=rows, dst0=dst0):
                        act(pltpu.make_async_copy(zero_ref.at[pl.ds(0, rows), :],
                                                  xbuf_ref.at[pl.ds(dst0, rows), :], fill_sem))

                    first = first + jnp.where(take, size, 0)
                    size //= 2

        n_blocks = nvalid_ref.shape[0]
        lax.fori_loop(0, n_blocks, lambda b, c: (padding_copies(b, lambda cp: cp.start()), c)[1], 0)
        lax.fori_loop(0, n_blocks, lambda b, c: (padding_copies(b, lambda cp: cp.wait()), c)[1], 0)

    def issue(g, carry):
        for u in range(ROW_UNROLL):
            r = g * ROW_UNROLL + u
            for kk in range(TOP_K):
                _token_copy(h_ref, r, xbuf_ref, dest_ref[r * TOP_K + kk], sem).start(priority=kk % 2)
        return carry

    lax.fori_loop(0, tm // ROW_UNROLL, issue, 0)
    _drain_tokens(h_ref, xbuf_ref, sem, tm)


def _dispatch(dest_flat, nvalid, h2_tiles):
    tm = ROUTE_TILE
    t = h2_tiles.shape[0] // TOKEN_TILE_ROWS
    n_rows = nvalid.shape[0] * EXPERT_BLOCK
    return pl.pallas_call(
        _dispatch_kernel,
        grid=(t // tm,),
        in_specs=[pl.BlockSpec((tm * TOP_K,), lambda i: (i,), memory_space=pltpu.SMEM),
                  pl.BlockSpec(memory_space=pltpu.SMEM),
                  pl.BlockSpec((tm * TOKEN_TILE_ROWS, LANES), lambda i: (i, 0))],
        out_specs=pl.BlockSpec(memory_space=pl.ANY),
        out_shape=jax.ShapeDtypeStruct((n_rows * TOKEN_TILE_ROWS, LANES), h2_tiles.dtype),
        scratch_shapes=[pltpu.VMEM((EXPERT_BLOCK * TOKEN_TILE_ROWS, LANES), h2_tiles.dtype),
                        pltpu.SemaphoreType.DMA(()), pltpu.SemaphoreType.DMA(())],
        compiler_params=_params(("arbitrary",)),
        name="dispatch",
    )(dest_flat, nvalid, h2_tiles)


def _experts_kernel(blk_e_ref, nvalid_ref, grp_ref, nxt_e_ref, x_ref, wgu_hbm, bgu_ref, wd_hbm, bd_ref, y_ref,
                    wgu32_ref, wd32_ref, wgu16_ref, wd16_ref, sems):
    b = pl.program_id(0)
    nvalid = nvalid_ref[b]
    e = blk_e_ref[b]
    slot = grp_ref[b] % 2
    first = (b == 0) | (e != blk_e_ref[jnp.maximum(b - 1, 0)])

    def weight_copies(expert, s):
        return (pltpu.make_async_copy(wgu_hbm.at[expert], wgu32_ref.at[s], sems.at[s, 0]),
                pltpu.make_async_copy(wd_hbm.at[expert], wd32_ref.at[s], sems.at[s, 1]))

    @pl.when(b == 0)
    def _():
        for cp in weight_copies(e, 0):
            cp.start()

    @pl.when(first)
    def _():
        for cp in weight_copies(e, slot):
            cp.wait()
        wgu16_ref[...] = wgu32_ref[slot].astype(BF16)
        wd16_ref[...] = wd32_ref[slot].astype(BF16)

    @pl.when(first & (nxt_e_ref[b] != e))
    def _():
        for cp in weight_copies(nxt_e_ref[b], 1 - slot):
            cp.start()

    n_blk = x_ref.shape[0] // TOKEN_TILE_ROWS
    n_half = n_blk // 2

    def mlp(n):
        x = jnp.concatenate([_load_token_slab(x_ref, s, n) for s in range(TOKEN_TILE_ROWS)], axis=1)
        gu = _dot(x.astype(BF16), wgu16_ref[...]) + bgu_ref[...]
        gate = jnp.minimum(gu[:, :FF], SWIGLU_LIMIT)
        up = jnp.clip(gu[:, FF:], -SWIGLU_LIMIT, SWIGLU_LIMIT)
        act = (up + 1.0) * (gate * _sigmoid(SWIGLU_ALPHA * gate))
        _store_token_tiles(y_ref, _dot(act.astype(BF16), wd16_ref[...]) + bd_ref[...])
        if n < n_blk:
            y_ref[n * TOKEN_TILE_ROWS:, :] = jnp.zeros(((n_blk - n) * TOKEN_TILE_ROWS, LANES), y_ref.dtype)

    @pl.when(nvalid > n_half)
    def _():
        mlp(n_blk)

    @pl.when((nvalid > 0) & (nvalid <= n_half))
    def _():
        mlp(n_half)

    @pl.when(nvalid <= 0)
    def _():
        y_ref[...] = jnp.zeros_like(y_ref)


def _experts(blk_e, nvalid, xbuf, wgu, bgu, wd, bd):
    d = D_MODEL
    blk_rows = EXPERT_BLOCK * TOKEN_TILE_ROWS
    nb = xbuf.shape[0] // blk_rows
    change = jnp.concatenate([jnp.zeros((1,), I32), (blk_e[1:] != blk_e[:-1]).astype(I32)])
    grp = jnp.cumsum(change)
    pos = jnp.where(change > 0, jnp.arange(nb, dtype=I32), nb)
    nxt_pos = lax.cummin(jnp.concatenate([pos[1:], jnp.full((1,), nb, I32)]), reverse=True)
    nxt_e = jnp.where(nxt_pos < nb, blk_e[jnp.minimum(nxt_pos, nb - 1)], blk_e)
    grid_spec = pltpu.PrefetchScalarGridSpec(
        num_scalar_prefetch=4,
        grid=(nb,),
        in_specs=[pl.BlockSpec((blk_rows, LANES), lambda b, e, *_: (b, 0)),
                  pl.BlockSpec(memory_space=pl.ANY),
                  pl.BlockSpec((None, 1, 2 * FF), lambda b, e, *_: (e[b], 0, 0)),
                  pl.BlockSpec(memory_space=pl.ANY),
                  pl.BlockSpec((None, 1, d), lambda b, e, *_: (e[b], 0, 0))],
        out_specs=pl.BlockSpec((blk_rows, LANES), lambda b, e, *_: (b, 0)),
        scratch_shapes=[pltpu.VMEM((2, d, 2 * FF), F32), pltpu.VMEM((2, FF, d), F32),
                        pltpu.VMEM((d, 2 * FF), BF16), pltpu.VMEM((FF, d), BF16),
                        pltpu.SemaphoreType.DMA((2, 2))],
    )
    return pl.pallas_call(
        _experts_kernel,
        grid_spec=grid_spec,
        out_shape=jax.ShapeDtypeStruct(xbuf.shape, F32),
        compiler_params=_params(("arbitrary",)),
        name="experts",
    )(blk_e, nvalid, grp, nxt_e, xbuf, wgu, bgu, wd, bd)


def _combine_kernel(dest_ref, dest_next_ref, ybuf_ref, x1_ref, gate_ref, gt2_ref, gf_ref, o_ref, rows_ref, sems):
    i = pl.program_id(0)
    tm = x1_ref.shape[0]
    slot = i % 2

    def start_gather(idx_ref, s):
        def issue(g, carry):
            for u in range(ROW_UNROLL):
                r = g * ROW_UNROLL + u
                for kk in range(TOP_K):
                    _token_copy(ybuf_ref, idx_ref[r * TOP_K + kk], rows_ref.at[s, kk], r,
                                sems.at[s]).start(priority=kk % 2)
            return carry

        lax.fori_loop(0, tm // ROW_UNROLL, issue, 0)

    @pl.when(i == 0)
    def _():
        start_gather(dest_ref, 0)

    @pl.when(i + 1 < pl.num_programs(0))
    def _():
        start_gather(dest_next_ref, 1 - slot)

    _drain_tokens(ybuf_ref, rows_ref.at[slot, 0], sems.at[slot], tm)

    gates = gate_ref[...]
    slabs = []
    for s in range(TOKEN_TILE_ROWS):
        moe = gates[:, 0:1] * _load_token_slab(rows_ref.at[slot, 0], s, tm)
        for kk in range(1, TOP_K):
            moe = moe + gates[:, kk:kk + 1] * _load_token_slab(rows_ref.at[slot, kk], s, tm)
        slabs.append(moe)
    xo = x1_ref[...] + gt2_ref[...] * jnp.concatenate(slabs, axis=1)
    o_ref[...] = xo * lax.rsqrt(jnp.mean(xo * xo, axis=-1, keepdims=True) + RMS_EPS) * gf_ref[...]


def _combine(dest_flat, ybuf, x1, gates, gt2, g_final):
    t, d = x1.shape
    tm = ROUTE_TILE
    n = t // tm
    return pl.pallas_call(
        _combine_kernel,
        grid=(n,),
        in_specs=[pl.BlockSpec((tm * TOP_K,), lambda i: (i,), memory_space=pltpu.SMEM),
                  pl.BlockSpec((tm * TOP_K,), lambda i: (jnp.minimum(i + 1, n - 1),), memory_space=pltpu.SMEM),
                  pl.BlockSpec(memory_space=pl.ANY),
                  pl.BlockSpec((tm, d), lambda i: (i, 0)),
                  pl.BlockSpec((tm, LANES), lambda i: (i, 0)),
                  pl.BlockSpec((1, d), lambda i: (0, 0)),
                  pl.BlockSpec((1, d), lambda i: (0, 0))],
        out_specs=pl.BlockSpec((tm, d), lambda i: (i, 0)),
        out_shape=jax.ShapeDtypeStruct((t, d), F32),
        scratch_shapes=[pltpu.VMEM((2, TOP_K, tm * TOKEN_TILE_ROWS, LANES), F32), pltpu.SemaphoreType.DMA((2,))],
        compiler_params=_params(("arbitrary",)),
        name="combine",
    )(dest_flat, dest_flat, ybuf, x1, gates, gt2, g_final)


def _pad_lanes(v, lane0):
    return jnp.zeros((1, LANES), F32).at[0, lane0:lane0 + v.shape[0]].set(v.astype(F32))


def _layer(x, mod, g_norm1, w_in, conv_w, a_log, dt_bias, g_onorm, b_fgate, w_o_delta, w_o_fox, w_out,
           g_norm2, w_router, b_router, w_gate_up, b_gate_up, w_down, b_down, g_final):
    t, d = x.shape
    sh1, sc1, gt1, sh2, sc2, gt2 = [mod[:, i * d:(i + 1) * d] for i in range(N_MOD)]

    w_t = w_in.T
    o = 0
    wq = w_t[o:o + QKV_W]; o += QKV_W
    wz = w_t[o:o + MIX_W]; o += MIX_W
    w_beta = w_t[o:o + N_HEADS]; o += N_HEADS
    w_dec = w_t[o:o + N_HEADS]; o += N_HEADS
    wf = w_t[o:o + QKV_W]; o += QKV_W
    wf = jnp.concatenate([wf[:MIX_W] * (LOG2E * HEAD_DIM ** -0.5), wf[MIX_W:]], axis=0)
    w_fg = w_t[o:o + N_HEADS]; o += N_HEADS
    wga = w_t[o:o + d]; o += d
    wgb = w_t[o:o + d]
    ws = jnp.zeros((LANES, d), F32)
    ws = ws.at[L_BETA:L_BETA + N_HEADS].set(w_beta).at[L_G:L_G + N_HEADS].set(w_dec)
    ws = ws.at[L_F:L_F + N_HEADS].set(w_fg)
    bf = lambda w: w.astype(BF16)

    tm = min(ROW_TILE, t)
    qkv, z, fx, gate_a, gate_b, small = _inproj(
        x, g_norm1.reshape(1, d), sc1, sh1, bf(wq), bf(wz), bf(wf), bf(wga), bf(wgb), bf(ws), tm)

    gcol, grow = _gates(small, _pad_lanes(a_log, L_G), _pad_lanes(dt_bias, L_G), _pad_lanes(b_fgate, L_F),
                        min(GATES_TILE, t))

    o_a = _delta(qkv, z, gcol, grow, conv_w, g_onorm.reshape(1, HEAD_DIM), min(DELTA_TILE, t))
    tq, tk = min(FOX_TQ, t), min(FOX_TK, t)
    f_rows = grow[L_F:L_F + N_HEADS].reshape(N_HEADS, t // tk, 1, tk)
    o_b = _fox(fx, f_rows, tq, tk)

    wr = jnp.zeros((d, LANES), F32).at[:, :N_EXPERTS].set(w_router)
    br = _pad_lanes(b_router, 0)
    x1, h2, ri, gates, cnt = _merge(o_a, o_b, gate_a, gate_b, x, bf(w_o_delta), bf(w_o_fox), bf(w_out), gt1,
                                    g_norm2.reshape(1, d), sc2, sh2, wr, br, tm)

    n_blocks = (t * TOP_K) // EXPERT_BLOCK + N_EXPERTS
    dest, blk, _ = _dest(ri, cnt, n_blocks, min(2048, t))
    dest_flat = dest[:, :TOP_K].reshape(t * TOP_K)
    blk_e, nvalid = blk[:n_blocks, 0], blk[:n_blocks, 1]

    xbuf = _dispatch(dest_flat, nvalid, h2)
    ybuf = _experts(blk_e, nvalid, xbuf, w_gate_up, b_gate_up.reshape(N_EXPERTS, 1, 2 * FF),
                    w_down, b_down.reshape(N_EXPERTS, 1, d))
    return _combine(dest_flat, ybuf, x1, gates, gt2, g_final.reshape(1, d))


def kernel(x, c, w_ada, b_ada, g_norm1, w_in, conv_w, a_log, dt_bias, g_onorm, b_fgate, w_o_delta, w_o_fox, w_out,
           g_norm2, w_router, b_router, w_gate_up, b_gate_up, w_down, b_down, g_final):
    b, s, d = x.shape
    assert b == 1 and d == D_MODEL and w_ada.shape[0] == 1
    h = x[0]
    for l in range(w_ada.shape[0]):
        mod = _modulation(c[0], w_ada[l], b_ada[l])
        h = _layer(h, mod, g_norm1[l], w_in[l], conv_w[l], a_log[l], dt_bias[l], g_onorm[l], b_fgate[l],
                   w_o_delta[l], w_o_fox[l], w_out[l], g_norm2[l], w_router[l], b_router[l], w_gate_up[l],
                   b_gate_up[l], w_down[l], b_down[l], g_final)
    return h[None]
```

```python
import functools

import jax
import jax.numpy as jnp
from jax import lax
from jax.experimental import pallas as pl
from jax.experimental.pallas import tpu as pltpu

F32 = jnp.float32
BF16 = jnp.bfloat16
I32 = jnp.int32

D_MODEL = 1024
HEAD_DIM = 128
N_HEADS = 4
MIX_W = N_HEADS * HEAD_DIM
QKV_W = 3 * MIX_W
CONV_K = 4
CHUNK = 64
STACK = N_HEADS * CHUNK
SUB = 16
N_EXPERTS = 32
TOP_K = 4
FF = D_MODEL
SWIGLU_LIMIT = 7.0
SWIGLU_ALPHA = 1.702
RMS_EPS = 1e-6
L2_EPS = 1e-6
N_MOD = 6
LANES = 128
NEG_BIG = -1e30
LOG2E = 1.4426950408889634

SUBLANES = 8
MOD_TILE = 1024
ROW_TILE = 512
GATES_TILE = 256
GATE_ROWS = 24
DELTA_TILE = 256
EXPERT_BLOCK = 512
ROUTE_TILE = 256
VMEM_LIMIT = 56 * 1024 * 1024


def _params(sem):
    return pltpu.CompilerParams(dimension_semantics=sem, vmem_limit_bytes=VMEM_LIMIT)


def _softplus(x):
    return jnp.maximum(x, 0.0) + jnp.log(1.0 + jnp.exp(-jnp.abs(x)))


def _sigmoid(x):
    return 1.0 / (1.0 + jnp.exp(-x))


def _silu(x):
    return x * _sigmoid(x)


def _dot(a, b):
    return jnp.dot(a, b, preferred_element_type=F32)


def _dot_nt(a, b):
    return lax.dot_general(a, b, (((1,), (1,)), ((), ())), preferred_element_type=F32)


def _split_bf16(x, terms):
    parts = []
    for _ in range(terms):
        p = x.astype(BF16)
        parts.append(p)
        x = x - p.astype(F32)
    return parts


def _dot_split(a, b):
    a_hi, a_lo = _split_bf16(a, 2)
    b_hi, b_lo = _split_bf16(b, 2)
    return _dot(a_hi, b_hi) + (_dot(a_hi, b_lo) + _dot(a_lo, b_hi))


def _dot_mask(mask01, x):
    return sum(_dot(mask01, p) for p in _split_bf16(x, 3))


TOKEN_TILE_ROWS = D_MODEL // LANES


def _store_token_tiles(ref, x):
    n = x.shape[0]
    for s in range(TOKEN_TILE_ROWS):
        ref[pl.ds(s, n, stride=TOKEN_TILE_ROWS), :] = x[:, s * LANES:(s + 1) * LANES]


def _load_token_slab(ref, s, n):
    return ref[pl.ds(s, n, stride=TOKEN_TILE_ROWS), :]


def _token_copy(src_ref, src_tok, dst_ref, dst_tok, sem):
    rows = lambda i: pl.ds(pl.multiple_of(i * TOKEN_TILE_ROWS, TOKEN_TILE_ROWS), TOKEN_TILE_ROWS)
    return pltpu.make_async_copy(src_ref.at[rows(src_tok), :], dst_ref.at[rows(dst_tok), :], sem)


def _mod_kernel(c_ref, w_ref, b_ref, o_ref):
    o_ref[...] = jnp.sum(c_ref[...] * w_ref[...], axis=0, keepdims=True) + b_ref[...]


def _modulation(c, w_ada, b_ada):
    d, n = w_ada.shape
    tn = MOD_TILE
    return pl.pallas_call(
        _mod_kernel,
        grid=(n // tn,),
        in_specs=[pl.BlockSpec((d, 1), lambda j: (0, 0)),
                  pl.BlockSpec((d, tn), lambda j: (0, j)),
                  pl.BlockSpec((1, tn), lambda j: (0, j))],
        out_specs=pl.BlockSpec((1, tn), lambda j: (0, j)),
        out_shape=jax.ShapeDtypeStruct((1, n), F32),
        compiler_params=_params(("arbitrary",)),
        name="mod",
    )(c.reshape(d, 1), w_ada, b_ada.reshape(1, n))


def _inproj_kernel(x_ref, g_ref, sc_ref, sh_ref, wq_ref, wz_ref, wf_ref, wga_ref, wgb_ref, ws_ref,
                   oq_ref, oz_ref, of_ref, oga_ref, ogb_ref, os_ref):
    x = x_ref[...]
    y = x * lax.rsqrt(jnp.mean(x * x, axis=-1, keepdims=True) + RMS_EPS) * g_ref[...]
    h = (y * (1.0 + sc_ref[...]) + sh_ref[...]).astype(BF16)
    oq_ref[...] = _dot_nt(h, wq_ref[...])
    oz_ref[...] = _dot_nt(h, wz_ref[...]).astype(BF16)
    of_ref[...] = _dot_nt(h, wf_ref[...]).astype(BF16)
    oga_ref[...] = _dot_nt(h, wga_ref[...]).astype(BF16)
    ogb_ref[...] = _dot_nt(h, wgb_ref[...]).astype(BF16)
    os_ref[...] = _dot_nt(h, ws_ref[...])


def _inproj(x, g1, sc1, sh1, wq, wz, wf, wga, wgb, ws, tm):
    t, d = x.shape
    row = lambda i: (i, 0)
    fixed = lambda i: (0, 0)
    ws_list = [wq, wz, wf, wga, wgb, ws]
    out_dt = [F32, BF16, BF16, BF16, BF16, F32]
    return pl.pallas_call(
        _inproj_kernel,
        grid=(t // tm,),
        in_specs=[pl.BlockSpec((tm, d), row)] + [pl.BlockSpec((1, d), fixed)] * 3
                 + [pl.BlockSpec(w.shape, fixed) for w in ws_list],
        out_specs=[pl.BlockSpec((tm, w.shape[0]), row) for w in ws_list],
        out_shape=[jax.ShapeDtypeStruct((t, w.shape[0]), dt) for w, dt in zip(ws_list, out_dt)],
        compiler_params=_params(("arbitrary",)),
        name="inproj",
    )(x, g1, sc1, sh1, *ws_list)


L_BETA, L_G, L_F, L_GC, L_GL = 0, 4, 8, 12, 16


def _gates_kernel(s_ref, alog_ref, dtb_ref, bf_ref, col_ref, row_ref, carry_ref):
    i = pl.program_id(0)
    tm = s_ref.shape[0]

    @pl.when(i == 0)
    def _():
        carry_ref[...] = jnp.zeros_like(carry_ref)

    s = s_ref[...]
    lane = lax.broadcasted_iota(I32, s.shape, 1)
    beta = _sigmoid(s)
    g = -jnp.exp(alog_ref[...]) * _softplus(s + dtb_ref[...])
    logf = -_softplus(-(s + bf_ref[...]))
    is_g = (lane >= L_G) & (lane < L_G + N_HEADS)
    is_f = (lane >= L_F) & (lane < L_F + N_HEADS)
    g = jnp.where(is_g, g, 0.0)
    logf = jnp.where(is_f, logf, 0.0)

    r = lax.broadcasted_iota(I32, (tm, tm), 0)
    c = lax.broadcasted_iota(I32, (tm, tm), 1)
    same_chunk = (r // CHUNK) == (c // CHUNK)
    tri = jnp.where(r >= c, 1.0, 0.0).astype(BF16)
    tri_chunk = jnp.where(same_chunk & (r >= c), 1.0, 0.0).astype(BF16)
    ones_chunk = jnp.where(same_chunk, 1.0, 0.0).astype(BF16)
    f_cum = _dot_mask(tri, logf) + carry_ref[...]
    carry_ref[...] = f_cum[tm - 1:tm, :]
    gc = _dot_mask(tri_chunk, g)
    gl = _dot_mask(ones_chunk, g)

    out = jnp.where(lane < N_HEADS, beta, 0.0) + g + f_cum
    out = out + pltpu.roll(gc, L_GC - L_G, 1) + pltpu.roll(gl, L_GL - L_G, 1)
    col_ref[...] = out
    row_ref[...] = out.T[:row_ref.shape[0], :]


def _gates(small, alog_row, dtb_row, bf_row, tm):
    t = small.shape[0]
    n_rows = GATE_ROWS
    return pl.pallas_call(
        _gates_kernel,
        grid=(t // tm,),
        in_specs=[pl.BlockSpec((tm, LANES), lambda i: (i, 0))] + [pl.BlockSpec((1, LANES), lambda i: (0, 0))] * 3,
        out_specs=[pl.BlockSpec((tm, LANES), lambda i: (i, 0)), pl.BlockSpec((n_rows, tm), lambda i: (0, i))],
        out_shape=[jax.ShapeDtypeStruct((t, LANES), F32), jax.ShapeDtypeStruct((n_rows, t), F32)],
        scratch_shapes=[pltpu.VMEM((1, LANES), F32)],
        compiler_params=_params(("arbitrary",)),
        name="gates",
    )(small, alog_row, dtb_row, bf_row)


def _stack_heads(a, col0):
    return jnp.concatenate([a[:, col0 + h * HEAD_DIM: col0 + (h + 1) * HEAD_DIM] for h in range(N_HEADS)], axis=0)


def _stack_cols(a, lane0):
    return jnp.concatenate([a[:, lane0 + h: lane0 + h + 1] for h in range(N_HEADS)], axis=0)


def _delta_kernel(qkv_ref, z_ref, gcol_ref, grow_ref, cw_ref, gon_ref, o_ref, ext_ref, tail_ref, s_ref):
    i = pl.program_id(0)
    tb = qkv_ref.shape[0]
    pad = tail_ref.shape[0]

    @pl.when(i == 0)
    def _():
        tail_ref[...] = jnp.zeros_like(tail_ref)
        s_ref[...] = jnp.zeros_like(s_ref)

    ext_ref[0:pad, :] = tail_ref[...]
    ext_ref[pad:pad + tb, :] = qkv_ref[...]
    tail_ref[...] = qkv_ref[tb - pad:tb, :]
    conv = cw_ref[0:1, :] * ext_ref[pad - 3:pad - 3 + tb, :]
    for j in range(1, CONV_K):
        conv = conv + cw_ref[j:j + 1, :] * ext_ref[pad - 3 + j:pad - 3 + j + tb, :]
    act = _silu(conv)

    r = lax.broadcasted_iota(I32, (STACK, STACK), 0)
    c = lax.broadcasted_iota(I32, (STACK, STACK), 1)
    same = (r // CHUNK) == (c // CHUNK)
    m_incl = same & (r >= c)
    m_strict = same & (r > c)
    eye = jnp.where(r == c, 1.0, 0.0)
    m_sub = []
    size = SUB
    while size <= CHUNK:
        m_sub.append((r // size) == (c // size))
        size *= 2
    rb = lax.broadcasted_iota(I32, (STACK, N_HEADS * HEAD_DIM), 0) // CHUNK
    cb = lax.broadcasted_iota(I32, (STACK, N_HEADS * HEAD_DIM), 1) // HEAD_DIM
    head_match = rb == cb

    chunks = range(tb // CHUNK)
    pre = []
    for ch in chunks:
        r0 = ch * CHUNK
        a = act[r0:r0 + CHUNK, :]
        q = _stack_heads(a, 0)
        k = _stack_heads(a, MIX_W)
        v = _stack_heads(a, 2 * MIX_W)
        q = q * lax.rsqrt(jnp.sum(q * q, axis=-1, keepdims=True) + L2_EPS) * (HEAD_DIM ** -0.5)
        k = k * lax.rsqrt(jnp.sum(k * k, axis=-1, keepdims=True) + L2_EPS)

        gcols = gcol_ref[r0:r0 + CHUNK, :]
        beta = _stack_cols(gcols, L_BETA)
        gc = _stack_cols(gcols, L_GC)
        gc_row = jnp.concatenate(
            [grow_ref[L_GC + h:L_GC + h + 1, r0:r0 + CHUNK] for h in range(N_HEADS)], axis=1)
        gl_row = jnp.concatenate(
            [grow_ref[L_GL + h:L_GL + h + 1, r0:r0 + CHUNK] for h in range(N_HEADS)], axis=1)

        decay = jnp.exp(jnp.where(m_incl, gc - gc_row, NEG_BIG))
        kb16 = k.astype(BF16)
        kk = _dot_nt(kb16, kb16)
        lmat = jnp.where(m_strict, kk * beta * decay, 0.0)
        attn = (_dot_nt(q.astype(BF16), kb16) * decay).astype(BF16)
        egc = jnp.exp(gc)
        rhs = jnp.concatenate([v * beta, k * (beta * egc)], axis=1).astype(BF16)
        kt_dec = (k.T * jnp.exp(gl_row - gc_row)).astype(BF16)
        pre.append(dict(lmat=lmat, attn=attn, rhs=rhs, kt_dec=kt_dec, q_dec=q * egc))

    l_d = [jnp.where(m_sub[0], p["lmat"], 0.0) for p in pre]
    inv = [eye - l for l in l_d]
    l16 = [l.astype(BF16) for l in l_d]
    pw = [_dot(l, l).astype(BF16) for l in l16]
    n_sq = SUB.bit_length() - 3
    for s in range(n_sq + 1):
        inv = [x + _dot(x.astype(BF16), p) for x, p in zip(inv, pw)]
        if s < n_sq:
            pw = [_dot(p, p).astype(BF16) for p in pw]
    for lvl in range(1, len(m_sub)):
        m_off = m_sub[lvl] & ~m_sub[lvl - 1]
        off = [jnp.where(m_off, p["lmat"], 0.0).astype(BF16) for p in pre]
        inv16 = [x.astype(BF16) for x in inv]
        half = [_dot(x, o).astype(BF16) for x, o in zip(inv16, off)]
        inv = [x - _dot(h, x16) for x, h, x16 in zip(inv, half, inv16)]
    uws = [_dot(x.astype(BF16), p["rhs"]) for x, p in zip(inv, pre)]

    for ch in chunks:
        r0 = ch * CHUNK
        attn, kt_dec = pre[ch]["attn"], pre[ch]["kt_dec"]
        u, w = uws[ch][:, :HEAD_DIM], uws[ch][:, HEAD_DIM:]

        state16 = s_ref[...].astype(BF16)
        wq_s = _dot(jnp.concatenate([w, pre[ch]["q_dec"]], axis=0).astype(BF16), state16)
        ws_d = jnp.concatenate([wq_s[h * CHUNK:(h + 1) * CHUNK, h * HEAD_DIM:(h + 1) * HEAD_DIM]
                                for h in range(N_HEADS)], axis=0)
        qs_d = jnp.concatenate([wq_s[STACK + h * CHUNK:STACK + (h + 1) * CHUNK, h * HEAD_DIM:(h + 1) * HEAD_DIM]
                                for h in range(N_HEADS)], axis=0)
        v_new = u - ws_d
        v16 = v_new.astype(BF16)
        o = qs_d + _dot(attn, v16)

        v_bd = jnp.where(head_match, jnp.concatenate([v_new] * N_HEADS, axis=1), 0.0).astype(BF16)
        upd = _dot(kt_dec, v_bd)
        for h in range(N_HEADS):
            e_h = jnp.exp(grow_ref[L_GL + h:L_GL + h + 1, r0:r0 + 1])
            sl = slice(h * HEAD_DIM, (h + 1) * HEAD_DIM)
            s_ref[:, sl] = s_ref[:, sl] * e_h + upd[:, sl]

        for h in range(N_HEADS):
            oh = o[h * CHUNK:(h + 1) * CHUNK, :]
            oh = oh * lax.rsqrt(jnp.mean(oh * oh, axis=-1, keepdims=True) + RMS_EPS) * gon_ref[...]
            zh = z_ref[r0:r0 + CHUNK, h * HEAD_DIM:(h + 1) * HEAD_DIM].astype(F32)
            o_ref[r0:r0 + CHUNK, h * HEAD_DIM:(h + 1) * HEAD_DIM] = (oh * _silu(zh)).astype(o_ref.dtype)


def _delta(qkv, z, gcol, grow, conv_w, g_onorm, tb):
    t = qkv.shape[0]
    pad = SUBLANES
    return pl.pallas_call(
        _delta_kernel,
        grid=(t // tb,),
        in_specs=[pl.BlockSpec((tb, QKV_W), lambda i: (i, 0)),
                  pl.BlockSpec((tb, MIX_W), lambda i: (i, 0)),
                  pl.BlockSpec((tb, LANES), lambda i: (i, 0)),
                  pl.BlockSpec((grow.shape[0], tb), lambda i: (0, i)),
                  pl.BlockSpec((CONV_K, QKV_W), lambda i: (0, 0)),
                  pl.BlockSpec((1, HEAD_DIM), lambda i: (0, 0))],
        out_specs=pl.BlockSpec((tb, MIX_W), lambda i: (i, 0)),
        out_shape=jax.ShapeDtypeStruct((t, MIX_W), BF16),
        scratch_shapes=[pltpu.VMEM((tb + pad, QKV_W), F32),
                        pltpu.VMEM((pad, QKV_W), F32),
                        pltpu.VMEM((HEAD_DIM, N_HEADS * HEAD_DIM), F32)],
        compiler_params=_params(("arbitrary",)),
        name="delta",
    )(qkv, z, gcol, grow, conv_w, g_onorm)


FOX_GROUP = 4
FOX_TQ, FOX_TK = 256, 2048
FOX_DIAG_PARTS = 2
FOX_ROW_STREAMS = 2
FOX_BOUND_SCALE, FOX_BOUND_SLACK = 1.01, 1.0
FOX_SKIP_LOG2 = -152.0


def _fox_kernel(*refs, tk):
    g = FOX_GROUP
    q_ref, o_ref, kmax_ref = refs[0], refs[-2], refs[-1]
    k_refs, v_refs, f_refs = (refs[1 + i * g:1 + (i + 1) * g] for i in range(3))
    qi = pl.program_id(1)
    tq = q_ref.shape[0]
    t_keys = k_refs[0].shape[0]

    @pl.when(qi == 0)
    def _():
        rows = min(t_keys, tk)
        for h in range(g):
            def norm_max(c, best, h=h):
                kc = k_refs[h][pl.ds(pl.multiple_of(c * rows, rows), rows), :].astype(F32)
                return jnp.maximum(best, jnp.sum(kc * kc, axis=-1, keepdims=True))

            best = lax.fori_loop(0, t_keys // rows, norm_max, jnp.zeros((rows, 1), F32))
            kmax_ref[h:h + 1, :] = jnp.broadcast_to(jnp.sqrt(jnp.max(best, axis=0, keepdims=True)), (1, LANES))

    def step(off, width, f_blk, carry, mask):
        ones = jnp.ones((width, HEAD_DIM), BF16)
        ss = []
        for h in range(g):
            q = q_ref[:, h * HEAD_DIM:(h + 1) * HEAD_DIM]
            s = _dot_nt(q, k_refs[h][pl.ds(off, width), :]) - f_blk(h) * LOG2E
            if mask is not None:
                s = jnp.where(mask, s, NEG_BIG)
            ss.append(s)
        ps, ms, alphas = [], [], []
        for h in range(g):
            m = carry[h][0]
            m_new = jnp.maximum(m, jnp.max(ss[h], axis=-1, keepdims=True))
            alphas.append(jnp.exp2(m - m_new))
            ps.append(jnp.exp2(ss[h] - m_new).astype(BF16))
            ms.append(m_new)
        out = []
        for h in range(g):
            v_aug = jnp.concatenate([v_refs[h][pl.ds(off, width), :], ones], axis=1)
            out.append((ms[h], alphas[h] * carry[h][1] + _dot(ps[h], v_aug)))
        return tuple(out)

    def head_chunk(h, j, mc):
        m, acc = mc
        off = pl.multiple_of(j * tk, tk)
        k_blk = k_refs[h][pl.ds(off, tk), :]
        v_aug = jnp.concatenate([v_refs[h][pl.ds(off, tk), :], jnp.ones((tk, HEAD_DIM), BF16)], axis=1)
        f_blk = f_refs[h][j] * LOG2E
        part = tq // FOX_ROW_STREAMS
        rows = [slice(i * part, (i + 1) * part) for i in range(FOX_ROW_STREAMS)]
        ss = [_dot_nt(q_ref[r, h * HEAD_DIM:(h + 1) * HEAD_DIM], k_blk) - f_blk for r in rows]
        m_new = [jnp.maximum(m[r], jnp.max(s, axis=-1, keepdims=True)) for r, s in zip(rows, ss)]
        ps = [jnp.exp2(s - mn).astype(BF16) for s, mn in zip(ss, m_new)]
        accs = [jnp.exp2(m[r] - mn) * acc[r] + _dot(p, v_aug) for r, mn, p in zip(rows, m_new, ps)]
        return jnp.concatenate(m_new, axis=0), jnp.concatenate(accs, axis=0)

    init = tuple((jnp.full((tq, 1), NEG_BIG, F32), jnp.zeros((tq, 2 * HEAD_DIM), F32)) for _ in range(g))
    n_full = (qi * tq) // tk

    parts = max(1, min(FOX_DIAG_PARTS, tk // tq))
    width = tk // parts
    row = lax.broadcasted_iota(I32, (tq, width), 0)
    col = lax.broadcasted_iota(I32, (tq, width), 1)
    carry = init
    for d in reversed(range(max(tq // tk, 1))):
        j = n_full + d
        for p in reversed(range(parts)):
            off = pl.multiple_of(j * tk + p * width, width)
            piece = functools.partial(step, off, width,
                                      lambda h, j=j, p=p: f_refs[h][j][:, p * width:(p + 1) * width],
                                      mask=col + (off - qi * tq) <= row)
            carry = piece(carry) if p == 0 else lax.cond(off < (qi + 1) * tq, piece, lambda c: c, carry)

    qn = []
    for h in range(g):
        q = q_ref[:, h * HEAD_DIM:(h + 1) * HEAD_DIM].astype(F32)
        qk_max = jnp.sqrt(jnp.sum(q * q, axis=-1, keepdims=True)) * kmax_ref[h:h + 1, 0:1]
        qn.append(qk_max * FOX_BOUND_SCALE + FOX_BOUND_SLACK)

    def reaches(j, carry):
        jc = jnp.maximum(j, 0)
        flags = []
        for h in range(g):
            bound = qn[h] - f_refs[h][jc][:, tk - 1:tk] * LOG2E
            flags.append((jnp.max(bound - carry[h][0]) >= FOX_SKIP_LOG2).astype(I32))
        return tuple(flags)

    def chunk_all_heads(state):
        j, _, carry = state
        carry = step(pl.multiple_of(j * tk, tk), tk, lambda h: f_refs[h][j], carry, None)
        return j - 1, reaches(j - 1, carry), carry

    def chunk_live_heads(state):
        j, live, carry = state
        carry = tuple(lax.cond(live[h] > 0, functools.partial(head_chunk, h, j), lambda mc: mc, carry[h])
                      for h in range(g))
        return j - 1, reaches(j - 1, carry), carry

    state = (n_full - 1, reaches(n_full - 1, carry), carry)
    state = lax.while_loop(lambda st: (st[0] >= 0) & (sum(st[1]) == g), chunk_all_heads, state)
    _, _, carry = lax.while_loop(lambda st: (st[0] >= 0) & (sum(st[1]) > 0), chunk_live_heads, state)
    for h in range(g):
        acc = carry[h][1]
        o_ref[:, h * HEAD_DIM:(h + 1) * HEAD_DIM] = (acc[:, :HEAD_DIM] / acc[:, HEAD_DIM:]).astype(o_ref.dtype)


def _fox(fx, f_rows, tq, tk):
    t = fx.shape[0]
    g = FOX_GROUP
    heads = lambda hg, u: hg * g + u
    q_specs = [pl.BlockSpec((tq, g * HEAD_DIM), lambda hg, i: (i, hg))]
    k_specs = [pl.BlockSpec((t, HEAD_DIM), lambda hg, i, u=u: (0, N_HEADS + heads(hg, u)),
                            pipeline_mode=pl.Buffered(1)) for u in range(g)]
    v_specs = [pl.BlockSpec((t, HEAD_DIM), lambda hg, i, u=u: (0, 2 * N_HEADS + heads(hg, u)),
                            pipeline_mode=pl.Buffered(1)) for u in range(g)]
    f_specs = [pl.BlockSpec((None, t // tk, 1, tk), lambda hg, i, u=u: (heads(hg, u), 0, 0, 0)) for u in range(g)]
    return pl.pallas_call(
        functools.partial(_fox_kernel, tk=tk),
        grid=(N_HEADS // g, t // tq),
        in_specs=q_specs + k_specs + v_specs + f_specs,
        out_specs=pl.BlockSpec((tq, g * HEAD_DIM), lambda hg, i: (i, hg)),
        out_shape=jax.ShapeDtypeStruct((t, MIX_W), BF16),
        scratch_shapes=[pltpu.VMEM((8, LANES), F32)],
        compiler_params=_params(("arbitrary", "arbitrary")),
        name="fox",
    )(*([fx] * (1 + 2 * g) + [f_rows] * g))


def _merge_kernel(oa_ref, ob_ref, ga_ref, gb_ref, x_ref, woa_ref, wob_ref, wout_ref, gt1_ref, g2_ref, sc2_ref,
                  sh2_ref, wr_ref, br_ref, x1_ref, h2_ref, ri_ref, gate_ref, cnt_ref, carry_ref):
    i = pl.program_id(0)
    tm = x_ref.shape[0]

    @pl.when(i == 0)
    def _():
        carry_ref[...] = jnp.zeros_like(carry_ref)

    ya = _dot(oa_ref[...], woa_ref[...])
    yb = _dot(ob_ref[...], wob_ref[...])
    merged = _sigmoid(ga_ref[...].astype(F32)) * ya + _sigmoid(gb_ref[...].astype(F32)) * yb
    x1 = x_ref[...] + gt1_ref[...] * _dot(merged.astype(BF16), wout_ref[...])
    x1_ref[...] = x1
    y = x1 * lax.rsqrt(jnp.mean(x1 * x1, axis=-1, keepdims=True) + RMS_EPS) * g2_ref[...]
    h2 = y * (1.0 + sc2_ref[...]) + sh2_ref[...]
    _store_token_tiles(h2_ref, h2)

    logits = _dot_split(h2, wr_ref[...]) + br_ref[...]
    lane = lax.broadcasted_iota(I32, logits.shape, 1)
    lanef = lane.astype(F32)
    cur = jnp.where(lane < N_EXPERTS, logits, -jnp.inf)
    vals, idxs = [], []
    for _ in range(TOP_K):
        m = jnp.max(cur, axis=-1, keepdims=True)
        ix = jnp.min(jnp.where(cur == m, lanef, float(LANES)), axis=-1, keepdims=True)
        vals.append(m)
        idxs.append(ix)
        cur = jnp.where(lanef == ix, -jnp.inf, cur)
    exps = [jnp.exp(v - vals[0]) for v in vals]
    denom = exps[0] + exps[1] + exps[2] + exps[3]

    onehot = jnp.zeros(logits.shape, F32)
    for ix in idxs:
        onehot = onehot + jnp.where(lanef == ix, 1.0, 0.0)
    r = lax.broadcasted_iota(I32, (tm, tm), 0)
    c = lax.broadcasted_iota(I32, (tm, tm), 1)
    strict = jnp.where(r > c, 1.0, 0.0).astype(BF16)
    before = _dot(strict, onehot.astype(BF16)) + carry_ref[...]
    carry_ref[...] = carry_ref[...] + jnp.sum(onehot, axis=0, keepdims=True)
    cnt_ref[...] = carry_ref[...].astype(I32)

    ri = jnp.zeros(logits.shape, F32)
    gates = jnp.zeros(logits.shape, F32)
    for kk in range(TOP_K):
        rank = jnp.sum(jnp.where(lanef == idxs[kk], before, 0.0), axis=-1, keepdims=True)
        ri = ri + jnp.where(lane == kk, idxs[kk], 0.0) + jnp.where(lane == TOP_K + kk, rank, 0.0)
        gates = gates + jnp.where(lane == kk, exps[kk] / denom, 0.0)
    ri_ref[...] = ri.astype(I32)
    gate_ref[...] = gates


def _merge(o_a, o_b, gate_a, gate_b, x, woa, wob, wout, gt1, g2, sc2, sh2, wr, br, tm):
    t, d = x.shape
    row = lambda i: (i, 0)
    fixed = lambda i: (0, 0)
    return pl.pallas_call(
        _merge_kernel,
        grid=(t // tm,),
        in_specs=[pl.BlockSpec((tm, MIX_W), row), pl.BlockSpec((tm, MIX_W), row),
                  pl.BlockSpec((tm, d), row), pl.BlockSpec((tm, d), row), pl.BlockSpec((tm, d), row),
                  pl.BlockSpec((MIX_W, d), fixed), pl.BlockSpec((MIX_W, d), fixed), pl.BlockSpec((d, d), fixed),
                  pl.BlockSpec((1, d), fixed), pl.BlockSpec((1, d), fixed), pl.BlockSpec((1, d), fixed),
                  pl.BlockSpec((1, d), fixed), pl.BlockSpec((d, LANES), fixed), pl.BlockSpec((1, LANES), fixed)],
        out_specs=[pl.BlockSpec((tm, d), row), pl.BlockSpec((tm * TOKEN_TILE_ROWS, LANES), row),
                   pl.BlockSpec((tm, LANES), row),
                   pl.BlockSpec((tm, LANES), row), pl.BlockSpec((1, LANES), fixed)],
        out_shape=[jax.ShapeDtypeStruct((t, d), F32), jax.ShapeDtypeStruct((t * TOKEN_TILE_ROWS, LANES), F32),
                   jax.ShapeDtypeStruct((t, LANES), I32), jax.ShapeDtypeStruct((t, LANES), F32),
                   jax.ShapeDtypeStruct((1, LANES), I32)],
        scratch_shapes=[pltpu.VMEM((1, LANES), F32)],
        compiler_params=_params(("arbitrary",)),
        name="merge",
    )(o_a, o_b, gate_a, gate_b, x, woa, wob, wout, gt1, g2, sc2, sh2, wr, br)


def _dest_kernel(ri_ref, cnt_ref, dest_ref, blk_ref, pend_ref):
    shift = EXPERT_BLOCK.bit_length() - 1
    cnt = jnp.broadcast_to(cnt_ref[...], (8, LANES))
    lane_row = lax.broadcasted_iota(I32, cnt.shape, 1)
    padded = jnp.where(lane_row < N_EXPERTS, ((cnt + (EXPERT_BLOCK - 1)) >> shift) << shift, 0)
    pend = padded
    s = 1
    while s < N_EXPERTS:
        pend = pend + jnp.where(lane_row >= s, pltpu.roll(pend, s, 1), 0)
        s *= 2
    pstart = (pend - padded)[0:1, :].astype(F32)
    pend = pend[0:1, :]
    pend_ref[...] = pend

    ri = ri_ref[...]
    lane = lax.broadcasted_iota(I32, ri.shape, 1)
    dest = jnp.zeros(ri.shape, F32)
    for kk in range(TOP_K):
        ix = ri[:, kk:kk + 1]
        rank = ri[:, TOP_K + kk:TOP_K + kk + 1].astype(F32)
        start = jnp.sum(jnp.where(lane == ix, pstart, 0.0), axis=-1, keepdims=True)
        dest = dest + jnp.where(lane == kk, start + rank, 0.0)
    dest_ref[...] = dest.astype(I32)

    bstart = lax.broadcasted_iota(I32, blk_ref.shape, 0) * EXPERT_BLOCK
    lane_b = lax.broadcasted_iota(I32, blk_ref.shape, 1)
    ended = jnp.where((lane_b < N_EXPERTS) & (pend <= bstart), 1.0, 0.0)
    e = jnp.minimum(jnp.sum(ended, axis=-1, keepdims=True), float(N_EXPERTS - 1))
    vend = pstart + cnt[0:1, :].astype(F32)
    vend_e = jnp.sum(jnp.where(lane_b.astype(F32) == e, vend, 0.0), axis=-1, keepdims=True)
    nvalid = jnp.clip(vend_e - bstart[:, 0:1].astype(F32), 0.0, float(EXPERT_BLOCK))
    blk_ref[...] = jnp.where(lane_b == 0, e, jnp.where(lane_b == 1, nvalid, 0.0)).astype(I32)


def _dest(ri, cnt, n_blocks, tm):
    t = ri.shape[0]
    nb_pad = -(-n_blocks // 8) * 8
    return pl.pallas_call(
        _dest_kernel,
        grid=(t // tm,),
        in_specs=[pl.BlockSpec((tm, LANES), lambda i: (i, 0)), pl.BlockSpec((1, LANES), lambda i: (0, 0))],
        out_specs=[pl.BlockSpec((tm, LANES), lambda i: (i, 0)), pl.BlockSpec((nb_pad, LANES), lambda i: (0, 0)),
                   pl.BlockSpec((1, LANES), lambda i: (0, 0))],
        out_shape=[jax.ShapeDtypeStruct((t, LANES), I32), jax.ShapeDtypeStruct((nb_pad, LANES), I32),
                   jax.ShapeDtypeStruct((1, LANES), I32)],
        compiler_params=_params(("arbitrary",)),
        name="dest",
    )(ri, cnt)


ROW_UNROLL = 8


def _drain_tokens(src_ref, dst_ref, sem, n_tokens):
    def drain(_, carry):
        for _ in range(ROW_UNROLL * TOP_K):
            _token_copy(src_ref, 0, dst_ref, 0, sem).wait()
        return carry

    lax.fori_loop(0, n_tokens // ROW_UNROLL, drain, 0)


def _dispatch_kernel(dest_ref, nvalid_ref, h_ref, xbuf_ref, zero_ref, sem, fill_sem):
    tm = h_ref.shape[0] // TOKEN_TILE_ROWS

    @pl.when(pl.program_id(0) == 0)
    def _():
        zero_ref[...] = jnp.zeros_like(zero_ref)

        def padding_copies(b, act):
            pad = EXPERT_BLOCK - nvalid_ref[b]

            @pl.when(pad > 0)
            def _():
                first = b * EXPERT_BLOCK + nvalid_ref[b]
                size = EXPERT_BLOCK
                while size >= 1:
                    take = (pad & size) != 0
                    rows = size * TOKEN_TILE_ROWS
                    dst0 = pl.multiple_of(first * TOKEN_TILE_ROWS, TOKEN_TILE_ROWS)

                    @pl.when(take)
                    def _(rows=rows, dst0=dst0):
                        act(pltpu.make_async_copy(zero_ref.at[pl.ds(0, rows), :],
                                                  xbuf_ref.at[pl.ds(dst0, rows), :], fill_sem))

                    first = first + jnp.where(take, size, 0)
                    size //= 2

        n_blocks = nvalid_ref.shape[0]
        lax.fori_loop(0, n_blocks, lambda b, c: (padding_copies(b, lambda cp: cp.start()), c)[1], 0)
        lax.fori_loop(0, n_blocks, lambda b, c: (padding_copies(b, lambda cp: cp.wait()), c)[1], 0)

    def issue(g, carry):
        for u in range(ROW_UNROLL):
            r = g * ROW_UNROLL + u
            for kk in range(TOP_K):
                _token_copy(h_ref, r, xbuf_ref, dest_ref[r * TOP_K + kk], sem).start(priority=kk % 2)
        return carry

    lax.fori_loop(0, tm // ROW_UNROLL, issue, 0)
    _drain_tokens(h_ref, xbuf_ref, sem, tm)


def _dispatch(dest_flat, nvalid, h2_tiles):
    tm = ROUTE_TILE
    t = h2_tiles.shape[0] // TOKEN_TILE_ROWS
    n_rows = nvalid.shape[0] * EXPERT_BLOCK
    return pl.pallas_call(
        _dispatch_kernel,
        grid=(t // tm,),
        in_specs=[pl.BlockSpec((tm * TOP_K,), lambda i: (i,), memory_space=pltpu.SMEM),
                  pl.BlockSpec(memory_space=pltpu.SMEM),
                  pl.BlockSpec((tm * TOKEN_TILE_ROWS, LANES), lambda i: (i, 0))],
        out_specs=pl.BlockSpec(memory_space=pl.ANY),
        out_shape=jax.ShapeDtypeStruct((n_rows * TOKEN_TILE_ROWS, LANES), h2_tiles.dtype),
        scratch_shapes=[pltpu.VMEM((EXPERT_BLOCK * TOKEN_TILE_ROWS, LANES), h2_tiles.dtype),
                        pltpu.SemaphoreType.DMA(()), pltpu.SemaphoreType.DMA(())],
        compiler_params=_params(("arbitrary",)),
        name="dispatch",
    )(dest_flat, nvalid, h2_tiles)


def _experts_kernel(blk_e_ref, nvalid_ref, grp_ref, nxt_e_ref, x_ref, wgu_hbm, bgu_ref, wd_hbm, bd_ref, y_ref,
                    wgu32_ref, wd32_ref, wgu16_ref, wd16_ref, sems):
    b = pl.program_id(0)
    nvalid = nvalid_ref[b]
    e = blk_e_ref[b]
    slot = grp_ref[b] % 2
    first = (b == 0) | (e != blk_e_ref[jnp.maximum(b - 1, 0)])

    def weight_copies(expert, s):
        return (pltpu.make_async_copy(wgu_hbm.at[expert], wgu32_ref.at[s], sems.at[s, 0]),
                pltpu.make_async_copy(wd_hbm.at[expert], wd32_ref.at[s], sems.at[s, 1]))

    @pl.when(b == 0)
    def _():
        for cp in weight_copies(e, 0):
            cp.start()

    @pl.when(first)
    def _():
        for cp in weight_copies(e, slot):
            cp.wait()
        wgu16_ref[...] = wgu32_ref[slot].astype(BF16)
        wd16_ref[...] = wd32_ref[slot].astype(BF16)

    @pl.when(first & (nxt_e_ref[b] != e))
    def _():
        for cp in weight_copies(nxt_e_ref[b], 1 - slot):
            cp.start()

    n_blk = x_ref.shape[0] // TOKEN_TILE_ROWS
    n_half = n_blk // 2

    def mlp(n):
        x = jnp.concatenate([_load_token_slab(x_ref, s, n) for s in range(TOKEN_TILE_ROWS)], axis=1)
        gu = _dot(x.astype(BF16), wgu16_ref[...]) + bgu_ref[...]
        gate = jnp.minimum(gu[:, :FF], SWIGLU_LIMIT)
        up = jnp.clip(gu[:, FF:], -SWIGLU_LIMIT, SWIGLU_LIMIT)
        act = (up + 1.0) * (gate * _sigmoid(SWIGLU_ALPHA * gate))
        _store_token_tiles(y_ref, _dot(act.astype(BF16), wd16_ref[...]) + bd_ref[...])
        if n < n_blk:
            y_ref[n * TOKEN_TILE_ROWS:, :] = jnp.zeros(((n_blk - n) * TOKEN_TILE_ROWS, LANES), y_ref.dtype)

    @pl.when(nvalid > n_half)
    def _():
        mlp(n_blk)

    @pl.when((nvalid > 0) & (nvalid <= n_half))
    def _():
        mlp(n_half)

    @pl.when(nvalid <= 0)
    def _():
        y_ref[...] = jnp.zeros_like(y_ref)


def _experts(blk_e, nvalid, xbuf, wgu, bgu, wd, bd):
    d = D_MODEL
    blk_rows = EXPERT_BLOCK * TOKEN_TILE_ROWS
    nb = xbuf.shape[0] // blk_rows
    change = jnp.concatenate([jnp.zeros((1,), I32), (blk_e[1:] != blk_e[:-1]).astype(I32)])
    grp = jnp.cumsum(change)
    pos = jnp.where(change > 0, jnp.arange(nb, dtype=I32), nb)
    nxt_pos = lax.cummin(jnp.concatenate([pos[1:], jnp.full((1,), nb, I32)]), reverse=True)
    nxt_e = jnp.where(nxt_pos < nb, blk_e[jnp.minimum(nxt_pos, nb - 1)], blk_e)
    grid_spec = pltpu.PrefetchScalarGridSpec(
        num_scalar_prefetch=4,
        grid=(nb,),
        in_specs=[pl.BlockSpec((blk_rows, LANES), lambda b, e, *_: (b, 0)),
                  pl.BlockSpec(memory_space=pl.ANY),
                  pl.BlockSpec((None, 1, 2 * FF), lambda b, e, *_: (e[b], 0, 0)),
                  pl.BlockSpec(memory_space=pl.ANY),
                  pl.BlockSpec((None, 1, d), lambda b, e, *_: (e[b], 0, 0))],
        out_specs=pl.BlockSpec((blk_rows, LANES), lambda b, e, *_: (b, 0)),
        scratch_shapes=[pltpu.VMEM((2, d, 2 * FF), F32), pltpu.VMEM((2, FF, d), F32),
                        pltpu.VMEM((d, 2 * FF), BF16), pltpu.VMEM((FF, d), BF16),
                        pltpu.SemaphoreType.DMA((2, 2))],
    )
    return pl.pallas_call(
        _experts_kernel,
        grid_spec=grid_spec,
        out_shape=jax.ShapeDtypeStruct(xbuf.shape, F32),
        compiler_params=_params(("arbitrary",)),
        name="experts",
    )(blk_e, nvalid, grp, nxt_e, xbuf, wgu, bgu, wd, bd)


def _combine_kernel(dest_ref, dest_next_ref, ybuf_ref, x1_ref, gate_ref, gt2_ref, gf_ref, o_ref, rows_ref, sems):
    i = pl.program_id(0)
    tm = x1_ref.shape[0]
    slot = i % 2

    def start_gather(idx_ref, s):
        def issue(g, carry):
            for u in range(ROW_UNROLL):
                r = g * ROW_UNROLL + u
                for kk in range(TOP_K):
                    _token_copy(ybuf_ref, idx_ref[r * TOP_K + kk], rows_ref.at[s, kk], r,
                                sems.at[s]).start(priority=kk % 2)
            return carry

        lax.fori_loop(0, tm // ROW_UNROLL, issue, 0)

    @pl.when(i == 0)
    def _():
        start_gather(dest_ref, 0)

    @pl.when(i + 1 < pl.num_programs(0))
    def _():
        start_gather(dest_next_ref, 1 - slot)

    _drain_tokens(ybuf_ref, rows_ref.at[slot, 0], sems.at[slot], tm)

    gates = gate_ref[...]
    slabs = []
    for s in range(TOKEN_TILE_ROWS):
        moe = gates[:, 0:1] * _load_token_slab(rows_ref.at[slot, 0], s, tm)
        for kk in range(1, TOP_K):
            moe = moe + gates[:, kk:kk + 1] * _load_token_slab(rows_ref.at[slot, kk], s, tm)
        slabs.append(moe)
    xo = x1_ref[...] + gt2_ref[...] * jnp.concatenate(slabs, axis=1)
    o_ref[...] = xo * lax.rsqrt(jnp.mean(xo * xo, axis=-1, keepdims=True) + RMS_EPS) * gf_ref[...]


def _combine(dest_flat, ybuf, x1, gates, gt2, g_final):
    t, d = x1.shape
    tm = ROUTE_TILE
    n = t // tm
    return pl.pallas_call(
        _combine_kernel,
        grid=(n,),
        in_specs=[pl.BlockSpec((tm * TOP_K,), lambda i: (i,), memory_space=pltpu.SMEM),
                  pl.BlockSpec((tm * TOP_K,), lambda i: (jnp.minimum(i + 1, n - 1),), memory_space=pltpu.SMEM),
                  pl.BlockSpec(memory_space=pl.ANY),
                  pl.BlockSpec((tm, d), lambda i: (i, 0)),
                  pl.BlockSpec((tm, LANES), lambda i: (i, 0)),
                  pl.BlockSpec((1, d), lambda i: (0, 0)),
                  pl.BlockSpec((1, d), lambda i: (0, 0))],
        out_specs=pl.BlockSpec((tm, d), lambda i: (i, 0)),
        out_shape=jax.ShapeDtypeStruct((t, d), F32),
        scratch_shapes=[pltpu.VMEM((2, TOP_K, tm * TOKEN_TILE_ROWS, LANES), F32), pltpu.SemaphoreType.DMA((2,))],
        compiler_params=_params(("arbitrary",)),
        name="combine",
    )(dest_flat, dest_flat, ybuf, x1, gates, gt2, g_final)


def _pad_lanes(v, lane0):
    return jnp.zeros((1, LANES), F32).at[0, lane0:lane0 + v.shape[0]].set(v.astype(F32))


def _layer(x, mod, g_norm1, w_in, conv_w, a_log, dt_bias, g_onorm, b_fgate, w_o_delta, w_o_fox, w_out,
           g_norm2, w_router, b_router, w_gate_up, b_gate_up, w_down, b_down, g_final):
    t, d = x.shape
    sh1, sc1, gt1, sh2, sc2, gt2 = [mod[:, i * d:(i + 1) * d] for i in range(N_MOD)]

    w_t = w_in.T
    o = 0
    wq = w_t[o:o + QKV_W]; o += QKV_W
    wz = w_t[o:o + MIX_W]; o += MIX_W
    w_beta = w_t[o:o + N_HEADS]; o += N_HEADS
    w_dec = w_t[o:o + N_HEADS]; o += N_HEADS
    wf = w_t[o:o + QKV_W]; o += QKV_W
    wf = jnp.concatenate([wf[:MIX_W] * (LOG2E * HEAD_DIM ** -0.5), wf[MIX_W:]], axis=0)
    w_fg = w_t[o:o + N_HEADS]; o += N_HEADS
    wga = w_t[o:o + d]; o += d
    wgb = w_t[o:o + d]
    ws = jnp.zeros((LANES, d), F32)
    ws = ws.at[L_BETA:L_BETA + N_HEADS].set(w_beta).at[L_G:L_G + N_HEADS].set(w_dec)
    ws = ws.at[L_F:L_F + N_HEADS].set(w_fg)
    bf = lambda w: w.astype(BF16)

    tm = min(ROW_TILE, t)
    qkv, z, fx, gate_a, gate_b, small = _inproj(
        x, g_norm1.reshape(1, d), sc1, sh1, bf(wq), bf(wz), bf(wf), bf(wga), bf(wgb), bf(ws), tm)

    gcol, grow = _gates(small, _pad_lanes(a_log, L_G), _pad_lanes(dt_bias, L_G), _pad_lanes(b_fgate, L_F),
                        min(GATES_TILE, t))

    o_a = _delta(qkv, z, gcol, grow, conv_w, g_onorm.reshape(1, HEAD_DIM), min(DELTA_TILE, t))
    tq, tk = min(FOX_TQ, t), min(FOX_TK, t)
    f_rows = grow[L_F:L_F + N_HEADS].reshape(N_HEADS, t // tk, 1, tk)
    o_b = _fox(fx, f_rows, tq, tk)

    wr = jnp.zeros((d, LANES), F32).at[:, :N_EXPERTS].set(w_router)
    br = _pad_lanes(b_router, 0)
    x1, h2, ri, gates, cnt = _merge(o_a, o_b, gate_a, gate_b, x, bf(w_o_delta), bf(w_o_fox), bf(w_out), gt1,
                                    g_norm2.reshape(1, d), sc2, sh2, wr, br, tm)

    n_blocks = (t * TOP_K) // EXPERT_BLOCK + N_EXPERTS
    dest, blk, _ = _dest(ri, cnt, n_blocks, min(2048, t))
    dest_flat = dest[:, :TOP_K].reshape(t * TOP_K)
    blk_e, nvalid = blk[:n_blocks, 0], blk[:n_blocks, 1]

    xbuf = _dispatch(dest_flat, nvalid, h2)
    ybuf = _experts(blk_e, nvalid, xbuf, w_gate_up, b_gate_up.reshape(N_EXPERTS, 1, 2 * FF),
                    w_down, b_down.reshape(N_EXPERTS, 1, d))
    return _combine(dest_flat, ybuf, x1, gates, gt2, g_final.reshape(1, d))


def kernel(x, c, w_ada, b_ada, g_norm1, w_in, conv_w, a_log, dt_bias, g_onorm, b_fgate, w_o_delta, w_o_fox, w_out,
           g_norm2, w_router, b_router, w_gate_up, b_gate_up, w_down, b_down, g_final):
    b, s, d = x.shape
    assert b == 1 and d == D_MODEL and w_ada.shape[0] == 1
    h = x[0]
    for l in range(w_ada.shape[0]):
        mod = _modulation(c[0], w_ada[l], b_ada[l])
        h = _layer(h, mod, g_norm1[l], w_in[l], conv_w[l], a_log[l], dt_bias[l], g_onorm[l], b_fgate[l],
                   w_o_delta[l], w_o_fox[l], w_out[l], g_norm2[l], w_router[l], b_router[l], w_gate_up[l],
                   b_gate_up[l], w_down[l], b_down[l], g_final)
    return h[None]
```

```python
import functools

import jax
import jax.numpy as jnp
from jax import lax
from jax.experimental import pallas as pl
from jax.experimental.pallas import tpu as pltpu

F32 = jnp.float32
BF16 = jnp.bfloat16
I32 = jnp.int32

D_MODEL = 1024
HEAD_DIM = 128
N_HEADS = 4
MIX_W = N_HEADS * HEAD_DIM
QKV_W = 3 * MIX_W
CONV_K = 4
CHUNK = 64
STACK = N_HEADS * CHUNK
SUB = 16
N_EXPERTS = 32
TOP_K = 4
FF = D_MODEL
SWIGLU_LIMIT = 7.0
SWIGLU_ALPHA = 1.702
RMS_EPS = 1e-6
L2_EPS = 1e-6
N_MOD = 6
LANES = 128
NEG_BIG = -1e30
LOG2E = 1.4426950408889634

SUBLANES = 8
MOD_TILE = 1024
ROW_TILE = 512
GATES_TILE = 256
GATE_ROWS = 24
DELTA_TILE = 512
EXPERT_BLOCK = 512
ROUTE_TILE = 256
VMEM_LIMIT = 56 * 1024 * 1024


def _params(sem):
    return pltpu.CompilerParams(dimension_semantics=sem, vmem_limit_bytes=VMEM_LIMIT)


def _softplus(x):
    return jnp.maximum(x, 0.0) + jnp.log(1.0 + jnp.exp(-jnp.abs(x)))


def _sigmoid(x):
    return 1.0 / (1.0 + jnp.exp(-x))


def _silu(x):
    return x * _sigmoid(x)


def _dot(a, b):
    return jnp.dot(a, b, preferred_element_type=F32)


def _dot_nt(a, b):
    return lax.dot_general(a, b, (((1,), (1,)), ((), ())), preferred_element_type=F32)


def _split_bf16(x, terms):
    parts = []
    for _ in range(terms):
        p = x.astype(BF16)
        parts.append(p)
        x = x - p.astype(F32)
    return parts


def _dot_split(a, b):
    a_hi, a_lo = _split_bf16(a, 2)
    b_hi, b_lo = _split_bf16(b, 2)
    return _dot(a_hi, b_hi) + (_dot(a_hi, b_lo) + _dot(a_lo, b_hi))


def _dot_mask(mask01, x):
    return sum(_dot(mask01, p) for p in _split_bf16(x, 3))


TOKEN_TILE_ROWS = D_MODEL // LANES


def _store_token_tiles(ref, x):
    n = x.shape[0]
    for s in range(TOKEN_TILE_ROWS):
        ref[pl.ds(s, n, stride=TOKEN_TILE_ROWS), :] = x[:, s * LANES:(s + 1) * LANES]


def _load_token_slab(ref, s, n):
    return ref[pl.ds(s, n, stride=TOKEN_TILE_ROWS), :]


def _token_copy(src_ref, src_tok, dst_ref, dst_tok, sem):
    rows = lambda i: pl.ds(pl.multiple_of(i * TOKEN_TILE_ROWS, TOKEN_TILE_ROWS), TOKEN_TILE_ROWS)
    return pltpu.make_async_copy(src_ref.at[rows(src_tok), :], dst_ref.at[rows(dst_tok), :], sem)


def _mod_kernel(c_ref, w_ref, b_ref, o_ref):
    o_ref[...] = jnp.sum(c_ref[...] * w_ref[...], axis=0, keepdims=True) + b_ref[...]


def _modulation(c, w_ada, b_ada):
    d, n = w_ada.shape
    tn = MOD_TILE
    return pl.pallas_call(
        _mod_kernel,
        grid=(n // tn,),
        in_specs=[pl.BlockSpec((d, 1), lambda j: (0, 0)),
                  pl.BlockSpec((d, tn), lambda j: (0, j)),
                  pl.BlockSpec((1, tn), lambda j: (0, j))],
        out_specs=pl.BlockSpec((1, tn), lambda j: (0, j)),
        out_shape=jax.ShapeDtypeStruct((1, n), F32),
        compiler_params=_params(("arbitrary",)),
        name="mod",
    )(c.reshape(d, 1), w_ada, b_ada.reshape(1, n))


def _inproj_kernel(x_ref, g_ref, sc_ref, sh_ref, wq_ref, wz_ref, wf_ref, wga_ref, wgb_ref, ws_ref,
                   oq_ref, oz_ref, of_ref, oga_ref, ogb_ref, os_ref):
    x = x_ref[...]
    y = x * lax.rsqrt(jnp.mean(x * x, axis=-1, keepdims=True) + RMS_EPS) * g_ref[...]
    h = (y * (1.0 + sc_ref[...]) + sh_ref[...]).astype(BF16)
    oq_ref[...] = _dot_nt(h, wq_ref[...])
    oz_ref[...] = _dot_nt(h, wz_ref[...]).astype(BF16)
    of_ref[...] = _dot_nt(h, wf_ref[...]).astype(BF16)
    oga_ref[...] = _dot_nt(h, wga_ref[...]).astype(BF16)
    ogb_ref[...] = _dot_nt(h, wgb_ref[...]).astype(BF16)
    os_ref[...] = _dot_nt(h, ws_ref[...])


def _inproj(x, g1, sc1, sh1, wq, wz, wf, wga, wgb, ws, tm):
    t, d = x.shape
    row = lambda i: (i, 0)
    fixed = lambda i: (0, 0)
    ws_list = [wq, wz, wf, wga, wgb, ws]
    out_dt = [F32, BF16, BF16, BF16, BF16, F32]
    return pl.pallas_call(
        _inproj_kernel,
        grid=(t // tm,),
        in_specs=[pl.BlockSpec((tm, d), row)] + [pl.BlockSpec((1, d), fixed)] * 3
                 + [pl.BlockSpec(w.shape, fixed) for w in ws_list],
        out_specs=[pl.BlockSpec((tm, w.shape[0]), row) for w in ws_list],
        out_shape=[jax.ShapeDtypeStruct((t, w.shape[0]), dt) for w, dt in zip(ws_list, out_dt)],
        compiler_params=_params(("arbitrary",)),
        name="inproj",
    )(x, g1, sc1, sh1, *ws_list)


L_BETA, L_G, L_F, L_GC, L_GL = 0, 4, 8, 12, 16


def _gates_kernel(s_ref, alog_ref, dtb_ref, bf_ref, col_ref, row_ref, carry_ref):
    i = pl.program_id(0)
    tm = s_ref.shape[0]

    @pl.when(i == 0)
    def _():
        carry_ref[...] = jnp.zeros_like(carry_ref)

    s = s_ref[...]
    lane = lax.broadcasted_iota(I32, s.shape, 1)
    beta = _sigmoid(s)
    g = -jnp.exp(alog_ref[...]) * _softplus(s + dtb_ref[...])
    logf = -_softplus(-(s + bf_ref[...]))
    is_g = (lane >= L_G) & (lane < L_G + N_HEADS)
    is_f = (lane >= L_F) & (lane < L_F + N_HEADS)
    g = jnp.where(is_g, g, 0.0)
    logf = jnp.where(is_f, logf, 0.0)

    r = lax.broadcasted_iota(I32, (tm, tm), 0)
    c = lax.broadcasted_iota(I32, (tm, tm), 1)
    same_chunk = (r // CHUNK) == (c // CHUNK)
    tri = jnp.where(r >= c, 1.0, 0.0).astype(BF16)
    tri_chunk = jnp.where(same_chunk & (r >= c), 1.0, 0.0).astype(BF16)
    ones_chunk = jnp.where(same_chunk, 1.0, 0.0).astype(BF16)
    f_cum = _dot_mask(tri, logf) + carry_ref[...]
    carry_ref[...] = f_cum[tm - 1:tm, :]
    gc = _dot_mask(tri_chunk, g)
    gl = _dot_mask(ones_chunk, g)

    out = jnp.where(lane < N_HEADS, beta, 0.0) + g + f_cum
    out = out + pltpu.roll(gc, L_GC - L_G, 1) + pltpu.roll(gl, L_GL - L_G, 1)
    col_ref[...] = out
    row_ref[...] = out.T[:row_ref.shape[0], :]


def _gates(small, alog_row, dtb_row, bf_row, tm):
    t = small.shape[0]
    n_rows = GATE_ROWS
    return pl.pallas_call(
        _gates_kernel,
        grid=(t // tm,),
        in_specs=[pl.BlockSpec((tm, LANES), lambda i: (i, 0))] + [pl.BlockSpec((1, LANES), lambda i: (0, 0))] * 3,
        out_specs=[pl.BlockSpec((tm, LANES), lambda i: (i, 0)), pl.BlockSpec((n_rows, tm), lambda i: (0, i))],
        out_shape=[jax.ShapeDtypeStruct((t, LANES), F32), jax.ShapeDtypeStruct((n_rows, t), F32)],
        scratch_shapes=[pltpu.VMEM((1, LANES), F32)],
        compiler_params=_params(("arbitrary",)),
        name="gates",
    )(small, alog_row, dtb_row, bf_row)


def _stack_heads(a, col0):
    return jnp.concatenate([a[:, col0 + h * HEAD_DIM: col0 + (h + 1) * HEAD_DIM] for h in range(N_HEADS)], axis=0)


def _stack_cols(a, lane0):
    return jnp.concatenate([a[:, lane0 + h: lane0 + h + 1] for h in range(N_HEADS)], axis=0)


def _delta_kernel(qkv_ref, z_ref, gcol_ref, grow_ref, cw_ref, gon_ref, o_ref, ext_ref, tail_ref, s_ref):
    i = pl.program_id(0)
    tb = qkv_ref.shape[0]
    pad = tail_ref.shape[0]

    @pl.when(i == 0)
    def _():
        tail_ref[...] = jnp.zeros_like(tail_ref)
        s_ref[...] = jnp.zeros_like(s_ref)

    ext_ref[0:pad, :] = tail_ref[...]
    ext_ref[pad:pad + tb, :] = qkv_ref[...]
    tail_ref[...] = qkv_ref[tb - pad:tb, :]
    conv = cw_ref[0:1, :] * ext_ref[pad - 3:pad - 3 + tb, :]
    for j in range(1, CONV_K):
        conv = conv + cw_ref[j:j + 1, :] * ext_ref[pad - 3 + j:pad - 3 + j + tb, :]
    act = _silu(conv)

    r = lax.broadcasted_iota(I32, (STACK, STACK), 0)
    c = lax.broadcasted_iota(I32, (STACK, STACK), 1)
    same = (r // CHUNK) == (c // CHUNK)
    m_incl = same & (r >= c)
    m_strict = same & (r > c)
    eye = jnp.where(r == c, 1.0, 0.0)
    m_sub = []
    size = SUB
    while size <= CHUNK:
        m_sub.append((r // size) == (c // size))
        size *= 2
    rb = lax.broadcasted_iota(I32, (STACK, N_HEADS * HEAD_DIM), 0) // CHUNK
    cb = lax.broadcasted_iota(I32, (STACK, N_HEADS * HEAD_DIM), 1) // HEAD_DIM
    head_match = rb == cb

    chunks = range(tb // CHUNK)
    pre = []
    for ch in chunks:
        r0 = ch * CHUNK
        a = act[r0:r0 + CHUNK, :]
        q = _stack_heads(a, 0)
        k = _stack_heads(a, MIX_W)
        v = _stack_heads(a, 2 * MIX_W)
        q = q * lax.rsqrt(jnp.sum(q * q, axis=-1, keepdims=True) + L2_EPS) * (HEAD_DIM ** -0.5)
        k = k * lax.rsqrt(jnp.sum(k * k, axis=-1, keepdims=True) + L2_EPS)

        gcols = gcol_ref[r0:r0 + CHUNK, :]
        beta = _stack_cols(gcols, L_BETA)
        gc = _stack_cols(gcols, L_GC)
        gc_row = jnp.concatenate(
            [grow_ref[L_GC + h:L_GC + h + 1, r0:r0 + CHUNK] for h in range(N_HEADS)], axis=1)
        gl_row = jnp.concatenate(
            [grow_ref[L_GL + h:L_GL + h + 1, r0:r0 + CHUNK] for h in range(N_HEADS)], axis=1)

        decay = jnp.exp(jnp.where(m_incl, gc - gc_row, NEG_BIG))
        kb16 = k.astype(BF16)
        kk = _dot_nt(kb16, kb16)
        lmat = jnp.where(m_strict, kk * beta * decay, 0.0)
        attn = (_dot_nt(q.astype(BF16), kb16) * decay).astype(BF16)
        egc = jnp.exp(gc)
        rhs = jnp.concatenate([v * beta, k * (beta * egc)], axis=1).astype(BF16)
        kt_dec = (k.T * jnp.exp(gl_row - gc_row)).astype(BF16)
        pre.append(dict(lmat=lmat, attn=attn, rhs=rhs, kt_dec=kt_dec, q_dec=q * egc))

    l_d = [jnp.where(m_sub[0], p["lmat"], 0.0) for p in pre]
    inv = [eye - l for l in l_d]
    l16 = [l.astype(BF16) for l in l_d]
    pw = [_dot(l, l).astype(BF16) for l in l16]
    n_sq = SUB.bit_length() - 3
    for s in range(n_sq + 1):
        inv = [x + _dot(x.astype(BF16), p) for x, p in zip(inv, pw)]
        if s < n_sq:
            pw = [_dot(p, p).astype(BF16) for p in pw]
    for lvl in range(1, len(m_sub)):
        m_off = m_sub[lvl] & ~m_sub[lvl - 1]
        off = [jnp.where(m_off, p["lmat"], 0.0).astype(BF16) for p in pre]
        inv16 = [x.astype(BF16) for x in inv]
        half = [_dot(x, o).astype(BF16) for x, o in zip(inv16, off)]
        inv = [x - _dot(h, x16) for x, h, x16 in zip(inv, half, inv16)]
    uws = [_dot(x.astype(BF16), p["rhs"]) for x, p in zip(inv, pre)]

    for ch in chunks:
        r0 = ch * CHUNK
        attn, kt_dec = pre[ch]["attn"], pre[ch]["kt_dec"]
        u, w = uws[ch][:, :HEAD_DIM], uws[ch][:, HEAD_DIM:]

        state16 = s_ref[...].astype(BF16)
        wq_s = _dot(jnp.concatenate([w, pre[ch]["q_dec"]], axis=0).astype(BF16), state16)
        ws_d = jnp.concatenate([wq_s[h * CHUNK:(h + 1) * CHUNK, h * HEAD_DIM:(h + 1) * HEAD_DIM]
                                for h in range(N_HEADS)], axis=0)
        qs_d = jnp.concatenate([wq_s[STACK + h * CHUNK:STACK + (h + 1) * CHUNK, h * HEAD_DIM:(h + 1) * HEAD_DIM]
                                for h in range(N_HEADS)], axis=0)
        v_new = u - ws_d
        v16 = v_new.astype(BF16)
        o = qs_d + _dot(attn, v16)

        v_bd = jnp.where(head_match, jnp.concatenate([v_new] * N_HEADS, axis=1), 0.0).astype(BF16)
        upd = _dot(kt_dec, v_bd)
        for h in range(N_HEADS):
            e_h = jnp.exp(grow_ref[L_GL + h:L_GL + h + 1, r0:r0 + 1])
            sl = slice(h * HEAD_DIM, (h + 1) * HEAD_DIM)
            s_ref[:, sl] = s_ref[:, sl] * e_h + upd[:, sl]

        for h in range(N_HEADS):
            oh = o[h * CHUNK:(h + 1) * CHUNK, :]
            oh = oh * lax.rsqrt(jnp.mean(oh * oh, axis=-1, keepdims=True) + RMS_EPS) * gon_ref[...]
            zh = z_ref[r0:r0 + CHUNK, h * HEAD_DIM:(h + 1) * HEAD_DIM].astype(F32)
            o_ref[r0:r0 + CHUNK, h * HEAD_DIM:(h + 1) * HEAD_DIM] = (oh * _silu(zh)).astype(o_ref.dtype)


def _delta(qkv, z, gcol, grow, conv_w, g_onorm, tb):
    t = qkv.shape[0]
    pad = SUBLANES
    return pl.pallas_call(
        _delta_kernel,
        grid=(t // tb,),
        in_specs=[pl.BlockSpec((tb, QKV_W), lambda i: (i, 0)),
                  pl.BlockSpec((tb, MIX_W), lambda i: (i, 0)),
                  pl.BlockSpec((tb, LANES), lambda i: (i, 0)),
                  pl.BlockSpec((grow.shape[0], tb), lambda i: (0, i)),
                  pl.BlockSpec((CONV_K, QKV_W), lambda i: (0, 0)),
                  pl.BlockSpec((1, HEAD_DIM), lambda i: (0, 0))],
        out_specs=pl.BlockSpec((tb, MIX_W), lambda i: (i, 0)),
        out_shape=jax.ShapeDtypeStruct((t, MIX_W), BF16),
        scratch_shapes=[pltpu.VMEM((tb + pad, QKV_W), F32),
                        pltpu.VMEM((pad, QKV_W), F32),
                        pltpu.VMEM((HEAD_DIM, N_HEADS * HEAD_DIM), F32)],
        compiler_params=_params(("arbitrary",)),
        name="delta",
    )(qkv, z, gcol, grow, conv_w, g_onorm)


FOX_GROUP = 4
FOX_TQ, FOX_TK = 256, 2048
FOX_DIAG_PARTS = 2
FOX_ROW_STREAMS = 2
FOX_BOUND_SCALE, FOX_BOUND_SLACK = 1.01, 1.0
FOX_SKIP_LOG2 = -152.0


def _fox_kernel(*refs, tk):
    g = FOX_GROUP
    q_ref, o_ref, kmax_ref = refs[0], refs[-2], refs[-1]
    k_refs, v_refs, f_refs = (refs[1 + i * g:1 + (i + 1) * g] for i in range(3))
    qi = pl.program_id(1)
    tq = q_ref.shape[0]
    t_keys = k_refs[0].shape[0]

    @pl.when(qi == 0)
    def _():
        rows = min(t_keys, tk)
        for h in range(g):
            def norm_max(c, best, h=h):
                kc = k_refs[h][pl.ds(pl.multiple_of(c * rows, rows), rows), :].astype(F32)
                return jnp.maximum(best, jnp.sum(kc * kc, axis=-1, keepdims=True))

            best = lax.fori_loop(0, t_keys // rows, norm_max, jnp.zeros((rows, 1), F32))
            kmax_ref[h:h + 1, :] = jnp.broadcast_to(jnp.sqrt(jnp.max(best, axis=0, keepdims=True)), (1, LANES))

    def step(off, width, f_blk, carry, mask):
        ones = jnp.ones((width, HEAD_DIM), BF16)
        ss = []
        for h in range(g):
            q = q_ref[:, h * HEAD_DIM:(h + 1) * HEAD_DIM]
            s = _dot_nt(q, k_refs[h][pl.ds(off, width), :]) - f_blk(h) * LOG2E
            if mask is not None:
                s = jnp.where(mask, s, NEG_BIG)
            ss.append(s)
        ps, ms, alphas = [], [], []
        for h in range(g):
            m = carry[h][0]
            m_new = jnp.maximum(m, jnp.max(ss[h], axis=-1, keepdims=True))
            alphas.append(jnp.exp2(m - m_new))
            ps.append(jnp.exp2(ss[h] - m_new).astype(BF16))
            ms.append(m_new)
        out = []
        for h in range(g):
            v_aug = jnp.concatenate([v_refs[h][pl.ds(off, width), :], ones], axis=1)
            out.append((ms[h], alphas[h] * carry[h][1] + _dot(ps[h], v_aug)))
        return tuple(out)

    def head_chunk(h, j, mc):
        m, acc = mc
        off = pl.multiple_of(j * tk, tk)
        k_blk = k_refs[h][pl.ds(off, tk), :]
        v_aug = jnp.concatenate([v_refs[h][pl.ds(off, tk), :], jnp.ones((tk, HEAD_DIM), BF16)], axis=1)
        f_blk = f_refs[h][j] * LOG2E
        part = tq // FOX_ROW_STREAMS
        rows = [slice(i * part, (i + 1) * part) for i in range(FOX_ROW_STREAMS)]
        ss = [_dot_nt(q_ref[r, h * HEAD_DIM:(h + 1) * HEAD_DIM], k_blk) - f_blk for r in rows]
        m_new = [jnp.maximum(m[r], jnp.max(s, axis=-1, keepdims=True)) for r, s in zip(rows, ss)]
        ps = [jnp.exp2(s - mn).astype(BF16) for s, mn in zip(ss, m_new)]
        accs = [jnp.exp2(m[r] - mn) * acc[r] + _dot(p, v_aug) for r, mn, p in zip(rows, m_new, ps)]
        return jnp.concatenate(m_new, axis=0), jnp.concatenate(accs, axis=0)

    init = tuple((jnp.full((tq, 1), NEG_BIG, F32), jnp.zeros((tq, 2 * HEAD_DIM), F32)) for _ in range(g))
    n_full = (qi * tq) // tk

    parts = max(1, min(FOX_DIAG_PARTS, tk // tq))
    width = tk // parts
    row = lax.broadcasted_iota(I32, (tq, width), 0)
    col = lax.broadcasted_iota(I32, (tq, width), 1)
    carry = init
    for d in reversed(range(max(tq // tk, 1))):
        j = n_full + d
        for p in reversed(range(parts)):
            off = pl.multiple_of(j * tk + p * width, width)
            piece = functools.partial(step, off, width,
                                      lambda h, j=j, p=p: f_refs[h][j][:, p * width:(p + 1) * width],
                                      mask=col + (off - qi * tq) <= row)
            carry = piece(carry) if p == 0 else lax.cond(off < (qi + 1) * tq, piece, lambda c: c, carry)

    qn = []
    for h in range(g):
        q = q_ref[:, h * HEAD_DIM:(h + 1) * HEAD_DIM].astype(F32)
        qk_max = jnp.sqrt(jnp.sum(q * q, axis=-1, keepdims=True)) * kmax_ref[h:h + 1, 0:1]
        qn.append(qk_max * FOX_BOUND_SCALE + FOX_BOUND_SLACK)

    def reaches(j, carry):
        jc = jnp.maximum(j, 0)
        flags = []
        for h in range(g):
            bound = qn[h] - f_refs[h][jc][:, tk - 1:tk] * LOG2E
            flags.append((jnp.max(bound - carry[h][0]) >= FOX_SKIP_LOG2).astype(I32))
        return tuple(flags)

    def chunk_all_heads(state):
        j, _, carry = state
        carry = step(pl.multiple_of(j * tk, tk), tk, lambda h: f_refs[h][j], carry, None)
        return j - 1, reaches(j - 1, carry), carry

    def chunk_live_heads(state):
        j, live, carry = state
        carry = tuple(lax.cond(live[h] > 0, functools.partial(head_chunk, h, j), lambda mc: mc, carry[h])
                      for h in range(g))
        return j - 1, reaches(j - 1, carry), carry

    state = (n_full - 1, reaches(n_full - 1, carry), carry)
    state = lax.while_loop(lambda st: (st[0] >= 0) & (sum(st[1]) == g), chunk_all_heads, state)
    _, _, carry = lax.while_loop(lambda st: (st[0] >= 0) & (sum(st[1]) > 0), chunk_live_heads, state)
    for h in range(g):
        acc = carry[h][1]
        o_ref[:, h * HEAD_DIM:(h + 1) * HEAD_DIM] = (acc[:, :HEAD_DIM] / acc[:, HEAD_DIM:]).astype(o_ref.dtype)


def _fox(fx, f_rows, tq, tk):
    t = fx.shape[0]
    g = FOX_GROUP
    heads = lambda hg, u: hg * g + u
    q_specs = [pl.BlockSpec((tq, g * HEAD_DIM), lambda hg, i: (i, hg))]
    k_specs = [pl.BlockSpec((t, HEAD_DIM), lambda hg, i, u=u: (0, N_HEADS + heads(hg, u)),
                            pipeline_mode=pl.Buffered(1)) for u in range(g)]
    v_specs = [pl.BlockSpec((t, HEAD_DIM), lambda hg, i, u=u: (0, 2 * N_HEADS + heads(hg, u)),
                            pipeline_mode=pl.Buffered(1)) for u in range(g)]
    f_specs = [pl.BlockSpec((None, t // tk, 1, tk), lambda hg, i, u=u: (heads(hg, u), 0, 0, 0)) for u in range(g)]
    return pl.pallas_call(
        functools.partial(_fox_kernel, tk=tk),
        grid=(N_HEADS // g, t // tq),
        in_specs=q_specs + k_specs + v_specs + f_specs,
        out_specs=pl.BlockSpec((tq, g * HEAD_DIM), lambda hg, i: (i, hg)),
        out_shape=jax.ShapeDtypeStruct((t, MIX_W), BF16),
        scratch_shapes=[pltpu.VMEM((8, LANES), F32)],
        compiler_params=_params(("arbitrary", "arbitrary")),
        name="fox",
    )(*([fx] * (1 + 2 * g) + [f_rows] * g))


def _merge_kernel(oa_ref, ob_ref, ga_ref, gb_ref, x_ref, woa_ref, wob_ref, wout_ref, gt1_ref, g2_ref, sc2_ref,
                  sh2_ref, wr_ref, br_ref, x1_ref, h2_ref, ri_ref, gate_ref, cnt_ref, carry_ref):
    i = pl.program_id(0)
    tm = x_ref.shape[0]

    @pl.when(i == 0)
    def _():
        carry_ref[...] = jnp.zeros_like(carry_ref)

    ya = _dot(oa_ref[...], woa_ref[...])
    yb = _dot(ob_ref[...], wob_ref[...])
    merged = _sigmoid(ga_ref[...].astype(F32)) * ya + _sigmoid(gb_ref[...].astype(F32)) * yb
    x1 = x_ref[...] + gt1_ref[...] * _dot(merged.astype(BF16), wout_ref[...])
    x1_ref[...] = x1
    y = x1 * lax.rsqrt(jnp.mean(x1 * x1, axis=-1, keepdims=True) + RMS_EPS) * g2_ref[...]
    h2 = y * (1.0 + sc2_ref[...]) + sh2_ref[...]
    _store_token_tiles(h2_ref, h2)

    logits = _dot_split(h2, wr_ref[...]) + br_ref[...]
    lane = lax.broadcasted_iota(I32, logits.shape, 1)
    lanef = lane.astype(F32)
    cur = jnp.where(lane < N_EXPERTS, logits, -jnp.inf)
    vals, idxs = [], []
    for _ in range(TOP_K):
        m = jnp.max(cur, axis=-1, keepdims=True)
        ix = jnp.min(jnp.where(cur == m, lanef, float(LANES)), axis=-1, keepdims=True)
        vals.append(m)
        idxs.append(ix)
        cur = jnp.where(lanef == ix, -jnp.inf, cur)
    exps = [jnp.exp(v - vals[0]) for v in vals]
    denom = exps[0] + exps[1] + exps[2] + exps[3]

    onehot = jnp.zeros(logits.shape, F32)
    for ix in idxs:
        onehot = onehot + jnp.where(lanef == ix, 1.0, 0.0)
    r = lax.broadcasted_iota(I32, (tm, tm), 0)
    c = lax.broadcasted_iota(I32, (tm, tm), 1)
    strict = jnp.where(r > c, 1.0, 0.0).astype(BF16)
    before = _dot(strict, onehot.astype(BF16)) + carry_ref[...]
    carry_ref[...] = carry_ref[...] + jnp.sum(onehot, axis=0, keepdims=True)
    cnt_ref[...] = carry_ref[...].astype(I32)

    ri = jnp.zeros(logits.shape, F32)
    gates = jnp.zeros(logits.shape, F32)
    for kk in range(TOP_K):
        rank = jnp.sum(jnp.where(lanef == idxs[kk], before, 0.0), axis=-1, keepdims=True)
        ri = ri + jnp.where(lane == kk, idxs[kk], 0.0) + jnp.where(lane == TOP_K + kk, rank, 0.0)
        gates = gates + jnp.where(lane == kk, exps[kk] / denom, 0.0)
    ri_ref[...] = ri.astype(I32)
    gate_ref[...] = gates


def _merge(o_a, o_b, gate_a, gate_b, x, woa, wob, wout, gt1, g2, sc2, sh2, wr, br, tm):
    t, d = x.shape
    row = lambda i: (i, 0)
    fixed = lambda i: (0, 0)
    return pl.pallas_call(
        _merge_kernel,
        grid=(t // tm,),
        in_specs=[pl.BlockSpec((tm, MIX_W), row), pl.BlockSpec((tm, MIX_W), row),
                  pl.BlockSpec((tm, d), row), pl.BlockSpec((tm, d), row), pl.BlockSpec((tm, d), row),
                  pl.BlockSpec((MIX_W, d), fixed), pl.BlockSpec((MIX_W, d), fixed), pl.BlockSpec((d, d), fixed),
                  pl.BlockSpec((1, d), fixed), pl.BlockSpec((1, d), fixed), pl.BlockSpec((1, d), fixed),
                  pl.BlockSpec((1, d), fixed), pl.BlockSpec((d, LANES), fixed), pl.BlockSpec((1, LANES), fixed)],
        out_specs=[pl.BlockSpec((tm, d), row), pl.BlockSpec((tm * TOKEN_TILE_ROWS, LANES), row),
                   pl.BlockSpec((tm, LANES), row),
                   pl.BlockSpec((tm, LANES), row), pl.BlockSpec((1, LANES), fixed)],
        out_shape=[jax.ShapeDtypeStruct((t, d), F32), jax.ShapeDtypeStruct((t * TOKEN_TILE_ROWS, LANES), F32),
                   jax.ShapeDtypeStruct((t, LANES), I32), jax.ShapeDtypeStruct((t, LANES), F32),
                   jax.ShapeDtypeStruct((1, LANES), I32)],
        scratch_shapes=[pltpu.VMEM((1, LANES), F32)],
        compiler_params=_params(("arbitrary",)),
        name="merge",
    )(o_a, o_b, gate_a, gate_b, x, woa, wob, wout, gt1, g2, sc2, sh2, wr, br)


def _dest_kernel(ri_ref, cnt_ref, dest_ref, blk_ref, pend_ref):
    shift = EXPERT_BLOCK.bit_length() - 1
    cnt = jnp.broadcast_to(cnt_ref[...], (8, LANES))
    lane_row = lax.broadcasted_iota(I32, cnt.shape, 1)
    padded = jnp.where(lane_row < N_EXPERTS, ((cnt + (EXPERT_BLOCK - 1)) >> shift) << shift, 0)
    pend = padded
    s = 1
    while s < N_EXPERTS:
        pend = pend + jnp.where(lane_row >= s, pltpu.roll(pend, s, 1), 0)
        s *= 2
    pstart = (pend - padded)[0:1, :].astype(F32)
    pend = pend[0:1, :]
    pend_ref[...] = pend

    ri = ri_ref[...]
    lane = lax.broadcasted_iota(I32, ri.shape, 1)
    dest = jnp.zeros(ri.shape, F32)
    for kk in range(TOP_K):
        ix = ri[:, kk:kk + 1]
        rank = ri[:, TOP_K + kk:TOP_K + kk + 1].astype(F32)
        start = jnp.sum(jnp.where(lane == ix, pstart, 0.0), axis=-1, keepdims=True)
        dest = dest + jnp.where(lane == kk, start + rank, 0.0)
    dest_ref[...] = dest.astype(I32)

    bstart = lax.broadcasted_iota(I32, blk_ref.shape, 0) * EXPERT_BLOCK
    lane_b = lax.broadcasted_iota(I32, blk_ref.shape, 1)
    ended = jnp.where((lane_b < N_EXPERTS) & (pend <= bstart), 1.0, 0.0)
    e = jnp.minimum(jnp.sum(ended, axis=-1, keepdims=True), float(N_EXPERTS - 1))
    vend = pstart + cnt[0:1, :].astype(F32)
    vend_e = jnp.sum(jnp.where(lane_b.astype(F32) == e, vend, 0.0), axis=-1, keepdims=True)
    nvalid = jnp.clip(vend_e - bstart[:, 0:1].astype(F32), 0.0, float(EXPERT_BLOCK))
    blk_ref[...] = jnp.where(lane_b == 0, e, jnp.where(lane_b == 1, nvalid, 0.0)).astype(I32)


def _dest(ri, cnt, n_blocks, tm):
    t = ri.shape[0]
    nb_pad = -(-n_blocks // 8) * 8
    return pl.pallas_call(
        _dest_kernel,
        grid=(t // tm,),
        in_specs=[pl.BlockSpec((tm, LANES), lambda i: (i, 0)), pl.BlockSpec((1, LANES), lambda i: (0, 0))],
        out_specs=[pl.BlockSpec((tm, LANES), lambda i: (i, 0)), pl.BlockSpec((nb_pad, LANES), lambda i: (0, 0)),
                   pl.BlockSpec((1, LANES), lambda i: (0, 0))],
        out_shape=[jax.ShapeDtypeStruct((t, LANES), I32), jax.ShapeDtypeStruct((nb_pad, LANES), I32),
                   jax.ShapeDtypeStruct((1, LANES), I32)],
        compiler_params=_params(("arbitrary",)),
        name="dest",
    )(ri, cnt)


ROW_UNROLL = 8


def _drain_tokens(src_ref, dst_ref, sem, n_tokens):
    def drain(_, carry):
        for _ in range(ROW_UNROLL * TOP_K):
            _token_copy(src_ref, 0, dst_ref, 0, sem).wait()
        return carry

    lax.fori_loop(0, n_tokens // ROW_UNROLL, drain, 0)


def _dispatch_kernel(dest_ref, nvalid_ref, h_ref, xbuf_ref, zero_ref, sem, fill_sem):
    tm = h_ref.shape[0] // TOKEN_TILE_ROWS

    @pl.when(pl.program_id(0) == 0)
    def _():
        zero_ref[...] = jnp.zeros_like(zero_ref)

        def padding_copies(b, act):
            pad = EXPERT_BLOCK - nvalid_ref[b]

            @pl.when(pad > 0)
            def _():
                first = b * EXPERT_BLOCK + nvalid_ref[b]
                size = EXPERT_BLOCK
                while size >= 1:
                    take = (pad & size) != 0
                    rows = size * TOKEN_TILE_ROWS
                    dst0 = pl.multiple_of(first * TOKEN_TILE_ROWS, TOKEN_TILE_ROWS)

                    @pl.when(take)
                    def _(rows=rows, dst0=dst0):
                        act(pltpu.make_async_copy(zero_ref.at[pl.ds(0, rows), :],
                                                  xbuf_ref.at[pl.ds(dst0, rows), :], fill_sem))

                    first = first + jnp.where(take, size, 0)
                    size //= 2

        n_blocks = nvalid_ref.shape[0]
        lax.fori_loop(0, n_blocks, lambda b, c: (padding_copies(b, lambda cp: cp.start()), c)[1], 0)
        lax.fori_loop(0, n_blocks, lambda b, c: (padding_copies(b, lambda cp: cp.wait()), c)[1], 0)

    def issue(g, carry):
        for u in range(ROW_UNROLL):
            r = g * ROW_UNROLL + u
            for kk in range(TOP_K):
                _token_copy(h_ref, r, xbuf_ref, dest_ref[r * TOP_K + kk], sem).start(priority=kk % 2)
        return carry

    lax.fori_loop(0, tm // ROW_UNROLL, issue, 0)
    _drain_tokens(h_ref, xbuf_ref, sem, tm)


def _dispatch(dest_flat, nvalid, h2_tiles):
    tm = ROUTE_TILE
    t = h2_tiles.shape[0] // TOKEN_TILE_ROWS
    n_rows = nvalid.shape[0] * EXPERT_BLOCK
    return pl.pallas_call(
        _dispatch_kernel,
        grid=(t // tm,),
        in_specs=[pl.BlockSpec((tm * TOP_K,), lambda i: (i,), memory_space=pltpu.SMEM),
                  pl.BlockSpec(memory_space=pltpu.SMEM),
                  pl.BlockSpec((tm * TOKEN_TILE_ROWS, LANES), lambda i: (i, 0))],
        out_specs=pl.BlockSpec(memory_space=pl.ANY),
        out_shape=jax.ShapeDtypeStruct((n_rows * TOKEN_TILE_ROWS, LANES), h2_tiles.dtype),
        scratch_shapes=[pltpu.VMEM((EXPERT_BLOCK * TOKEN_TILE_ROWS, LANES), h2_tiles.dtype),
                        pltpu.SemaphoreType.DMA(()), pltpu.SemaphoreType.DMA(())],
        compiler_params=_params(("arbitrary",)),
        name="dispatch",
    )(dest_flat, nvalid, h2_tiles)


def _experts_kernel(blk_e_ref, nvalid_ref, grp_ref, nxt_e_ref, x_ref, wgu_hbm, bgu_ref, wd_hbm, bd_ref, y_ref,
                    wgu32_ref, wd32_ref, wgu16_ref, wd16_ref, sems):
    b = pl.program_id(0)
    nvalid = nvalid_ref[b]
    e = blk_e_ref[b]
    slot = grp_ref[b] % 2
    first = (b == 0) | (e != blk_e_ref[jnp.maximum(b - 1, 0)])

    def weight_copies(expert, s):
        return (pltpu.make_async_copy(wgu_hbm.at[expert], wgu32_ref.at[s], sems.at[s, 0]),
                pltpu.make_async_copy(wd_hbm.at[expert], wd32_ref.at[s], sems.at[s, 1]))

    @pl.when(b == 0)
    def _():
        for cp in weight_copies(e, 0):
            cp.start()

    @pl.when(first)
    def _():
        for cp in weight_copies(e, slot):
            cp.wait()
        wgu16_ref[...] = wgu32_ref[slot].astype(BF16)
        wd16_ref[...] = wd32_ref[slot].astype(BF16)

    @pl.when(first & (nxt_e_ref[b] != e))
    def _():
        for cp in weight_copies(nxt_e_ref[b], 1 - slot):
            cp.start()

    n_blk = x_ref.shape[0] // TOKEN_TILE_ROWS
    n_half = n_blk // 2

    def mlp(n):
        x = jnp.concatenate([_load_token_slab(x_ref, s, n) for s in range(TOKEN_TILE_ROWS)], axis=1)
        gu = _dot(x.astype(BF16), wgu16_ref[...]) + bgu_ref[...]
        gate = jnp.minimum(gu[:, :FF], SWIGLU_LIMIT)
        up = jnp.clip(gu[:, FF:], -SWIGLU_LIMIT, SWIGLU_LIMIT)
        act = (up + 1.0) * (gate * _sigmoid(SWIGLU_ALPHA * gate))
        _store_token_tiles(y_ref, _dot(act.astype(BF16), wd16_ref[...]) + bd_ref[...])
        if n < n_blk:
            y_ref[n * TOKEN_TILE_ROWS:, :] = jnp.zeros(((n_blk - n) * TOKEN_TILE_ROWS, LANES), y_ref.dtype)

    @pl.when(nvalid > n_half)
    def _():
        mlp(n_blk)

    @pl.when((nvalid > 0) & (nvalid <= n_half))
    def _():
        mlp(n_half)

    @pl.when(nvalid <= 0)
    def _():
        y_ref[...] = jnp.zeros_like(y_ref)


def _experts(blk_e, nvalid, xbuf, wgu, bgu, wd, bd):
    d = D_MODEL
    blk_rows = EXPERT_BLOCK * TOKEN_TILE_ROWS
    nb = xbuf.shape[0] // blk_rows
    change = jnp.concatenate([jnp.zeros((1,), I32), (blk_e[1:] != blk_e[:-1]).astype(I32)])
    grp = jnp.cumsum(change)
    pos = jnp.where(change > 0, jnp.arange(nb, dtype=I32), nb)
    nxt_pos = lax.cummin(jnp.concatenate([pos[1:], jnp.full((1,), nb, I32)]), reverse=True)
    nxt_e = jnp.where(nxt_pos < nb, blk_e[jnp.minimum(nxt_pos, nb - 1)], blk_e)
    grid_spec = pltpu.PrefetchScalarGridSpec(
        num_scalar_prefetch=4,
        grid=(nb,),
        in_specs=[pl.BlockSpec((blk_rows, LANES), lambda b, e, *_: (b, 0)),
                  pl.BlockSpec(memory_space=pl.ANY),
                  pl.BlockSpec((None, 1, 2 * FF), lambda b, e, *_: (e[b], 0, 0)),
                  pl.BlockSpec(memory_space=pl.ANY),
                  pl.BlockSpec((None, 1, d), lambda b, e, *_: (e[b], 0, 0))],
        out_specs=pl.BlockSpec((blk_rows, LANES), lambda b, e, *_: (b, 0)),
        scratch_shapes=[pltpu.VMEM((2, d, 2 * FF), F32), pltpu.VMEM((2, FF, d), F32),
                        pltpu.VMEM((d, 2 * FF), BF16), pltpu.VMEM((FF, d), BF16),
                        pltpu.SemaphoreType.DMA((2, 2))],
    )
    return pl.pallas_call(
        _experts_kernel,
        grid_spec=grid_spec,
        out_shape=jax.ShapeDtypeStruct(xbuf.shape, F32),
        compiler_params=_params(("arbitrary",)),
        name="experts",
    )(blk_e, nvalid, grp, nxt_e, xbuf, wgu, bgu, wd, bd)


def _combine_kernel(dest_ref, dest_next_ref, ybuf_ref, x1_ref, gate_ref, gt2_ref, gf_ref, o_ref, rows_ref, sems):
    i = pl.program_id(0)
    tm = x1_ref.shape[0]
    slot = i % 2

    def start_gather(idx_ref, s):
        def issue(g, carry):
            for u in range(ROW_UNROLL):
                r = g * ROW_UNROLL + u
                for kk in range(TOP_K):
                    _token_copy(ybuf_ref, idx_ref[r * TOP_K + kk], rows_ref.at[s, kk], r,
                                sems.at[s]).start(priority=kk % 2)
            return carry

        lax.fori_loop(0, tm // ROW_UNROLL, issue, 0)

    @pl.when(i == 0)
    def _():
        start_gather(dest_ref, 0)

    @pl.when(i + 1 < pl.num_programs(0))
    def _():
        start_gather(dest_next_ref, 1 - slot)

    _drain_tokens(ybuf_ref, rows_ref.at[slot, 0], sems.at[slot], tm)

    gates = gate_ref[...]
    slabs = []
    for s in range(TOKEN_TILE_ROWS):
        moe = gates[:, 0:1] * _load_token_slab(rows_ref.at[slot, 0], s, tm)
        for kk in range(1, TOP_K):
            moe = moe + gates[:, kk:kk + 1] * _load_token_slab(rows_ref.at[slot, kk], s, tm)
        slabs.append(moe)
    xo = x1_ref[...] + gt2_ref[...] * jnp.concatenate(slabs, axis=1)
    o_ref[...] = xo * lax.rsqrt(jnp.mean(xo * xo, axis=-1, keepdims=True) + RMS_EPS) * gf_ref[...]


def _combine(dest_flat, ybuf, x1, gates, gt2, g_final):
    t, d = x1.shape
    tm = ROUTE_TILE
    n = t // tm
    return pl.pallas_call(
        _combine_kernel,
        grid=(n,),
        in_specs=[pl.BlockSpec((tm * TOP_K,), lambda i: (i,), memory_space=pltpu.SMEM),
                  pl.BlockSpec((tm * TOP_K,), lambda i: (jnp.minimum(i + 1, n - 1),), memory_space=pltpu.SMEM),
                  pl.BlockSpec(memory_space=pl.ANY),
                  pl.BlockSpec((tm, d), lambda i: (i, 0)),
                  pl.BlockSpec((tm, LANES), lambda i: (i, 0)),
                  pl.BlockSpec((1, d), lambda i: (0, 0)),
                  pl.BlockSpec((1, d), lambda i: (0, 0))],
        out_specs=pl.BlockSpec((tm, d), lambda i: (i, 0)),
        out_shape=jax.ShapeDtypeStruct((t, d), F32),
        scratch_shapes=[pltpu.VMEM((2, TOP_K, tm * TOKEN_TILE_ROWS, LANES), F32), pltpu.SemaphoreType.DMA((2,))],
        compiler_params=_params(("arbitrary",)),
        name="combine",
    )(dest_flat, dest_flat, ybuf, x1, gates, gt2, g_final)


def _pad_lanes(v, lane0):
    return jnp.zeros((1, LANES), F32).at[0, lane0:lane0 + v.shape[0]].set(v.astype(F32))


def _layer(x, mod, g_norm1, w_in, conv_w, a_log, dt_bias, g_onorm, b_fgate, w_o_delta, w_o_fox, w_out,
           g_norm2, w_router, b_router, w_gate_up, b_gate_up, w_down, b_down, g_final):
    t, d = x.shape
    sh1, sc1, gt1, sh2, sc2, gt2 = [mod[:, i * d:(i + 1) * d] for i in range(N_MOD)]

    w_t = w_in.T
    o = 0
    wq = w_t[o:o + QKV_W]; o += QKV_W
    wz = w_t[o:o + MIX_W]; o += MIX_W
    w_beta = w_t[o:o + N_HEADS]; o += N_HEADS
    w_dec = w_t[o:o + N_HEADS]; o += N_HEADS
    wf = w_t[o:o + QKV_W]; o += QKV_W
    wf = jnp.concatenate([wf[:MIX_W] * (LOG2E * HEAD_DIM ** -0.5), wf[MIX_W:]], axis=0)
    w_fg = w_t[o:o + N_HEADS]; o += N_HEADS
    wga = w_t[o:o + d]; o += d
    wgb = w_t[o:o + d]
    ws = jnp.zeros((LANES, d), F32)
    ws = ws.at[L_BETA:L_BETA + N_HEADS].set(w_beta).at[L_G:L_G + N_HEADS].set(w_dec)
    ws = ws.at[L_F:L_F + N_HEADS].set(w_fg)
    bf = lambda w: w.astype(BF16)

    tm = min(ROW_TILE, t)
    qkv, z, fx, gate_a, gate_b, small = _inproj(
        x, g_norm1.reshape(1, d), sc1, sh1, bf(wq), bf(wz), bf(wf), bf(wga), bf(wgb), bf(ws), tm)

    gcol, grow = _gates(small, _pad_lanes(a_log, L_G), _pad_lanes(dt_bias, L_G), _pad_lanes(b_fgate, L_F),
                        min(GATES_TILE, t))

    o_a = _delta(qkv, z, gcol, grow, conv_w, g_onorm.reshape(1, HEAD_DIM), min(DELTA_TILE, t))
    tq, tk = min(FOX_TQ, t), min(FOX_TK, t)
    f_rows = grow[L_F:L_F + N_HEADS].reshape(N_HEADS, t // tk, 1, tk)
    o_b = _fox(fx, f_rows, tq, tk)

    wr = jnp.zeros((d, LANES), F32).at[:, :N_EXPERTS].set(w_router)
    br = _pad_lanes(b_router, 0)
    x1, h2, ri, gates, cnt = _merge(o_a, o_b, gate_a, gate_b, x, bf(w_o_delta), bf(w_o_fox), bf(w_out), gt1,
                                    g_norm2.reshape(1, d), sc2, sh2, wr, br, tm)

    n_blocks = (t * TOP_K) // EXPERT_BLOCK + N_EXPERTS
    dest, blk, _ = _dest(ri, cnt, n_blocks, min(2048, t))
    dest_flat = dest[:, :TOP_K].reshape(t * TOP_K)
    blk_e, nvalid = blk[:n_blocks, 0], blk[:n_blocks, 1]

    xbuf = _dispatch(dest_flat, nvalid, h2)
    ybuf = _experts(blk_e, nvalid, xbuf, w_gate_up, b_gate_up.reshape(N_EXPERTS, 1, 2 * FF),
                    w_down, b_down.reshape(N_EXPERTS, 1, d))
    return _combine(dest_flat, ybuf, x1, gates, gt2, g_final.reshape(1, d))


def kernel(x, c, w_ada, b_ada, g_norm1, w_in, conv_w, a_log, dt_bias, g_onorm, b_fgate, w_o_delta, w_o_fox, w_out,
           g_norm2, w_router, b_router, w_gate_up, b_gate_up, w_down, b_down, g_final):
    b, s, d = x.shape
    assert b == 1 and d == D_MODEL and w_ada.shape[0] == 1
    h = x[0]
    for l in range(w_ada.shape[0]):
        mod = _modulation(c[0], w_ada[l], b_ada[l])
        h = _layer(h, mod, g_norm1[l], w_in[l], conv_w[l], a_log[l], dt_bias[l], g_onorm[l], b_fgate[l],
                   w_o_delta[l], w_o_fox[l], w_out[l], g_norm2[l], w_router[l], b_router[l], w_gate_up[l],
                   b_gate_up[l], w_down[l], b_down[l], g_final)
    return h[None]
```

```python
import functools

import jax
import jax.numpy as jnp
from jax import lax
from jax.experimental import pallas as pl
from jax.experimental.pallas import tpu as pltpu

F32 = jnp.float32
BF16 = jnp.bfloat16
I32 = jnp.int32

D_MODEL = 1024
HEAD_DIM = 128
N_HEADS = 4
MIX_W = N_HEADS * HEAD_DIM
QKV_W = 3 * MIX_W
CONV_K = 4
CHUNK = 64
STACK = N_HEADS * CHUNK
SUB = 16
N_EXPERTS = 32
TOP_K = 4
FF = D_MODEL
SWIGLU_LIMIT = 7.0
SWIGLU_ALPHA = 1.702
RMS_EPS = 1e-6
L2_EPS = 1e-6
N_MOD = 6
LANES = 128
NEG_BIG = -1e30
LOG2E = 1.4426950408889634

SUBLANES = 8
MOD_TILE = 1024
ROW_TILE = 512
GATES_TILE = 256
GATE_ROWS = 24
DELTA_TILE = 256
EXPERT_BLOCK = 512
ROUTE_TILE = 256
VMEM_LIMIT = 56 * 1024 * 1024


def _params(sem):
    return pltpu.CompilerParams(dimension_semantics=sem, vmem_limit_bytes=VMEM_LIMIT)


def _softplus(x):
    return jnp.maximum(x, 0.0) + jnp.log(1.0 + jnp.exp(-jnp.abs(x)))


def _sigmoid(x):
    return 1.0 / (1.0 + jnp.exp(-x))


def _silu(x):
    return x * _sigmoid(x)


def _dot(a, b):
    return jnp.dot(a, b, preferred_element_type=F32)


def _dot_nt(a, b):
    return lax.dot_general(a, b, (((1,), (1,)), ((), ())), preferred_element_type=F32)


def _split_bf16(x, terms):
    parts = []
    for _ in range(terms):
        p = x.astype(BF16)
        parts.append(p)
        x = x - p.astype(F32)
    return parts


def _dot_split(a, b):
    a_hi, a_lo = _split_bf16(a, 2)
    b_hi, b_lo = _split_bf16(b, 2)
    return _dot(a_hi, b_hi) + (_dot(a_hi, b_lo) + _dot(a_lo, b_hi))


def _dot_mask(mask01, x):
    return sum(_dot(mask01, p) for p in _split_bf16(x, 3))


TOKEN_TILE_ROWS = D_MODEL // LANES


def _store_token_tiles(ref, x):
    n = x.shape[0]
    for s in range(TOKEN_TILE_ROWS):
        ref[pl.ds(s, n, stride=TOKEN_TILE_ROWS), :] = x[:, s * LANES:(s + 1) * LANES]


def _load_token_slab(ref, s, n):
    return ref[pl.ds(s, n, stride=TOKEN_TILE_ROWS), :]


def _token_copy(src_ref, src_tok, dst_ref, dst_tok, sem):
    rows = lambda i: pl.ds(pl.multiple_of(i * TOKEN_TILE_ROWS, TOKEN_TILE_ROWS), TOKEN_TILE_ROWS)
    return pltpu.make_async_copy(src_ref.at[rows(src_tok), :], dst_ref.at[rows(dst_tok), :], sem)


def _mod_kernel(c_ref, w_ref, b_ref, o_ref):
    o_ref[...] = jnp.sum(c_ref[...] * w_ref[...], axis=0, keepdims=True) + b_ref[...]


def _modulation(c, w_ada, b_ada):
    d, n = w_ada.shape
    tn = MOD_TILE
    return pl.pallas_call(
        _mod_kernel,
        grid=(n // tn,),
        in_specs=[pl.BlockSpec((d, 1), lambda j: (0, 0)),
                  pl.BlockSpec((d, tn), lambda j: (0, j)),
                  pl.BlockSpec((1, tn), lambda j: (0, j))],
        out_specs=pl.BlockSpec((1, tn), lambda j: (0, j)),
        out_shape=jax.ShapeDtypeStruct((1, n), F32),
        compiler_params=_params(("arbitrary",)),
        name="mod",
    )(c.reshape(d, 1), w_ada, b_ada.reshape(1, n))


def _inproj_kernel(x_ref, g_ref, sc_ref, sh_ref, wq_ref, wz_ref, wf_ref, wga_ref, wgb_ref, ws_ref,
                   oq_ref, oz_ref, of_ref, oga_ref, ogb_ref, os_ref):
    x = x_ref[...]
    y = x * lax.rsqrt(jnp.mean(x * x, axis=-1, keepdims=True) + RMS_EPS) * g_ref[...]
    h = (y * (1.0 + sc_ref[...]) + sh_ref[...]).astype(BF16)
    oq_ref[...] = _dot_nt(h, wq_ref[...])
    oz_ref[...] = _dot_nt(h, wz_ref[...]).astype(BF16)
    of_ref[...] = _dot_nt(h, wf_ref[...]).astype(BF16)
    oga_ref[...] = _dot_nt(h, wga_ref[...]).astype(BF16)
    ogb_ref[...] = _dot_nt(h, wgb_ref[...]).astype(BF16)
    os_ref[...] = _dot_nt(h, ws_ref[...])


def _inproj(x, g1, sc1, sh1, wq, wz, wf, wga, wgb, ws, tm):
    t, d = x.shape
    row = lambda i: (i, 0)
    fixed = lambda i: (0, 0)
    ws_list = [wq, wz, wf, wga, wgb, ws]
    out_dt = [F32, BF16, BF16, BF16, BF16, F32]
    return pl.pallas_call(
        _inproj_kernel,
        grid=(t // tm,),
        in_specs=[pl.BlockSpec((tm, d), row)] + [pl.BlockSpec((1, d), fixed)] * 3
                 + [pl.BlockSpec(w.shape, fixed) for w in ws_list],
        out_specs=[pl.BlockSpec((tm, w.shape[0]), row) for w in ws_list],
        out_shape=[jax.ShapeDtypeStruct((t, w.shape[0]), dt) for w, dt in zip(ws_list, out_dt)],
        compiler_params=_params(("arbitrary",)),
        name="inproj",
    )(x, g1, sc1, sh1, *ws_list)


L_BETA, L_G, L_F, L_GC, L_GL = 0, 4, 8, 12, 16


def _gates_kernel(s_ref, alog_ref, dtb_ref, bf_ref, col_ref, row_ref, carry_ref):
    i = pl.program_id(0)
    tm = s_ref.shape[0]

    @pl.when(i == 0)
    def _():
        carry_ref[...] = jnp.zeros_like(carry_ref)

    s = s_ref[...]
    lane = lax.broadcasted_iota(I32, s.shape, 1)
    beta = _sigmoid(s)
    g = -jnp.exp(alog_ref[...]) * _softplus(s + dtb_ref[...])
    logf = -_softplus(-(s + bf_ref[...]))
    is_g = (lane >= L_G) & (lane < L_G + N_HEADS)
    is_f = (lane >= L_F) & (lane < L_F + N_HEADS)
    g = jnp.where(is_g, g, 0.0)
    logf = jnp.where(is_f, logf, 0.0)

    r = lax.broadcasted_iota(I32, (tm, tm), 0)
    c = lax.broadcasted_iota(I32, (tm, tm), 1)
    same_chunk = (r // CHUNK) == (c // CHUNK)
    tri = jnp.where(r >= c, 1.0, 0.0).astype(BF16)
    tri_chunk = jnp.where(same_chunk & (r >= c), 1.0, 0.0).astype(BF16)
    ones_chunk = jnp.where(same_chunk, 1.0, 0.0).astype(BF16)
    f_cum = _dot_mask(tri, logf) + carry_ref[...]
    carry_ref[...] = f_cum[tm - 1:tm, :]
    gc = _dot_mask(tri_chunk, g)
    gl = _dot_mask(ones_chunk, g)

    out = jnp.where(lane < N_HEADS, beta, 0.0) + g + f_cum
    out = out + pltpu.roll(gc, L_GC - L_G, 1) + pltpu.roll(gl, L_GL - L_G, 1)
    col_ref[...] = out
    row_ref[...] = out.T[:row_ref.shape[0], :]


def _gates(small, alog_row, dtb_row, bf_row, tm):
    t = small.shape[0]
    n_rows = GATE_ROWS
    return pl.pallas_call(
        _gates_kernel,
        grid=(t // tm,),
        in_specs=[pl.BlockSpec((tm, LANES), lambda i: (i, 0))] + [pl.BlockSpec((1, LANES), lambda i: (0, 0))] * 3,
        out_specs=[pl.BlockSpec((tm, LANES), lambda i: (i, 0)), pl.BlockSpec((n_rows, tm), lambda i: (0, i))],
        out_shape=[jax.ShapeDtypeStruct((t, LANES), F32), jax.ShapeDtypeStruct((n_rows, t), F32)],
        scratch_shapes=[pltpu.VMEM((1, LANES), F32)],
        compiler_params=_params(("arbitrary",)),
        name="gates",
    )(small, alog_row, dtb_row, bf_row)


def _stack_heads(a, col0):
    return jnp.concatenate([a[:, col0 + h * HEAD_DIM: col0 + (h + 1) * HEAD_DIM] for h in range(N_HEADS)], axis=0)


def _stack_cols(a, lane0):
    return jnp.concatenate([a[:, lane0 + h: lane0 + h + 1] for h in range(N_HEADS)], axis=0)


def _delta_kernel(qkv_ref, z_ref, gcol_ref, grow_ref, cw_ref, gon_ref, o_ref, ext_ref, tail_ref, s_ref):
    i = pl.program_id(0)
    tb = qkv_ref.shape[0]
    pad = tail_ref.shape[0]

    @pl.when(i == 0)
    def _():
        tail_ref[...] = jnp.zeros_like(tail_ref)
        s_ref[...] = jnp.zeros_like(s_ref)

    ext_ref[0:pad, :] = tail_ref[...]
    ext_ref[pad:pad + tb, :] = qkv_ref[...]
    tail_ref[...] = qkv_ref[tb - pad:tb, :]
    conv = cw_ref[0:1, :] * ext_ref[pad - 3:pad - 3 + tb, :]
    for j in range(1, CONV_K):
        conv = conv + cw_ref[j:j + 1, :] * ext_ref[pad - 3 + j:pad - 3 + j + tb, :]
    act = _silu(conv)

    r = lax.broadcasted_iota(I32, (STACK, STACK), 0)
    c = lax.broadcasted_iota(I32, (STACK, STACK), 1)
    same = (r // CHUNK) == (c // CHUNK)
    m_incl = same & (r >= c)
    m_strict = same & (r > c)
    eye = jnp.where(r == c, 1.0, 0.0)
    m_sub = []
    size = SUB
    while size <= CHUNK:
        m_sub.append((r // size) == (c // size))
        size *= 2
    rb = lax.broadcasted_iota(I32, (STACK, N_HEADS * HEAD_DIM), 0) // CHUNK
    cb = lax.broadcasted_iota(I32, (STACK, N_HEADS * HEAD_DIM), 1) // HEAD_DIM
    head_match = rb == cb

    chunks = range(tb // CHUNK)
    pre = []
    for ch in chunks:
        r0 = ch * CHUNK
        a = act[r0:r0 + CHUNK, :]
        q = _stack_heads(a, 0)
        k = _stack_heads(a, MIX_W)
        v = _stack_heads(a, 2 * MIX_W)
        q = q * lax.rsqrt(jnp.sum(q * q, axis=-1, keepdims=True) + L2_EPS) * (HEAD_DIM ** -0.5)
        k = k * lax.rsqrt(jnp.sum(k * k, axis=-1, keepdims=True) + L2_EPS)

        gcols = gcol_ref[r0:r0 + CHUNK, :]
        beta = _stack_cols(gcols, L_BETA)
        gc = _stack_cols(gcols, L_GC)
        gc_row = jnp.concatenate(
            [grow_ref[L_GC + h:L_GC + h + 1, r0:r0 + CHUNK] for h in range(N_HEADS)], axis=1)
        gl_row = jnp.concatenate(
            [grow_ref[L_GL + h:L_GL + h + 1, r0:r0 + CHUNK] for h in range(N_HEADS)], axis=1)

        decay = jnp.exp(jnp.where(m_incl, gc - gc_row, NEG_BIG))
        kb16 = k.astype(BF16)
        kk = _dot_nt(kb16, kb16)
        lmat = jnp.where(m_strict, kk * beta * decay, 0.0)
        attn = (_dot_nt(q.astype(BF16), kb16) * decay).astype(BF16)
        egc = jnp.exp(gc)
        rhs = jnp.concatenate([v * beta, k * (beta * egc)], axis=1).astype(BF16)
        kt_dec = (k.T * jnp.exp(gl_row - gc_row)).astype(BF16)
        pre.append(dict(lmat=lmat, attn=attn, rhs=rhs, kt_dec=kt_dec, q_dec=q * egc))

    l_d = [jnp.where(m_sub[0], p["lmat"], 0.0) for p in pre]
    inv = [eye - l for l in l_d]
    l16 = [l.astype(BF16) for l in l_d]
    pw = [_dot(l, l).astype(BF16) for l in l16]
    n_sq = SUB.bit_length() - 3
    for s in range(n_sq + 1):
        inv = [x + _dot(x.astype(BF16), p) for x, p in zip(inv, pw)]
        if s < n_sq:
            pw = [_dot(p, p).astype(BF16) for p in pw]
    for lvl in range(1, len(m_sub)):
        m_off = m_sub[lvl] & ~m_sub[lvl - 1]
        off = [jnp.where(m_off, p["lmat"], 0.0).astype(BF16) for p in pre]
        inv16 = [x.astype(BF16) for x in inv]
        half = [_dot(x, o).astype(BF16) for x, o in zip(inv16, off)]
        inv = [x - _dot(h, x16) for x, h, x16 in zip(inv, half, inv16)]
    uws = [_dot(x.astype(BF16), p["rhs"]) for x, p in zip(inv, pre)]

    for ch in chunks:
        r0 = ch * CHUNK
        attn, kt_dec = pre[ch]["attn"], pre[ch]["kt_dec"]
        u, w = uws[ch][:, :HEAD_DIM], uws[ch][:, HEAD_DIM:]

        state16 = s_ref[...].astype(BF16)
        wq_s = _dot(jnp.concatenate([w, pre[ch]["q_dec"]], axis=0).astype(BF16), state16)
        ws_d = jnp.concatenate([wq_s[h * CHUNK:(h + 1) * CHUNK, h * HEAD_DIM:(h + 1) * HEAD_DIM]
                                for h in range(N_HEADS)], axis=0)
        qs_d = jnp.concatenate([wq_s[STACK + h * CHUNK:STACK + (h + 1) * CHUNK, h * HEAD_DIM:(h + 1) * HEAD_DIM]
                                for h in range(N_HEADS)], axis=0)
        v_new = u - ws_d
        v16 = v_new.astype(BF16)
        o = qs_d + _dot(attn, v16)

        v_bd = jnp.where(head_match, jnp.concatenate([v_new] * N_HEADS, axis=1), 0.0).astype(BF16)
        upd = _dot(kt_dec, v_bd)
        for h in range(N_HEADS):
            e_h = jnp.exp(grow_ref[L_GL + h:L_GL + h + 1, r0:r0 + 1])
            sl = slice(h * HEAD_DIM, (h + 1) * HEAD_DIM)
            s_ref[:, sl] = s_ref[:, sl] * e_h + upd[:, sl]

        for h in range(N_HEADS):
            oh = o[h * CHUNK:(h + 1) * CHUNK, :]
            oh = oh * lax.rsqrt(jnp.mean(oh * oh, axis=-1, keepdims=True) + RMS_EPS) * gon_ref[...]
            zh = z_ref[r0:r0 + CHUNK, h * HEAD_DIM:(h + 1) * HEAD_DIM].astype(F32)
            o_ref[r0:r0 + CHUNK, h * HEAD_DIM:(h + 1) * HEAD_DIM] = (oh * _silu(zh)).astype(o_ref.dtype)


def _delta(qkv, z, gcol, grow, conv_w, g_onorm, tb):
    t = qkv.shape[0]
    pad = SUBLANES
    return pl.pallas_call(
        _delta_kernel,
        grid=(t // tb,),
        in_specs=[pl.BlockSpec((tb, QKV_W), lambda i: (i, 0)),
                  pl.BlockSpec((tb, MIX_W), lambda i: (i, 0)),
                  pl.BlockSpec((tb, LANES), lambda i: (i, 0)),
                  pl.BlockSpec((grow.shape[0], tb), lambda i: (0, i)),
                  pl.BlockSpec((CONV_K, QKV_W), lambda i: (0, 0)),
                  pl.BlockSpec((1, HEAD_DIM), lambda i: (0, 0))],
        out_specs=pl.BlockSpec((tb, MIX_W), lambda i: (i, 0)),
        out_shape=jax.ShapeDtypeStruct((t, MIX_W), BF16),
        scratch_shapes=[pltpu.VMEM((tb + pad, QKV_W), F32),
                        pltpu.VMEM((pad, QKV_W), F32),
                        pltpu.VMEM((HEAD_DIM, N_HEADS * HEAD_DIM), F32)],
        compiler_params=_params(("arbitrary",)),
        name="delta",
    )(qkv, z, gcol, grow, conv_w, g_onorm)


FOX_GROUP = 4
FOX_TQ, FOX_TK = 256, 2048
FOX_DIAG_PARTS = 2
FOX_ROW_STREAMS = 2
FOX_BOUND_SCALE, FOX_BOUND_SLACK = 1.01, 1.0
FOX_SKIP_LOG2 = -152.0


def _fox_kernel(*refs, tk):
    g = FOX_GROUP
    q_ref, o_ref, kmax_ref = refs[0], refs[-2], refs[-1]
    k_refs, v_refs, f_refs = (refs[1 + i * g:1 + (i + 1) * g] for i in range(3))
    qi = pl.program_id(1)
    tq = q_ref.shape[0]
    t_keys = k_refs[0].shape[0]

    @pl.when(qi == 0)
    def _():
        rows = min(t_keys, tk)
        for h in range(g):
            def norm_max(c, best, h=h):
                kc = k_refs[h][pl.ds(pl.multiple_of(c * rows, rows), rows), :].astype(F32)
                return jnp.maximum(best, jnp.sum(kc * kc, axis=-1, keepdims=True))

            best = lax.fori_loop(0, t_keys // rows, norm_max, jnp.zeros((rows, 1), F32))
            kmax_ref[h:h + 1, :] = jnp.broadcast_to(jnp.sqrt(jnp.max(best, axis=0, keepdims=True)), (1, LANES))

    def step(off, width, f_blk, carry, mask):
        ones = jnp.ones((width, HEAD_DIM), BF16)
        ss = []
        for h in range(g):
            q = q_ref[:, h * HEAD_DIM:(h + 1) * HEAD_DIM]
            s = _dot_nt(q, k_refs[h][pl.ds(off, width), :]) - f_blk(h) * LOG2E
            if mask is not None:
                s = jnp.where(mask, s, NEG_BIG)
            ss.append(s)
        ps, ms, alphas = [], [], []
        for h in range(g):
            m = carry[h][0]
            m_new = jnp.maximum(m, jnp.max(ss[h], axis=-1, keepdims=True))
            alphas.append(jnp.exp2(m - m_new))
            ps.append(jnp.exp2(ss[h] - m_new).astype(BF16))
            ms.append(m_new)
        out = []
        for h in range(g):
            v_aug = jnp.concatenate([v_refs[h][pl.ds(off, width), :], ones], axis=1)
            out.append((ms[h], alphas[h] * carry[h][1] + _dot(ps[h], v_aug)))
        return tuple(out)

    def head_chunk(h, j, mc):
        m, acc = mc
        off = pl.multiple_of(j * tk, tk)
        k_blk = k_refs[h][pl.ds(off, tk), :]
        v_aug = jnp.concatenate([v_refs[h][pl.ds(off, tk), :], jnp.ones((tk, HEAD_DIM), BF16)], axis=1)
        f_blk = f_refs[h][j] * LOG2E
        part = tq // FOX_ROW_STREAMS
        rows = [slice(i * part, (i + 1) * part) for i in range(FOX_ROW_STREAMS)]
        ss = [_dot_nt(q_ref[r, h * HEAD_DIM:(h + 1) * HEAD_DIM], k_blk) - f_blk for r in rows]
        m_new = [jnp.maximum(m[r], jnp.max(s, axis=-1, keepdims=True)) for r, s in zip(rows, ss)]
        ps = [jnp.exp2(s - mn).astype(BF16) for s, mn in zip(ss, m_new)]
        accs = [jnp.exp2(m[r] - mn) * acc[r] + _dot(p, v_aug) for r, mn, p in zip(rows, m_new, ps)]
        return jnp.concatenate(m_new, axis=0), jnp.concatenate(accs, axis=0)

    init = tuple((jnp.full((tq, 1), NEG_BIG, F32), jnp.zeros((tq, 2 * HEAD_DIM), F32)) for _ in range(g))
    n_full = (qi * tq) // tk

    parts = max(1, min(FOX_DIAG_PARTS, tk // tq))
    width = tk // parts
    row = lax.broadcasted_iota(I32, (tq, width), 0)
    col = lax.broadcasted_iota(I32, (tq, width), 1)
    carry = init
    for d in reversed(range(max(tq // tk, 1))):
        j = n_full + d
        for p in reversed(range(parts)):
            off = pl.multiple_of(j * tk + p * width, width)
            piece = functools.partial(step, off, width,
                                      lambda h, j=j, p=p: f_refs[h][j][:, p * width:(p + 1) * width],
                                      mask=col + (off - qi * tq) <= row)
            carry = piece(carry) if p == 0 else lax.cond(off < (qi + 1) * tq, piece, lambda c: c, carry)

    qn = []
    for h in range(g):
        q = q_ref[:, h * HEAD_DIM:(h + 1) * HEAD_DIM].astype(F32)
        qk_max = jnp.sqrt(jnp.sum(q * q, axis=-1, keepdims=True)) * kmax_ref[h:h + 1, 0:1]
        qn.append(qk_max * FOX_BOUND_SCALE + FOX_BOUND_SLACK)

    def reaches(j, carry):
        jc = jnp.maximum(j, 0)
        flags = []
        for h in range(g):
            bound = qn[h] - f_refs[h][jc][:, tk - 1:tk] * LOG2E
            flags.append((jnp.max(bound - carry[h][0]) >= FOX_SKIP_LOG2).astype(I32))
        return tuple(flags)

    def chunk_all_heads(state):
        j, _, carry = state
        carry = step(pl.multiple_of(j * tk, tk), tk, lambda h: f_refs[h][j], carry, None)
        return j - 1, reaches(j - 1, carry), carry

    def chunk_live_heads(state):
        j, live, carry = state
        carry = tuple(lax.cond(live[h] > 0, functools.partial(head_chunk, h, j), lambda mc: mc, carry[h])
                      for h in range(g))
        return j - 1, reaches(j - 1, carry), carry

    state = (n_full - 1, reaches(n_full - 1, carry), carry)
    state = lax.while_loop(lambda st: (st[0] >= 0) & (sum(st[1]) == g), chunk_all_heads, state)
    _, _, carry = lax.while_loop(lambda st: (st[0] >= 0) & (sum(st[1]) > 0), chunk_live_heads, state)
    for h in range(g):
        acc = carry[h][1]
        o_ref[:, h * HEAD_DIM:(h + 1) * HEAD_DIM] = (acc[:, :HEAD_DIM] / acc[:, HEAD_DIM:]).astype(o_ref.dtype)


def _fox(fx, f_rows, tq, tk):
    t = fx.shape[0]
    g = FOX_GROUP
    heads = lambda hg, u: hg * g + u
    q_specs = [pl.BlockSpec((tq, g * HEAD_DIM), lambda hg, i: (i, hg))]
    k_specs = [pl.BlockSpec((t, HEAD_DIM), lambda hg, i, u=u: (0, N_HEADS + heads(hg, u)),
                            pipeline_mode=pl.Buffered(1)) for u in range(g)]
    v_specs = [pl.BlockSpec((t, HEAD_DIM), lambda hg, i, u=u: (0, 2 * N_HEADS + heads(hg, u)),
                            pipeline_mode=pl.Buffered(1)) for u in range(g)]
    f_specs = [pl.BlockSpec((None, t // tk, 1, tk), lambda hg, i, u=u: (heads(hg, u), 0, 0, 0)) for u in range(g)]
    return pl.pallas_call(
        functools.partial(_fox_kernel, tk=tk),
        grid=(N_HEADS // g, t // tq),
        in_specs=q_specs + k_specs + v_specs + f_specs,
        out_specs=pl.BlockSpec((tq, g * HEAD_DIM), lambda hg, i: (i, hg)),
        out_shape=jax.ShapeDtypeStruct((t, MIX_W), BF16),
        scratch_shapes=[pltpu.VMEM((8, LANES), F32)],
        compiler_params=_params(("arbitrary", "arbitrary")),
        name="fox",
    )(*([fx] * (1 + 2 * g) + [f_rows] * g))


def _merge_kernel(oa_ref, ob_ref, ga_ref, gb_ref, x_ref, woa_ref, wob_ref, wout_ref, gt1_ref, g2_ref, sc2_ref,
                  sh2_ref, wr_ref, br_ref, x1_ref, h2_ref, ri_ref, gate_ref, cnt_ref, carry_ref):
    i = pl.program_id(0)
    tm = x_ref.shape[0]

    @pl.when(i == 0)
    def _():
        carry_ref[...] = jnp.zeros_like(carry_ref)

    ya = _dot(oa_ref[...], woa_ref[...])
    yb = _dot(ob_ref[...], wob_ref[...])
    merged = _sigmoid(ga_ref[...].astype(F32)) * ya + _sigmoid(gb_ref[...].astype(F32)) * yb
    x1 = x_ref[...] + gt1_ref[...] * _dot(merged.astype(BF16), wout_ref[...])
    x1_ref[...] = x1
    y = x1 * lax.rsqrt(jnp.mean(x1 * x1, axis=-1, keepdims=True) + RMS_EPS) * g2_ref[...]
    h2 = y * (1.0 + sc2_ref[...]) + sh2_ref[...]
    _store_token_tiles(h2_ref, h2)

    logits = _dot_split(h2, wr_ref[...]) + br_ref[...]
    lane = lax.broadcasted_iota(I32, logits.shape, 1)
    lanef = lane.astype(F32)
    cur = jnp.where(lane < N_EXPERTS, logits, -jnp.inf)
    vals, idxs = [], []
    for _ in range(TOP_K):
        m = jnp.max(cur, axis=-1, keepdims=True)
        ix = jnp.min(jnp.where(cur == m, lanef, float(LANES)), axis=-1, keepdims=True)
        vals.append(m)
        idxs.append(ix)
        cur = jnp.where(lanef == ix, -jnp.inf, cur)
    exps = [jnp.exp(v - vals[0]) for v in vals]
    denom = exps[0] + exps[1] + exps[2] + exps[3]

    onehot = jnp.zeros(logits.shape, F32)
    for ix in idxs:
        onehot = onehot + jnp.where(lanef == ix, 1.0, 0.0)
    r = lax.broadcasted_iota(I32, (tm, tm), 0)
    c = lax.broadcasted_iota(I32, (tm, tm), 1)
    strict = jnp.where(r > c, 1.0, 0.0).astype(BF16)
    before = _dot(strict, onehot.astype(BF16)) + carry_ref[...]
    carry_ref[...] = carry_ref[...] + jnp.sum(onehot, axis=0, keepdims=True)
    cnt_ref[...] = carry_ref[...].astype(I32)

    ri = jnp.zeros(logits.shape, F32)
    gates = jnp.zeros(logits.shape, F32)
    for kk in range(TOP_K):
        rank = jnp.sum(jnp.where(lanef == idxs[kk], before, 0.0), axis=-1, keepdims=True)
        ri = ri + jnp.where(lane == kk, idxs[kk], 0.0) + jnp.where(lane == TOP_K + kk, rank, 0.0)
        gates = gates + jnp.where(lane == kk, exps[kk] / denom, 0.0)
    ri_ref[...] = ri.astype(I32)
    gate_ref[...] = gates


def _merge(o_a, o_b, gate_a, gate_b, x, woa, wob, wout, gt1, g2, sc2, sh2, wr, br, tm):
    t, d = x.shape
    row = lambda i: (i, 0)
    fixed = lambda i: (0, 0)
    return pl.pallas_call(
        _merge_kernel,
        grid=(t // tm,),
        in_specs=[pl.BlockSpec((tm, MIX_W), row), pl.BlockSpec((tm, MIX_W), row),
                  pl.BlockSpec((tm, d), row), pl.BlockSpec((tm, d), row), pl.BlockSpec((tm, d), row),
                  pl.BlockSpec((MIX_W, d), fixed), pl.BlockSpec((MIX_W, d), fixed), pl.BlockSpec((d, d), fixed),
                  pl.BlockSpec((1, d), fixed), pl.BlockSpec((1, d), fixed), pl.BlockSpec((1, d), fixed),
                  pl.BlockSpec((1, d), fixed), pl.BlockSpec((d, LANES), fixed), pl.BlockSpec((1, LANES), fixed)],
        out_specs=[pl.BlockSpec((tm, d), row), pl.BlockSpec((tm * TOKEN_TILE_ROWS, LANES), row),
                   pl.BlockSpec((tm, LANES), row),
                   pl.BlockSpec((tm, LANES), row), pl.BlockSpec((1, LANES), fixed)],
        out_shape=[jax.ShapeDtypeStruct((t, d), F32), jax.ShapeDtypeStruct((t * TOKEN_TILE_ROWS, LANES), F32),
                   jax.ShapeDtypeStruct((t, LANES), I32), jax.ShapeDtypeStruct((t, LANES), F32),
                   jax.ShapeDtypeStruct((1, LANES), I32)],
        scratch_shapes=[pltpu.VMEM((1, LANES), F32)],
        compiler_params=_params(("arbitrary",)),
        name="merge",
    )(o_a, o_b, gate_a, gate_b, x, woa, wob, wout, gt1, g2, sc2, sh2, wr, br)


def _dest_kernel(ri_ref, cnt_ref, dest_ref, blk_ref, pend_ref):
    shift = EXPERT_BLOCK.bit_length() - 1
    cnt = jnp.broadcast_to(cnt_ref[...], (8, LANES))
    lane_row = lax.broadcasted_iota(I32, cnt.shape, 1)
    padded = jnp.where(lane_row < N_EXPERTS, ((cnt + (EXPERT_BLOCK - 1)) >> shift) << shift, 0)
    pend = padded
    s = 1
    while s < N_EXPERTS:
        pend = pend + jnp.where(lane_row >= s, pltpu.roll(pend, s, 1), 0)
        s *= 2
    pstart = (pend - padded)[0:1, :].astype(F32)
    pend = pend[0:1, :]
    pend_ref[...] = pend

    ri = ri_ref[...]
    lane = lax.broadcasted_iota(I32, ri.shape, 1)
    dest = jnp.zeros(ri.shape, F32)
    for kk in range(TOP_K):
        ix = ri[:, kk:kk + 1]
        rank = ri[:, TOP_K + kk:TOP_K + kk + 1].astype(F32)
        start = jnp.sum(jnp.where(lane == ix, pstart, 0.0), axis=-1, keepdims=True)
        dest = dest + jnp.where(lane == kk, start + rank, 0.0)
    dest_ref[...] = dest.astype(I32)

    bstart = lax.broadcasted_iota(I32, blk_ref.shape, 0) * EXPERT_BLOCK
    lane_b = lax.broadcasted_iota(I32, blk_ref.shape, 1)
    ended = jnp.where((lane_b < N_EXPERTS) & (pend <= bstart), 1.0, 0.0)
    e = jnp.minimum(jnp.sum(ended, axis=-1, keepdims=True), float(N_EXPERTS - 1))
    vend = pstart + cnt[0:1, :].astype(F32)
    vend_e = jnp.sum(jnp.where(lane_b.astype(F32) == e, vend, 0.0), axis=-1, keepdims=True)
    nvalid = jnp.clip(vend_e - bstart[:, 0:1].astype(F32), 0.0, float(EXPERT_BLOCK))
    blk_ref[...] = jnp.where(lane_b == 0, e, jnp.where(lane_b == 1, nvalid, 0.0)).astype(I32)


def _dest(ri, cnt, n_blocks, tm):
    t = ri.shape[0]
    nb_pad = -(-n_blocks // 8) * 8
    return pl.pallas_call(
        _dest_kernel,
        grid=(t // tm,),
        in_specs=[pl.BlockSpec((tm, LANES), lambda i: (i, 0)), pl.BlockSpec((1, LANES), lambda i: (0, 0))],
        out_specs=[pl.BlockSpec((tm, LANES), lambda i: (i, 0)), pl.BlockSpec((nb_pad, LANES), lambda i: (0, 0)),
                   pl.BlockSpec((1, LANES), lambda i: (0, 0))],
        out_shape=[jax.ShapeDtypeStruct((t, LANES), I32), jax.ShapeDtypeStruct((nb_pad, LANES), I32),
                   jax.ShapeDtypeStruct((1, LANES), I32)],
        compiler_params=_params(("arbitrary",)),
        name="dest",
    )(ri, cnt)


ROW_UNROLL = 8


def _drain_tokens(src_ref, dst_ref, sem, n_tokens):
    def drain(_, carry):
        for _ in range(ROW_UNROLL * TOP_K):
            _token_copy(src_ref, 0, dst_ref, 0, sem).wait()
        return carry

    lax.fori_loop(0, n_tokens // ROW_UNROLL, drain, 0)


def _dispatch_kernel(dest_ref, nvalid_ref, h_ref, xbuf_ref, zero_ref, sem, fill_sem):
    tm = h_ref.shape[0] // TOKEN_TILE_ROWS

    @pl.when(pl.program_id(0) == 0)
    def _():
        zero_ref[...] = jnp.zeros_like(zero_ref)

        def padding_copies(b, act):
            pad = EXPERT_BLOCK - nvalid_ref[b]

            @pl.when(pad > 0)
            def _():
                first = b * EXPERT_BLOCK + nvalid_ref[b]
                size = EXPERT_BLOCK
                while size >= 1:
                    take = (pad & size) != 0
                    rows = size * TOKEN_TILE_ROWS
                    dst0 = pl.multiple_of(first * TOKEN_TILE_ROWS, TOKEN_TILE_ROWS)

                    @pl.when(take)
                    def _(rows=rows, dst0=dst0):
                        act(pltpu.make_async_copy(zero_ref.at[pl.ds(0, rows), :],
                                                  xbuf_ref.at[pl.ds(dst0, rows), :], fill_sem))

                    first = first + jnp.where(take, size, 0)
                    size //= 2

        n_blocks = nvalid_ref.shape[0]
        lax.fori_loop(0, n_blocks, lambda b, c: (padding_copies(b, lambda cp: cp.start(priority=1)), c)[1], 0)
        lax.fori_loop(0, n_blocks, lambda b, c: (padding_copies(b, lambda cp: cp.wait()), c)[1], 0)

    def issue(g, carry):
        for u in range(ROW_UNROLL):
            r = g * ROW_UNROLL + u
            for kk in range(TOP_K):
                _token_copy(h_ref, r, xbuf_ref, dest_ref[r * TOP_K + kk], sem).start(priority=kk % 2)
        return carry

    lax.fori_loop(0, tm // ROW_UNROLL, issue, 0)
    _drain_tokens(h_ref, xbuf_ref, sem, tm)


def _dispatch(dest_flat, nvalid, h2_tiles):
    tm = ROUTE_TILE
    t = h2_tiles.shape[0] // TOKEN_TILE_ROWS
    n_rows = nvalid.shape[0] * EXPERT_BLOCK
    return pl.pallas_call(
        _dispatch_kernel,
        grid=(t // tm,),
        in_specs=[pl.BlockSpec((tm * TOP_K,), lambda i: (i,), memory_space=pltpu.SMEM),
                  pl.BlockSpec(memory_space=pltpu.SMEM),
                  pl.BlockSpec((tm * TOKEN_TILE_ROWS, LANES), lambda i: (i, 0))],
        out_specs=pl.BlockSpec(memory_space=pl.ANY),
        out_shape=jax.ShapeDtypeStruct((n_rows * TOKEN_TILE_ROWS, LANES), h2_tiles.dtype),
        scratch_shapes=[pltpu.VMEM((EXPERT_BLOCK * TOKEN_TILE_ROWS, LANES), h2_tiles.dtype),
                        pltpu.SemaphoreType.DMA(()), pltpu.SemaphoreType.DMA(())],
        compiler_params=_params(("arbitrary",)),
        name="dispatch",
    )(dest_flat, nvalid, h2_tiles)


def _experts_kernel(blk_e_ref, nvalid_ref, grp_ref, nxt_e_ref, x_ref, wgu_hbm, bgu_ref, wd_hbm, bd_ref, y_ref,
                    wgu32_ref, wd32_ref, wgu16_ref, wd16_ref, sems):
    b = pl.program_id(0)
    nvalid = nvalid_ref[b]
    e = blk_e_ref[b]
    slot = grp_ref[b] % 2
    first = (b == 0) | (e != blk_e_ref[jnp.maximum(b - 1, 0)])

    def weight_copies(expert, s):
        return (pltpu.make_async_copy(wgu_hbm.at[expert], wgu32_ref.at[s], sems.at[s, 0]),
                pltpu.make_async_copy(wd_hbm.at[expert], wd32_ref.at[s], sems.at[s, 1]))

    @pl.when(b == 0)
    def _():
        for cp in weight_copies(e, 0):
            cp.start()

    @pl.when(first)
    def _():
        for cp in weight_copies(e, slot):
            cp.wait()
        wgu16_ref[...] = wgu32_ref[slot].astype(BF16)
        wd16_ref[...] = wd32_ref[slot].astype(BF16)

    @pl.when(first & (nxt_e_ref[b] != e))
    def _():
        for cp in weight_copies(nxt_e_ref[b], 1 - slot):
            cp.start()

    n_blk = x_ref.shape[0] // TOKEN_TILE_ROWS
    n_half = n_blk // 2

    def mlp(n):
        x = jnp.concatenate([_load_token_slab(x_ref, s, n) for s in range(TOKEN_TILE_ROWS)], axis=1)
        gu = _dot(x.astype(BF16), wgu16_ref[...]) + bgu_ref[...]
        gate = jnp.minimum(gu[:, :FF], SWIGLU_LIMIT)
        up = jnp.clip(gu[:, FF:], -SWIGLU_LIMIT, SWIGLU_LIMIT)
        act = (up + 1.0) * (gate * _sigmoid(SWIGLU_ALPHA * gate))
        _store_token_tiles(y_ref, _dot(act.astype(BF16), wd16_ref[...]) + bd_ref[...])
        if n < n_blk:
            y_ref[n * TOKEN_TILE_ROWS:, :] = jnp.zeros(((n_blk - n) * TOKEN_TILE_ROWS, LANES), y_ref.dtype)

    @pl.when(nvalid > n_half)
    def _():
        mlp(n_blk)

    @pl.when((nvalid > 0) & (nvalid <= n_half))
    def _():
        mlp(n_half)

    @pl.when(nvalid <= 0)
    def _():
        y_ref[...] = jnp.zeros_like(y_ref)


def _experts(blk_e, nvalid, xbuf, wgu, bgu, wd, bd):
    d = D_MODEL
    blk_rows = EXPERT_BLOCK * TOKEN_TILE_ROWS
    nb = xbuf.shape[0] // blk_rows
    change = jnp.concatenate([jnp.zeros((1,), I32), (blk_e[1:] != blk_e[:-1]).astype(I32)])
    grp = jnp.cumsum(change)
    pos = jnp.where(change > 0, jnp.arange(nb, dtype=I32), nb)
    nxt_pos = lax.cummin(jnp.concatenate([pos[1:], jnp.full((1,), nb, I32)]), reverse=True)
    nxt_e = jnp.where(nxt_pos < nb, blk_e[jnp.minimum(nxt_pos, nb - 1)], blk_e)
    grid_spec = pltpu.PrefetchScalarGridSpec(
        num_scalar_prefetch=4,
        grid=(nb,),
        in_specs=[pl.BlockSpec((blk_rows, LANES), lambda b, e, *_: (b, 0)),
                  pl.BlockSpec(memory_space=pl.ANY),
                  pl.BlockSpec((None, 1, 2 * FF), lambda b, e, *_: (e[b], 0, 0)),
                  pl.BlockSpec(memory_space=pl.ANY),
                  pl.BlockSpec((None, 1, d), lambda b, e, *_: (e[b], 0, 0))],
        out_specs=pl.BlockSpec((blk_rows, LANES), lambda b, e, *_: (b, 0)),
        scratch_shapes=[pltpu.VMEM((2, d, 2 * FF), F32), pltpu.VMEM((2, FF, d), F32),
                        pltpu.VMEM((d, 2 * FF), BF16), pltpu.VMEM((FF, d), BF16),
                        pltpu.SemaphoreType.DMA((2, 2))],
    )
    return pl.pallas_call(
        _experts_kernel,
        grid_spec=grid_spec,
        out_shape=jax.ShapeDtypeStruct(xbuf.shape, F32),
        compiler_params=_params(("arbitrary",)),
        name="experts",
    )(blk_e, nvalid, grp, nxt_e, xbuf, wgu, bgu, wd, bd)


def _combine_kernel(dest_ref, dest_next_ref, ybuf_ref, x1_ref, gate_ref, gt2_ref, gf_ref, o_ref, rows_ref, sems):
    i = pl.program_id(0)
    tm = x1_ref.shape[0]
    slot = i % 2

    def start_gather(idx_ref, s):
        def issue(g, carry):
            for u in range(ROW_UNROLL):
                r = g * ROW_UNROLL + u
                for kk in range(TOP_K):
                    _token_copy(ybuf_ref, idx_ref[r * TOP_K + kk], rows_ref.at[s, kk], r,
                                sems.at[s]).start(priority=kk % 2)
            return carry

        lax.fori_loop(0, tm // ROW_UNROLL, issue, 0)

    @pl.when(i == 0)
    def _():
        start_gather(dest_ref, 0)

    @pl.when(i + 1 < pl.num_programs(0))
    def _():
        start_gather(dest_next_ref, 1 - slot)

    _drain_tokens(ybuf_ref, rows_ref.at[slot, 0], sems.at[slot], tm)

    gates = gate_ref[...]
    slabs = []
    for s in range(TOKEN_TILE_ROWS):
        moe = gates[:, 0:1] * _load_token_slab(rows_ref.at[slot, 0], s, tm)
        for kk in range(1, TOP_K):
            moe = moe + gates[:, kk:kk + 1] * _load_token_slab(rows_ref.at[slot, kk], s, tm)
        slabs.append(moe)
    xo = x1_ref[...] + gt2_ref[...] * jnp.concatenate(slabs, axis=1)
    o_ref[...] = xo * lax.rsqrt(jnp.mean(xo * xo, axis=-1, keepdims=True) + RMS_EPS) * gf_ref[...]


def _combine(dest_flat, ybuf, x1, gates, gt2, g_final):
    t, d = x1.shape
    tm = ROUTE_TILE
    n = t // tm
    return pl.pallas_call(
        _combine_kernel,
        grid=(n,),
        in_specs=[pl.BlockSpec((tm * TOP_K,), lambda i: (i,), memory_space=pltpu.SMEM),
                  pl.BlockSpec((tm * TOP_K,), lambda i: (jnp.minimum(i + 1, n - 1),), memory_space=pltpu.SMEM),
                  pl.BlockSpec(memory_space=pl.ANY),
                  pl.BlockSpec((tm, d), lambda i: (i, 0)),
                  pl.BlockSpec((tm, LANES), lambda i: (i, 0)),
                  pl.BlockSpec((1, d), lambda i: (0, 0)),
                  pl.BlockSpec((1, d), lambda i: (0, 0))],
        out_specs=pl.BlockSpec((tm, d), lambda i: (i, 0)),
        out_shape=jax.ShapeDtypeStruct((t, d), F32),
        scratch_shapes=[pltpu.VMEM((2, TOP_K, tm * TOKEN_TILE_ROWS, LANES), F32), pltpu.SemaphoreType.DMA((2,))],
        compiler_params=_params(("arbitrary",)),
        name="combine",
    )(dest_flat, dest_flat, ybuf, x1, gates, gt2, g_final)


def _pad_lanes(v, lane0):
    return jnp.zeros((1, LANES), F32).at[0, lane0:lane0 + v.shape[0]].set(v.astype(F32))


def _layer(x, mod, g_norm1, w_in, conv_w, a_log, dt_bias, g_onorm, b_fgate, w_o_delta, w_o_fox, w_out,
           g_norm2, w_router, b_router, w_gate_up, b_gate_up, w_down, b_down, g_final):
    t, d = x.shape
    sh1, sc1, gt1, sh2, sc2, gt2 = [mod[:, i * d:(i + 1) * d] for i in range(N_MOD)]

    w_t = w_in.T
    o = 0
    wq = w_t[o:o + QKV_W]; o += QKV_W
    wz = w_t[o:o + MIX_W]; o += MIX_W
    w_beta = w_t[o:o + N_HEADS]; o += N_HEADS
    w_dec = w_t[o:o + N_HEADS]; o += N_HEADS
    wf = w_t[o:o + QKV_W]; o += QKV_W
    wf = jnp.concatenate([wf[:MIX_W] * (LOG2E * HEAD_DIM ** -0.5), wf[MIX_W:]], axis=0)
    w_fg = w_t[o:o + N_HEADS]; o += N_HEADS
    wga = w_t[o:o + d]; o += d
    wgb = w_t[o:o + d]
    ws = jnp.zeros((LANES, d), F32)
    ws = ws.at[L_BETA:L_BETA + N_HEADS].set(w_beta).at[L_G:L_G + N_HEADS].set(w_dec)
    ws = ws.at[L_F:L_F + N_HEADS].set(w_fg)
    bf = lambda w: w.astype(BF16)

    tm = min(ROW_TILE, t)
    qkv, z, fx, gate_a, gate_b, small = _inproj(
        x, g_norm1.reshape(1, d), sc1, sh1, bf(wq), bf(wz), bf(wf), bf(wga), bf(wgb), bf(ws), tm)

    gcol, grow = _gates(small, _pad_lanes(a_log, L_G), _pad_lanes(dt_bias, L_G), _pad_lanes(b_fgate, L_F),
                        min(GATES_TILE, t))

    o_a = _delta(qkv, z, gcol, grow, conv_w, g_onorm.reshape(1, HEAD_DIM), min(DELTA_TILE, t))
    tq, tk = min(FOX_TQ, t), min(FOX_TK, t)
    f_rows = grow[L_F:L_F + N_HEADS].reshape(N_HEADS, t // tk, 1, tk)
    o_b = _fox(fx, f_rows, tq, tk)

    wr = jnp.zeros((d, LANES), F32).at[:, :N_EXPERTS].set(w_router)
    br = _pad_lanes(b_router, 0)
    x1, h2, ri, gates, cnt = _merge(o_a, o_b, gate_a, gate_b, x, bf(w_o_delta), bf(w_o_fox), bf(w_out), gt1,
                                    g_norm2.reshape(1, d), sc2, sh2, wr, br, tm)

    n_blocks = (t * TOP_K) // EXPERT_BLOCK + N_EXPERTS
    dest, blk, _ = _dest(ri, cnt, n_blocks, min(2048, t))
    dest_flat = dest[:, :TOP_K].reshape(t * TOP_K)
    blk_e, nvalid = blk[:n_blocks, 0], blk[:n_blocks, 1]

    xbuf = _dispatch(dest_flat, nvalid, h2)
    ybuf = _experts(blk_e, nvalid, xbuf, w_gate_up, b_gate_up.reshape(N_EXPERTS, 1, 2 * FF),
                    w_down, b_down.reshape(N_EXPERTS, 1, d))
    return _combine(dest_flat, ybuf, x1, gates, gt2, g_final.reshape(1, d))


def kernel(x, c, w_ada, b_ada, g_norm1, w_in, conv_w, a_log, dt_bias, g_onorm, b_fgate, w_o_delta, w_o_fox, w_out,
           g_norm2, w_router, b_router, w_gate_up, b_gate_up, w_down, b_down, g_final):
    b, s, d = x.shape
    assert b == 1 and d == D_MODEL and w_ada.shape[0] == 1
    h = x[0]
    for l in range(w_ada.shape[0]):
        mod = _modulation(c[0], w_ada[l], b_ada[l])
        h = _layer(h, mod, g_norm1[l], w_in[l], conv_w[l], a_log[l], dt_bias[l], g_onorm[l], b_fgate[l],
                   w_o_delta[l], w_o_fox[l], w_out[l], g_norm2[l], w_router[l], b_router[l], w_gate_up[l],
                   b_gate_up[l], w_down[l], b_down[l], g_final)
    return h[None]
```

```python
import functools

import jax
import jax.numpy as jnp
from jax import lax
from jax.experimental import pallas as pl
from jax.experimental.pallas import tpu as pltpu

F32 = jnp.float32
BF16 = jnp.bfloat16
I32 = jnp.int32

D_MODEL = 1024
HEAD_DIM = 128
N_HEADS = 4
MIX_W = N_HEADS * HEAD_DIM
QKV_W = 3 * MIX_W
CONV_K = 4
CHUNK = 64
STACK = N_HEADS * CHUNK
SUB = 16
N_EXPERTS = 32
TOP_K = 4
FF = D_MODEL
SWIGLU_LIMIT = 7.0
SWIGLU_ALPHA = 1.702
RMS_EPS = 1e-6
L2_EPS = 1e-6
N_MOD = 6
LANES = 128
NEG_BIG = -1e30
LOG2E = 1.4426950408889634

SUBLANES = 8
MOD_TILE = 1024
ROW_TILE = 512
GATES_TILE = 256
GATE_ROWS = 24
DELTA_TILE = 256
EXPERT_BLOCK = 512
ROUTE_TILE = 256
VMEM_LIMIT = 56 * 1024 * 1024


def _params(sem):
    return pltpu.CompilerParams(dimension_semantics=sem, vmem_limit_bytes=VMEM_LIMIT)


def _softplus(x):
    return jnp.maximum(x, 0.0) + jnp.log(1.0 + jnp.exp(-jnp.abs(x)))


def _sigmoid(x):
    return 1.0 / (1.0 + jnp.exp(-x))


def _silu(x):
    return x * _sigmoid(x)


def _dot(a, b):
    return jnp.dot(a, b, preferred_element_type=F32)


def _dot_nt(a, b):
    return lax.dot_general(a, b, (((1,), (1,)), ((), ())), preferred_element_type=F32)


def _split_bf16(x, terms):
    parts = []
    for _ in range(terms):
        p = x.astype(BF16)
        parts.append(p)
        x = x - p.astype(F32)
    return parts


def _dot_split(a, b):
    a_hi, a_lo = _split_bf16(a, 2)
    b_hi, b_lo = _split_bf16(b, 2)
    return _dot(a_hi, b_hi) + (_dot(a_hi, b_lo) + _dot(a_lo, b_hi))


def _dot_mask(mask01, x):
    return sum(_dot(mask01, p) for p in _split_bf16(x, 3))


TOKEN_TILE_ROWS = D_MODEL // LANES


def _store_token_tiles(ref, x):
    n = x.shape[0]
    for s in range(TOKEN_TILE_ROWS):
        ref[pl.ds(s, n, stride=TOKEN_TILE_ROWS), :] = x[:, s * LANES:(s + 1) * LANES]


def _load_token_slab(ref, s, n):
    return ref[pl.ds(s, n, stride=TOKEN_TILE_ROWS), :]


def _token_copy(src_ref, src_tok, dst_ref, dst_tok, sem):
    rows = lambda i: pl.ds(pl.multiple_of(i * TOKEN_TILE_ROWS, TOKEN_TILE_ROWS), TOKEN_TILE_ROWS)
    return pltpu.make_async_copy(src_ref.at[rows(src_tok), :], dst_ref.at[rows(dst_tok), :], sem)


def _mod_kernel(c_ref, w_ref, b_ref, o_ref):
    o_ref[...] = jnp.sum(c_ref[...] * w_ref[...], axis=0, keepdims=True) + b_ref[...]


def _modulation(c, w_ada, b_ada):
    d, n = w_ada.shape
    tn = MOD_TILE
    return pl.pallas_call(
        _mod_kernel,
        grid=(n // tn,),
        in_specs=[pl.BlockSpec((d, 1), lambda j: (0, 0)),
                  pl.BlockSpec((d, tn), lambda j: (0, j)),
                  pl.BlockSpec((1, tn), lambda j: (0, j))],
        out_specs=pl.BlockSpec((1, tn), lambda j: (0, j)),
        out_shape=jax.ShapeDtypeStruct((1, n), F32),
        compiler_params=_params(("arbitrary",)),
        name="mod",
    )(c.reshape(d, 1), w_ada, b_ada.reshape(1, n))


def _inproj_kernel(x_ref, g_ref, sc_ref, sh_ref, wq_ref, wz_ref, wf_ref, wga_ref, wgb_ref, ws_ref,
                   oq_ref, oz_ref, of_ref, oga_ref, ogb_ref, os_ref):
    x = x_ref[...]
    y = x * lax.rsqrt(jnp.mean(x * x, axis=-1, keepdims=True) + RMS_EPS) * g_ref[...]
    h = (y * (1.0 + sc_ref[...]) + sh_ref[...]).astype(BF16)
    oq_ref[...] = _dot_nt(h, wq_ref[...])
    oz_ref[...] = _dot_nt(h, wz_ref[...]).astype(BF16)
    of_ref[...] = _dot_nt(h, wf_ref[...]).astype(BF16)
    oga_ref[...] = _dot_nt(h, wga_ref[...]).astype(BF16)
    ogb_ref[...] = _dot_nt(h, wgb_ref[...]).astype(BF16)
    os_ref[...] = _dot_nt(h, ws_ref[...])


def _inproj(x, g1, sc1, sh1, wq, wz, wf, wga, wgb, ws, tm):
    t, d = x.shape
    row = lambda i: (i, 0)
    fixed = lambda i: (0, 0)
    ws_list = [wq, wz, wf, wga, wgb, ws]
    out_dt = [F32, BF16, BF16, BF16, BF16, F32]
    return pl.pallas_call(
        _inproj_kernel,
        grid=(t // tm,),
        in_specs=[pl.BlockSpec((tm, d), row)] + [pl.BlockSpec((1, d), fixed)] * 3
                 + [pl.BlockSpec(w.shape, fixed) for w in ws_list],
        out_specs=[pl.BlockSpec((tm, w.shape[0]), row) for w in ws_list],
        out_shape=[jax.ShapeDtypeStruct((t, w.shape[0]), dt) for w, dt in zip(ws_list, out_dt)],
        compiler_params=_params(("arbitrary",)),
        name="inproj",
    )(x, g1, sc1, sh1, *ws_list)


L_BETA, L_G, L_F, L_GC, L_GL = 0, 4, 8, 12, 16


def _gates_kernel(s_ref, alog_ref, dtb_ref, bf_ref, col_ref, row_ref, carry_ref):
    i = pl.program_id(0)
    tm = s_ref.shape[0]

    @pl.when(i == 0)
    def _():
        carry_ref[...] = jnp.zeros_like(carry_ref)

    s = s_ref[...]
    lane = lax.broadcasted_iota(I32, s.shape, 1)
    beta = _sigmoid(s)
    g = -jnp.exp(alog_ref[...]) * _softplus(s + dtb_ref[...])
    logf = -_softplus(-(s + bf_ref[...]))
    is_g = (lane >= L_G) & (lane < L_G + N_HEADS)
    is_f = (lane >= L_F) & (lane < L_F + N_HEADS)
    g = jnp.where(is_g, g, 0.0)
    logf = jnp.where(is_f, logf, 0.0)

    r = lax.broadcasted_iota(I32, (tm, tm), 0)
    c = lax.broadcasted_iota(I32, (tm, tm), 1)
    same_chunk = (r // CHUNK) == (c // CHUNK)
    tri = jnp.where(r >= c, 1.0, 0.0).astype(BF16)
    tri_chunk = jnp.where(same_chunk & (r >= c), 1.0, 0.0).astype(BF16)
    ones_chunk = jnp.where(same_chunk, 1.0, 0.0).astype(BF16)
    f_cum = _dot_mask(tri, logf) + carry_ref[...]
    carry_ref[...] = f_cum[tm - 1:tm, :]
    gc = _dot_mask(tri_chunk, g)
    gl = _dot_mask(ones_chunk, g)

    out = jnp.where(lane < N_HEADS, beta, 0.0) + g + f_cum
    out = out + pltpu.roll(gc, L_GC - L_G, 1) + pltpu.roll(gl, L_GL - L_G, 1)
    col_ref[...] = out
    row_ref[...] = out.T[:row_ref.shape[0], :]


def _gates(small, alog_row, dtb_row, bf_row, tm):
    t = small.shape[0]
    n_rows = GATE_ROWS
    return pl.pallas_call(
        _gates_kernel,
        grid=(t // tm,),
        in_specs=[pl.BlockSpec((tm, LANES), lambda i: (i, 0))] + [pl.BlockSpec((1, LANES), lambda i: (0, 0))] * 3,
        out_specs=[pl.BlockSpec((tm, LANES), lambda i: (i, 0)), pl.BlockSpec((n_rows, tm), lambda i: (0, i))],
        out_shape=[jax.ShapeDtypeStruct((t, LANES), F32), jax.ShapeDtypeStruct((n_rows, t), F32)],
        scratch_shapes=[pltpu.VMEM((1, LANES), F32)],
        compiler_params=_params(("arbitrary",)),
        name="gates",
    )(small, alog_row, dtb_row, bf_row)


def _stack_heads(a, col0):
    return jnp.concatenate([a[:, col0 + h * HEAD_DIM: col0 + (h + 1) * HEAD_DIM] for h in range(N_HEADS)], axis=0)


def _stack_cols(a, lane0):
    return jnp.concatenate([a[:, lane0 + h: lane0 + h + 1] for h in range(N_HEADS)], axis=0)


def _delta_kernel(qkv_ref, z_ref, gcol_ref, grow_ref, cw_ref, gon_ref, o_ref, ext_ref, tail_ref, s_ref):
    i = pl.program_id(0)
    tb = qkv_ref.shape[0]
    pad = tail_ref.shape[0]

    @pl.when(i == 0)
    def _():
        tail_ref[...] = jnp.zeros_like(tail_ref)
        s_ref[...] = jnp.zeros_like(s_ref)

    ext_ref[0:pad, :] = tail_ref[...]
    ext_ref[pad:pad + tb, :] = qkv_ref[...]
    tail_ref[...] = qkv_ref[tb - pad:tb, :]
    conv = cw_ref[0:1, :] * ext_ref[pad - 3:pad - 3 + tb, :]
    for j in range(1, CONV_K):
        conv = conv + cw_ref[j:j + 1, :] * ext_ref[pad - 3 + j:pad - 3 + j + tb, :]
    act = _silu(conv)

    r = lax.broadcasted_iota(I32, (STACK, STACK), 0)
    c = lax.broadcasted_iota(I32, (STACK, STACK), 1)
    same = (r // CHUNK) == (c // CHUNK)
    m_incl = same & (r >= c)
    m_strict = same & (r > c)
    eye = jnp.where(r == c, 1.0, 0.0)
    m_sub = []
    size = SUB
    while size <= CHUNK:
        m_sub.append((r // size) == (c // size))
        size *= 2
    rb = lax.broadcasted_iota(I32, (STACK, N_HEADS * HEAD_DIM), 0) // CHUNK
    cb = lax.broadcasted_iota(I32, (STACK, N_HEADS * HEAD_DIM), 1) // HEAD_DIM
    head_match = rb == cb

    chunks = range(tb // CHUNK)
    pre = []
    for ch in chunks:
        r0 = ch * CHUNK
        a = act[r0:r0 + CHUNK, :]
        q = _stack_heads(a, 0)
        k = _stack_heads(a, MIX_W)
        v = _stack_heads(a, 2 * MIX_W)
        q = q * lax.rsqrt(jnp.sum(q * q, axis=-1, keepdims=True) + L2_EPS) * (HEAD_DIM ** -0.5)
        k = k * lax.rsqrt(jnp.sum(k * k, axis=-1, keepdims=True) + L2_EPS)

        gcols = gcol_ref[r0:r0 + CHUNK, :]
        beta = _stack_cols(gcols, L_BETA)
        gc = _stack_cols(gcols, L_GC)
        gc_row = jnp.concatenate(
            [grow_ref[L_GC + h:L_GC + h + 1, r0:r0 + CHUNK] for h in range(N_HEADS)], axis=1)
        gl_row = jnp.concatenate(
            [grow_ref[L_GL + h:L_GL + h + 1, r0:r0 + CHUNK] for h in range(N_HEADS)], axis=1)

        decay = jnp.exp(jnp.where(m_incl, gc - gc_row, NEG_BIG))
        kb16 = k.astype(BF16)
        kk = _dot_nt(kb16, kb16)
        lmat = jnp.where(m_strict, kk * beta * decay, 0.0)
        attn = (_dot_nt(q.astype(BF16), kb16) * decay).astype(BF16)
        egc = jnp.exp(gc)
        rhs = jnp.concatenate([v * beta, k * (beta * egc)], axis=1).astype(BF16)
        kt_dec = (k.T * jnp.exp(gl_row - gc_row)).astype(BF16)
        pre.append(dict(lmat=lmat, attn=attn, rhs=rhs, kt_dec=kt_dec, q_dec=q * egc))

    l_d = [jnp.where(m_sub[0], p["lmat"], 0.0) for p in pre]
    inv = [eye - l for l in l_d]
    l16 = [l.astype(BF16) for l in l_d]
    pw = [_dot(l, l).astype(BF16) for l in l16]
    n_sq = SUB.bit_length() - 3
    for s in range(n_sq + 1):
        inv = [x + _dot(x.astype(BF16), p) for x, p in zip(inv, pw)]
        if s < n_sq:
            pw = [_dot(p, p).astype(BF16) for p in pw]
    for lvl in range(1, len(m_sub)):
        m_off = m_sub[lvl] & ~m_sub[lvl - 1]
        off = [jnp.where(m_off, p["lmat"], 0.0).astype(BF16) for p in pre]
        inv16 = [x.astype(BF16) for x in inv]
        half = [_dot(x, o).astype(BF16) for x, o in zip(inv16, off)]
        inv = [x - _dot(h, x16) for x, h, x16 in zip(inv, half, inv16)]
    uws = [_dot(x.astype(BF16), p["rhs"]) for x, p in zip(inv, pre)]

    for ch in chunks:
        r0 = ch * CHUNK
        attn, kt_dec = pre[ch]["attn"], pre[ch]["kt_dec"]
        u, w = uws[ch][:, :HEAD_DIM], uws[ch][:, HEAD_DIM:]

        state16 = s_ref[...].astype(BF16)
        wq_s = _dot(jnp.concatenate([w, pre[ch]["q_dec"]], axis=0).astype(BF16), state16)
        ws_d = jnp.concatenate([wq_s[h * CHUNK:(h + 1) * CHUNK, h * HEAD_DIM:(h + 1) * HEAD_DIM]
                                for h in range(N_HEADS)], axis=0)
        qs_d = jnp.concatenate([wq_s[STACK + h * CHUNK:STACK + (h + 1) * CHUNK, h * HEAD_DIM:(h + 1) * HEAD_DIM]
                                for h in range(N_HEADS)], axis=0)
        v_new = u - ws_d
        v16 = v_new.astype(BF16)
        o = qs_d + _dot(attn, v16)

        v_bd = jnp.where(head_match, jnp.concatenate([v_new] * N_HEADS, axis=1), 0.0).astype(BF16)
        upd = _dot(kt_dec, v_bd)
        for h in range(N_HEADS):
            e_h = jnp.exp(grow_ref[L_GL + h:L_GL + h + 1, r0:r0 + 1])
            sl = slice(h * HEAD_DIM, (h + 1) * HEAD_DIM)
            s_ref[:, sl] = s_ref[:, sl] * e_h + upd[:, sl]

        for h in range(N_HEADS):
            oh = o[h * CHUNK:(h + 1) * CHUNK, :]
            oh = oh * lax.rsqrt(jnp.mean(oh * oh, axis=-1, keepdims=True) + RMS_EPS) * gon_ref[...]
            zh = z_ref[r0:r0 + CHUNK, h * HEAD_DIM:(h + 1) * HEAD_DIM].astype(F32)
            o_ref[r0:r0 + CHUNK, h * HEAD_DIM:(h + 1) * HEAD_DIM] = (oh * _silu(zh)).astype(o_ref.dtype)


def _delta(qkv, z, gcol, grow, conv_w, g_onorm, tb):
    t = qkv.shape[0]
    pad = SUBLANES
    return pl.pallas_call(
        _delta_kernel,
        grid=(t // tb,),
        in_specs=[pl.BlockSpec((tb, QKV_W), lambda i: (i, 0)),
                  pl.BlockSpec((tb, MIX_W), lambda i: (i, 0)),
                  pl.BlockSpec((tb, LANES), lambda i: (i, 0)),
                  pl.BlockSpec((grow.shape[0], tb), lambda i: (0, i)),
                  pl.BlockSpec((CONV_K, QKV_W), lambda i: (0, 0)),
                  pl.BlockSpec((1, HEAD_DIM), lambda i: (0, 0))],
        out_specs=pl.BlockSpec((tb, MIX_W), lambda i: (i, 0)),
        out_shape=jax.ShapeDtypeStruct((t, MIX_W), BF16),
        scratch_shapes=[pltpu.VMEM((tb + pad, QKV_W), F32),
                        pltpu.VMEM((pad, QKV_W), F32),
                        pltpu.VMEM((HEAD_DIM, N_HEADS * HEAD_DIM), F32)],
        compiler_params=_params(("arbitrary",)),
        name="delta",
    )(qkv, z, gcol, grow, conv_w, g_onorm)


FOX_GROUP = 4
FOX_TQ, FOX_TK = 256, 2048
FOX_DIAG_PARTS = 2
FOX_ROW_STREAMS = 2
FOX_BOUND_SCALE, FOX_BOUND_SLACK = 1.01, 1.0
FOX_SKIP_LOG2 = -152.0


def _fox_kernel(*refs, tk):
    g = FOX_GROUP
    q_ref, o_ref, kmax_ref = refs[0], refs[-2], refs[-1]
    k_refs, v_refs, f_refs = (refs[1 + i * g:1 + (i + 1) * g] for i in range(3))
    qi = pl.program_id(1)
    tq = q_ref.shape[0]
    t_keys = k_refs[0].shape[0]

    @pl.when(qi == 0)
    def _():
        rows = min(t_keys, tk)
        for h in range(g):
            def norm_max(c, best, h=h):
                kc = k_refs[h][pl.ds(pl.multiple_of(c * rows, rows), rows), :].astype(F32)
                return jnp.maximum(best, jnp.sum(kc * kc, axis=-1, keepdims=True))

            best = lax.fori_loop(0, t_keys // rows, norm_max, jnp.zeros((rows, 1), F32))
            kmax_ref[h:h + 1, :] = jnp.broadcast_to(jnp.sqrt(jnp.max(best, axis=0, keepdims=True)), (1, LANES))

    def step(off, width, f_blk, carry, mask):
        ones = jnp.ones((width, HEAD_DIM), BF16)
        ss = []
        for h in range(g):
            q = q_ref[:, h * HEAD_DIM:(h + 1) * HEAD_DIM]
            s = _dot_nt(q, k_refs[h][pl.ds(off, width), :]) - f_blk(h) * LOG2E
            if mask is not None:
                s = jnp.where(mask, s, NEG_BIG)
            ss.append(s)
        ps, ms, alphas = [], [], []
        for h in range(g):
            m = carry[h][0]
            m_new = jnp.maximum(m, jnp.max(ss[h], axis=-1, keepdims=True))
            alphas.append(jnp.exp2(m - m_new))
            ps.append(jnp.exp2(ss[h] - m_new).astype(BF16))
            ms.append(m_new)
        out = []
        for h in range(g):
            v_aug = jnp.concatenate([v_refs[h][pl.ds(off, width), :], ones], axis=1)
            out.append((ms[h], alphas[h] * carry[h][1] + _dot(ps[h], v_aug)))
        return tuple(out)

    def head_chunk(h, j, mc):
        m, acc = mc
        off = pl.multiple_of(j * tk, tk)
        k_blk = k_refs[h][pl.ds(off, tk), :]
        v_aug = jnp.concatenate([v_refs[h][pl.ds(off, tk), :], jnp.ones((tk, HEAD_DIM), BF16)], axis=1)
        f_blk = f_refs[h][j] * LOG2E
        part = tq // FOX_ROW_STREAMS
        rows = [slice(i * part, (i + 1) * part) for i in range(FOX_ROW_STREAMS)]
        ss = [_dot_nt(q_ref[r, h * HEAD_DIM:(h + 1) * HEAD_DIM], k_blk) - f_blk for r in rows]
        m_new = [jnp.maximum(m[r], jnp.max(s, axis=-1, keepdims=True)) for r, s in zip(rows, ss)]
        ps = [jnp.exp2(s - mn).astype(BF16) for s, mn in zip(ss, m_new)]
        accs = [jnp.exp2(m[r] - mn) * acc[r] + _dot(p, v_aug) for r, mn, p in zip(rows, m_new, ps)]
        return jnp.concatenate(m_new, axis=0), jnp.concatenate(accs, axis=0)

    init = tuple((jnp.full((tq, 1), NEG_BIG, F32), jnp.zeros((tq, 2 * HEAD_DIM), F32)) for _ in range(g))
    n_full = (qi * tq) // tk

    parts = max(1, min(FOX_DIAG_PARTS, tk // tq))
    width = tk // parts
    row = lax.broadcasted_iota(I32, (tq, width), 0)
    col = lax.broadcasted_iota(I32, (tq, width), 1)
    carry = init
    for d in reversed(range(max(tq // tk, 1))):
        j = n_full + d
        for p in reversed(range(parts)):
            off = pl.multiple_of(j * tk + p * width, width)
            piece = functools.partial(step, off, width,
                                      lambda h, j=j, p=p: f_refs[h][j][:, p * width:(p + 1) * width],
                                      mask=col + (off - qi * tq) <= row)
            carry = piece(carry) if p == 0 else lax.cond(off < (qi + 1) * tq, piece, lambda c: c, carry)

    qn = []
    for h in range(g):
        q = q_ref[:, h * HEAD_DIM:(h + 1) * HEAD_DIM].astype(F32)
        qk_max = jnp.sqrt(jnp.sum(q * q, axis=-1, keepdims=True)) * kmax_ref[h:h + 1, 0:1]
        qn.append(qk_max * FOX_BOUND_SCALE + FOX_BOUND_SLACK)

    def reaches(j, carry):
        jc = jnp.maximum(j, 0)
        flags = []
        for h in range(g):
            bound = qn[h] - f_refs[h][jc][:, tk - 1:tk] * LOG2E
            flags.append((jnp.max(bound - carry[h][0]) >= FOX_SKIP_LOG2).astype(I32))
        return tuple(flags)

    def chunk_all_heads(state):
        j, _, carry = state
        carry = step(pl.multiple_of(j * tk, tk), tk, lambda h: f_refs[h][j], carry, None)
        return j - 1, reaches(j - 1, carry), carry

    def chunk_live_heads(state):
        j, live, carry = state
        carry = tuple(lax.cond(live[h] > 0, functools.partial(head_chunk, h, j), lambda mc: mc, carry[h])
                      for h in range(g))
        return j - 1, reaches(j - 1, carry), carry

    state = (n_full - 1, reaches(n_full - 1, carry), carry)
    state = lax.while_loop(lambda st: (st[0] >= 0) & (sum(st[1]) == g), chunk_all_heads, state)
    _, _, carry = lax.while_loop(lambda st: (st[0] >= 0) & (sum(st[1]) > 0), chunk_live_heads, state)
    for h in range(g):
        acc = carry[h][1]
        o_ref[:, h * HEAD_DIM:(h + 1) * HEAD_DIM] = (acc[:, :HEAD_DIM] / acc[:, HEAD_DIM:]).astype(o_ref.dtype)


def _fox(fx, f_rows, tq, tk):
    t = fx.shape[0]
    g = FOX_GROUP
    heads = lambda hg, u: hg * g + u
    q_specs = [pl.BlockSpec((tq, g * HEAD_DIM), lambda hg, i: (i, hg))]
    k_specs = [pl.BlockSpec((t, HEAD_DIM), lambda hg, i, u=u: (0, N_HEADS + heads(hg, u)),
                            pipeline_mode=pl.Buffered(1)) for u in range(g)]
    v_specs = [pl.BlockSpec((t, HEAD_DIM), lambda hg, i, u=u: (0, 2 * N_HEADS + heads(hg, u)),
                            pipeline_mode=pl.Buffered(1)) for u in range(g)]
    f_specs = [pl.BlockSpec((None, t // tk, 1, tk), lambda hg, i, u=u: (heads(hg, u), 0, 0, 0)) for u in range(g)]
    return pl.pallas_call(
        functools.partial(_fox_kernel, tk=tk),
        grid=(N_HEADS // g, t // tq),
        in_specs=q_specs + k_specs + v_specs + f_specs,
        out_specs=pl.BlockSpec((tq, g * HEAD_DIM), lambda hg, i: (i, hg)),
        out_shape=jax.ShapeDtypeStruct((t, MIX_W), BF16),
        scratch_shapes=[pltpu.VMEM((8, LANES), F32)],
        compiler_params=_params(("arbitrary", "arbitrary")),
        name="fox",
    )(*([fx] * (1 + 2 * g) + [f_rows] * g))


def _merge_kernel(oa_ref, ob_ref, ga_ref, gb_ref, x_ref, woa_ref, wob_ref, wout_ref, gt1_ref, g2_ref, sc2_ref,
                  sh2_ref, wr_ref, br_ref, x1_ref, h2_ref, ri_ref, gate_ref, cnt_ref, carry_ref):
    i = pl.program_id(0)
    tm = x_ref.shape[0]

    @pl.when(i == 0)
    def _():
        carry_ref[...] = jnp.zeros_like(carry_ref)

    ya = _dot(oa_ref[...], woa_ref[...])
    yb = _dot(ob_ref[...], wob_ref[...])
    merged = _sigmoid(ga_ref[...].astype(F32)) * ya + _sigmoid(gb_ref[...].astype(F32)) * yb
    x1 = x_ref[...] + gt1_ref[...] * _dot(merged.astype(BF16), wout_ref[...])
    x1_ref[...] = x1
    y = x1 * lax.rsqrt(jnp.mean(x1 * x1, axis=-1, keepdims=True) + RMS_EPS) * g2_ref[...]
    h2 = y * (1.0 + sc2_ref[...]) + sh2_ref[...]
    _store_token_tiles(h2_ref, h2)

    logits = _dot_split(h2, wr_ref[...]) + br_ref[...]
    lane = lax.broadcasted_iota(I32, logits.shape, 1)
    lanef = lane.astype(F32)
    cur = jnp.where(lane < N_EXPERTS, logits, -jnp.inf)
    vals, idxs = [], []
    for _ in range(TOP_K):
        m = jnp.max(cur, axis=-1, keepdims=True)
        ix = jnp.min(jnp.where(cur == m, lanef, float(LANES)), axis=-1, keepdims=True)
        vals.append(m)
        idxs.append(ix)
        cur = jnp.where(lanef == ix, -jnp.inf, cur)
    exps = [jnp.exp(v - vals[0]) for v in vals]
    denom = exps[0] + exps[1] + exps[2] + exps[3]

    onehot = jnp.zeros(logits.shape, F32)
    for ix in idxs:
        onehot = onehot + jnp.where(lanef == ix, 1.0, 0.0)
    r = lax.broadcasted_iota(I32, (tm, tm), 0)
    c = lax.broadcasted_iota(I32, (tm, tm), 1)
    strict = jnp.where(r > c, 1.0, 0.0).astype(BF16)
    before = _dot(strict, onehot.astype(BF16)) + carry_ref[...]
    carry_ref[...] = carry_ref[...] + jnp.sum(onehot, axis=0, keepdims=True)
    cnt_ref[...] = carry_ref[...].astype(I32)

    ri = jnp.zeros(logits.shape, F32)
    gates = jnp.zeros(logits.shape, F32)
    for kk in range(TOP_K):
        rank = jnp.sum(jnp.where(lanef == idxs[kk], before, 0.0), axis=-1, keepdims=True)
        ri = ri + jnp.where(lane == kk, idxs[kk], 0.0) + jnp.where(lane == TOP_K + kk, rank, 0.0)
        gates = gates + jnp.where(lane == kk, exps[kk] / denom, 0.0)
    ri_ref[...] = ri.astype(I32)
    gate_ref[...] = gates


def _merge(o_a, o_b, gate_a, gate_b, x, woa, wob, wout, gt1, g2, sc2, sh2, wr, br, tm):
    t, d = x.shape
    row = lambda i: (i, 0)
    fixed = lambda i: (0, 0)
    return pl.pallas_call(
        _merge_kernel,
        grid=(t // tm,),
        in_specs=[pl.BlockSpec((tm, MIX_W), row), pl.BlockSpec((tm, MIX_W), row),
                  pl.BlockSpec((tm, d), row), pl.BlockSpec((tm, d), row), pl.BlockSpec((tm, d), row),
                  pl.BlockSpec((MIX_W, d), fixed), pl.BlockSpec((MIX_W, d), fixed), pl.BlockSpec((d, d), fixed),
                  pl.BlockSpec((1, d), fixed), pl.BlockSpec((1, d), fixed), pl.BlockSpec((1, d), fixed),
                  pl.BlockSpec((1, d), fixed), pl.BlockSpec((d, LANES), fixed), pl.BlockSpec((1, LANES), fixed)],
        out_specs=[pl.BlockSpec((tm, d), row), pl.BlockSpec((tm * TOKEN_TILE_ROWS, LANES), row),
                   pl.BlockSpec((tm, LANES), row),
                   pl.BlockSpec((tm, LANES), row), pl.BlockSpec((1, LANES), fixed)],
        out_shape=[jax.ShapeDtypeStruct((t, d), F32), jax.ShapeDtypeStruct((t * TOKEN_TILE_ROWS, LANES), F32),
                   jax.ShapeDtypeStruct((t, LANES), I32), jax.ShapeDtypeStruct((t, LANES), F32),
                   jax.ShapeDtypeStruct((1, LANES), I32)],
        scratch_shapes=[pltpu.VMEM((1, LANES), F32)],
        compiler_params=_params(("arbitrary",)),
        name="merge",
    )(o_a, o_b, gate_a, gate_b, x, woa, wob, wout, gt1, g2, sc2, sh2, wr, br)


def _dest_kernel(ri_ref, cnt_ref, dest_ref, blk_ref, pend_ref):
    shift = EXPERT_BLOCK.bit_length() - 1
    cnt = jnp.broadcast_to(cnt_ref[...], (8, LANES))
    lane_row = lax.broadcasted_iota(I32, cnt.shape, 1)
    padded = jnp.where(lane_row < N_EXPERTS, ((cnt + (EXPERT_BLOCK - 1)) >> shift) << shift, 0)
    pend = padded
    s = 1
    while s < N_EXPERTS:
        pend = pend + jnp.where(lane_row >= s, pltpu.roll(pend, s, 1), 0)
        s *= 2
    pstart = (pend - padded)[0:1, :].astype(F32)
    pend = pend[0:1, :]
    pend_ref[...] = pend

    ri = ri_ref[...]
    lane = lax.broadcasted_iota(I32, ri.shape, 1)
    dest = jnp.zeros(ri.shape, F32)
    for kk in range(TOP_K):
        ix = ri[:, kk:kk + 1]
        rank = ri[:, TOP_K + kk:TOP_K + kk + 1].astype(F32)
        start = jnp.sum(jnp.where(lane == ix, pstart, 0.0), axis=-1, keepdims=True)
        dest = dest + jnp.where(lane == kk, start + rank, 0.0)
    dest_ref[...] = dest.astype(I32)

    bstart = lax.broadcasted_iota(I32, blk_ref.shape, 0) * EXPERT_BLOCK
    lane_b = lax.broadcasted_iota(I32, blk_ref.shape, 1)
    ended = jnp.where((lane_b < N_EXPERTS) & (pend <= bstart), 1.0, 0.0)
    e = jnp.minimum(jnp.sum(ended, axis=-1, keepdims=True), float(N_EXPERTS - 1))
    vend = pstart + cnt[0:1, :].astype(F32)
    vend_e = jnp.sum(jnp.where(lane_b.astype(F32) == e, vend, 0.0), axis=-1, keepdims=True)
    nvalid = jnp.clip(vend_e - bstart[:, 0:1].astype(F32), 0.0, float(EXPERT_BLOCK))
    blk_ref[...] = jnp.where(lane_b == 0, e, jnp.where(lane_b == 1, nvalid, 0.0)).astype(I32)


def _dest(ri, cnt, n_blocks, tm):
    t = ri.shape[0]
    nb_pad = -(-n_blocks // 8) * 8
    return pl.pallas_call(
        _dest_kernel,
        grid=(t // tm,),
        in_specs=[pl.BlockSpec((tm, LANES), lambda i: (i, 0)), pl.BlockSpec((1, LANES), lambda i: (0, 0))],
        out_specs=[pl.BlockSpec((tm, LANES), lambda i: (i, 0)), pl.BlockSpec((nb_pad, LANES), lambda i: (0, 0)),
                   pl.BlockSpec((1, LANES), lambda i: (0, 0))],
        out_shape=[jax.ShapeDtypeStruct((t, LANES), I32), jax.ShapeDtypeStruct((nb_pad, LANES), I32),
                   jax.ShapeDtypeStruct((1, LANES), I32)],
        compiler_params=_params(("arbitrary",)),
        name="dest",
    )(ri, cnt)


ROW_UNROLL = 8


def _drain_tokens(src_ref, dst_ref, sem, n_tokens):
    def drain(_, carry):
        for _ in range(ROW_UNROLL * TOP_K):
            _token_copy(src_ref, 0, dst_ref, 0, sem).wait()
        return carry

    lax.fori_loop(0, n_tokens // ROW_UNROLL, drain, 0)


def _dispatch_kernel(dest_ref, nvalid_ref, h_ref, xbuf_ref, zero_ref, sem, fill_sem):
    tm = h_ref.shape[0] // TOKEN_TILE_ROWS

    @pl.when(pl.program_id(0) == 0)
    def _():
        zero_ref[...] = jnp.zeros_like(zero_ref)

        def padding_copies(b, act):
            pad = EXPERT_BLOCK - nvalid_ref[b]

            @pl.when(pad > 0)
            def _():
                first = b * EXPERT_BLOCK + nvalid_ref[b]
                size = EXPERT_BLOCK
                while size >= 1:
                    take = (pad & size) != 0
                    rows = size * TOKEN_TILE_ROWS
                    dst0 = pl.multiple_of(first * TOKEN_TILE_ROWS, TOKEN_TILE_ROWS)

                    @pl.when(take)
                    def _(rows=rows, dst0=dst0):
                        act(pltpu.make_async_copy(zero_ref.at[pl.ds(0, rows), :],
                                                  xbuf_ref.at[pl.ds(dst0, rows), :], fill_sem))

                    first = first + jnp.where(take, size, 0)
                    size //= 2

        n_blocks = nvalid_ref.shape[0]
        lax.fori_loop(0, n_blocks, lambda b, c: (padding_copies(b, lambda cp: cp.start()), c)[1], 0)
        lax.fori_loop(0, n_blocks, lambda b, c: (padding_copies(b, lambda cp: cp.wait()), c)[1], 0)

    def issue(g, carry):
        for u in range(ROW_UNROLL):
            r = g * ROW_UNROLL + u
            for kk in range(TOP_K):
                _token_copy(h_ref, r, xbuf_ref, dest_ref[r, kk], sem).start(priority=kk % 2)
        return carry

    lax.fori_loop(0, tm // ROW_UNROLL, issue, 0)
    _drain_tokens(h_ref, xbuf_ref, sem, tm)


def _dispatch(dest, nvalid, h2_tiles):
    tm = ROUTE_TILE
    t = h2_tiles.shape[0] // TOKEN_TILE_ROWS
    n_rows = nvalid.shape[0] * EXPERT_BLOCK
    return pl.pallas_call(
        _dispatch_kernel,
        grid=(t // tm,),
        in_specs=[pl.BlockSpec((tm, LANES), lambda i: (i, 0), memory_space=pltpu.SMEM),
                  pl.BlockSpec(memory_space=pltpu.SMEM),
                  pl.BlockSpec((tm * TOKEN_TILE_ROWS, LANES), lambda i: (i, 0))],
        out_specs=pl.BlockSpec(memory_space=pl.ANY),
        out_shape=jax.ShapeDtypeStruct((n_rows * TOKEN_TILE_ROWS, LANES), h2_tiles.dtype),
        scratch_shapes=[pltpu.VMEM((EXPERT_BLOCK * TOKEN_TILE_ROWS, LANES), h2_tiles.dtype),
                        pltpu.SemaphoreType.DMA(()), pltpu.SemaphoreType.DMA(())],
        compiler_params=_params(("arbitrary",)),
        name="dispatch",
    )(dest, nvalid, h2_tiles)


def _experts_kernel(blk_e_ref, nvalid_ref, grp_ref, nxt_e_ref, x_ref, wgu_hbm, bgu_ref, wd_hbm, bd_ref, y_ref,
                    wgu32_ref, wd32_ref, wgu16_ref, wd16_ref, sems):
    b = pl.program_id(0)
    nvalid = nvalid_ref[b]
    e = blk_e_ref[b]
    slot = grp_ref[b] % 2
    first = (b == 0) | (e != blk_e_ref[jnp.maximum(b - 1, 0)])

    def weight_copies(expert, s):
        return (pltpu.make_async_copy(wgu_hbm.at[expert], wgu32_ref.at[s], sems.at[s, 0]),
                pltpu.make_async_copy(wd_hbm.at[expert], wd32_ref.at[s], sems.at[s, 1]))

    @pl.when(b == 0)
    def _():
        for cp in weight_copies(e, 0):
            cp.start()

    @pl.when(first)
    def _():
        for cp in weight_copies(e, slot):
            cp.wait()
        wgu16_ref[...] = wgu32_ref[slot].astype(BF16)
        wd16_ref[...] = wd32_ref[slot].astype(BF16)

    @pl.when(first & (nxt_e_ref[b] != e))
    def _():
        for cp in weight_copies(nxt_e_ref[b], 1 - slot):
            cp.start()

    n_blk = x_ref.shape[0] // TOKEN_TILE_ROWS
    n_half = n_blk // 2

    def mlp(n):
        x = jnp.concatenate([_load_token_slab(x_ref, s, n) for s in range(TOKEN_TILE_ROWS)], axis=1)
        gu = _dot(x.astype(BF16), wgu16_ref[...]) + bgu_ref[...]
        gate = jnp.minimum(gu[:, :FF], SWIGLU_LIMIT)
        up = jnp.clip(gu[:, FF:], -SWIGLU_LIMIT, SWIGLU_LIMIT)
        act = (up + 1.0) * (gate * _sigmoid(SWIGLU_ALPHA * gate))
        _store_token_tiles(y_ref, _dot(act.astype(BF16), wd16_ref[...]) + bd_ref[...])
        if n < n_blk:
            y_ref[n * TOKEN_TILE_ROWS:, :] = jnp.zeros(((n_blk - n) * TOKEN_TILE_ROWS, LANES), y_ref.dtype)

    @pl.when(nvalid > n_half)
    def _():
        mlp(n_blk)

    @pl.when((nvalid > 0) & (nvalid <= n_half))
    def _():
        mlp(n_half)

    @pl.when(nvalid <= 0)
    def _():
        y_ref[...] = jnp.zeros_like(y_ref)


def _experts(blk_e, nvalid, xbuf, wgu, bgu, wd, bd):
    d = D_MODEL
    blk_rows = EXPERT_BLOCK * TOKEN_TILE_ROWS
    nb = xbuf.shape[0] // blk_rows
    change = jnp.concatenate([jnp.zeros((1,), I32), (blk_e[1:] != blk_e[:-1]).astype(I32)])
    grp = jnp.cumsum(change)
    pos = jnp.where(change > 0, jnp.arange(nb, dtype=I32), nb)
    nxt_pos = lax.cummin(jnp.concatenate([pos[1:], jnp.full((1,), nb, I32)]), reverse=True)
    nxt_e = jnp.where(nxt_pos < nb, blk_e[jnp.minimum(nxt_pos, nb - 1)], blk_e)
    grid_spec = pltpu.PrefetchScalarGridSpec(
        num_scalar_prefetch=4,
        grid=(nb,),
        in_specs=[pl.BlockSpec((blk_rows, LANES), lambda b, e, *_: (b, 0)),
                  pl.BlockSpec(memory_space=pl.ANY),
                  pl.BlockSpec((None, 1, 2 * FF), lambda b, e, *_: (e[b], 0, 0)),
                  pl.BlockSpec(memory_space=pl.ANY),
                  pl.BlockSpec((None, 1, d), lambda b, e, *_: (e[b], 0, 0))],
        out_specs=pl.BlockSpec((blk_rows, LANES), lambda b, e, *_: (b, 0)),
        scratch_shapes=[pltpu.VMEM((2, d, 2 * FF), F32), pltpu.VMEM((2, FF, d), F32),
                        pltpu.VMEM((d, 2 * FF), BF16), pltpu.VMEM((FF, d), BF16),
                        pltpu.SemaphoreType.DMA((2, 2))],
    )
    return pl.pallas_call(
        _experts_kernel,
        grid_spec=grid_spec,
        out_shape=jax.ShapeDtypeStruct(xbuf.shape, F32),
        compiler_params=_params(("arbitrary",)),
        name="experts",
    )(blk_e, nvalid, grp, nxt_e, xbuf, wgu, bgu, wd, bd)


def _combine_kernel(dest_ref, dest_next_ref, ybuf_ref, x1_ref, gate_ref, gt2_ref, gf_ref, o_ref, rows_ref, sems):
    i = pl.program_id(0)
    tm = x1_ref.shape[0]
    slot = i % 2

    def start_gather(idx_ref, s):
        def issue(g, carry):
            for u in range(ROW_UNROLL):
                r = g * ROW_UNROLL + u
                for kk in range(TOP_K):
                    _token_copy(ybuf_ref, idx_ref[r, kk], rows_ref.at[s, kk], r,
                                sems.at[s]).start(priority=kk % 2)
            return carry

        lax.fori_loop(0, tm // ROW_UNROLL, issue, 0)

    @pl.when(i == 0)
    def _():
        start_gather(dest_ref, 0)

    @pl.when(i + 1 < pl.num_programs(0))
    def _():
        start_gather(dest_next_ref, 1 - slot)

    _drain_tokens(ybuf_ref, rows_ref.at[slot, 0], sems.at[slot], tm)

    gates = gate_ref[...]
    slabs = []
    for s in range(TOKEN_TILE_ROWS):
        moe = gates[:, 0:1] * _load_token_slab(rows_ref.at[slot, 0], s, tm)
        for kk in range(1, TOP_K):
            moe = moe + gates[:, kk:kk + 1] * _load_token_slab(rows_ref.at[slot, kk], s, tm)
        slabs.append(moe)
    xo = x1_ref[...] + gt2_ref[...] * jnp.concatenate(slabs, axis=1)
    o_ref[...] = xo * lax.rsqrt(jnp.mean(xo * xo, axis=-1, keepdims=True) + RMS_EPS) * gf_ref[...]


def _combine(dest, ybuf, x1, gates, gt2, g_final):
    t, d = x1.shape
    tm = ROUTE_TILE
    n = t // tm
    return pl.pallas_call(
        _combine_kernel,
        grid=(n,),
        in_specs=[pl.BlockSpec((tm, LANES), lambda i: (i, 0), memory_space=pltpu.SMEM),
                  pl.BlockSpec((tm, LANES), lambda i: (jnp.minimum(i + 1, n - 1), 0), memory_space=pltpu.SMEM),
                  pl.BlockSpec(memory_space=pl.ANY),
                  pl.BlockSpec((tm, d), lambda i: (i, 0)),
                  pl.BlockSpec((tm, LANES), lambda i: (i, 0)),
                  pl.BlockSpec((1, d), lambda i: (0, 0)),
                  pl.BlockSpec((1, d), lambda i: (0, 0))],
        out_specs=pl.BlockSpec((tm, d), lambda i: (i, 0)),
        out_shape=jax.ShapeDtypeStruct((t, d), F32),
        scratch_shapes=[pltpu.VMEM((2, TOP_K, tm * TOKEN_TILE_ROWS, LANES), F32), pltpu.SemaphoreType.DMA((2,))],
        compiler_params=_params(("arbitrary",)),
        name="combine",
    )(dest, dest, ybuf, x1, gates, gt2, g_final)


def _pad_lanes(v, lane0):
    return jnp.zeros((1, LANES), F32).at[0, lane0:lane0 + v.shape[0]].set(v.astype(F32))


def _layer(x, mod, g_norm1, w_in, conv_w, a_log, dt_bias, g_onorm, b_fgate, w_o_delta, w_o_fox, w_out,
           g_norm2, w_router, b_router, w_gate_up, b_gate_up, w_down, b_down, g_final):
    t, d = x.shape
    sh1, sc1, gt1, sh2, sc2, gt2 = [mod[:, i * d:(i + 1) * d] for i in range(N_MOD)]

    w_t = w_in.T
    o = 0
    wq = w_t[o:o + QKV_W]; o += QKV_W
    wz = w_t[o:o + MIX_W]; o += MIX_W
    w_beta = w_t[o:o + N_HEADS]; o += N_HEADS
    w_dec = w_t[o:o + N_HEADS]; o += N_HEADS
    wf = w_t[o:o + QKV_W]; o += QKV_W
    wf = jnp.concatenate([wf[:MIX_W] * (LOG2E * HEAD_DIM ** -0.5), wf[MIX_W:]], axis=0)
    w_fg = w_t[o:o + N_HEADS]; o += N_HEADS
    wga = w_t[o:o + d]; o += d
    wgb = w_t[o:o + d]
    ws = jnp.zeros((LANES, d), F32)
    ws = ws.at[L_BETA:L_BETA + N_HEADS].set(w_beta).at[L_G:L_G + N_HEADS].set(w_dec)
    ws = ws.at[L_F:L_F + N_HEADS].set(w_fg)
    bf = lambda w: w.astype(BF16)

    tm = min(ROW_TILE, t)
    qkv, z, fx, gate_a, gate_b, small = _inproj(
        x, g_norm1.reshape(1, d), sc1, sh1, bf(wq), bf(wz), bf(wf), bf(wga), bf(wgb), bf(ws), tm)

    gcol, grow = _gates(small, _pad_lanes(a_log, L_G), _pad_lanes(dt_bias, L_G), _pad_lanes(b_fgate, L_F),
                        min(GATES_TILE, t))

    o_a = _delta(qkv, z, gcol, grow, conv_w, g_onorm.reshape(1, HEAD_DIM), min(DELTA_TILE, t))
    tq, tk = min(FOX_TQ, t), min(FOX_TK, t)
    f_rows = grow[L_F:L_F + N_HEADS].reshape(N_HEADS, t // tk, 1, tk)
    o_b = _fox(fx, f_rows, tq, tk)

    wr = jnp.zeros((d, LANES), F32).at[:, :N_EXPERTS].set(w_router)
    br = _pad_lanes(b_router, 0)
    x1, h2, ri, gates, cnt = _merge(o_a, o_b, gate_a, gate_b, x, bf(w_o_delta), bf(w_o_fox), bf(w_out), gt1,
                                    g_norm2.reshape(1, d), sc2, sh2, wr, br, tm)

    n_blocks = (t * TOP_K) // EXPERT_BLOCK + N_EXPERTS
    dest, blk, _ = _dest(ri, cnt, n_blocks, min(2048, t))
    blk_e, nvalid = blk[:n_blocks, 0], blk[:n_blocks, 1]

    xbuf = _dispatch(dest, nvalid, h2)
    ybuf = _experts(blk_e, nvalid, xbuf, w_gate_up, b_gate_up.reshape(N_EXPERTS, 1, 2 * FF),
                    w_down, b_down.reshape(N_EXPERTS, 1, d))
    return _combine(dest, ybuf, x1, gates, gt2, g_final.reshape(1, d))


def kernel(x, c, w_ada, b_ada, g_norm1, w_in, conv_w, a_log, dt_bias, g_onorm, b_fgate, w_o_delta, w_o_fox, w_out,
           g_norm2, w_router, b_router, w_gate_up, b_gate_up, w_down, b_down, g_final):
    b, s, d = x.shape
    assert b == 1 and d == D_MODEL and w_ada.shape[0] == 1
    h = x[0]
    for l in range(w_ada.shape[0]):
        mod = _modulation(c[0], w_ada[l], b_ada[l])
        h = _layer(h, mod, g_norm1[l], w_in[l], conv_w[l], a_log[l], dt_bias[l], g_onorm[l], b_fgate[l],
                   w_o_delta[l], w_o_fox[l], w_out[l], g_norm2[l], w_router[l], b_router[l], w_gate_up[l],
                   b_gate_up[l], w_down[l], b_down[l], g_final)
    return h[None]
```
